```python
import math
import jax, jax.numpy as jnp
from jax import lax
import numpy as np


D_MODEL = 4096
BATCH = 2
SEQ = 4096
DEPTH = 2

HEAD_DIM = 128
EPS = 1e-6

NSA_HEADS = 16
NSA_KV_GROUPS = 2
NSA_GROUP_SIZE = NSA_HEADS // NSA_KV_GROUPS
CMP_BLOCK = 32
CMP_STRIDE = 16
SEL_BLOCK = 64
SEL_TOP_N = 16
SEL_LOCAL = 2
WINDOW = 512
Q_BLOCK = 128
FORCE = 1e9

GDN_HEADS = 16
GDN_CONV = 4
GDN_CHUNK = 64

D_NSA = NSA_HEADS * HEAD_DIM
D_KV_NSA = NSA_KV_GROUPS * HEAD_DIM
D_GDN = GDN_HEADS * HEAD_DIM
D_MIX = D_NSA + D_GDN
IN_SPLITS = (D_NSA, 6 * D_KV_NSA, 3 * NSA_HEADS, 3 * D_GDN, GDN_HEADS, GDN_HEADS, D_GDN)
N_IN = D_NSA + 6 * D_KV_NSA + 3 * NSA_HEADS + 3 * D_GDN + 2 * GDN_HEADS + D_GDN

MEM_LEN = 256
XA_HEADS = 4
D_XA = XA_HEADS * HEAD_DIM

D_FF_DENSE = 11008
N_EXPERTS = 8
TOP_K = 2
D_FF_EXPERT = 5632

kernel_name = 'hybrid_nsa_gdn_moe_block'


def rmsnorm(x, g):
    xf = x.astype(jnp.float32)
    y = xf * lax.rsqrt(jnp.mean(xf * xf, axis=-1, keepdims=True) + EPS)
    return (y * g.astype(jnp.float32)).astype(x.dtype)


def l2norm(x):
    return x * lax.rsqrt(jnp.sum(x * x, axis=-1, keepdims=True) + EPS)


def masked_softmax(s, mask):
    s = jnp.where(mask, s.astype(jnp.float32), -jnp.inf)
    m = jnp.max(s, axis=-1, keepdims=True)
    m = jnp.where(jnp.isfinite(m), m, 0.0)
    p = jnp.exp(s - m)
    den = jnp.sum(p, axis=-1, keepdims=True)
    return p / jnp.where(den > 0, den, 1.0)


def cmp_to_sel_matrix(n_cmp, n_sel):
    c_start = np.arange(n_cmp)[:, None] * CMP_STRIDE
    s_start = np.arange(n_sel)[None, :] * SEL_BLOCK
    overlap = np.minimum(c_start + CMP_BLOCK, s_start + SEL_BLOCK) - np.maximum(c_start, s_start)
    return jnp.asarray(np.clip(overlap, 0, None) / CMP_STRIDE, dtype=jnp.float32)


def nsa_mixer(q, kv, gate_logits, q_gain, k_gain, cmp_pos, cmp_w1, cmp_w2, out_gain):
    B, S = q.shape[0], q.shape[1]
    G, Hg, Dh = NSA_KV_GROUPS, NSA_GROUP_SIZE, HEAD_DIM
    scale = Dh ** -0.5
    q = rmsnorm(q.reshape(B, S, G, Hg, Dh), q_gain)
    kv = kv.reshape(B, S, 3, 2, G, Dh)
    pos = jnp.arange(S)

    n_cmp = (S - CMP_BLOCK) // CMP_STRIDE + 1
    blk = np.arange(n_cmp)[:, None] * CMP_STRIDE + np.arange(CMP_BLOCK)[None, :]
    blocks = kv[:, :, 0][:, blk]
    blocks = blocks + cmp_pos.transpose(1, 0, 2)[:, :, None, :]
    blocks = blocks.transpose(0, 1, 3, 4, 2, 5).reshape(B, n_cmp, 2, G, CMP_BLOCK * Dh)
    hid = jax.nn.gelu(jnp.einsum('bnkgf,kfe->bnkge', blocks, cmp_w1))
    comp = jnp.einsum('bnkge,ked->bnkgd', hid, cmp_w2)
    k_cmp = rmsnorm(comp[:, :, 0], k_gain[0])
    v_cmp = comp[:, :, 1]
    s_c = jnp.einsum('bsghd,bngd->bghsn', q, k_cmp) * scale
    c_end = jnp.arange(n_cmp) * CMP_STRIDE + CMP_BLOCK - 1
    p_c = masked_softmax(s_c, c_end[None, :] <= pos[:, None])
    o_cmp = jnp.einsum('bghsn,bngd->bsghd', p_c.astype(v_cmp.dtype), v_cmp)

    n_sel = S // SEL_BLOCK
    k_top = min(SEL_TOP_N, n_sel)
    imp = jnp.einsum('bghsn,nm->bgsm', p_c, cmp_to_sel_matrix(n_cmp, n_sel))
    cur = pos // SEL_BLOCK
    j = jnp.arange(n_sel)
    valid = j[None, :] <= cur[:, None]
    forced = valid & ((j[None, :] == 0) | (j[None, :] > cur[:, None] - SEL_LOCAL))
    score = jnp.where(forced, FORCE, jnp.where(valid, imp, -FORCE))
    _, sel_idx = lax.top_k(score, k_top)

    k_slc = rmsnorm(kv[:, :, 1, 0], k_gain[1])
    v_slc = kv[:, :, 1, 1]
    ks_blk = k_slc.reshape(B, n_sel, SEL_BLOCK, G, Dh).transpose(0, 3, 1, 2, 4)
    vs_blk = v_slc.reshape(B, n_sel, SEL_BLOCK, G, Dh).transpose(0, 3, 1, 2, 4)
    k_win = rmsnorm(kv[:, :, 2, 0], k_gain[2])
    v_win = kv[:, :, 2, 1]
    kw_pad = jnp.pad(k_win, ((0, 0), (WINDOW, 0), (0, 0), (0, 0)))
    vw_pad = jnp.pad(v_win, ((0, 0), (WINDOW, 0), (0, 0), (0, 0)))
    b_ix = jnp.arange(B)[:, None, None, None]
    g_ix = jnp.arange(G)[None, :, None, None]
    n_sel_tok = k_top * SEL_BLOCK

    def query_block(c):
        t0 = c * Q_BLOCK
        qc = lax.dynamic_slice_in_dim(q, t0, Q_BLOCK, axis=1)
        qpos = t0 + jnp.arange(Q_BLOCK)
        idx = lax.dynamic_slice_in_dim(sel_idx, t0, Q_BLOCK, axis=2)
        k_sel = ks_blk[b_ix, g_ix, idx].reshape(B, G, Q_BLOCK, n_sel_tok, Dh)
        v_sel = vs_blk[b_ix, g_ix, idx].reshape(B, G, Q_BLOCK, n_sel_tok, Dh)
        kpos = (idx[..., None] * SEL_BLOCK + jnp.arange(SEL_BLOCK)).reshape(B, G, Q_BLOCK, n_sel_tok)
        s_s = jnp.einsum('bqghd,bgqnd->bghqn', qc, k_sel) * scale
        p_s = masked_softmax(s_s, (kpos <= qpos[:, None])[:, :, None])
        o_sel = jnp.einsum('bghqn,bgqnd->bqghd', p_s.astype(v_sel.dtype), v_sel)
        kw = lax.dynamic_slice_in_dim(kw_pad, t0, WINDOW + Q_BLOCK, axis=1)
        vw = lax.dynamic_slice_in_dim(vw_pad, t0, WINDOW + Q_BLOCK, axis=1)
        wpos = t0 - WINDOW + jnp.arange(WINDOW + Q_BLOCK)
        wmask = (wpos[None, :] <= qpos[:, None]) & (wpos[None, :] > qpos[:, None] - WINDOW) & (wpos[None, :] >= 0)
        s_w = jnp.einsum('bqghd,bkgd->bghqk', qc, kw) * scale
        p_w = masked_softmax(s_w, wmask)
        o_win = jnp.einsum('bghqk,bkgd->bqghd', p_w.astype(vw.dtype), vw)
        return o_sel, o_win

    o_sel, o_win = lax.map(query_block, jnp.arange(S // Q_BLOCK))
    o_sel = jnp.moveaxis(o_sel, 0, 1).reshape(B, S, G, Hg, Dh)
    o_win = jnp.moveaxis(o_win, 0, 1).reshape(B, S, G, Hg, Dh)

    gates = jax.nn.sigmoid(gate_logits).reshape(B, S, G, Hg, 3)
    o = gates[..., 0:1] * o_cmp + gates[..., 1:2] * o_sel + gates[..., 2:3] * o_win
    return rmsnorm(o, out_gain).reshape(B, S, D_NSA)


def causal_depthwise_conv(x, w):
    K, S = w.shape[0], x.shape[1]
    xp = jnp.pad(x, ((0, 0), (K - 1, 0), (0, 0)))
    return sum(xp[:, j:j + S] * w[j] for j in range(K))


def chunk_gated_delta_rule(q, k, v, g, beta):
    B, H, S, Dk = q.shape
    Dv = v.shape[-1]
    C = GDN_CHUNK
    N = S // C
    q = q.reshape(B, H, N, C, Dk) * Dk ** -0.5
    k = k.reshape(B, H, N, C, Dk)
    v = v.reshape(B, H, N, C, Dv)
    g = g.reshape(B, H, N, C)
    beta = beta.reshape(B, H, N, C)
    gc = jnp.cumsum(g, axis=-1)
    tri = jnp.tril(jnp.ones((C, C), bool))
    strict = jnp.tril(jnp.ones((C, C), bool), -1)
    decay = jnp.exp(jnp.where(tri, gc[..., :, None] - gc[..., None, :], -jnp.inf))
    kb = k * beta[..., None]
    A = jnp.where(strict, jnp.einsum('bhnid,bhnjd->bhnij', kb, k) * decay, 0.0)
    eye = jnp.eye(C, dtype=q.dtype)
    T = lax.linalg.triangular_solve(A + eye, jnp.broadcast_to(eye, A.shape), left_side=True, lower=True, unit_diagonal=True)
    u = T @ (v * beta[..., None])
    w = T @ (kb * jnp.exp(gc)[..., None])
    attn = jnp.einsum('bhnid,bhnjd->bhnij', q, k) * decay
    q_dec = q * jnp.exp(gc)[..., None]
    g_last = gc[..., -1]
    k_dec = k * jnp.exp(g_last[..., None] - gc)[..., None]

    def step(state, inp):
        u_n, w_n, q_n, k_n, attn_n, gl_n = inp
        v_new = u_n - jnp.einsum('bhcd,bhde->bhce', w_n, state)
        o = jnp.einsum('bhcd,bhde->bhce', q_n, state) + jnp.einsum('bhij,bhje->bhie', attn_n, v_new)
        state = state * jnp.exp(gl_n)[..., None, None] + jnp.einsum('bhcd,bhce->bhde', k_n, v_new)
        return state, o

    xs = tuple(jnp.moveaxis(t, 2, 0) for t in (u, w, q_dec, k_dec, attn, g_last))
    _, o = lax.scan(step, jnp.zeros((B, H, Dk, Dv), q.dtype), xs)
    return jnp.moveaxis(o, 0, 2).reshape(B, H, S, Dv)


def gdn_mixer(qkv, beta_logit, a, z, conv_w, A_log, dt_bias, out_gain):
    B, S = qkv.shape[0], qkv.shape[1]
    H, Dh = GDN_HEADS, HEAD_DIM
    f32 = jnp.float32
    qkv = jax.nn.silu(causal_depthwise_conv(qkv, conv_w)).reshape(B, S, 3, H, Dh).astype(f32)
    q = l2norm(qkv[:, :, 0])
    k = l2norm(qkv[:, :, 1])
    v = qkv[:, :, 2]
    beta = jax.nn.sigmoid(beta_logit.astype(f32))
    g = -jnp.exp(A_log.astype(f32)) * jax.nn.softplus(a.astype(f32) + dt_bias.astype(f32))
    o = chunk_gated_delta_rule(q.transpose(0, 2, 1, 3), k.transpose(0, 2, 1, 3), v.transpose(0, 2, 1, 3),
                               g.transpose(0, 2, 1), beta.transpose(0, 2, 1)).transpose(0, 2, 1, 3)
    o = rmsnorm(o, out_gain) * jax.nn.silu(z.reshape(B, S, H, Dh).astype(f32))
    return o.reshape(B, S, D_GDN).astype(z.dtype)


def memory_cross_attention(h, mem_n, wq, wkv, q_gain, k_gain, wo):
    B, S = h.shape[0], h.shape[1]
    M = mem_n.shape[1]
    q = rmsnorm((h @ wq).reshape(B, S, XA_HEADS, HEAD_DIM), q_gain)
    kv = (mem_n @ wkv).reshape(B, M, 2, XA_HEADS, HEAD_DIM)
    k = rmsnorm(kv[:, :, 0], k_gain)
    v = kv[:, :, 1]
    s = jnp.einsum('bshd,bmhd->bhsm', q, k).astype(jnp.float32) * HEAD_DIM ** -0.5
    p = jax.nn.softmax(s, axis=-1).astype(v.dtype)
    return jnp.einsum('bhsm,bmhd->bshd', p, v).reshape(B, S, D_XA) @ wo


def swiglu(h, w13, w2):
    gate, up = jnp.split(h @ w13, 2, axis=-1)
    return (jax.nn.silu(gate) * up) @ w2


def moe_ffn(h, router_w, w13, w2):
    logits = (h @ router_w).astype(jnp.float32)
    top_v, top_i = lax.top_k(logits, TOP_K)
    wts = jax.nn.softmax(top_v, axis=-1)
    comb = jnp.sum(jax.nn.one_hot(top_i, N_EXPERTS, dtype=jnp.float32) * wts[..., None], axis=-2)
    y = jnp.zeros_like(h)
    for e in range(N_EXPERTS):
        y = y + comb[..., e:e + 1].astype(h.dtype) * swiglu(h, w13[e], w2[e])
    return y


def setup_inputs(seed: int = 0) -> dict:
    key = jax.random.key(seed)
    keys = jax.random.split(key, 32)
    counter = [0]
    f32 = jnp.float32

    def nxt():
        counter[0] += 1
        return keys[counter[0] - 1]

    def nrm(shape, fan_in):
        return jax.random.normal(nxt(), shape, f32) * fan_in ** -0.5

    def gain(shape):
        return 1.0 + 0.02 * jax.random.normal(nxt(), shape, f32)

    n_dense = (DEPTH + 1) // 2
    n_moe = DEPTH // 2
    x = jax.random.normal(nxt(), (BATCH, SEQ, D_MODEL), f32)
    mem = jax.random.normal(nxt(), (BATCH, MEM_LEN, D_MODEL), f32)
    attn_norm = gain((DEPTH, D_MODEL))
    w_in = nrm((DEPTH, D_MODEL, N_IN), D_MODEL)
    nsa_q_gain = gain((DEPTH, HEAD_DIM))
    nsa_k_gain = gain((DEPTH, 3, HEAD_DIM))
    nsa_cmp_pos = 0.02 * jax.random.normal(nxt(), (DEPTH, 2, CMP_BLOCK, HEAD_DIM), f32)
    nsa_cmp_w1 = nrm((DEPTH, 2, CMP_BLOCK * HEAD_DIM, HEAD_DIM), CMP_BLOCK * HEAD_DIM)
    nsa_cmp_w2 = nrm((DEPTH, 2, HEAD_DIM, HEAD_DIM), HEAD_DIM)
    nsa_out_gain = gain((DEPTH, HEAD_DIM))
    gdn_conv_w = nrm((DEPTH, GDN_CONV, 3 * D_GDN), GDN_CONV)
    gdn_A_log = jnp.log(jax.random.uniform(nxt(), (DEPTH, GDN_HEADS), f32, 1.0, 16.0))
    dt = jnp.exp(jax.random.uniform(nxt(), (DEPTH, GDN_HEADS), f32, math.log(1e-3), math.log(1e-1)))
    gdn_dt_bias = dt + jnp.log(-jnp.expm1(-dt))
    gdn_out_gain = gain((DEPTH, HEAD_DIM))
    w_out = nrm((DEPTH, D_MIX, D_MODEL), D_MIX)
    xa_norm = gain((DEPTH, D_MODEL))
    mem_norm = gain((DEPTH, D_MODEL))
    xa_wq = nrm((DEPTH, D_MODEL, D_XA), D_MODEL)
    xa_wkv = nrm((DEPTH, D_MODEL, 2 * D_XA), D_MODEL)
    xa_q_gain = gain((DEPTH, HEAD_DIM))
    xa_k_gain = gain((DEPTH, HEAD_DIM))
    xa_wo = nrm((DEPTH, D_XA, D_MODEL), D_XA)
    ffn_norm = gain((DEPTH, D_MODEL))
    dense_w13 = nrm((n_dense, D_MODEL, 2 * D_FF_DENSE), D_MODEL)
    dense_w2 = nrm((n_dense, D_FF_DENSE, D_MODEL), D_FF_DENSE)
    router_w = nrm((n_moe, D_MODEL, N_EXPERTS), D_MODEL)
    moe_w13 = nrm((n_moe, N_EXPERTS, D_MODEL, 2 * D_FF_EXPERT), D_MODEL)
    moe_w2 = nrm((n_moe, N_EXPERTS, D_FF_EXPERT, D_MODEL), D_FF_EXPERT)
    return {'x': x, 'mem': mem, 'attn_norm': attn_norm, 'w_in': w_in,
            'nsa_q_gain': nsa_q_gain, 'nsa_k_gain': nsa_k_gain, 'nsa_cmp_pos': nsa_cmp_pos,
            'nsa_cmp_w1': nsa_cmp_w1, 'nsa_cmp_w2': nsa_cmp_w2, 'nsa_out_gain': nsa_out_gain,
            'gdn_conv_w': gdn_conv_w, 'gdn_A_log': gdn_A_log, 'gdn_dt_bias': gdn_dt_bias,
            'gdn_out_gain': gdn_out_gain, 'w_out': w_out, 'xa_norm': xa_norm, 'mem_norm': mem_norm,
            'xa_wq': xa_wq, 'xa_wkv': xa_wkv, 'xa_q_gain': xa_q_gain, 'xa_k_gain': xa_k_gain,
            'xa_wo': xa_wo, 'ffn_norm': ffn_norm, 'dense_w13': dense_w13, 'dense_w2': dense_w2,
            'router_w': router_w, 'moe_w13': moe_w13, 'moe_w2': moe_w2}


def reference(x, mem, attn_norm, w_in, nsa_q_gain, nsa_k_gain, nsa_cmp_pos, nsa_cmp_w1, nsa_cmp_w2,
              nsa_out_gain, gdn_conv_w, gdn_A_log, gdn_dt_bias, gdn_out_gain, w_out, xa_norm, mem_norm,
              xa_wq, xa_wkv, xa_q_gain, xa_k_gain, xa_wo, ffn_norm, dense_w13, dense_w2,
              router_w, moe_w13, moe_w2):
    offsets = np.cumsum(IN_SPLITS)[:-1].tolist()
    for l in range(DEPTH):
        h = rmsnorm(x, attn_norm[l])
        q_a, kv_a, gate_a, qkv_b, beta_b, a_b, z_b = jnp.split(h @ w_in[l], offsets, axis=-1)
        y_a = nsa_mixer(q_a, kv_a, gate_a, nsa_q_gain[l], nsa_k_gain[l], nsa_cmp_pos[l],
                        nsa_cmp_w1[l], nsa_cmp_w2[l], nsa_out_gain[l])
        y_b = gdn_mixer(qkv_b, beta_b, a_b, z_b, gdn_conv_w[l], gdn_A_log[l], gdn_dt_bias[l], gdn_out_gain[l])
        x = x + jnp.concatenate([y_a, y_b], axis=-1) @ w_out[l]
        x = x + memory_cross_attention(rmsnorm(x, xa_norm[l]), rmsnorm(mem, mem_norm[l]), xa_wq[l],
                                       xa_wkv[l], xa_q_gain[l], xa_k_gain[l], xa_wo[l])
        h = rmsnorm(x, ffn_norm[l])
        if l % 2 == 0:
            x = x + swiglu(h, dense_w13[l // 2], dense_w2[l // 2])
        else:
            x = x + moe_ffn(h, router_w[l // 2], moe_w13[l // 2], moe_w2[l // 2])
    return x
```

```python
import functools

import jax
import jax.numpy as jnp
import numpy as np
from jax import lax
from jax.experimental import pallas as pl
from jax.experimental.pallas import tpu as pltpu

F32 = jnp.float32
BF16 = jnp.bfloat16
I32 = jnp.int32

D_MODEL = 4096
HEAD_DIM = 128
EPS = 1e-6
SCALE = HEAD_DIM ** -0.5

NSA_HEADS = 16
NSA_KV_GROUPS = 2
NSA_GROUP_SIZE = NSA_HEADS // NSA_KV_GROUPS
CMP_BLOCK = 32
CMP_STRIDE = 16
SEL_BLOCK = 64
SEL_TOP_N = 16
SEL_LOCAL = 2
WINDOW = 512
Q_BLOCK = 128
FORCE = 1e9
NEG = -1e30
SEL_TILE = 256

GDN_HEADS = 16
GDN_CONV = 4
GDN_CHUNK = 64
GDN_PACK = 4
GDN_SUB = 16

D_NSA = NSA_HEADS * HEAD_DIM
D_KV_NSA = NSA_KV_GROUPS * HEAD_DIM
D_GDN = GDN_HEADS * HEAD_DIM
D_MIX = D_NSA + D_GDN

XA_HEADS = 4
D_XA = XA_HEADS * HEAD_DIM
N_EXPERTS = 8
TOP_K = 2

COL_QKV_B = 0
COL_Q_A = COL_QKV_B + 3 * D_GDN
COL_Z_B = COL_Q_A + D_NSA
COL_KV_A = COL_Z_B + D_GDN
COL_SMALL = COL_KV_A + 6 * D_KV_NSA
N_IN_R = COL_SMALL + 256
SM_BETA = 32
SM_A = 48

VMEM_LIMIT = 58 * 1024 * 1024

NT_DIMS = (((1,), (1,)), ((), ()))


def _rms(x, gain):
    return x * lax.rsqrt(jnp.mean(x * x, axis=-1, keepdims=True) + EPS) * gain


def _dot(a, b):
    return jnp.dot(a, b, preferred_element_type=F32)


def _dot_nt(a, b):
    return lax.dot_general(a, b, NT_DIMS, preferred_element_type=F32)


def _params(sem, vmem=VMEM_LIMIT):
    return pltpu.CompilerParams(dimension_semantics=sem, vmem_limit_bytes=vmem)


def _norm_kernel(x_ref, g_ref, o_ref):
    o_ref[...] = _rms(x_ref[...], g_ref[...]).astype(o_ref.dtype)


def rmsnorm_bf16(x, gain, tm=512):
    M, D = x.shape
    return pl.pallas_call(
        _norm_kernel,
        grid=(M // tm,),
        in_specs=[pl.BlockSpec((tm, D), lambda i: (i, 0)), pl.BlockSpec((1, D), lambda i: (0, 0))],
        out_specs=pl.BlockSpec((tm, D), lambda i: (i, 0)),
        out_shape=jax.ShapeDtypeStruct((M, D), BF16),
        compiler_params=_params(("parallel",)),
        name="rmsnorm",
    )(x, gain.reshape(1, D))


def _norm_router_kernel(x_ref, g_ref, rw_ref, o_ref, idx_ref, wt_ref):
    h = _rms(x_ref[...], g_ref[...])
    o_ref[...] = h.astype(o_ref.dtype)
    logits = jnp.dot(h, rw_ref[...], preferred_element_type=F32, precision=lax.Precision.HIGHEST)
    lane = lax.broadcasted_iota(I32, logits.shape, 1).astype(F32)
    logits = jnp.where(lane < N_EXPERTS, logits, -jnp.inf)
    m1 = jnp.max(logits, axis=-1, keepdims=True)
    i1 = jnp.min(jnp.where(logits == m1, lane, 128.0), axis=-1, keepdims=True)
    rest = jnp.where(lane == i1, -jnp.inf, logits)
    m2 = jnp.max(rest, axis=-1, keepdims=True)
    i2 = jnp.min(jnp.where(rest == m2, lane, 128.0), axis=-1, keepdims=True)
    e2 = jnp.exp(m2 - m1)
    den = 1.0 + e2
    idx_ref[...] = jnp.where(lane == 0, i1, jnp.where(lane == 1, i2, 0.0)).astype(I32)
    wt_ref[...] = jnp.where(lane == 0, 1.0 / den, jnp.where(lane == 1, e2 / den, 0.0))


def rmsnorm_router(x, gain, router_w, tm=256):
    M, D = x.shape
    rw = jnp.pad(router_w, ((0, 0), (0, 128 - N_EXPERTS)))
    row = lambda i: (i, 0)
    return pl.pallas_call(
        _norm_router_kernel,
        grid=(M // tm,),
        in_specs=[pl.BlockSpec((tm, D), row), pl.BlockSpec((1, D), lambda i: (0, 0)),
                  pl.BlockSpec((D, 128), lambda i: (0, 0))],
        out_specs=[pl.BlockSpec((tm, D), row), pl.BlockSpec((tm, 128), row), pl.BlockSpec((tm, 128), row)],
        out_shape=[jax.ShapeDtypeStruct((M, D), BF16), jax.ShapeDtypeStruct((M, 128), I32),
                   jax.ShapeDtypeStruct((M, 128), F32)],
        compiler_params=_params(("parallel",)),
        name="rmsnorm_router",
    )(x, gain.reshape(1, D), rw)


def _mm_kernel(x_ref, w_ref, o_ref):
    o_ref[...] = _dot(x_ref[...], w_ref[...].astype(BF16)).astype(o_ref.dtype)


def _mm_res_kernel(x_ref, w_ref, r_ref, o_ref):
    o_ref[...] = r_ref[...] + _dot(x_ref[...], w_ref[...].astype(BF16))


def matmul_cols(x, w, *, tm, tn, out_dtype=F32, residual=None, name="matmul"):
    M, K = x.shape
    N = w.shape[1]
    in_specs = [pl.BlockSpec((tm, K), lambda i, j: (i, 0)), pl.BlockSpec((K, tn), lambda i, j: (0, j))]
    args = [x, w]
    kern = _mm_kernel
    if residual is not None:
        in_specs.append(pl.BlockSpec((tm, tn), lambda i, j: (i, j)))
        args.append(residual)
        kern = _mm_res_kernel
    return pl.pallas_call(
        kern,
        grid=(M // tm, N // tn),
        in_specs=in_specs,
        out_specs=pl.BlockSpec((tm, tn), lambda i, j: (i, j)),
        out_shape=jax.ShapeDtypeStruct((M, N), out_dtype),
        compiler_params=_params(("parallel", "parallel")),
        name=name,
    )(*args)


def _swiglu_up_kernel(x_ref, wg_ref, wu_ref, o_ref):
    x = x_ref[...]
    gate = _dot(x, wg_ref[...].astype(BF16))
    up = _dot(x, wu_ref[...].astype(BF16))
    o_ref[...] = (gate * jax.nn.sigmoid(gate) * up).astype(o_ref.dtype)


def swiglu_up(x, w13, *, tm, tn):
    M, K = x.shape
    F = w13.shape[1] // 2
    nj = F // tn
    return pl.pallas_call(
        _swiglu_up_kernel,
        grid=(M // tm, nj),
        in_specs=[pl.BlockSpec((tm, K), lambda i, j: (i, 0)),
                  pl.BlockSpec((K, tn), lambda i, j: (0, j)),
                  pl.BlockSpec((K, tn), lambda i, j: (0, j + nj))],
        out_specs=pl.BlockSpec((tm, tn), lambda i, j: (i, j)),
        out_shape=jax.ShapeDtypeStruct((M, F), BF16),
        compiler_params=_params(("parallel", "parallel")),
        name="swiglu_up",
    )(x, w13, w13)


def _down_res_kernel(a_ref, w_ref, r_ref, o_ref):
    @pl.when(pl.program_id(2) == 0)
    def _():
        o_ref[...] = r_ref[...]

    o_ref[...] += _dot(a_ref[...], w_ref[...].astype(BF16))


def down_proj_residual(a, w2, residual, *, tm, tn, tk):
    M, F = a.shape
    N = w2.shape[1]
    return pl.pallas_call(
        _down_res_kernel,
        grid=(M // tm, N // tn, F // tk),
        in_specs=[pl.BlockSpec((tm, tk), lambda i, j, k: (i, k)),
                  pl.BlockSpec((tk, tn), lambda i, j, k: (k, j)),
                  pl.BlockSpec((tm, tn), lambda i, j, k: (i, j))],
        out_specs=pl.BlockSpec((tm, tn), lambda i, j, k: (i, j)),
        out_shape=jax.ShapeDtypeStruct((M, N), F32),
        compiler_params=_params(("parallel", "parallel", "arbitrary")),
        name="down_proj",
    )(a, w2, residual)


def _moe_up_kernel(te_ref, nv_ref, x_ref, wg_ref, wu_ref, o_ref):
    valid = pl.program_id(0) < nv_ref[0]

    @pl.when(valid)
    def _():
        x = x_ref[...]
        gate = _dot(x, wg_ref[...].astype(BF16))
        up = _dot(x, wu_ref[...].astype(BF16))
        o_ref[...] = (gate * jax.nn.sigmoid(gate) * up).astype(o_ref.dtype)

    @pl.when(jnp.logical_not(valid))
    def _():
        o_ref[...] = jnp.zeros_like(o_ref)


def moe_up(xs, w13, tile_expert, n_valid, *, tm, tn):
    P, K = xs.shape
    F = w13.shape[2] // 2
    nj = F // tn

    def wmap(off):
        def f(i, j, te, nv):
            return (te[i], 0, jnp.where(i < nv[0], j, nj - 1) + off)
        return f

    return pl.pallas_call(
        _moe_up_kernel,
        grid_spec=pltpu.PrefetchScalarGridSpec(
            num_scalar_prefetch=2,
            grid=(P // tm, nj),
            in_specs=[pl.BlockSpec((tm, K), lambda i, j, te, nv: (jnp.minimum(i, nv[0] - 1), 0)),
                      pl.BlockSpec((None, K, tn), wmap(0)),
                      pl.BlockSpec((None, K, tn), wmap(nj))],
            out_specs=pl.BlockSpec((tm, tn), lambda i, j, te, nv: (i, j)),
        ),
        out_shape=jax.ShapeDtypeStruct((P, F), BF16),
        compiler_params=_params(("arbitrary", "arbitrary")),
        name="moe_up",
    )(tile_expert, n_valid, xs, w13, w13)


def _moe_down_kernel(te_ref, nv_ref, a_ref, w_ref, rw_ref, o_ref):
    valid = pl.program_id(0) < nv_ref[0]
    k = pl.program_id(2)
    nk = pl.num_programs(2)

    @pl.when(k == 0)
    def _():
        o_ref[...] = jnp.zeros_like(o_ref)

    @pl.when(valid)
    def _():
        o_ref[...] += _dot(a_ref[...], w_ref[...].astype(BF16))

    @pl.when(jnp.logical_and(valid, k == nk - 1))
    def _():
        o_ref[...] = o_ref[...] * rw_ref[:, 0:1]


def moe_down(act, w2, row_w, tile_expert, n_valid, *, tm, tn, tk):
    P, F = act.shape
    N = w2.shape[2]
    nk = F // tk

    def frozen_k(i, k, nv):
        return jnp.where(i < nv[0], k, nk - 1)

    return pl.pallas_call(
        _moe_down_kernel,
        grid_spec=pltpu.PrefetchScalarGridSpec(
            num_scalar_prefetch=2,
            grid=(P // tm, N // tn, nk),
            in_specs=[pl.BlockSpec((tm, tk), lambda i, j, k, te, nv: (jnp.minimum(i, nv[0] - 1), frozen_k(i, k, nv))),
                      pl.BlockSpec((None, tk, tn), lambda i, j, k, te, nv: (te[i], frozen_k(i, k, nv), j)),
                      pl.BlockSpec((tm, 128), lambda i, j, k, te, nv: (i, 0))],
            out_specs=pl.BlockSpec((tm, tn), lambda i, j, k, te, nv: (i, j)),
        ),
        out_shape=jax.ShapeDtypeStruct((P, N), F32),
        compiler_params=_params(("arbitrary", "arbitrary", "arbitrary")),
        name="moe_down",
    )(tile_expert, n_valid, act, w2, row_w)


def moe_dispatch(top_i, top_w, tm):
    M = top_i.shape[0]
    A = M * TOP_K
    P = A + N_EXPERTS * tm
    e_flat = top_i.reshape(A)
    order = jnp.argsort(e_flat, stable=True).astype(I32)
    e_sorted = e_flat[order]
    counts = jnp.sum((e_flat[:, None] == jnp.arange(N_EXPERTS, dtype=I32)[None, :]).astype(I32), axis=0)
    padded = ((counts + tm - 1) // tm) * tm
    pend = jnp.cumsum(padded)
    pstart = pend - padded
    gstart = jnp.cumsum(counts) - counts
    dest = pstart[e_sorted] + jnp.arange(A, dtype=I32) - gstart[e_sorted]
    src_tok = jnp.zeros((P,), I32).at[dest].set(order // TOP_K)
    row_w = jnp.zeros((P,), F32).at[dest].set(top_w.reshape(A)[order])
    pos = jnp.zeros((A,), I32).at[order].set(dest).reshape(M, TOP_K)
    tile_start = jnp.arange(P // tm, dtype=I32) * tm
    tile_expert = jnp.minimum(jnp.searchsorted(pend, tile_start, side="right"), N_EXPERTS - 1).astype(I32)
    n_valid = (pend[-1:] // tm).astype(I32)
    last_e = tile_expert[jnp.maximum(n_valid[0] - 1, 0)]
    tile_expert = jnp.where(tile_start // tm < n_valid[0], tile_expert, last_e)
    return src_tok, row_w, pos, tile_expert, n_valid


def _nsa_prep_kernel(kc_ref, vc_ref, ks_ref, vs_ref, kw_ref, vw_ref, kg_ref, raw_ref, kso_ref, vso_ref, kwo_ref, vwo_ref):
    raw_ref[0] = kc_ref[...]
    raw_ref[1] = vc_ref[...]
    kg = kg_ref[...]
    kso_ref[...] = _rms(ks_ref[...], kg[1:2]).astype(BF16)
    vso_ref[...] = vs_ref[...].astype(BF16)
    kwo_ref[...] = _rms(kw_ref[...], kg[2:3]).astype(BF16)
    vwo_ref[...] = vw_ref[...].astype(BF16)


def nsa_prep(z, k_gain, B, S, ts=512):
    G = NSA_KV_GROUPS
    nt = S // ts
    cb = COL_KV_A // HEAD_DIM

    def col(br, kvi):
        return pl.BlockSpec((ts, HEAD_DIM), lambda b, g, t: (b * nt + t, cb + br * 4 + kvi * 2 + g))

    kv_out = pl.BlockSpec((None, None, ts, HEAD_DIM), lambda b, g, t: (b, g, t, 0))
    kv_shape = jax.ShapeDtypeStruct((B, G, S, HEAD_DIM), BF16)
    return pl.pallas_call(
        _nsa_prep_kernel,
        grid=(B, G, nt),
        in_specs=[col(0, 0), col(0, 1), col(1, 0), col(1, 1), col(2, 0), col(2, 1),
                  pl.BlockSpec((3, HEAD_DIM), lambda b, g, t: (0, 0))],
        out_specs=[pl.BlockSpec((None, 2, None, ts, HEAD_DIM), lambda b, g, t: (b, 0, g, t, 0)),
                   kv_out, kv_out, kv_out, kv_out],
        out_shape=[jax.ShapeDtypeStruct((B, 2, G, S, HEAD_DIM), F32), kv_shape, kv_shape, kv_shape, kv_shape],
        compiler_params=_params(("parallel", "parallel", "parallel")),
        name="nsa_prep",
    )(z, z, z, z, z, z, k_gain)


def _nsa_compress_kernel(r_ref, pos_ref, w1_ref, w2_ref, kg_ref, o_ref):
    kv = pl.program_id(1)
    half = (CMP_BLOCK // 2) * HEAD_DIM
    r = r_ref[...]
    pos = pos_ref[...]
    n = r.shape[0]
    first = _dot((r + pos[:, :half]).astype(BF16), w1_ref[:half, :].astype(BF16))
    second = _dot((r + pos[:, half:]).astype(BF16), w1_ref[half:, :].astype(BF16))
    pre = first + pltpu.roll(second, n - 1, axis=0)
    hid = jax.nn.gelu(pre)
    comp = _dot(hid.astype(BF16), w2_ref[...].astype(BF16))
    o_ref[...] = jnp.where(kv == 0, _rms(comp, kg_ref[...]), comp).astype(o_ref.dtype)


def nsa_compress(raw, cmp_pos, cmp_w1, cmp_w2, k_gain0, B, S):
    G = NSA_KV_GROUPS
    n = S // CMP_STRIDE
    feat = CMP_STRIDE * HEAD_DIM
    r = raw.reshape(B, 2, G, n, feat)
    pos = cmp_pos.reshape(2, 1, CMP_BLOCK * HEAD_DIM)
    return pl.pallas_call(
        _nsa_compress_kernel,
        grid=(B, 2, G),
        in_specs=[pl.BlockSpec((None, None, None, n, feat), lambda b, kv, g: (b, kv, g, 0, 0)),
                  pl.BlockSpec((None, 1, CMP_BLOCK * HEAD_DIM), lambda b, kv, g: (kv, 0, 0)),
                  pl.BlockSpec((None, CMP_BLOCK * HEAD_DIM, HEAD_DIM), lambda b, kv, g: (kv, 0, 0)),
                  pl.BlockSpec((None, HEAD_DIM, HEAD_DIM), lambda b, kv, g: (kv, 0, 0)),
                  pl.BlockSpec((1, HEAD_DIM), lambda b, kv, g: (0, 0))],
        out_specs=pl.BlockSpec((None, None, None, n, HEAD_DIM), lambda b, kv, g: (b, kv, g, 0, 0)),
        out_shape=jax.ShapeDtypeStruct((B, 2, G, n, HEAD_DIM), BF16),
        compiler_params=_params(("parallel", "parallel", "parallel")),
        name="nsa_compress",
    )(r, pos, cmp_w1, cmp_w2, k_gain0.reshape(1, HEAD_DIM))


def _rep(x, n):
    return x if n == HEAD_DIM else jnp.concatenate([x] * (n // HEAD_DIM), axis=1)


def _flash_step(qb, kt, vt, bias, m_ref, l_ref, acc_ref):
    nk = kt.shape[0]
    rows = qb.shape[0]
    s = _dot_nt(qb, kt)
    s = (s.reshape(rows // Q_BLOCK, Q_BLOCK, nk) + bias[None]).reshape(rows, nk)
    m_prev = m_ref[...]
    m_new = jnp.maximum(m_prev, jnp.max(s, axis=-1, keepdims=True))
    alpha = jnp.exp(m_prev - m_new)
    p = jnp.exp(s - _rep(m_new, nk))
    l_ref[...] = alpha * l_ref[...] + jnp.sum(p, axis=-1, keepdims=True)
    acc_ref[...] = alpha * acc_ref[...] + _dot(p.astype(BF16), vt)
    m_ref[...] = m_new


def _nsa_kernel(q_ref, sm_ref, kc_ref, vc_ref, ks_ref, vs_ref, kw_ref, vw_ref, qg_ref, og_ref, c2s_ref, exp_ref,
                o_ref, selb_ref, m_ref, l_ref, acc_ref, *, k_top):
    Hg = NSA_GROUP_SIZE
    rows = Hg * Q_BLOCK
    c = pl.program_id(2)
    t0 = c * Q_BLOCK
    n_cmp = kc_ref.shape[0]
    n_sel = c2s_ref.shape[1]

    q = q_ref[...]
    qs = jnp.concatenate([q[:, h * HEAD_DIM:(h + 1) * HEAD_DIM] for h in range(Hg)], axis=0)
    qb = (_rms(qs, qg_ref[...]) * SCALE).astype(BF16)

    tq = t0 + lax.broadcasted_iota(I32, (Q_BLOCK, n_cmp), 0)
    c_end = lax.broadcasted_iota(I32, (Q_BLOCK, n_cmp), 1) * CMP_STRIDE + (CMP_BLOCK - 1)
    sc = _dot_nt(qb, kc_ref[...]).reshape(Hg, Q_BLOCK, n_cmp)
    sc = jnp.where((c_end <= tq)[None], sc, -jnp.inf)
    mc = jnp.max(sc, axis=-1, keepdims=True)
    mc = jnp.where(mc > -jnp.inf, mc, 0.0)
    pc = jnp.exp(sc - mc)
    dc = jnp.sum(pc, axis=-1, keepdims=True)
    pc = pc / jnp.where(dc > 0, dc, 1.0)
    o_cmp = _dot(pc.reshape(rows, n_cmp).astype(BF16), vc_ref[...])

    psum = jnp.sum(pc, axis=0)
    c2s = c2s_ref[...]
    p_hi = psum.astype(BF16)
    rem = psum - p_hi.astype(F32)
    p_mid = rem.astype(BF16)
    p_lo = (rem - p_mid.astype(F32)).astype(BF16)
    imp = _dot(p_hi, c2s) + _dot(p_mid, c2s) + _dot(p_lo, c2s)

    jj = lax.broadcasted_iota(I32, (Q_BLOCK, n_sel), 1)
    cur = lax.shift_right_logical(t0 + lax.broadcasted_iota(I32, (Q_BLOCK, n_sel), 0), 6)
    valid = jj <= cur
    forced = valid & ((jj == 0) | (jj > cur - SEL_LOCAL))
    score = jnp.where(forced, FORCE, jnp.where(valid, imp, -FORCE))
    rank = jnp.zeros((Q_BLOCK, n_sel), F32)
    for j2 in range(n_sel):
        col = score[:, j2:j2 + 1]
        tie = jnp.where(jj > j2, 1.0, 0.0)
        rank = rank + jnp.where(col > score, 1.0, jnp.where(col == score, tie, 0.0))
    sel = jnp.where(rank < k_top, 1.0, 0.0).astype(BF16)
    sel_keys = _dot(sel, exp_ref[...])
    for i in range(selb_ref.shape[0]):
        selb_ref[i] = (sel_keys[:, i * SEL_TILE:(i + 1) * SEL_TILE] - 1.0) * (-NEG)

    def reset():
        m_ref[...] = jnp.full(m_ref.shape, NEG, F32)
        l_ref[...] = jnp.zeros(l_ref.shape, F32)
        acc_ref[...] = jnp.zeros(acc_ref.shape, F32)

    reset()
    tq_s = t0 + lax.broadcasted_iota(I32, (Q_BLOCK, SEL_TILE), 0)
    kk_s = lax.broadcasted_iota(I32, (Q_BLOCK, SEL_TILE), 1)

    def sel_body(i, carry):
        k0 = pl.multiple_of(i * SEL_TILE, SEL_TILE)
        bias = selb_ref[i] + jnp.where(kk_s + k0 <= tq_s, 0.0, NEG)
        _flash_step(qb, ks_ref[pl.ds(k0, SEL_TILE), :], vs_ref[pl.ds(k0, SEL_TILE), :], bias, m_ref, l_ref, acc_ref)
        return carry

    lax.fori_loop(0, (t0 + Q_BLOCK + SEL_TILE - 1) // SEL_TILE, sel_body, 0)
    o_sel = acc_ref[...] / l_ref[...]

    reset()
    tq_w = t0 + lax.broadcasted_iota(I32, (Q_BLOCK, Q_BLOCK), 0)
    kk_w = lax.broadcasted_iota(I32, (Q_BLOCK, Q_BLOCK), 1)

    def win_body(i, carry):
        k0 = pl.multiple_of((c - i) * Q_BLOCK, Q_BLOCK)
        wpos = kk_w + k0
        ok = (wpos <= tq_w) & (wpos > tq_w - WINDOW)
        bias = jnp.where(ok, 0.0, NEG)
        _flash_step(qb, kw_ref[pl.ds(k0, Q_BLOCK), :], vw_ref[pl.ds(k0, Q_BLOCK), :], bias, m_ref, l_ref, acc_ref)
        return carry

    lax.fori_loop(0, jnp.minimum(c, WINDOW // Q_BLOCK) + 1, win_body, 0)
    o_win = acc_ref[...] / l_ref[...]

    gates = jax.nn.sigmoid(sm_ref[...])
    og = og_ref[...]
    for h in range(Hg):
        r = slice(h * Q_BLOCK, (h + 1) * Q_BLOCK)
        o = (gates[:, h:h + 1] * o_cmp[r] + gates[:, Hg + h:Hg + h + 1] * o_sel[r]
             + gates[:, 2 * Hg + h:2 * Hg + h + 1] * o_win[r])
        o_ref[:, h * HEAD_DIM:(h + 1) * HEAD_DIM] = _rms(o, og).astype(o_ref.dtype)


def nsa_attention(z, comp, ks, vs, kw, vw, q_gain, out_gain, B, S):
    G, Hg = NSA_KV_GROUPS, NSA_GROUP_SIZE
    nq = S // Q_BLOCK
    n_cmp = S // CMP_STRIDE
    n_sel = S // SEL_BLOCK
    k_top = min(SEL_TOP_N, n_sel)
    c_start = np.arange(n_cmp)[:, None] * CMP_STRIDE
    s_start = np.arange(n_sel)[None, :] * SEL_BLOCK
    overlap = np.minimum(c_start + CMP_BLOCK, s_start + SEL_BLOCK) - np.maximum(c_start, s_start)
    c2s = jnp.asarray(np.clip(overlap, 0, None) / CMP_STRIDE, dtype=BF16)
    expand = jnp.asarray(np.arange(S)[None, :] // SEL_BLOCK == np.arange(n_sel)[:, None], dtype=BF16)
    rows = Hg * Q_BLOCK
    wq = Hg * HEAD_DIM
    kv_spec = pl.BlockSpec((None, None, S, HEAD_DIM), lambda b, g, c: (b, g, 0, 0))
    const = lambda b, g, c: (0, 0)
    return pl.pallas_call(
        functools.partial(_nsa_kernel, k_top=k_top),
        grid=(B, G, nq),
        in_specs=[pl.BlockSpec((Q_BLOCK, wq), lambda b, g, c: (b * nq + c, COL_Q_A // wq + g)),
                  pl.BlockSpec((Q_BLOCK, HEAD_DIM), lambda b, g, c: (b * nq + c, COL_SMALL // HEAD_DIM + g)),
                  pl.BlockSpec((None, None, None, n_cmp, HEAD_DIM), lambda b, g, c: (b, 0, g, 0, 0)),
                  pl.BlockSpec((None, None, None, n_cmp, HEAD_DIM), lambda b, g, c: (b, 1, g, 0, 0)),
                  kv_spec, kv_spec, kv_spec, kv_spec,
                  pl.BlockSpec((1, HEAD_DIM), const), pl.BlockSpec((1, HEAD_DIM), const),
                  pl.BlockSpec((n_cmp, n_sel), const), pl.BlockSpec((n_sel, S), const)],
        out_specs=pl.BlockSpec((Q_BLOCK, wq), lambda b, g, c: (b * nq + c, g)),
        out_shape=jax.ShapeDtypeStruct((B * S, D_NSA), BF16),
        scratch_shapes=[pltpu.VMEM((S // SEL_TILE, Q_BLOCK, SEL_TILE), F32),
                        pltpu.VMEM((rows, HEAD_DIM), F32), pltpu.VMEM((rows, HEAD_DIM), F32),
                        pltpu.VMEM((rows, HEAD_DIM), F32)],
        compiler_params=_params(("parallel", "parallel", "arbitrary")),
        name="nsa_attention",
    )(z, z, comp, comp, ks, vs, kw, vw, q_gain.reshape(1, HEAD_DIM), out_gain.reshape(1, HEAD_DIM), c2s, expand)


def _gdn_chunk_kernel(xc_ref, xh_ref, sm_ref, cw_ref, alog_ref, dtb_ref, u_ref, wq_ref, ak_ref, egl_ref):
    C = GDN_CHUNK
    PK = GDN_PACK
    R = PK * C
    n = pl.program_id(1)
    xp = jnp.concatenate([jnp.where(n > 0, xh_ref[...], 0.0), xc_ref[...]], axis=0)
    cw = cw_ref[...]
    y = None
    for j in range(GDN_CONV):
        shift = GDN_CONV - 1 - j
        xs = xp if shift == 0 else pltpu.roll(xp, shift, axis=0)
        term = xs[8:] * cw[j:j + 1, :]
        y = term if y is None else y + term
    y = y * jax.nn.sigmoid(y)

    sm = sm_ref[...]
    beta = jax.nn.sigmoid(sm)
    g = -jnp.exp(alog_ref[...]) * jax.nn.softplus(sm + dtb_ref[...])
    row = lax.broadcasted_iota(I32, g.shape, 0)
    gc = g
    d = 1
    while d < C:
        gc = gc + jnp.where(row >= d, pltpu.roll(gc, d, axis=0), 0.0)
        d *= 2
    g_last = gc[C - 1:C, :]

    ri = lax.broadcasted_iota(I32, (R, R), 0)
    ci = lax.broadcasted_iota(I32, (R, R), 1)
    same = lax.shift_right_logical(ri, 6) == lax.shift_right_logical(ci, 6)
    tri = same & (ri >= ci)
    strict = same & (ri > ci)
    SUB = GDN_SUB
    nb = R // SUB
    same16 = lax.shift_right_logical(ri, 4) == lax.shift_right_logical(ci, 4)
    same32 = lax.shift_right_logical(ri, 5) == lax.shift_right_logical(ci, 5)
    off32 = same32 & jnp.logical_not(same16)
    off64 = jnp.logical_not(same32)
    row_c = lax.broadcasted_iota(I32, (R, SUB), 0) & (SUB - 1)
    eye_c = jnp.where(row_c == lax.broadcasted_iota(I32, (R, SUB), 1), 1.0, 0.0)
    col_s = lax.broadcasted_iota(I32, (nb, SUB), 1)
    unfold = jnp.where((lax.broadcasted_iota(I32, (SUB, R), 1) & (SUB - 1)) == lax.broadcasted_iota(I32, (SUB, R), 0),
                       1.0, 0.0).astype(BF16)

    for grp in range(GDN_HEADS // PK):
        heads = [grp * PK + i for i in range(PK)]

        def stack(off):
            return jnp.concatenate([y[:, off + h * HEAD_DIM: off + (h + 1) * HEAD_DIM] for h in heads], axis=0)

        def col(x, base):
            return jnp.concatenate([x[:, base + h: base + h + 1] for h in heads], axis=0)

        q4 = stack(0)
        k4 = stack(D_GDN)
        v4 = stack(2 * D_GDN)
        q4 = q4 * lax.rsqrt(jnp.sum(q4 * q4, axis=-1, keepdims=True) + EPS) * SCALE
        k4 = k4 * lax.rsqrt(jnp.sum(k4 * k4, axis=-1, keepdims=True) + EPS)
        beta4 = col(beta, SM_BETA)
        gc4 = col(gc, SM_A)
        gl4 = jnp.concatenate([jnp.broadcast_to(g_last[:, SM_A + h: SM_A + h + 1], (C, 1)) for h in heads], axis=0)

        gb = jnp.broadcast_to(gc4, (R, R))
        decay = jnp.exp(jnp.where(tri, gb - gb.T, -jnp.inf))
        kb4 = k4 * beta4
        k4b = k4.astype(BF16)
        a = jnp.where(strict, _dot_nt(kb4.astype(BF16), k4b) * decay, 0.0)
        attn = _dot_nt(q4.astype(BF16), k4b) * decay
        eg = jnp.exp(gc4)
        at = a.T
        dct = jnp.concatenate([at[SUB * b:SUB * (b + 1), SUB * b:SUB * (b + 1)] for b in range(nb)], axis=0)
        t = eye_c
        for i in range(1, SUB):
            s = jnp.sum((dct[:, i:i + 1] * t).reshape(nb, SUB, SUB), axis=1)
            new = jnp.where(col_s == i, 1.0, 0.0) - s
            t = jnp.where(row_c == i, jnp.broadcast_to(new[:, None, :], (nb, SUB, SUB)).reshape(R, SUB), t)
        t_hi = t.astype(BF16)
        t_rem = t - t_hi.astype(F32)
        t_mid = t_rem.astype(BF16)
        t_lo = (t_rem - t_mid.astype(F32)).astype(BF16)
        t16 = jnp.where(same16, _dot(t_hi, unfold) + _dot(t_mid, unfold) + _dot(t_lo, unfold), 0.0)
        t16b = t16.astype(BF16)
        a32 = jnp.where(off32, a, 0.0).astype(BF16)
        t32 = t16 - _dot(_dot(t16b, a32).astype(BF16), t16b)
        t32b = t32.astype(BF16)
        a64 = jnp.where(off64, a, 0.0).astype(BF16)
        t64 = t32 - _dot(_dot(t32b, a64).astype(BF16), t32b)
        rhs = jnp.concatenate([v4 * beta4, kb4 * eg], axis=1)
        rhs = _dot(t64.astype(BF16), rhs.astype(BF16))
        qd4 = q4 * eg
        kdt = (k4 * jnp.exp(gl4 - gc4)).T
        for i, h in enumerate(heads):
            r = slice(i * C, (i + 1) * C)
            u_ref[h] = rhs[r, :HEAD_DIM]
            wq_ref[h] = jnp.concatenate([rhs[r, HEAD_DIM:], qd4[r]], axis=0).astype(BF16)
            ak_ref[h] = jnp.concatenate([attn[r, r], kdt[:, r]], axis=0).astype(BF16)
            egl_ref[h] = jnp.broadcast_to(jnp.exp(g_last[:, SM_A + h: SM_A + h + 1]), (8, HEAD_DIM))


def gdn_chunks(z, conv_w, a_log, dt_bias, B, S):
    C, H = GDN_CHUNK, GDN_HEADS
    N = S // C
    W = 3 * D_GDN
    pad = lambda v: jnp.zeros((1, HEAD_DIM), F32).at[0, SM_A:SM_A + H].set(v)
    out5 = lambda r, cdim: pl.BlockSpec((None, None, H, r, cdim), lambda b, n: (b, n, 0, 0, 0))
    return pl.pallas_call(
        _gdn_chunk_kernel,
        grid=(B, N),
        in_specs=[pl.BlockSpec((C, W), lambda b, n: (b * N + n, 0)),
                  pl.BlockSpec((8, W), lambda b, n: (jnp.maximum((b * N + n) * (C // 8) - 1, 0), 0)),
                  pl.BlockSpec((C, HEAD_DIM), lambda b, n: (b * N + n, COL_SMALL // HEAD_DIM)),
                  pl.BlockSpec((GDN_CONV, W), lambda b, n: (0, 0)),
                  pl.BlockSpec((1, HEAD_DIM), lambda b, n: (0, 0)),
                  pl.BlockSpec((1, HEAD_DIM), lambda b, n: (0, 0))],
        out_specs=[out5(C, HEAD_DIM), out5(2 * C, HEAD_DIM), out5(C + HEAD_DIM, C), out5(8, HEAD_DIM)],
        out_shape=[jax.ShapeDtypeStruct((B, N, H, C, HEAD_DIM), F32),
                   jax.ShapeDtypeStruct((B, N, H, 2 * C, HEAD_DIM), BF16),
                   jax.ShapeDtypeStruct((B, N, H, C + HEAD_DIM, C), BF16),
                   jax.ShapeDtypeStruct((B, N, H, 8, HEAD_DIM), F32)],
        compiler_params=_params(("parallel", "parallel")),
        name="gdn_chunks",
    )(z, z, z, conv_w, pad(a_log), pad(dt_bias))


def _gdn_scan_kernel(u_ref, wq_ref, ak_ref, egl_ref, z_ref, og_ref, o_ref, s_ref):
    C = GDN_CHUNK

    @pl.when(pl.program_id(1) == 0)
    def _():
        s_ref[...] = jnp.zeros(s_ref.shape, F32)

    og = og_ref[...]
    for h in range(GDN_HEADS):
        state = s_ref[h]
        ws = _dot(wq_ref[h], state.astype(BF16))
        v_new = u_ref[h] - ws[:C]
        av = _dot(ak_ref[h], v_new.astype(BF16))
        o = ws[C:] + av[:C]
        decayed = (state.reshape(HEAD_DIM // 8, 8, HEAD_DIM) * egl_ref[h][None]).reshape(HEAD_DIM, HEAD_DIM)
        s_ref[h] = decayed + av[C:]
        zh = z_ref[:, h * HEAD_DIM:(h + 1) * HEAD_DIM]
        o_ref[:, h * HEAD_DIM:(h + 1) * HEAD_DIM] = (_rms(o, og) * (zh * jax.nn.sigmoid(zh))).astype(o_ref.dtype)


def gdn_scan(u, wq, ak, egl, z, out_gain, B, S):
    C, H = GDN_CHUNK, GDN_HEADS
    N = S // C
    in5 = lambda r, cdim: pl.BlockSpec((None, None, H, r, cdim), lambda b, n: (b, n, 0, 0, 0))
    return pl.pallas_call(
        _gdn_scan_kernel,
        grid=(B, N),
        in_specs=[in5(C, HEAD_DIM), in5(2 * C, HEAD_DIM), in5(C + HEAD_DIM, C), in5(8, HEAD_DIM),
                  pl.BlockSpec((C, D_GDN), lambda b, n: (b * N + n, COL_Z_B // D_GDN)),
                  pl.BlockSpec((1, HEAD_DIM), lambda b, n: (0, 0))],
        out_specs=pl.BlockSpec((C, D_GDN), lambda b, n: (b * N + n, 0)),
        out_shape=jax.ShapeDtypeStruct((B * S, D_GDN), BF16),
        scratch_shapes=[pltpu.VMEM((H, HEAD_DIM, HEAD_DIM), F32)],
        compiler_params=_params(("parallel", "arbitrary")),
        name="gdn_scan",
    )(u, wq, ak, egl, z, out_gain.reshape(1, HEAD_DIM))


def _xa_kv_kernel(mem_ref, g_ref, w_ref, kg_ref, k_ref, v_ref):
    kv = _dot(_rms(mem_ref[...], g_ref[...]).astype(BF16), w_ref[...])
    for h in range(XA_HEADS):
        k_ref[h] = _rms(kv[:, h * HEAD_DIM:(h + 1) * HEAD_DIM], kg_ref[...]).astype(BF16)
        v_ref[h] = kv[:, D_XA + h * HEAD_DIM: D_XA + (h + 1) * HEAD_DIM].astype(BF16)


def xa_kv(mem, mem_norm, wkv_bf16, k_gain):
    B, Mm, D = mem.shape
    spec = pl.BlockSpec((None, XA_HEADS, Mm, HEAD_DIM), lambda b: (b, 0, 0, 0))
    shape = jax.ShapeDtypeStruct((B, XA_HEADS, Mm, HEAD_DIM), BF16)
    return pl.pallas_call(
        _xa_kv_kernel,
        grid=(B,),
        in_specs=[pl.BlockSpec((None, Mm, D), lambda b: (b, 0, 0)), pl.BlockSpec((1, D), lambda b: (0, 0)),
                  pl.BlockSpec((D, 2 * D_XA), lambda b: (0, 0)), pl.BlockSpec((1, HEAD_DIM), lambda b: (0, 0))],
        out_specs=[spec, spec],
        out_shape=[shape, shape],
        compiler_params=_params(("parallel",)),
        name="xa_kv",
    )(mem, mem_norm.reshape(1, D), wkv_bf16, k_gain.reshape(1, HEAD_DIM))


def _xa_kernel(x_ref, g_ref, wq_ref, k_ref, v_ref, qg_ref, wo_ref, o_ref):
    x = x_ref[...]
    q = _dot(_rms(x, g_ref[...]).astype(BF16), wq_ref[...])
    outs = []
    for h in range(XA_HEADS):
        qh = _rms(q[:, h * HEAD_DIM:(h + 1) * HEAD_DIM], qg_ref[...]).astype(BF16)
        s = _dot_nt(qh, k_ref[h]) * SCALE
        p = jnp.exp(s - jnp.max(s, axis=-1, keepdims=True))
        p = p / jnp.sum(p, axis=-1, keepdims=True)
        outs.append(_dot(p.astype(BF16), v_ref[h]))
    o = jnp.concatenate(outs, axis=1).astype(BF16)
    o_ref[...] = x + _dot(o, wo_ref[...])


def cross_attention(x, xa_norm, wq_bf16, k, v, q_gain, wo_bf16, B, S, tm=256):
    M, D = x.shape
    Mm = k.shape[2]
    per_b = S // tm
    kv_spec = pl.BlockSpec((None, XA_HEADS, Mm, HEAD_DIM), lambda i: (i // per_b, 0, 0, 0))
    const = lambda i: (0, 0)
    return pl.pallas_call(
        _xa_kernel,
        grid=(M // tm,),
        in_specs=[pl.BlockSpec((tm, D), lambda i: (i, 0)), pl.BlockSpec((1, D), const),
                  pl.BlockSpec((D, D_XA), const), kv_spec, kv_spec,
                  pl.BlockSpec((1, HEAD_DIM), const), pl.BlockSpec((D_XA, D), const)],
        out_specs=pl.BlockSpec((tm, D), lambda i: (i, 0)),
        out_shape=jax.ShapeDtypeStruct((M, D), F32),
        compiler_params=_params(("parallel",)),
        name="cross_attention",
    )(x, xa_norm.reshape(1, D), wq_bf16, k, v, q_gain.reshape(1, HEAD_DIM), wo_bf16)


def _reorder_w_in(w_in):
    o = np.cumsum((D_NSA, 6 * D_KV_NSA, 3 * NSA_HEADS, 3 * D_GDN, GDN_HEADS, GDN_HEADS, D_GDN))
    q_a, kv_a, gate_a, qkv_b = (0, o[0]), (o[0], o[1]), o[1], (o[2], o[3])
    beta_b, a_b, z_b = o[3], o[4], (o[5], o[6])
    Hg = NSA_GROUP_SIZE
    src = np.zeros((256,), np.int64)
    used = np.zeros((256,), bool)
    for g in range(NSA_KV_GROUPS):
        for br in range(3):
            for h in range(Hg):
                src[g * HEAD_DIM + br * Hg + h] = gate_a + (g * Hg + h) * 3 + br
                used[g * HEAD_DIM + br * Hg + h] = True
    for h in range(GDN_HEADS):
        src[SM_BETA + h] = beta_b + h
        src[SM_A + h] = a_b + h
        used[SM_BETA + h] = used[SM_A + h] = True
    small = jnp.where(jnp.asarray(used)[None, :], jnp.take(w_in, jnp.asarray(src), axis=1), 0.0)
    parts = [w_in[:, qkv_b[0]:qkv_b[1]], w_in[:, q_a[0]:q_a[1]], w_in[:, z_b[0]:z_b[1]], w_in[:, kv_a[0]:kv_a[1]], small]
    return jnp.concatenate([p.astype(BF16) for p in parts], axis=1)


def mixer_layer(x2, B, S, attn_norm, w_in, nsa_q_gain, nsa_k_gain, nsa_cmp_pos, nsa_cmp_w1, nsa_cmp_w2, nsa_out_gain,
                gdn_conv_w, gdn_A_log, gdn_dt_bias, gdn_out_gain, w_out, tm=1024):
    h = rmsnorm_bf16(x2, attn_norm)
    z = matmul_cols(h, _reorder_w_in(w_in), tm=tm, tn=256, name="in_proj")
    raw, ks, vs, kw, vw = nsa_prep(z, nsa_k_gain, B, S)
    comp = nsa_compress(raw, nsa_cmp_pos, nsa_cmp_w1, nsa_cmp_w2, nsa_k_gain[0], B, S)
    y_a = nsa_attention(z, comp, ks, vs, kw, vw, nsa_q_gain, nsa_out_gain, B, S)
    u, wq, ak, egl = gdn_chunks(z, gdn_conv_w, gdn_A_log, gdn_dt_bias, B, S)
    y_b = gdn_scan(u, wq, ak, egl, z, gdn_out_gain, B, S)
    y = jnp.concatenate([y_a, y_b], axis=1)
    return matmul_cols(y, w_out, tm=tm, tn=256, residual=x2, name="out_proj")


def xa_layer(x2, mem, B, S, xa_norm, mem_norm, xa_wq, xa_wkv, xa_q_gain, xa_k_gain, xa_wo):
    k, v = xa_kv(mem, mem_norm, xa_wkv.astype(BF16), xa_k_gain)
    return cross_attention(x2, xa_norm, xa_wq.astype(BF16), k, v, xa_q_gain, xa_wo.astype(BF16), B, S)


def dense_ffn_layer(x2, ffn_norm, w13, w2, tm=1024):
    h = rmsnorm_bf16(x2, ffn_norm)
    act = swiglu_up(h, w13, tm=tm, tn=256)
    return down_proj_residual(act, w2, x2, tm=tm, tn=2048, tk=256)


def moe_ffn_layer(x2, ffn_norm, router_w, w13, w2, tm=1024):
    h, idx, wts = rmsnorm_router(x2, ffn_norm, router_w)
    src_tok, row_w, pos, tile_expert, n_valid = moe_dispatch(idx[:, :TOP_K], wts[:, :TOP_K], tm)
    xs = jnp.take(h, src_tok, axis=0)
    act = moe_up(xs, w13, tile_expert, n_valid, tm=tm, tn=256)
    row_w_rep = jnp.broadcast_to(row_w[:, None], (row_w.shape[0], 128))
    out = moe_down(act, w2, row_w_rep, tile_expert, n_valid, tm=tm, tn=2048, tk=256)
    return x2 + (jnp.take(out, pos[:, 0], axis=0) + jnp.take(out, pos[:, 1], axis=0))


def kernel(x, mem, attn_norm, w_in, nsa_q_gain, nsa_k_gain, nsa_cmp_pos, nsa_cmp_w1, nsa_cmp_w2, nsa_out_gain, gdn_conv_w, gdn_A_log, gdn_dt_bias, gdn_out_gain, w_out, xa_norm, mem_norm, xa_wq, xa_wkv, xa_q_gain, xa_k_gain, xa_wo, ffn_norm, dense_w13, dense_w2, router_w, moe_w13, moe_w2):
    B, S, D = x.shape
    x2 = x.reshape(B * S, D)
    for l in range(attn_norm.shape[0]):
        x2 = mixer_layer(x2, B, S, attn_norm[l], w_in[l], nsa_q_gain[l], nsa_k_gain[l], nsa_cmp_pos[l], nsa_cmp_w1[l],
                         nsa_cmp_w2[l], nsa_out_gain[l], gdn_conv_w[l], gdn_A_log[l], gdn_dt_bias[l], gdn_out_gain[l],
                         w_out[l])
        x2 = xa_layer(x2, mem, B, S, xa_norm[l], mem_norm[l], xa_wq[l], xa_wkv[l], xa_q_gain[l], xa_k_gain[l], xa_wo[l])
        if l % 2 == 0:
            x2 = dense_ffn_layer(x2, ffn_norm[l], dense_w13[l // 2], dense_w2[l // 2])
        else:
            x2 = moe_ffn_layer(x2, ffn_norm[l], router_w[l // 2], moe_w13[l // 2], moe_w2[l // 2])
    return x2.reshape(B, S, D)
```

```python
import functools

import jax
import jax.numpy as jnp
import numpy as np
from jax import lax
from jax.experimental import pallas as pl
from jax.experimental.pallas import tpu as pltpu

F32 = jnp.float32
BF16 = jnp.bfloat16
I32 = jnp.int32

D_MODEL = 4096
HEAD_DIM = 128
EPS = 1e-6
SCALE = HEAD_DIM ** -0.5

NSA_HEADS = 16
NSA_KV_GROUPS = 2
NSA_GROUP_SIZE = NSA_HEADS // NSA_KV_GROUPS
CMP_BLOCK = 32
CMP_STRIDE = 16
SEL_BLOCK = 64
SEL_TOP_N = 16
SEL_LOCAL = 2
WINDOW = 512
Q_BLOCK = 128
FORCE = 1e9
NEG = -1e30
SEL_TILE = 256
LOG2E = 1.4426950408889634

GDN_HEADS = 16
GDN_CONV = 4
GDN_CHUNK = 64
GDN_PACK = 4
GDN_SUB = 16

D_NSA = NSA_HEADS * HEAD_DIM
D_KV_NSA = NSA_KV_GROUPS * HEAD_DIM
D_GDN = GDN_HEADS * HEAD_DIM
D_MIX = D_NSA + D_GDN

XA_HEADS = 4
D_XA = XA_HEADS * HEAD_DIM
N_EXPERTS = 8
TOP_K = 2

COL_QKV_B = 0
COL_Q_A = COL_QKV_B + 3 * D_GDN
COL_Z_B = COL_Q_A + D_NSA
COL_KV_A = COL_Z_B + D_GDN
COL_SMALL = COL_KV_A + 6 * D_KV_NSA
N_IN_R = COL_SMALL + 256
SM_BETA = 32
SM_A = 48

VMEM_LIMIT = 58 * 1024 * 1024

NT_DIMS = (((1,), (1,)), ((), ()))


def _rms(x, gain):
    return x * lax.rsqrt(jnp.mean(x * x, axis=-1, keepdims=True) + EPS) * gain


def _dot(a, b):
    return jnp.dot(a, b, preferred_element_type=F32)


def _dot_nt(a, b):
    return lax.dot_general(a, b, NT_DIMS, preferred_element_type=F32)


def _params(sem, vmem=VMEM_LIMIT):
    return pltpu.CompilerParams(dimension_semantics=sem, vmem_limit_bytes=vmem)


def _norm_kernel(x_ref, g_ref, o_ref):
    o_ref[...] = _rms(x_ref[...], g_ref[...]).astype(o_ref.dtype)


def rmsnorm_bf16(x, gain, tm=512):
    M, D = x.shape
    return pl.pallas_call(
        _norm_kernel,
        grid=(M // tm,),
        in_specs=[pl.BlockSpec((tm, D), lambda i: (i, 0)), pl.BlockSpec((1, D), lambda i: (0, 0))],
        out_specs=pl.BlockSpec((tm, D), lambda i: (i, 0)),
        out_shape=jax.ShapeDtypeStruct((M, D), BF16),
        compiler_params=_params(("parallel",)),
        name="rmsnorm",
    )(x, gain.reshape(1, D))


def _norm_router_kernel(x_ref, g_ref, rw_ref, o_ref, idx_ref, wt_ref):
    h = _rms(x_ref[...], g_ref[...])
    o_ref[...] = h.astype(o_ref.dtype)
    logits = jnp.dot(h, rw_ref[...], preferred_element_type=F32, precision=lax.Precision.HIGHEST)
    lane = lax.broadcasted_iota(I32, logits.shape, 1).astype(F32)
    logits = jnp.where(lane < N_EXPERTS, logits, -jnp.inf)
    m1 = jnp.max(logits, axis=-1, keepdims=True)
    i1 = jnp.min(jnp.where(logits == m1, lane, 128.0), axis=-1, keepdims=True)
    rest = jnp.where(lane == i1, -jnp.inf, logits)
    m2 = jnp.max(rest, axis=-1, keepdims=True)
    i2 = jnp.min(jnp.where(rest == m2, lane, 128.0), axis=-1, keepdims=True)
    e2 = jnp.exp(m2 - m1)
    den = 1.0 + e2
    idx_ref[...] = jnp.where(lane == 0, i1, jnp.where(lane == 1, i2, 0.0)).astype(I32)
    wt_ref[...] = jnp.where(lane == 0, 1.0 / den, jnp.where(lane == 1, e2 / den, 0.0))


def rmsnorm_router(x, gain, router_w, tm=256):
    M, D = x.shape
    rw = jnp.pad(router_w, ((0, 0), (0, 128 - N_EXPERTS)))
    row = lambda i: (i, 0)
    return pl.pallas_call(
        _norm_router_kernel,
        grid=(M // tm,),
        in_specs=[pl.BlockSpec((tm, D), row), pl.BlockSpec((1, D), lambda i: (0, 0)),
                  pl.BlockSpec((D, 128), lambda i: (0, 0))],
        out_specs=[pl.BlockSpec((tm, D), row), pl.BlockSpec((tm, 128), row), pl.BlockSpec((tm, 128), row)],
        out_shape=[jax.ShapeDtypeStruct((M, D), BF16), jax.ShapeDtypeStruct((M, 128), I32),
                   jax.ShapeDtypeStruct((M, 128), F32)],
        compiler_params=_params(("parallel",)),
        name="rmsnorm_router",
    )(x, gain.reshape(1, D), rw)


def _mm_kernel(x_ref, w_ref, o_ref):
    o_ref[...] = _dot(x_ref[...], w_ref[...].astype(BF16)).astype(o_ref.dtype)


def _mm_res_kernel(x_ref, w_ref, r_ref, o_ref):
    o_ref[...] = r_ref[...] + _dot(x_ref[...], w_ref[...].astype(BF16))


def matmul_cols(x, w, *, tm, tn, out_dtype=F32, residual=None, name="matmul"):
    M, K = x.shape
    N = w.shape[1]
    in_specs = [pl.BlockSpec((tm, K), lambda i, j: (i, 0)), pl.BlockSpec((K, tn), lambda i, j: (0, j))]
    args = [x, w]
    kern = _mm_kernel
    if residual is not None:
        in_specs.append(pl.BlockSpec((tm, tn), lambda i, j: (i, j)))
        args.append(residual)
        kern = _mm_res_kernel
    return pl.pallas_call(
        kern,
        grid=(M // tm, N // tn),
        in_specs=in_specs,
        out_specs=pl.BlockSpec((tm, tn), lambda i, j: (i, j)),
        out_shape=jax.ShapeDtypeStruct((M, N), out_dtype),
        compiler_params=_params(("parallel", "parallel")),
        name=name,
    )(*args)


def _mm2_res_kernel(xa_ref, xb_ref, wa_ref, wb_ref, r_ref, o_ref):
    acc = _dot(xa_ref[...], wa_ref[...].astype(BF16)) + _dot(xb_ref[...], wb_ref[...].astype(BF16))
    o_ref[...] = r_ref[...] + acc


def matmul2_residual(xa, xb, w, residual, *, tm, tn, name):
    M, Ka = xa.shape
    Kb = xb.shape[1]
    assert Ka == Kb and w.shape[0] == Ka + Kb
    N = w.shape[1]
    return pl.pallas_call(
        _mm2_res_kernel,
        grid=(M // tm, N // tn),
        in_specs=[pl.BlockSpec((tm, Ka), lambda i, j: (i, 0)), pl.BlockSpec((tm, Kb), lambda i, j: (i, 0)),
                  pl.BlockSpec((Ka, tn), lambda i, j: (0, j)), pl.BlockSpec((Kb, tn), lambda i, j: (1, j)),
                  pl.BlockSpec((tm, tn), lambda i, j: (i, j))],
        out_specs=pl.BlockSpec((tm, tn), lambda i, j: (i, j)),
        out_shape=jax.ShapeDtypeStruct((M, N), F32),
        compiler_params=_params(("parallel", "parallel")),
        name=name,
    )(xa, xb, w, w, residual)


def _swiglu_up_kernel(x_ref, wg_ref, wu_ref, o_ref):
    x = x_ref[...]
    gate = _dot(x, wg_ref[...].astype(BF16))
    up = _dot(x, wu_ref[...].astype(BF16))
    o_ref[...] = (gate * jax.nn.sigmoid(gate) * up).astype(o_ref.dtype)


def swiglu_up(x, w13, *, tm, tn):
    M, K = x.shape
    F = w13.shape[1] // 2
    nj = F // tn
    return pl.pallas_call(
        _swiglu_up_kernel,
        grid=(M // tm, nj),
        in_specs=[pl.BlockSpec((tm, K), lambda i, j: (i, 0)),
                  pl.BlockSpec((K, tn), lambda i, j: (0, j)),
                  pl.BlockSpec((K, tn), lambda i, j: (0, j + nj))],
        out_specs=pl.BlockSpec((tm, tn), lambda i, j: (i, j)),
        out_shape=jax.ShapeDtypeStruct((M, F), BF16),
        compiler_params=_params(("parallel", "parallel")),
        name="swiglu_up",
    )(x, w13, w13)


def _down_res_kernel(a_ref, w_ref, r_ref, o_ref):
    @pl.when(pl.program_id(2) == 0)
    def _():
        o_ref[...] = r_ref[...]

    o_ref[...] += _dot(a_ref[...], w_ref[...].astype(BF16))


def down_proj_residual(a, w2, residual, *, tm, tn, tk):
    M, F = a.shape
    N = w2.shape[1]
    return pl.pallas_call(
        _down_res_kernel,
        grid=(M // tm, N // tn, F // tk),
        in_specs=[pl.BlockSpec((tm, tk), lambda i, j, k: (i, k)),
                  pl.BlockSpec((tk, tn), lambda i, j, k: (k, j)),
                  pl.BlockSpec((tm, tn), lambda i, j, k: (i, j))],
        out_specs=pl.BlockSpec((tm, tn), lambda i, j, k: (i, j)),
        out_shape=jax.ShapeDtypeStruct((M, N), F32),
        compiler_params=_params(("parallel", "parallel", "arbitrary")),
        name="down_proj",
    )(a, w2, residual)


def _new_weights(te_ref, i):
    return jnp.logical_or(i == 0, te_ref[i] != te_ref[jnp.maximum(i - 1, 0)])


def _moe_up_kernel(te_ref, nv_ref, x_ref, wg_ref, wu_ref, o_ref, wgb_ref, wub_ref):
    i = pl.program_id(1)
    valid = i < nv_ref[0]

    @pl.when(_new_weights(te_ref, i))
    def _():
        wgb_ref[...] = wg_ref[...].astype(BF16)
        wub_ref[...] = wu_ref[...].astype(BF16)

    @pl.when(valid)
    def _():
        x = x_ref[...]
        gate = _dot(x, wgb_ref[...])
        up = _dot(x, wub_ref[...])
        o_ref[...] = (gate * jax.nn.sigmoid(gate) * up).astype(o_ref.dtype)

    @pl.when(jnp.logical_not(valid))
    def _():
        o_ref[...] = jnp.zeros_like(o_ref)


def moe_up(xs, w13, tile_expert, n_valid, *, tm, tn):
    P, K = xs.shape
    F = w13.shape[2] // 2
    nj = F // tn
    return pl.pallas_call(
        _moe_up_kernel,
        grid_spec=pltpu.PrefetchScalarGridSpec(
            num_scalar_prefetch=2,
            grid=(nj, P // tm),
            in_specs=[pl.BlockSpec((tm, K), lambda j, i, te, nv: (jnp.minimum(i, nv[0] - 1), 0)),
                      pl.BlockSpec((None, K, tn), lambda j, i, te, nv: (te[i], 0, j)),
                      pl.BlockSpec((None, K, tn), lambda j, i, te, nv: (te[i], 0, j + nj))],
            out_specs=pl.BlockSpec((tm, tn), lambda j, i, te, nv: (i, j)),
            scratch_shapes=[pltpu.VMEM((K, tn), BF16), pltpu.VMEM((K, tn), BF16)],
        ),
        out_shape=jax.ShapeDtypeStruct((P, F), BF16),
        compiler_params=_params(("arbitrary", "arbitrary")),
        name="moe_up",
    )(tile_expert, n_valid, xs, w13, w13)


def _moe_down_kernel(te_ref, nv_ref, a_ref, w_ref, rw_ref, o_ref, wb_ref):
    i = pl.program_id(1)
    valid = i < nv_ref[0]

    @pl.when(_new_weights(te_ref, i))
    def _():
        wb_ref[...] = w_ref[...].astype(BF16)

    @pl.when(valid)
    def _():
        o_ref[...] = _dot(a_ref[...], wb_ref[...]) * rw_ref[:, 0:1]

    @pl.when(jnp.logical_not(valid))
    def _():
        o_ref[...] = jnp.zeros_like(o_ref)


def moe_down(act, w2, row_w, tile_expert, n_valid, *, tm, tn):
    P, F = act.shape
    N = w2.shape[2]
    return pl.pallas_call(
        _moe_down_kernel,
        grid_spec=pltpu.PrefetchScalarGridSpec(
            num_scalar_prefetch=2,
            grid=(N // tn, P // tm),
            in_specs=[pl.BlockSpec((tm, F), lambda j, i, te, nv: (jnp.minimum(i, nv[0] - 1), 0)),
                      pl.BlockSpec((None, F, tn), lambda j, i, te, nv: (te[i], 0, j)),
                      pl.BlockSpec((tm, 128), lambda j, i, te, nv: (i, 0))],
            out_specs=pl.BlockSpec((tm, tn), lambda j, i, te, nv: (i, j)),
            scratch_shapes=[pltpu.VMEM((F, tn), BF16)],
        ),
        out_shape=jax.ShapeDtypeStruct((P, N), F32),
        compiler_params=_params(("arbitrary", "arbitrary")),
        name="moe_down",
    )(tile_expert, n_valid, act, w2, row_w)


def moe_dispatch(top_i, top_w, tm):
    M = top_i.shape[0]
    A = M * TOP_K
    P = A + N_EXPERTS * tm
    e_flat = top_i.reshape(A)
    onehot = (e_flat[:, None] == jnp.arange(N_EXPERTS, dtype=I32)[None, :]).astype(I32)
    csum = jnp.cumsum(onehot, axis=0)
    counts = csum[-1]
    padded = ((counts + tm - 1) // tm) * tm
    pend = jnp.cumsum(padded)
    pstart = pend - padded
    dest = jnp.sum(onehot * (pstart[None, :] + csum - onehot), axis=1)
    src_tok = jnp.zeros((P,), I32).at[dest].set(jnp.arange(A, dtype=I32) // TOP_K)
    row_w = jnp.zeros((P,), F32).at[dest].set(top_w.reshape(A))
    pos = dest.reshape(M, TOP_K)
    tile_start = jnp.arange(P // tm, dtype=I32) * tm
    tile_expert = jnp.minimum(jnp.sum((tile_start[:, None] >= pend[None, :]).astype(I32), axis=1), N_EXPERTS - 1)
    n_valid = (pend[-1:] // tm).astype(I32)
    last_e = tile_expert[jnp.maximum(n_valid[0] - 1, 0)]
    tile_expert = jnp.where(tile_start // tm < n_valid[0], tile_expert, last_e)
    return src_tok, row_w, pos, tile_expert, n_valid


def _nsa_prep_kernel(kc_ref, vc_ref, ks_ref, vs_ref, kw_ref, vw_ref, kg_ref, raw_ref, kso_ref, vso_ref, kwo_ref, vwo_ref):
    raw_ref[0] = kc_ref[...]
    raw_ref[1] = vc_ref[...]
    kg = kg_ref[...]
    kso_ref[...] = _rms(ks_ref[...], kg[1:2]).astype(BF16)
    vso_ref[...] = vs_ref[...].astype(BF16)
    kwo_ref[...] = _rms(kw_ref[...], kg[2:3]).astype(BF16)
    vwo_ref[...] = vw_ref[...].astype(BF16)


def nsa_prep(z, k_gain, B, S, ts=512):
    G = NSA_KV_GROUPS
    nt = S // ts
    cb = COL_KV_A // HEAD_DIM

    def col(br, kvi):
        return pl.BlockSpec((ts, HEAD_DIM), lambda b, g, t: (b * nt + t, cb + br * 4 + kvi * 2 + g))

    kv_out = pl.BlockSpec((None, None, ts, HEAD_DIM), lambda b, g, t: (b, g, t, 0))
    kv_shape = jax.ShapeDtypeStruct((B, G, S, HEAD_DIM), BF16)
    return pl.pallas_call(
        _nsa_prep_kernel,
        grid=(B, G, nt),
        in_specs=[col(0, 0), col(0, 1), col(1, 0), col(1, 1), col(2, 0), col(2, 1),
                  pl.BlockSpec((3, HEAD_DIM), lambda b, g, t: (0, 0))],
        out_specs=[pl.BlockSpec((None, 2, None, ts, HEAD_DIM), lambda b, g, t: (b, 0, g, t, 0)),
                   kv_out, kv_out, kv_out, kv_out],
        out_shape=[jax.ShapeDtypeStruct((B, 2, G, S, HEAD_DIM), F32), kv_shape, kv_shape, kv_shape, kv_shape],
        compiler_params=_params(("parallel", "parallel", "parallel")),
        name="nsa_prep",
    )(z, z, z, z, z, z, k_gain)


def _nsa_compress_kernel(r_ref, pos_ref, w1_ref, w2_ref, kg_ref, o_ref):
    kv = pl.program_id(1)
    half = (CMP_BLOCK // 2) * HEAD_DIM
    r = r_ref[...]
    pos = pos_ref[...]
    n = r.shape[0]
    first = _dot((r + pos[:, :half]).astype(BF16), w1_ref[:half, :].astype(BF16))
    second = _dot((r + pos[:, half:]).astype(BF16), w1_ref[half:, :].astype(BF16))
    pre = first + pltpu.roll(second, n - 1, axis=0)
    hid = jax.nn.gelu(pre)
    comp = _dot(hid.astype(BF16), w2_ref[...].astype(BF16))
    o_ref[...] = jnp.where(kv == 0, _rms(comp, kg_ref[...]), comp).astype(o_ref.dtype)


def nsa_compress(raw, cmp_pos, cmp_w1, cmp_w2, k_gain0, B, S):
    G = NSA_KV_GROUPS
    n = S // CMP_STRIDE
    feat = CMP_STRIDE * HEAD_DIM
    r = raw.reshape(B, 2, G, n, feat)
    pos = cmp_pos.reshape(2, 1, CMP_BLOCK * HEAD_DIM)
    return pl.pallas_call(
        _nsa_compress_kernel,
        grid=(B, 2, G),
        in_specs=[pl.BlockSpec((None, None, None, n, feat), lambda b, kv, g: (b, kv, g, 0, 0)),
                  pl.BlockSpec((None, 1, CMP_BLOCK * HEAD_DIM), lambda b, kv, g: (kv, 0, 0)),
                  pl.BlockSpec((None, CMP_BLOCK * HEAD_DIM, HEAD_DIM), lambda b, kv, g: (kv, 0, 0)),
                  pl.BlockSpec((None, HEAD_DIM, HEAD_DIM), lambda b, kv, g: (kv, 0, 0)),
                  pl.BlockSpec((1, HEAD_DIM), lambda b, kv, g: (0, 0))],
        out_specs=pl.BlockSpec((None, None, None, n, HEAD_DIM), lambda b, kv, g: (b, kv, g, 0, 0)),
        out_shape=jax.ShapeDtypeStruct((B, 2, G, n, HEAD_DIM), BF16),
        compiler_params=_params(("parallel", "parallel", "parallel")),
        name="nsa_compress",
    )(r, pos, cmp_w1, cmp_w2, k_gain0.reshape(1, HEAD_DIM))


def _rep(x, n):
    return x if n == HEAD_DIM else jnp.concatenate([x] * (n // HEAD_DIM), axis=1)


def _flash_step(qb, kt, vt, bias, m_ref, l_ref, acc_ref):
    nk = kt.shape[0]
    rows = qb.shape[0]
    s = _dot_nt(qb, kt)
    s = (s.reshape(rows // Q_BLOCK, Q_BLOCK, nk) + bias[None]).reshape(rows, nk)
    m_prev = m_ref[...]
    m_new = jnp.maximum(m_prev, jnp.max(s, axis=-1, keepdims=True))
    alpha = jnp.exp2(m_prev - m_new)
    p = jnp.exp2(s - _rep(m_new, nk))
    l_ref[...] = alpha * l_ref[...] + jnp.sum(p, axis=-1, keepdims=True)
    acc_ref[...] = alpha * acc_ref[...] + _dot(p.astype(BF16), vt)
    m_ref[...] = m_new


def _nsa_kernel(q_ref, sm_ref, kc_ref, vc_ref, ks_ref, vs_ref, kw_ref, vw_ref, qg_ref, og_ref, c2s_ref, exp_ref,
                o_ref, selb_ref, m_ref, l_ref, acc_ref, *, k_top):
    Hg = NSA_GROUP_SIZE
    rows = Hg * Q_BLOCK
    c = pl.program_id(2)
    t0 = c * Q_BLOCK
    n_cmp = kc_ref.shape[0]
    n_sel = c2s_ref.shape[0]

    q = q_ref[...]
    qs = jnp.concatenate([q[:, h * HEAD_DIM:(h + 1) * HEAD_DIM] for h in range(Hg)], axis=0)
    qb = (_rms(qs, qg_ref[...]) * (SCALE * LOG2E)).astype(BF16)

    tq = t0 + lax.broadcasted_iota(I32, (Q_BLOCK, n_cmp), 0)
    c_end = lax.broadcasted_iota(I32, (Q_BLOCK, n_cmp), 1) * CMP_STRIDE + (CMP_BLOCK - 1)
    sc = _dot_nt(qb, kc_ref[...]).reshape(Hg, Q_BLOCK, n_cmp)
    sc = jnp.where((c_end <= tq)[None], sc, -jnp.inf)
    mc = jnp.max(sc, axis=-1, keepdims=True)
    mc = jnp.where(mc > -jnp.inf, mc, 0.0)
    pc = jnp.exp2(sc - mc)
    dc = jnp.sum(pc, axis=-1, keepdims=True)
    pc = pc / jnp.where(dc > 0, dc, 1.0)
    o_cmp = _dot(pc.reshape(rows, n_cmp).astype(BF16), vc_ref[...])

    psum = jnp.sum(pc, axis=0)
    c2s = c2s_ref[...]
    p_hi = psum.astype(BF16)
    rem = psum - p_hi.astype(F32)
    p_mid = rem.astype(BF16)
    p_lo = (rem - p_mid.astype(F32)).astype(BF16)
    imp = _dot_nt(c2s, p_hi) + _dot_nt(c2s, p_mid) + _dot_nt(c2s, p_lo)

    jj = lax.broadcasted_iota(I32, (n_sel, Q_BLOCK), 0)
    cur = lax.shift_right_logical(t0 + lax.broadcasted_iota(I32, (n_sel, Q_BLOCK), 1), 6)
    valid = jj <= cur
    forced = valid & ((jj == 0) | (jj > cur - SEL_LOCAL))
    score = jnp.where(forced, FORCE, jnp.where(valid, imp, -FORCE))
    rank = jnp.zeros((n_sel, Q_BLOCK), F32)
    for j2 in range(n_sel):
        other = score[j2:j2 + 1, :]
        tie = jnp.where(jj > j2, 1.0, 0.0)
        rank = rank + jnp.where(other > score, 1.0, jnp.where(other == score, tie, 0.0))
    sel_t = jnp.where(rank < k_top, 1.0, 0.0)
    if n_sel < Q_BLOCK:
        sel_t = jnp.concatenate([sel_t, jnp.zeros((Q_BLOCK - n_sel, Q_BLOCK), F32)], axis=0)
    sel = sel_t.T[:, :n_sel].astype(BF16)
    sel_keys = _dot(sel, exp_ref[...])
    for i in range(selb_ref.shape[0]):
        selb_ref[i] = (sel_keys[:, i * SEL_TILE:(i + 1) * SEL_TILE] - 1.0) * (-NEG)

    def reset():
        m_ref[...] = jnp.full(m_ref.shape, NEG, F32)
        l_ref[...] = jnp.zeros(l_ref.shape, F32)
        acc_ref[...] = jnp.zeros(acc_ref.shape, F32)

    reset()
    tq_s = t0 + lax.broadcasted_iota(I32, (Q_BLOCK, SEL_TILE), 0)
    kk_s = lax.broadcasted_iota(I32, (Q_BLOCK, SEL_TILE), 1)

    def sel_body(i, carry):
        k0 = pl.multiple_of(i * SEL_TILE, SEL_TILE)
        bias = selb_ref[i] + jnp.where(kk_s + k0 <= tq_s, 0.0, NEG)
        _flash_step(qb, ks_ref[pl.ds(k0, SEL_TILE), :], vs_ref[pl.ds(k0, SEL_TILE), :], bias, m_ref, l_ref, acc_ref)
        return carry

    lax.fori_loop(0, (t0 + Q_BLOCK + SEL_TILE - 1) // SEL_TILE, sel_body, 0)
    o_sel = acc_ref[...] / l_ref[...]

    reset()
    tq_w = t0 + lax.broadcasted_iota(I32, (Q_BLOCK, Q_BLOCK), 0)
    kk_w = lax.broadcasted_iota(I32, (Q_BLOCK, Q_BLOCK), 1)

    def win_body(i, carry):
        k0 = pl.multiple_of((c - i) * Q_BLOCK, Q_BLOCK)
        wpos = kk_w + k0
        ok = (wpos <= tq_w) & (wpos > tq_w - WINDOW)
        bias = jnp.where(ok, 0.0, NEG)
        _flash_step(qb, kw_ref[pl.ds(k0, Q_BLOCK), :], vw_ref[pl.ds(k0, Q_BLOCK), :], bias, m_ref, l_ref, acc_ref)
        return carry

    lax.fori_loop(0, jnp.minimum(c, WINDOW // Q_BLOCK) + 1, win_body, 0)
    o_win = acc_ref[...] / l_ref[...]

    gates = jax.nn.sigmoid(sm_ref[...])
    og = og_ref[...]
    for h in range(Hg):
        r = slice(h * Q_BLOCK, (h + 1) * Q_BLOCK)
        o = (gates[:, h:h + 1] * o_cmp[r] + gates[:, Hg + h:Hg + h + 1] * o_sel[r]
             + gates[:, 2 * Hg + h:2 * Hg + h + 1] * o_win[r])
        o_ref[:, h * HEAD_DIM:(h + 1) * HEAD_DIM] = _rms(o, og).astype(o_ref.dtype)


def nsa_attention(z, comp, ks, vs, kw, vw, q_gain, out_gain, B, S):
    G, Hg = NSA_KV_GROUPS, NSA_GROUP_SIZE
    nq = S // Q_BLOCK
    n_cmp = S // CMP_STRIDE
    n_sel = S // SEL_BLOCK
    k_top = min(SEL_TOP_N, n_sel)
    c_start = np.arange(n_cmp)[:, None] * CMP_STRIDE
    s_start = np.arange(n_sel)[None, :] * SEL_BLOCK
    overlap = np.minimum(c_start + CMP_BLOCK, s_start + SEL_BLOCK) - np.maximum(c_start, s_start)
    assert n_sel <= Q_BLOCK
    c2s = jnp.asarray((np.clip(overlap, 0, None) / CMP_STRIDE).T, dtype=BF16)
    expand = jnp.asarray(np.arange(S)[None, :] // SEL_BLOCK == np.arange(n_sel)[:, None], dtype=BF16)
    rows = Hg * Q_BLOCK
    wq = Hg * HEAD_DIM
    kv_spec = pl.BlockSpec((None, None, S, HEAD_DIM), lambda b, g, c: (b, g, 0, 0))
    const = lambda b, g, c: (0, 0)
    return pl.pallas_call(
        functools.partial(_nsa_kernel, k_top=k_top),
        grid=(B, G, nq),
        in_specs=[pl.BlockSpec((Q_BLOCK, wq), lambda b, g, c: (b * nq + c, COL_Q_A // wq + g)),
                  pl.BlockSpec((Q_BLOCK, HEAD_DIM), lambda b, g, c: (b * nq + c, COL_SMALL // HEAD_DIM + g)),
                  pl.BlockSpec((None, None, None, n_cmp, HEAD_DIM), lambda b, g, c: (b, 0, g, 0, 0)),
                  pl.BlockSpec((None, None, None, n_cmp, HEAD_DIM), lambda b, g, c: (b, 1, g, 0, 0)),
                  kv_spec, kv_spec, kv_spec, kv_spec,
                  pl.BlockSpec((1, HEAD_DIM), const), pl.BlockSpec((1, HEAD_DIM), const),
                  pl.BlockSpec((n_sel, n_cmp), const), pl.BlockSpec((n_sel, S), const)],
        out_specs=pl.BlockSpec((Q_BLOCK, wq), lambda b, g, c: (b * nq + c, g)),
        out_shape=jax.ShapeDtypeStruct((B * S, D_NSA), BF16),
        scratch_shapes=[pltpu.VMEM((S // SEL_TILE, Q_BLOCK, SEL_TILE), F32),
                        pltpu.VMEM((rows, HEAD_DIM), F32), pltpu.VMEM((rows, HEAD_DIM), F32),
                        pltpu.VMEM((rows, HEAD_DIM), F32)],
        compiler_params=_params(("parallel", "parallel", "arbitrary")),
        name="nsa_attention",
    )(z, z, comp, comp, ks, vs, kw, vw, q_gain.reshape(1, HEAD_DIM), out_gain.reshape(1, HEAD_DIM), c2s, expand)


def _gdn_chunk_kernel(xc_ref, xh_ref, sm_ref, cw_ref, alog_ref, dtb_ref, u_ref, wq_ref, ak_ref, egl_ref):
    C = GDN_CHUNK
    PK = GDN_PACK
    R = PK * C
    n = pl.program_id(1)
    xp = jnp.concatenate([jnp.where(n > 0, xh_ref[...], 0.0), xc_ref[...]], axis=0)
    cw = cw_ref[...]
    y = None
    for j in range(GDN_CONV):
        shift = GDN_CONV - 1 - j
        xs = xp if shift == 0 else pltpu.roll(xp, shift, axis=0)
        term = xs[8:] * cw[j:j + 1, :]
        y = term if y is None else y + term
    y = y * jax.nn.sigmoid(y)

    sm = sm_ref[...]
    beta = jax.nn.sigmoid(sm)
    g = -jnp.exp(alog_ref[...]) * jax.nn.softplus(sm + dtb_ref[...])
    row = lax.broadcasted_iota(I32, g.shape, 0)
    gc = g
    d = 1
    while d < C:
        gc = gc + jnp.where(row >= d, pltpu.roll(gc, d, axis=0), 0.0)
        d *= 2
    g_last = gc[C - 1:C, :]

    ri = lax.broadcasted_iota(I32, (R, R), 0)
    ci = lax.broadcasted_iota(I32, (R, R), 1)
    same = lax.shift_right_logical(ri, 6) == lax.shift_right_logical(ci, 6)
    tri = same & (ri >= ci)
    strict = same & (ri > ci)
    SUB = GDN_SUB
    nb = R // SUB
    same16 = lax.shift_right_logical(ri, 4) == lax.shift_right_logical(ci, 4)
    same32 = lax.shift_right_logical(ri, 5) == lax.shift_right_logical(ci, 5)
    off32 = same32 & jnp.logical_not(same16)
    off64 = jnp.logical_not(same32)
    row_c = lax.broadcasted_iota(I32, (R, SUB), 0) & (SUB - 1)
    eye_c = jnp.where(row_c == lax.broadcasted_iota(I32, (R, SUB), 1), 1.0, 0.0)
    col_s = lax.broadcasted_iota(I32, (nb, SUB), 1)
    unfold = jnp.where((lax.broadcasted_iota(I32, (SUB, R), 1) & (SUB - 1)) == lax.broadcasted_iota(I32, (SUB, R), 0),
                       1.0, 0.0).astype(BF16)

    for grp in range(GDN_HEADS // PK):
        heads = [grp * PK + i for i in range(PK)]

        def stack(off):
            return jnp.concatenate([y[:, off + h * HEAD_DIM: off + (h + 1) * HEAD_DIM] for h in heads], axis=0)

        def col(x, base):
            return jnp.concatenate([x[:, base + h: base + h + 1] for h in heads], axis=0)

        q4 = stack(0)
        k4 = stack(D_GDN)
        v4 = stack(2 * D_GDN)
        q4 = q4 * lax.rsqrt(jnp.sum(q4 * q4, axis=-1, keepdims=True) + EPS) * SCALE
        k4 = k4 * lax.rsqrt(jnp.sum(k4 * k4, axis=-1, keepdims=True) + EPS)
        beta4 = col(beta, SM_BETA)
        gc4 = col(gc, SM_A)
        gl4 = jnp.concatenate([jnp.broadcast_to(g_last[:, SM_A + h: SM_A + h + 1], (C, 1)) for h in heads], axis=0)

        gb = jnp.broadcast_to(gc4, (R, R))
        decay = jnp.exp(jnp.where(tri, gb - gb.T, -jnp.inf))
        kb4 = k4 * beta4
        k4b = k4.astype(BF16)
        a = jnp.where(strict, _dot_nt(kb4.astype(BF16), k4b) * decay, 0.0)
        attn = _dot_nt(q4.astype(BF16), k4b) * decay
        eg = jnp.exp(gc4)
        at = a.T
        dct = jnp.concatenate([at[SUB * b:SUB * (b + 1), SUB * b:SUB * (b + 1)] for b in range(nb)], axis=0)
        t = eye_c
        for i in range(1, SUB):
            s = jnp.sum((dct[:, i:i + 1] * t).reshape(nb, SUB, SUB), axis=1)
            new = jnp.where(col_s == i, 1.0, 0.0) - s
            t = jnp.where(row_c == i, jnp.broadcast_to(new[:, None, :], (nb, SUB, SUB)).reshape(R, SUB), t)
        t_hi = t.astype(BF16)
        t_rem = t - t_hi.astype(F32)
        t_mid = t_rem.astype(BF16)
        t_lo = (t_rem - t_mid.astype(F32)).astype(BF16)
        t16 = jnp.where(same16, _dot(t_hi, unfold) + _dot(t_mid, unfold) + _dot(t_lo, unfold), 0.0)
        t16b = t16.astype(BF16)
        a32 = jnp.where(off32, a, 0.0).astype(BF16)
        t32 = t16 - _dot(_dot(t16b, a32).astype(BF16), t16b)
        t32b = t32.astype(BF16)
        a64 = jnp.where(off64, a, 0.0).astype(BF16)
        t64 = t32 - _dot(_dot(t32b, a64).astype(BF16), t32b)
        rhs = jnp.concatenate([v4 * beta4, kb4 * eg], axis=1)
        rhs = _dot(t64.astype(BF16), rhs.astype(BF16))
        qd4 = q4 * eg
        kdt = (k4 * jnp.exp(gl4 - gc4)).T
        for i, h in enumerate(heads):
            r = slice(i * C, (i + 1) * C)
            u_ref[h] = rhs[r, :HEAD_DIM]
            wq_ref[h] = jnp.concatenate([rhs[r, HEAD_DIM:], qd4[r]], axis=0).astype(BF16)
            ak_ref[h] = jnp.concatenate([attn[r, r], kdt[:, r]], axis=0).astype(BF16)
            egl_ref[h] = jnp.broadcast_to(jnp.exp(g_last[:, SM_A + h: SM_A + h + 1]), (8, HEAD_DIM))


def gdn_chunks(z, conv_w, a_log, dt_bias, B, S):
    C, H = GDN_CHUNK, GDN_HEADS
    N = S // C
    W = 3 * D_GDN
    pad = lambda v: jnp.zeros((1, HEAD_DIM), F32).at[0, SM_A:SM_A + H].set(v)
    out5 = lambda r, cdim: pl.BlockSpec((None, None, H, r, cdim), lambda b, n: (b, n, 0, 0, 0))
    return pl.pallas_call(
        _gdn_chunk_kernel,
        grid=(B, N),
        in_specs=[pl.BlockSpec((C, W), lambda b, n: (b * N + n, 0)),
                  pl.BlockSpec((8, W), lambda b, n: (jnp.maximum((b * N + n) * (C // 8) - 1, 0), 0)),
                  pl.BlockSpec((C, HEAD_DIM), lambda b, n: (b * N + n, COL_SMALL // HEAD_DIM)),
                  pl.BlockSpec((GDN_CONV, W), lambda b, n: (0, 0)),
                  pl.BlockSpec((1, HEAD_DIM), lambda b, n: (0, 0)),
                  pl.BlockSpec((1, HEAD_DIM), lambda b, n: (0, 0))],
        out_specs=[out5(C, HEAD_DIM), out5(2 * C, HEAD_DIM), out5(C + HEAD_DIM, C), out5(8, HEAD_DIM)],
        out_shape=[jax.ShapeDtypeStruct((B, N, H, C, HEAD_DIM), F32),
                   jax.ShapeDtypeStruct((B, N, H, 2 * C, HEAD_DIM), BF16),
                   jax.ShapeDtypeStruct((B, N, H, C + HEAD_DIM, C), BF16),
                   jax.ShapeDtypeStruct((B, N, H, 8, HEAD_DIM), F32)],
        compiler_params=_params(("parallel", "parallel")),
        name="gdn_chunks",
    )(z, z, z, conv_w, pad(a_log), pad(dt_bias))


def _gdn_scan_kernel(u_ref, wq_ref, ak_ref, egl_ref, z_ref, og_ref, o_ref, s_ref):
    C = GDN_CHUNK

    @pl.when(pl.program_id(1) == 0)
    def _():
        s_ref[...] = jnp.zeros(s_ref.shape, F32)

    og = og_ref[...]
    for h in range(GDN_HEADS):
        state = s_ref[h]
        ws = _dot(wq_ref[h], state.astype(BF16))
        v_new = u_ref[h] - ws[:C]
        av = _dot(ak_ref[h], v_new.astype(BF16))
        o = ws[C:] + av[:C]
        decayed = (state.reshape(HEAD_DIM // 8, 8, HEAD_DIM) * egl_ref[h][None]).reshape(HEAD_DIM, HEAD_DIM)
        s_ref[h] = decayed + av[C:]
        zh = z_ref[:, h * HEAD_DIM:(h + 1) * HEAD_DIM]
        o_ref[:, h * HEAD_DIM:(h + 1) * HEAD_DIM] = (_rms(o, og) * (zh * jax.nn.sigmoid(zh))).astype(o_ref.dtype)


def gdn_scan(u, wq, ak, egl, z, out_gain, B, S):
    C, H = GDN_CHUNK, GDN_HEADS
    N = S // C
    in5 = lambda r, cdim: pl.BlockSpec((None, None, H, r, cdim), lambda b, n: (b, n, 0, 0, 0))
    return pl.pallas_call(
        _gdn_scan_kernel,
        grid=(B, N),
        in_specs=[in5(C, HEAD_DIM), in5(2 * C, HEAD_DIM), in5(C + HEAD_DIM, C), in5(8, HEAD_DIM),
                  pl.BlockSpec((C, D_GDN), lambda b, n: (b * N + n, COL_Z_B // D_GDN)),
                  pl.BlockSpec((1, HEAD_DIM), lambda b, n: (0, 0))],
        out_specs=pl.BlockSpec((C, D_GDN), lambda b, n: (b * N + n, 0)),
        out_shape=jax.ShapeDtypeStruct((B * S, D_GDN), BF16),
        scratch_shapes=[pltpu.VMEM((H, HEAD_DIM, HEAD_DIM), F32)],
        compiler_params=_params(("parallel", "arbitrary")),
        name="gdn_scan",
    )(u, wq, ak, egl, z, out_gain.reshape(1, HEAD_DIM))


def _xa_kv_kernel(mem_ref, g_ref, w_ref, kg_ref, k_ref, v_ref):
    kv = _dot(_rms(mem_ref[...], g_ref[...]).astype(BF16), w_ref[...])
    for h in range(XA_HEADS):
        k_ref[h] = _rms(kv[:, h * HEAD_DIM:(h + 1) * HEAD_DIM], kg_ref[...]).astype(BF16)
        v_ref[h] = kv[:, D_XA + h * HEAD_DIM: D_XA + (h + 1) * HEAD_DIM].astype(BF16)


def xa_kv(mem, mem_norm, wkv_bf16, k_gain):
    B, Mm, D = mem.shape
    spec = pl.BlockSpec((None, XA_HEADS, Mm, HEAD_DIM), lambda b: (b, 0, 0, 0))
    shape = jax.ShapeDtypeStruct((B, XA_HEADS, Mm, HEAD_DIM), BF16)
    return pl.pallas_call(
        _xa_kv_kernel,
        grid=(B,),
        in_specs=[pl.BlockSpec((None, Mm, D), lambda b: (b, 0, 0)), pl.BlockSpec((1, D), lambda b: (0, 0)),
                  pl.BlockSpec((D, 2 * D_XA), lambda b: (0, 0)), pl.BlockSpec((1, HEAD_DIM), lambda b: (0, 0))],
        out_specs=[spec, spec],
        out_shape=[shape, shape],
        compiler_params=_params(("parallel",)),
        name="xa_kv",
    )(mem, mem_norm.reshape(1, D), wkv_bf16, k_gain.reshape(1, HEAD_DIM))


def _xa_kernel(x_ref, g_ref, wq_ref, k_ref, v_ref, qg_ref, wo_ref, o_ref):
    x = x_ref[...]
    q = _dot(_rms(x, g_ref[...]).astype(BF16), wq_ref[...])
    outs = []
    for h in range(XA_HEADS):
        qh = _rms(q[:, h * HEAD_DIM:(h + 1) * HEAD_DIM], qg_ref[...]).astype(BF16)
        s = _dot_nt(qh, k_ref[h]) * SCALE
        p = jnp.exp(s - jnp.max(s, axis=-1, keepdims=True))
        p = p / jnp.sum(p, axis=-1, keepdims=True)
        outs.append(_dot(p.astype(BF16), v_ref[h]))
    o = jnp.concatenate(outs, axis=1).astype(BF16)
    o_ref[...] = x + _dot(o, wo_ref[...])


def cross_attention(x, xa_norm, wq_bf16, k, v, q_gain, wo_bf16, B, S, tm=256):
    M, D = x.shape
    Mm = k.shape[2]
    per_b = S // tm
    kv_spec = pl.BlockSpec((None, XA_HEADS, Mm, HEAD_DIM), lambda i: (i // per_b, 0, 0, 0))
    const = lambda i: (0, 0)
    return pl.pallas_call(
        _xa_kernel,
        grid=(M // tm,),
        in_specs=[pl.BlockSpec((tm, D), lambda i: (i, 0)), pl.BlockSpec((1, D), const),
                  pl.BlockSpec((D, D_XA), const), kv_spec, kv_spec,
                  pl.BlockSpec((1, HEAD_DIM), const), pl.BlockSpec((D_XA, D), const)],
        out_specs=pl.BlockSpec((tm, D), lambda i: (i, 0)),
        out_shape=jax.ShapeDtypeStruct((M, D), F32),
        compiler_params=_params(("parallel",)),
        name="cross_attention",
    )(x, xa_norm.reshape(1, D), wq_bf16, k, v, q_gain.reshape(1, HEAD_DIM), wo_bf16)


def _reorder_w_in(w_in):
    o = np.cumsum((D_NSA, 6 * D_KV_NSA, 3 * NSA_HEADS, 3 * D_GDN, GDN_HEADS, GDN_HEADS, D_GDN))
    q_a, kv_a, gate_a, qkv_b = (0, o[0]), (o[0], o[1]), o[1], (o[2], o[3])
    beta_b, a_b, z_b = o[3], o[4], (o[5], o[6])
    Hg = NSA_GROUP_SIZE
    src = np.zeros((256,), np.int64)
    used = np.zeros((256,), bool)
    for g in range(NSA_KV_GROUPS):
        for br in range(3):
            for h in range(Hg):
                src[g * HEAD_DIM + br * Hg + h] = gate_a + (g * Hg + h) * 3 + br
                used[g * HEAD_DIM + br * Hg + h] = True
    for h in range(GDN_HEADS):
        src[SM_BETA + h] = beta_b + h
        src[SM_A + h] = a_b + h
        used[SM_BETA + h] = used[SM_A + h] = True
    small = jnp.where(jnp.asarray(used)[None, :], jnp.take(w_in, jnp.asarray(src), axis=1), 0.0)
    parts = [w_in[:, qkv_b[0]:qkv_b[1]], w_in[:, q_a[0]:q_a[1]], w_in[:, z_b[0]:z_b[1]], w_in[:, kv_a[0]:kv_a[1]], small]
    return jnp.concatenate([p.astype(BF16) for p in parts], axis=1)


def mixer_layer(x2, B, S, attn_norm, w_in, nsa_q_gain, nsa_k_gain, nsa_cmp_pos, nsa_cmp_w1, nsa_cmp_w2, nsa_out_gain,
                gdn_conv_w, gdn_A_log, gdn_dt_bias, gdn_out_gain, w_out, tm=1024):
    h = rmsnorm_bf16(x2, attn_norm)
    z = matmul_cols(h, _reorder_w_in(w_in), tm=tm, tn=256, name="in_proj")
    raw, ks, vs, kw, vw = nsa_prep(z, nsa_k_gain, B, S)
    comp = nsa_compress(raw, nsa_cmp_pos, nsa_cmp_w1, nsa_cmp_w2, nsa_k_gain[0], B, S)
    y_a = nsa_attention(z, comp, ks, vs, kw, vw, nsa_q_gain, nsa_out_gain, B, S)
    u, wq, ak, egl = gdn_chunks(z, gdn_conv_w, gdn_A_log, gdn_dt_bias, B, S)
    y_b = gdn_scan(u, wq, ak, egl, z, gdn_out_gain, B, S)
    return matmul2_residual(y_a, y_b, w_out, x2, tm=tm, tn=256, name="out_proj")


def xa_layer(x2, mem, B, S, xa_norm, mem_norm, xa_wq, xa_wkv, xa_q_gain, xa_k_gain, xa_wo):
    k, v = xa_kv(mem, mem_norm, xa_wkv.astype(BF16), xa_k_gain)
    return cross_attention(x2, xa_norm, xa_wq.astype(BF16), k, v, xa_q_gain, xa_wo.astype(BF16), B, S)


def dense_ffn_layer(x2, ffn_norm, w13, w2, tm=1024):
    h = rmsnorm_bf16(x2, ffn_norm)
    act = swiglu_up(h, w13, tm=tm, tn=256)
    return down_proj_residual(act, w2, x2, tm=tm, tn=2048, tk=256)


def moe_ffn_layer(x2, ffn_norm, router_w, w13, w2, tm=256, tn=512):
    h, idx, wts = rmsnorm_router(x2, ffn_norm, router_w)
    src_tok, row_w, pos, tile_expert, n_valid = moe_dispatch(idx[:, :TOP_K], wts[:, :TOP_K], tm)
    xs = jnp.take(h, src_tok, axis=0)
    act = moe_up(xs, w13, tile_expert, n_valid, tm=tm, tn=tn)
    row_w_rep = jnp.broadcast_to(row_w[:, None], (row_w.shape[0], 128))
    out = moe_down(act, w2, row_w_rep, tile_expert, n_valid, tm=tm, tn=tn)
    return x2 + (jnp.take(out, pos[:, 0], axis=0) + jnp.take(out, pos[:, 1], axis=0))


def kernel(x, mem, attn_norm, w_in, nsa_q_gain, nsa_k_gain, nsa_cmp_pos, nsa_cmp_w1, nsa_cmp_w2, nsa_out_gain, gdn_conv_w, gdn_A_log, gdn_dt_bias, gdn_out_gain, w_out, xa_norm, mem_norm, xa_wq, xa_wkv, xa_q_gain, xa_k_gain, xa_wo, ffn_norm, dense_w13, dense_w2, router_w, moe_w13, moe_w2):
    B, S, D = x.shape
    x2 = x.reshape(B * S, D)
    for l in range(attn_norm.shape[0]):
        x2 = mixer_layer(x2, B, S, attn_norm[l], w_in[l], nsa_q_gain[l], nsa_k_gain[l], nsa_cmp_pos[l], nsa_cmp_w1[l],
                         nsa_cmp_w2[l], nsa_out_gain[l], gdn_conv_w[l], gdn_A_log[l], gdn_dt_bias[l], gdn_out_gain[l],
                         w_out[l])
        x2 = xa_layer(x2, mem, B, S, xa_norm[l], mem_norm[l], xa_wq[l], xa_wkv[l], xa_q_gain[l], xa_k_gain[l], xa_wo[l])
        if l % 2 == 0:
            x2 = dense_ffn_layer(x2, ffn_norm[l], dense_w13[l // 2], dense_w2[l // 2])
        else:
            x2 = moe_ffn_layer(x2, ffn_norm[l], router_w[l // 2], moe_w13[l // 2], moe_w2[l // 2])
    return x2.reshape(B, S, D)
```

```python
import functools

import jax
import jax.numpy as jnp
import numpy as np
from jax import lax
from jax.experimental import pallas as pl
from jax.experimental.pallas import tpu as pltpu

F32 = jnp.float32
BF16 = jnp.bfloat16
I32 = jnp.int32

D_MODEL = 4096
HEAD_DIM = 128
EPS = 1e-6
SCALE = HEAD_DIM ** -0.5

NSA_HEADS = 16
NSA_KV_GROUPS = 2
NSA_GROUP_SIZE = NSA_HEADS // NSA_KV_GROUPS
CMP_BLOCK = 32
CMP_STRIDE = 16
SEL_BLOCK = 64
SEL_TOP_N = 16
SEL_LOCAL = 2
WINDOW = 512
Q_BLOCK = 128
FORCE = 1e9
NEG = -1e30
SEL_TILE = 256
LOG2E = 1.4426950408889634

GDN_HEADS = 16
GDN_CONV = 4
GDN_CHUNK = 64
GDN_PACK = 4
GDN_SUB = 16

D_NSA = NSA_HEADS * HEAD_DIM
D_KV_NSA = NSA_KV_GROUPS * HEAD_DIM
D_GDN = GDN_HEADS * HEAD_DIM
D_MIX = D_NSA + D_GDN

XA_HEADS = 4
D_XA = XA_HEADS * HEAD_DIM
N_EXPERTS = 8
TOP_K = 2

COL_QKV_B = 0
COL_Q_A = COL_QKV_B + 3 * D_GDN
COL_Z_B = COL_Q_A + D_NSA
COL_KV_A = COL_Z_B + D_GDN
COL_SMALL = COL_KV_A + 6 * D_KV_NSA
N_IN_R = COL_SMALL + 256
SM_BETA = 32
SM_A = 48

VMEM_LIMIT = 58 * 1024 * 1024

NT_DIMS = (((1,), (1,)), ((), ()))


def _rms(x, gain):
    return x * lax.rsqrt(jnp.mean(x * x, axis=-1, keepdims=True) + EPS) * gain


def _dot(a, b):
    return jnp.dot(a, b, preferred_element_type=F32)


def _dot_nt(a, b):
    return lax.dot_general(a, b, NT_DIMS, preferred_element_type=F32)


def _dot3(x, sel):
    hi = x.astype(BF16)
    rem = x - hi.astype(F32)
    mid = rem.astype(BF16)
    lo = (rem - mid.astype(F32)).astype(BF16)
    return _dot(hi, sel) + _dot(mid, sel) + _dot(lo, sel)


def _params(sem, vmem=VMEM_LIMIT):
    return pltpu.CompilerParams(dimension_semantics=sem, vmem_limit_bytes=vmem)


def _norm_kernel(x_ref, g_ref, o_ref):
    o_ref[...] = _rms(x_ref[...], g_ref[...]).astype(o_ref.dtype)


def rmsnorm_bf16(x, gain, tm=512):
    M, D = x.shape
    return pl.pallas_call(
        _norm_kernel,
        grid=(M // tm,),
        in_specs=[pl.BlockSpec((tm, D), lambda i: (i, 0)), pl.BlockSpec((1, D), lambda i: (0, 0))],
        out_specs=pl.BlockSpec((tm, D), lambda i: (i, 0)),
        out_shape=jax.ShapeDtypeStruct((M, D), BF16),
        compiler_params=_params(("parallel",)),
        name="rmsnorm",
    )(x, gain.reshape(1, D))


def _norm_router_kernel(x_ref, g_ref, rw_ref, o_ref, idx_ref, wt_ref):
    h = _rms(x_ref[...], g_ref[...])
    o_ref[...] = h.astype(o_ref.dtype)
    logits = jnp.dot(h, rw_ref[...], preferred_element_type=F32, precision=lax.Precision.HIGHEST)
    lane = lax.broadcasted_iota(I32, logits.shape, 1).astype(F32)
    logits = jnp.where(lane < N_EXPERTS, logits, -jnp.inf)
    m1 = jnp.max(logits, axis=-1, keepdims=True)
    i1 = jnp.min(jnp.where(logits == m1, lane, 128.0), axis=-1, keepdims=True)
    rest = jnp.where(lane == i1, -jnp.inf, logits)
    m2 = jnp.max(rest, axis=-1, keepdims=True)
    i2 = jnp.min(jnp.where(rest == m2, lane, 128.0), axis=-1, keepdims=True)
    e2 = jnp.exp(m2 - m1)
    den = 1.0 + e2
    idx_ref[...] = jnp.where(lane == 0, i1, jnp.where(lane == 1, i2, 0.0)).astype(I32)
    wt_ref[...] = jnp.where(lane == 0, 1.0 / den, jnp.where(lane == 1, e2 / den, 0.0))


def rmsnorm_router(x, gain, router_w, tm=256):
    M, D = x.shape
    rw = jnp.pad(router_w, ((0, 0), (0, 128 - N_EXPERTS)))
    row = lambda i: (i, 0)
    return pl.pallas_call(
        _norm_router_kernel,
        grid=(M // tm,),
        in_specs=[pl.BlockSpec((tm, D), row), pl.BlockSpec((1, D), lambda i: (0, 0)),
                  pl.BlockSpec((D, 128), lambda i: (0, 0))],
        out_specs=[pl.BlockSpec((tm, D), row), pl.BlockSpec((tm, 128), row), pl.BlockSpec((tm, 128), row)],
        out_shape=[jax.ShapeDtypeStruct((M, D), BF16), jax.ShapeDtypeStruct((M, 128), I32),
                   jax.ShapeDtypeStruct((M, 128), F32)],
        compiler_params=_params(("parallel",)),
        name="rmsnorm_router",
    )(x, gain.reshape(1, D), rw)


def _mm_kernel(x_ref, w_ref, o_ref):
    o_ref[...] = _dot(x_ref[...], w_ref[...].astype(BF16)).astype(o_ref.dtype)


def _mm_res_kernel(x_ref, w_ref, r_ref, o_ref):
    o_ref[...] = r_ref[...] + _dot(x_ref[...], w_ref[...].astype(BF16))


def matmul_cols(x, w, *, tm, tn, out_dtype=F32, residual=None, name="matmul"):
    M, K = x.shape
    N = w.shape[1]
    in_specs = [pl.BlockSpec((tm, K), lambda i, j: (i, 0)), pl.BlockSpec((K, tn), lambda i, j: (0, j))]
    args = [x, w]
    kern = _mm_kernel
    if residual is not None:
        in_specs.append(pl.BlockSpec((tm, tn), lambda i, j: (i, j)))
        args.append(residual)
        kern = _mm_res_kernel
    return pl.pallas_call(
        kern,
        grid=(M // tm, N // tn),
        in_specs=in_specs,
        out_specs=pl.BlockSpec((tm, tn), lambda i, j: (i, j)),
        out_shape=jax.ShapeDtypeStruct((M, N), out_dtype),
        compiler_params=_params(("parallel", "parallel")),
        name=name,
    )(*args)


def _mm2_res_kernel(xa_ref, xb_ref, wa_ref, wb_ref, r_ref, o_ref):
    acc = _dot(xa_ref[...], wa_ref[...].astype(BF16)) + _dot(xb_ref[...], wb_ref[...].astype(BF16))
    o_ref[...] = r_ref[...] + acc


def matmul2_residual(xa, xb, w, residual, *, tm, tn, name):
    M, Ka = xa.shape
    Kb = xb.shape[1]
    assert Ka == Kb and w.shape[0] == Ka + Kb
    N = w.shape[1]
    return pl.pallas_call(
        _mm2_res_kernel,
        grid=(M // tm, N // tn),
        in_specs=[pl.BlockSpec((tm, Ka), lambda i, j: (i, 0)), pl.BlockSpec((tm, Kb), lambda i, j: (i, 0)),
                  pl.BlockSpec((Ka, tn), lambda i, j: (0, j)), pl.BlockSpec((Kb, tn), lambda i, j: (1, j)),
                  pl.BlockSpec((tm, tn), lambda i, j: (i, j))],
        out_specs=pl.BlockSpec((tm, tn), lambda i, j: (i, j)),
        out_shape=jax.ShapeDtypeStruct((M, N), F32),
        compiler_params=_params(("parallel", "parallel")),
        name=name,
    )(xa, xb, w, w, residual)


def _swiglu_up_kernel(x_ref, wg_ref, wu_ref, o_ref):
    x = x_ref[...]
    gate = _dot(x, wg_ref[...].astype(BF16))
    up = _dot(x, wu_ref[...].astype(BF16))
    o_ref[...] = (gate * jax.nn.sigmoid(gate) * up).astype(o_ref.dtype)


def swiglu_up(x, w13, *, tm, tn):
    M, K = x.shape
    F = w13.shape[1] // 2
    nj = F // tn
    return pl.pallas_call(
        _swiglu_up_kernel,
        grid=(M // tm, nj),
        in_specs=[pl.BlockSpec((tm, K), lambda i, j: (i, 0)),
                  pl.BlockSpec((K, tn), lambda i, j: (0, j)),
                  pl.BlockSpec((K, tn), lambda i, j: (0, j + nj))],
        out_specs=pl.BlockSpec((tm, tn), lambda i, j: (i, j)),
        out_shape=jax.ShapeDtypeStruct((M, F), BF16),
        compiler_params=_params(("parallel", "parallel")),
        name="swiglu_up",
    )(x, w13, w13)


def _down_res_kernel(a_ref, w_ref, at_ref, wt_ref, r_ref, o_ref, *, n_main):
    k = pl.program_id(2)

    @pl.when(k == 0)
    def _():
        o_ref[...] = r_ref[...]

    @pl.when(k < n_main)
    def _():
        o_ref[...] += _dot(a_ref[...], w_ref[...].astype(BF16))

    @pl.when(k >= n_main)
    def _():
        o_ref[...] += _dot(at_ref[...], wt_ref[...].astype(BF16))


def down_proj_residual(a, w2, residual, *, tm, tn, tk, tk_tail):
    M, F = a.shape
    N = w2.shape[1]
    n_main = F // tk
    tail = F - n_main * tk
    assert tail % tk_tail == 0 and (n_main * tk) % tk_tail == 0
    n_tail = tail // tk_tail
    first_tail = (n_main * tk) // tk_tail if n_tail else 0
    main_k = lambda k: jnp.minimum(k, n_main - 1)
    tail_k = lambda k: first_tail + jnp.maximum(k - n_main, 0)
    return pl.pallas_call(
        functools.partial(_down_res_kernel, n_main=n_main),
        grid=(M // tm, N // tn, n_main + n_tail),
        in_specs=[pl.BlockSpec((tm, tk), lambda i, j, k: (i, main_k(k))),
                  pl.BlockSpec((tk, tn), lambda i, j, k: (main_k(k), j)),
                  pl.BlockSpec((tm, tk_tail), lambda i, j, k: (i, tail_k(k))),
                  pl.BlockSpec((tk_tail, tn), lambda i, j, k: (tail_k(k), j)),
                  pl.BlockSpec((tm, tn), lambda i, j, k: (i, j))],
        out_specs=pl.BlockSpec((tm, tn), lambda i, j, k: (i, j)),
        out_shape=jax.ShapeDtypeStruct((M, N), F32),
        compiler_params=_params(("parallel", "parallel", "arbitrary")),
        name="down_proj",
    )(a, w2, a, w2, residual)


def _new_weights(te_ref, i):
    return jnp.logical_or(i == 0, te_ref[i] != te_ref[jnp.maximum(i - 1, 0)])


def _moe_up_kernel(te_ref, nv_ref, x_ref, wg_ref, wu_ref, o_ref, wgb_ref, wub_ref):
    i = pl.program_id(1)
    valid = i < nv_ref[0]

    @pl.when(_new_weights(te_ref, i))
    def _():
        wgb_ref[...] = wg_ref[...].astype(BF16)
        wub_ref[...] = wu_ref[...].astype(BF16)

    @pl.when(valid)
    def _():
        x = x_ref[...]
        gate = _dot(x, wgb_ref[...])
        up = _dot(x, wub_ref[...])
        o_ref[...] = (gate * jax.nn.sigmoid(gate) * up).astype(o_ref.dtype)

    @pl.when(jnp.logical_not(valid))
    def _():
        o_ref[...] = jnp.zeros_like(o_ref)


def moe_up(xs, w13, tile_expert, n_valid, *, tm, tn):
    P, K = xs.shape
    F = w13.shape[2] // 2
    nj = F // tn
    return pl.pallas_call(
        _moe_up_kernel,
        grid_spec=pltpu.PrefetchScalarGridSpec(
            num_scalar_prefetch=2,
            grid=(nj, P // tm),
            in_specs=[pl.BlockSpec((tm, K), lambda j, i, te, nv: (jnp.minimum(i, nv[0] - 1), 0)),
                      pl.BlockSpec((None, K, tn), lambda j, i, te, nv: (te[i], 0, j)),
                      pl.BlockSpec((None, K, tn), lambda j, i, te, nv: (te[i], 0, j + nj))],
            out_specs=pl.BlockSpec((tm, tn), lambda j, i, te, nv: (i, j)),
            scratch_shapes=[pltpu.VMEM((K, tn), BF16), pltpu.VMEM((K, tn), BF16)],
        ),
        out_shape=jax.ShapeDtypeStruct((P, F), BF16),
        compiler_params=_params(("arbitrary", "arbitrary")),
        name="moe_up",
    )(tile_expert, n_valid, xs, w13, w13)


def _moe_down_kernel(te_ref, nv_ref, a_ref, w_ref, rw_ref, o_ref, wb_ref):
    i = pl.program_id(1)
    valid = i < nv_ref[0]

    @pl.when(_new_weights(te_ref, i))
    def _():
        wb_ref[...] = w_ref[...].astype(BF16)

    @pl.when(valid)
    def _():
        o_ref[...] = _dot(a_ref[...], wb_ref[...]) * rw_ref[:, 0:1]

    @pl.when(jnp.logical_not(valid))
    def _():
        o_ref[...] = jnp.zeros_like(o_ref)


def moe_down(act, w2, row_w, tile_expert, n_valid, *, tm, tn):
    P, F = act.shape
    N = w2.shape[2]
    return pl.pallas_call(
        _moe_down_kernel,
        grid_spec=pltpu.PrefetchScalarGridSpec(
            num_scalar_prefetch=2,
            grid=(N // tn, P // tm),
            in_specs=[pl.BlockSpec((tm, F), lambda j, i, te, nv: (jnp.minimum(i, nv[0] - 1), 0)),
                      pl.BlockSpec((None, F, tn), lambda j, i, te, nv: (te[i], 0, j)),
                      pl.BlockSpec((tm, 128), lambda j, i, te, nv: (i, 0))],
            out_specs=pl.BlockSpec((tm, tn), lambda j, i, te, nv: (i, j)),
            scratch_shapes=[pltpu.VMEM((F, tn), BF16)],
        ),
        out_shape=jax.ShapeDtypeStruct((P, N), F32),
        compiler_params=_params(("arbitrary", "arbitrary")),
        name="moe_down",
    )(tile_expert, n_valid, act, w2, row_w)


def moe_dispatch(top_i, top_w, tm):
    M = top_i.shape[0]
    A = M * TOP_K
    P = A + N_EXPERTS * tm
    e_flat = top_i.reshape(A)
    onehot = (e_flat[:, None] == jnp.arange(N_EXPERTS, dtype=I32)[None, :]).astype(I32)
    csum = jnp.cumsum(onehot, axis=0)
    counts = csum[-1]
    padded = ((counts + tm - 1) // tm) * tm
    pend = jnp.cumsum(padded)
    pstart = pend - padded
    dest = jnp.sum(onehot * (pstart[None, :] + csum - onehot), axis=1)
    src_tok = jnp.zeros((P,), I32).at[dest].set(jnp.arange(A, dtype=I32) // TOP_K)
    row_w = jnp.zeros((P,), F32).at[dest].set(top_w.reshape(A))
    pos = dest.reshape(M, TOP_K)
    tile_start = jnp.arange(P // tm, dtype=I32) * tm
    tile_expert = jnp.minimum(jnp.sum((tile_start[:, None] >= pend[None, :]).astype(I32), axis=1), N_EXPERTS - 1)
    n_valid = (pend[-1:] // tm).astype(I32)
    last_e = tile_expert[jnp.maximum(n_valid[0] - 1, 0)]
    tile_expert = jnp.where(tile_start // tm < n_valid[0], tile_expert, last_e)
    return src_tok, row_w, pos, tile_expert, n_valid


def _nsa_prep_kernel(kc_ref, vc_ref, ks_ref, vs_ref, kw_ref, vw_ref, kg_ref, raw_ref, kso_ref, vso_ref, kwo_ref, vwo_ref):
    raw_ref[0] = kc_ref[...]
    raw_ref[1] = vc_ref[...]
    kg = kg_ref[...]
    kso_ref[...] = _rms(ks_ref[...], kg[1:2]).astype(BF16)
    vso_ref[...] = vs_ref[...].astype(BF16)
    kwo_ref[...] = _rms(kw_ref[...], kg[2:3]).astype(BF16)
    vwo_ref[...] = vw_ref[...].astype(BF16)


def nsa_prep(z, k_gain, B, S, ts=512):
    G = NSA_KV_GROUPS
    nt = S // ts
    cb = COL_KV_A // HEAD_DIM

    def col(br, kvi):
        return pl.BlockSpec((ts, HEAD_DIM), lambda b, g, t: (b * nt + t, cb + br * 4 + kvi * 2 + g))

    kv_out = pl.BlockSpec((None, None, ts, HEAD_DIM), lambda b, g, t: (b, g, t, 0))
    kv_shape = jax.ShapeDtypeStruct((B, G, S, HEAD_DIM), BF16)
    return pl.pallas_call(
        _nsa_prep_kernel,
        grid=(B, G, nt),
        in_specs=[col(0, 0), col(0, 1), col(1, 0), col(1, 1), col(2, 0), col(2, 1),
                  pl.BlockSpec((3, HEAD_DIM), lambda b, g, t: (0, 0))],
        out_specs=[pl.BlockSpec((None, 2, None, ts, HEAD_DIM), lambda b, g, t: (b, 0, g, t, 0)),
                   kv_out, kv_out, kv_out, kv_out],
        out_shape=[jax.ShapeDtypeStruct((B, 2, G, S, HEAD_DIM), F32), kv_shape, kv_shape, kv_shape, kv_shape],
        compiler_params=_params(("parallel", "parallel", "parallel")),
        name="nsa_prep",
    )(z, z, z, z, z, z, k_gain)


def _nsa_compress_kernel(r_ref, pos_ref, w1_ref, w2_ref, kg_ref, o_ref):
    kv = pl.program_id(1)
    half = (CMP_BLOCK // 2) * HEAD_DIM
    r = r_ref[...]
    pos = pos_ref[...]
    n = r.shape[0]
    first = _dot((r + pos[:, :half]).astype(BF16), w1_ref[:half, :].astype(BF16))
    second = _dot((r + pos[:, half:]).astype(BF16), w1_ref[half:, :].astype(BF16))
    pre = first + pltpu.roll(second, n - 1, axis=0)
    hid = jax.nn.gelu(pre)
    comp = _dot(hid.astype(BF16), w2_ref[...].astype(BF16))
    o_ref[...] = jnp.where(kv == 0, _rms(comp, kg_ref[...]), comp).astype(o_ref.dtype)


def nsa_compress(raw, cmp_pos, cmp_w1, cmp_w2, k_gain0, B, S):
    G = NSA_KV_GROUPS
    n = S // CMP_STRIDE
    feat = CMP_STRIDE * HEAD_DIM
    r = raw.reshape(B, 2, G, n, feat)
    pos = cmp_pos.reshape(2, 1, CMP_BLOCK * HEAD_DIM)
    return pl.pallas_call(
        _nsa_compress_kernel,
        grid=(B, 2, G),
        in_specs=[pl.BlockSpec((None, None, None, n, feat), lambda b, kv, g: (b, kv, g, 0, 0)),
                  pl.BlockSpec((None, 1, CMP_BLOCK * HEAD_DIM), lambda b, kv, g: (kv, 0, 0)),
                  pl.BlockSpec((None, CMP_BLOCK * HEAD_DIM, HEAD_DIM), lambda b, kv, g: (kv, 0, 0)),
                  pl.BlockSpec((None, HEAD_DIM, HEAD_DIM), lambda b, kv, g: (kv, 0, 0)),
                  pl.BlockSpec((1, HEAD_DIM), lambda b, kv, g: (0, 0))],
        out_specs=pl.BlockSpec((None, None, None, n, HEAD_DIM), lambda b, kv, g: (b, kv, g, 0, 0)),
        out_shape=jax.ShapeDtypeStruct((B, 2, G, n, HEAD_DIM), BF16),
        compiler_params=_params(("parallel", "parallel", "parallel")),
        name="nsa_compress",
    )(r, pos, cmp_w1, cmp_w2, k_gain0.reshape(1, HEAD_DIM))


def _rep(x, n):
    return x if n == HEAD_DIM else jnp.concatenate([x] * (n // HEAD_DIM), axis=1)


def _flash_step(qb, kt, vt, bias, m_ref, l_ref, acc_ref):
    nk = kt.shape[0]
    rows = qb.shape[0]
    s = _dot_nt(qb, kt)
    s = (s.reshape(rows // Q_BLOCK, Q_BLOCK, nk) + bias[None]).reshape(rows, nk)
    m_prev = m_ref[...]
    m_new = jnp.maximum(m_prev, jnp.max(s, axis=-1, keepdims=True))
    alpha = jnp.exp2(m_prev - m_new)
    p = jnp.exp2(s - _rep(m_new, nk))
    l_ref[...] = alpha * l_ref[...] + jnp.sum(p, axis=-1, keepdims=True)
    acc_ref[...] = alpha * acc_ref[...] + _dot(p.astype(BF16), vt)
    m_ref[...] = m_new


def _nsa_kernel(q_ref, sm_ref, kc_ref, vc_ref, ks_ref, vs_ref, kw_ref, vw_ref, qg_ref, og_ref, c2s_ref, exp_ref,
                o_ref, selb_ref, m_ref, l_ref, acc_ref, *, k_top):
    Hg = NSA_GROUP_SIZE
    rows = Hg * Q_BLOCK
    c = pl.program_id(2)
    t0 = c * Q_BLOCK
    n_cmp = kc_ref.shape[0]
    n_sel = c2s_ref.shape[0]

    q = q_ref[...]
    qs = jnp.concatenate([q[:, h * HEAD_DIM:(h + 1) * HEAD_DIM] for h in range(Hg)], axis=0)
    qb = (_rms(qs, qg_ref[...]) * (SCALE * LOG2E)).astype(BF16)

    tq = t0 + lax.broadcasted_iota(I32, (Q_BLOCK, n_cmp), 0)
    c_end = lax.broadcasted_iota(I32, (Q_BLOCK, n_cmp), 1) * CMP_STRIDE + (CMP_BLOCK - 1)
    sc = _dot_nt(qb, kc_ref[...]).reshape(Hg, Q_BLOCK, n_cmp)
    sc = jnp.where((c_end <= tq)[None], sc, -jnp.inf)
    mc = jnp.max(sc, axis=-1, keepdims=True)
    mc = jnp.where(mc > -jnp.inf, mc, 0.0)
    pc = jnp.exp2(sc - mc)
    dc = jnp.sum(pc, axis=-1, keepdims=True)
    pc = pc / jnp.where(dc > 0, dc, 1.0)
    o_cmp = _dot(pc.reshape(rows, n_cmp).astype(BF16), vc_ref[...])

    psum = jnp.sum(pc, axis=0)
    c2s = c2s_ref[...]
    p_hi = psum.astype(BF16)
    rem = psum - p_hi.astype(F32)
    p_mid = rem.astype(BF16)
    p_lo = (rem - p_mid.astype(F32)).astype(BF16)
    imp = _dot_nt(c2s, p_hi) + _dot_nt(c2s, p_mid) + _dot_nt(c2s, p_lo)

    jj = lax.broadcasted_iota(I32, (n_sel, Q_BLOCK), 0)
    cur = lax.shift_right_logical(t0 + lax.broadcasted_iota(I32, (n_sel, Q_BLOCK), 1), 6)
    valid = jj <= cur
    forced = valid & ((jj == 0) | (jj > cur - SEL_LOCAL))
    score = jnp.where(forced, FORCE, jnp.where(valid, imp, -FORCE))
    rank = jnp.zeros((n_sel, Q_BLOCK), F32)
    for j2 in range(n_sel):
        other = score[j2:j2 + 1, :]
        tie = jnp.where(jj > j2, 1.0, 0.0)
        rank = rank + jnp.where(other > score, 1.0, jnp.where(other == score, tie, 0.0))
    sel_t = jnp.where(rank < k_top, 1.0, 0.0)
    if n_sel < Q_BLOCK:
        sel_t = jnp.concatenate([sel_t, jnp.zeros((Q_BLOCK - n_sel, Q_BLOCK), F32)], axis=0)
    sel = sel_t.T[:, :n_sel].astype(BF16)
    sel_keys = _dot(sel, exp_ref[...])
    for i in range(selb_ref.shape[0]):
        selb_ref[i] = (sel_keys[:, i * SEL_TILE:(i + 1) * SEL_TILE] - 1.0) * (-NEG)

    def reset():
        m_ref[...] = jnp.full(m_ref.shape, NEG, F32)
        l_ref[...] = jnp.zeros(l_ref.shape, F32)
        acc_ref[...] = jnp.zeros(acc_ref.shape, F32)

    reset()
    tq_s = t0 + lax.broadcasted_iota(I32, (Q_BLOCK, SEL_TILE), 0)
    kk_s = lax.broadcasted_iota(I32, (Q_BLOCK, SEL_TILE), 1)

    def sel_body(i, carry):
        k0 = pl.multiple_of(i * SEL_TILE, SEL_TILE)
        bias = selb_ref[i] + jnp.where(kk_s + k0 <= tq_s, 0.0, NEG)
        _flash_step(qb, ks_ref[pl.ds(k0, SEL_TILE), :], vs_ref[pl.ds(k0, SEL_TILE), :], bias, m_ref, l_ref, acc_ref)
        return carry

    lax.fori_loop(0, (t0 + Q_BLOCK + SEL_TILE - 1) // SEL_TILE, sel_body, 0)
    o_sel = acc_ref[...] / l_ref[...]

    reset()
    tq_w = t0 + lax.broadcasted_iota(I32, (Q_BLOCK, Q_BLOCK), 0)
    kk_w = lax.broadcasted_iota(I32, (Q_BLOCK, Q_BLOCK), 1)

    def win_body(i, carry):
        k0 = pl.multiple_of((c - i) * Q_BLOCK, Q_BLOCK)
        wpos = kk_w + k0
        ok = (wpos <= tq_w) & (wpos > tq_w - WINDOW)
        bias = jnp.where(ok, 0.0, NEG)
        _flash_step(qb, kw_ref[pl.ds(k0, Q_BLOCK), :], vw_ref[pl.ds(k0, Q_BLOCK), :], bias, m_ref, l_ref, acc_ref)
        return carry

    lax.fori_loop(0, jnp.minimum(c, WINDOW // Q_BLOCK) + 1, win_body, 0)
    o_win = acc_ref[...] / l_ref[...]

    gates = jax.nn.sigmoid(sm_ref[...])
    og = og_ref[...]
    for h in range(Hg):
        r = slice(h * Q_BLOCK, (h + 1) * Q_BLOCK)
        o = (gates[:, h:h + 1] * o_cmp[r] + gates[:, Hg + h:Hg + h + 1] * o_sel[r]
             + gates[:, 2 * Hg + h:2 * Hg + h + 1] * o_win[r])
        o_ref[:, h * HEAD_DIM:(h + 1) * HEAD_DIM] = _rms(o, og).astype(o_ref.dtype)


def nsa_attention(z, comp, ks, vs, kw, vw, q_gain, out_gain, B, S):
    G, Hg = NSA_KV_GROUPS, NSA_GROUP_SIZE
    nq = S // Q_BLOCK
    n_cmp = S // CMP_STRIDE
    n_sel = S // SEL_BLOCK
    k_top = min(SEL_TOP_N, n_sel)
    c_start = np.arange(n_cmp)[:, None] * CMP_STRIDE
    s_start = np.arange(n_sel)[None, :] * SEL_BLOCK
    overlap = np.minimum(c_start + CMP_BLOCK, s_start + SEL_BLOCK) - np.maximum(c_start, s_start)
    assert n_sel <= Q_BLOCK
    c2s = jnp.asarray((np.clip(overlap, 0, None) / CMP_STRIDE).T, dtype=BF16)
    expand = jnp.asarray(np.arange(S)[None, :] // SEL_BLOCK == np.arange(n_sel)[:, None], dtype=BF16)
    rows = Hg * Q_BLOCK
    wq = Hg * HEAD_DIM
    kv_spec = pl.BlockSpec((None, None, S, HEAD_DIM), lambda b, g, c: (b, g, 0, 0))
    const = lambda b, g, c: (0, 0)
    return pl.pallas_call(
        functools.partial(_nsa_kernel, k_top=k_top),
        grid=(B, G, nq),
        in_specs=[pl.BlockSpec((Q_BLOCK, wq), lambda b, g, c: (b * nq + c, COL_Q_A // wq + g)),
                  pl.BlockSpec((Q_BLOCK, HEAD_DIM), lambda b, g, c: (b * nq + c, COL_SMALL // HEAD_DIM + g)),
                  pl.BlockSpec((None, None, None, n_cmp, HEAD_DIM), lambda b, g, c: (b, 0, g, 0, 0)),
                  pl.BlockSpec((None, None, None, n_cmp, HEAD_DIM), lambda b, g, c: (b, 1, g, 0, 0)),
                  kv_spec, kv_spec, kv_spec, kv_spec,
                  pl.BlockSpec((1, HEAD_DIM), const), pl.BlockSpec((1, HEAD_DIM), const),
                  pl.BlockSpec((n_sel, n_cmp), const), pl.BlockSpec((n_sel, S), const)],
        out_specs=pl.BlockSpec((Q_BLOCK, wq), lambda b, g, c: (b * nq + c, g)),
        out_shape=jax.ShapeDtypeStruct((B * S, D_NSA), BF16),
        scratch_shapes=[pltpu.VMEM((S // SEL_TILE, Q_BLOCK, SEL_TILE), F32),
                        pltpu.VMEM((rows, HEAD_DIM), F32), pltpu.VMEM((rows, HEAD_DIM), F32),
                        pltpu.VMEM((rows, HEAD_DIM), F32)],
        compiler_params=_params(("parallel", "parallel", "arbitrary")),
        name="nsa_attention",
    )(z, z, comp, comp, ks, vs, kw, vw, q_gain.reshape(1, HEAD_DIM), out_gain.reshape(1, HEAD_DIM), c2s, expand)


def _gdn_chunk_kernel(xc_ref, xh_ref, sm_ref, cw_ref, alog_ref, dtb_ref, rep_ref, unfold_ref, u_ref, wq_ref, ak_ref,
                      egl_ref):
    C = GDN_CHUNK
    PK = GDN_PACK
    R = PK * C
    n = pl.program_id(1)
    xp = jnp.concatenate([jnp.where(n > 0, xh_ref[...], 0.0), xc_ref[...]], axis=0)
    cw = cw_ref[...]
    y = None
    for j in range(GDN_CONV):
        shift = GDN_CONV - 1 - j
        xs = xp if shift == 0 else pltpu.roll(xp, shift, axis=0)
        term = xs[8:] * cw[j:j + 1, :]
        y = term if y is None else y + term
    y = y * jax.nn.sigmoid(y)

    sm = sm_ref[...]
    beta = jax.nn.sigmoid(sm)
    g = -jnp.exp(alog_ref[...]) * jax.nn.softplus(sm + dtb_ref[...])
    row = lax.broadcasted_iota(I32, g.shape, 0)
    gc = g
    d = 1
    while d < C:
        gc = gc + jnp.where(row >= d, pltpu.roll(gc, d, axis=0), 0.0)
        d *= 2
    g_last = gc[C - 1:C, :]

    ri = lax.broadcasted_iota(I32, (R, R), 0)
    ci = lax.broadcasted_iota(I32, (R, R), 1)
    same = lax.shift_right_logical(ri, 6) == lax.shift_right_logical(ci, 6)
    tri = same & (ri >= ci)
    strict = same & (ri > ci)
    SUB = GDN_SUB
    nb = R // SUB
    same16 = lax.shift_right_logical(ri, 4) == lax.shift_right_logical(ci, 4)
    same32 = lax.shift_right_logical(ri, 5) == lax.shift_right_logical(ci, 5)
    off32 = same32 & jnp.logical_not(same16)
    off64 = jnp.logical_not(same32)
    NG = GDN_HEADS // PK
    W4 = NG * SUB
    row_c = lax.broadcasted_iota(I32, (R, W4), 0) & (SUB - 1)
    col_s = lax.broadcasted_iota(I32, (nb, W4), 1) & (SUB - 1)

    groups = []
    for grp in range(NG):
        heads = [grp * PK + i for i in range(PK)]

        def stack(off):
            return jnp.concatenate([y[:, off + h * HEAD_DIM: off + (h + 1) * HEAD_DIM] for h in heads], axis=0)

        def col(x, base):
            return jnp.concatenate([x[:, base + h: base + h + 1] for h in heads], axis=0)

        q4 = stack(0)
        k4 = stack(D_GDN)
        v4 = stack(2 * D_GDN)
        q4 = q4 * lax.rsqrt(jnp.sum(q4 * q4, axis=-1, keepdims=True) + EPS) * SCALE
        k4 = k4 * lax.rsqrt(jnp.sum(k4 * k4, axis=-1, keepdims=True) + EPS)
        beta4 = col(beta, SM_BETA)
        gc4 = col(gc, SM_A)
        gl4 = jnp.concatenate([jnp.broadcast_to(g_last[:, SM_A + h: SM_A + h + 1], (C, 1)) for h in heads], axis=0)

        gb = jnp.broadcast_to(gc4, (R, R))
        decay = jnp.exp(jnp.where(tri, gb - gb.T, -jnp.inf))
        kb4 = k4 * beta4
        k4b = k4.astype(BF16)
        a = jnp.where(strict, _dot_nt(kb4.astype(BF16), k4b) * decay, 0.0)
        attn = _dot_nt(q4.astype(BF16), k4b) * decay
        at = a.T
        dct = jnp.concatenate([at[SUB * b:SUB * (b + 1), SUB * b:SUB * (b + 1)] for b in range(nb)], axis=0)
        groups.append((heads, q4, k4, v4, beta4, gc4, gl4, kb4, a, attn, dct))

    coef = _dot3(jnp.concatenate([g[-1] for g in groups], axis=1), rep_ref[...])
    t4 = jnp.where(row_c == (lax.broadcasted_iota(I32, (R, W4), 1) & (SUB - 1)), 1.0, 0.0)
    for i in range(1, SUB):
        s = jnp.sum((coef[:, i * HEAD_DIM: i * HEAD_DIM + W4] * t4).reshape(nb, SUB, W4), axis=1)
        new = jnp.where(col_s == i, 1.0, 0.0) - s
        t4 = jnp.where(row_c == i, jnp.broadcast_to(new[:, None, :], (nb, SUB, W4)).reshape(R, W4), t4)
    t_tiled = _dot3(t4, unfold_ref[...])

    for gi, (heads, q4, k4, v4, beta4, gc4, gl4, kb4, a, attn, _) in enumerate(groups):
        t16 = jnp.where(same16, t_tiled[:, gi * R:(gi + 1) * R], 0.0)
        t16b = t16.astype(BF16)
        a32 = jnp.where(off32, a, 0.0).astype(BF16)
        t32 = t16 - _dot(_dot(t16b, a32).astype(BF16), t16b)
        t32b = t32.astype(BF16)
        a64 = jnp.where(off64, a, 0.0).astype(BF16)
        t64 = t32 - _dot(_dot(t32b, a64).astype(BF16), t32b)
        eg = jnp.exp(gc4)
        rhs = jnp.concatenate([v4 * beta4, kb4 * eg], axis=1)
        rhs = _dot(t64.astype(BF16), rhs.astype(BF16))
        qd4 = q4 * eg
        kdt = (k4 * jnp.exp(gl4 - gc4)).T
        for i, h in enumerate(heads):
            r = slice(i * C, (i + 1) * C)
            u_ref[h] = rhs[r, :HEAD_DIM]
            wq_ref[h] = jnp.concatenate([rhs[r, HEAD_DIM:], qd4[r]], axis=0).astype(BF16)
            ak_ref[h] = jnp.concatenate([attn[r, r], kdt[:, r]], axis=0).astype(BF16)
            egl_ref[h] = jnp.broadcast_to(jnp.exp(g_last[:, SM_A + h: SM_A + h + 1]), (8, HEAD_DIM))


def gdn_chunks(z, conv_w, a_log, dt_bias, B, S):
    C, H = GDN_CHUNK, GDN_HEADS
    N = S // C
    W = 3 * D_GDN
    pad = lambda v: jnp.zeros((1, HEAD_DIM), F32).at[0, SM_A:SM_A + H].set(v)
    out5 = lambda r, cdim: pl.BlockSpec((None, None, H, r, cdim), lambda b, n: (b, n, 0, 0, 0))
    NG, SUB, R = H // GDN_PACK, GDN_SUB, GDN_PACK * C
    W4 = NG * SUB
    src = np.arange(W4)[:, None]
    dst = np.arange(SUB * HEAD_DIM)[None, :]
    lane = dst % HEAD_DIM
    rep = (lane < W4) & (src // SUB == lane // SUB) & (src % SUB == dst // HEAD_DIM)
    dst = np.arange(NG * R)[None, :]
    unfold = (src // SUB == dst // R) & (src % SUB == dst % SUB)
    return pl.pallas_call(
        _gdn_chunk_kernel,
        grid=(B, N),
        in_specs=[pl.BlockSpec((C, W), lambda b, n: (b * N + n, 0)),
                  pl.BlockSpec((8, W), lambda b, n: (jnp.maximum((b * N + n) * (C // 8) - 1, 0), 0)),
                  pl.BlockSpec((C, HEAD_DIM), lambda b, n: (b * N + n, COL_SMALL // HEAD_DIM)),
                  pl.BlockSpec((GDN_CONV, W), lambda b, n: (0, 0)),
                  pl.BlockSpec((1, HEAD_DIM), lambda b, n: (0, 0)),
                  pl.BlockSpec((1, HEAD_DIM), lambda b, n: (0, 0)),
                  pl.BlockSpec((W4, SUB * HEAD_DIM), lambda b, n: (0, 0)),
                  pl.BlockSpec((W4, NG * R), lambda b, n: (0, 0))],
        out_specs=[out5(C, HEAD_DIM), out5(2 * C, HEAD_DIM), out5(C + HEAD_DIM, C), out5(8, HEAD_DIM)],
        out_shape=[jax.ShapeDtypeStruct((B, N, H, C, HEAD_DIM), F32),
                   jax.ShapeDtypeStruct((B, N, H, 2 * C, HEAD_DIM), BF16),
                   jax.ShapeDtypeStruct((B, N, H, C + HEAD_DIM, C), BF16),
                   jax.ShapeDtypeStruct((B, N, H, 8, HEAD_DIM), F32)],
        compiler_params=_params(("parallel", "parallel")),
        name="gdn_chunks",
    )(z, z, z, conv_w, pad(a_log), pad(dt_bias), jnp.asarray(rep, dtype=BF16), jnp.asarray(unfold, dtype=BF16))


def _gdn_scan_kernel(u_ref, wq_ref, ak_ref, egl_ref, z_ref, og_ref, o_ref, s_ref):
    C = GDN_CHUNK

    @pl.when(pl.program_id(1) == 0)
    def _():
        s_ref[...] = jnp.zeros(s_ref.shape, F32)

    og = og_ref[...]
    for h in range(GDN_HEADS):
        state = s_ref[h]
        ws = _dot(wq_ref[h], state.astype(BF16))
        v_new = u_ref[h] - ws[:C]
        av = _dot(ak_ref[h], v_new.astype(BF16))
        o = ws[C:] + av[:C]
        decayed = (state.reshape(HEAD_DIM // 8, 8, HEAD_DIM) * egl_ref[h][None]).reshape(HEAD_DIM, HEAD_DIM)
        s_ref[h] = decayed + av[C:]
        zh = z_ref[:, h * HEAD_DIM:(h + 1) * HEAD_DIM]
        o_ref[:, h * HEAD_DIM:(h + 1) * HEAD_DIM] = (_rms(o, og) * (zh * jax.nn.sigmoid(zh))).astype(o_ref.dtype)


def gdn_scan(u, wq, ak, egl, z, out_gain, B, S):
    C, H = GDN_CHUNK, GDN_HEADS
    N = S // C
    in5 = lambda r, cdim: pl.BlockSpec((None, None, H, r, cdim), lambda b, n: (b, n, 0, 0, 0))
    return pl.pallas_call(
        _gdn_scan_kernel,
        grid=(B, N),
        in_specs=[in5(C, HEAD_DIM), in5(2 * C, HEAD_DIM), in5(C + HEAD_DIM, C), in5(8, HEAD_DIM),
                  pl.BlockSpec((C, D_GDN), lambda b, n: (b * N + n, COL_Z_B // D_GDN)),
                  pl.BlockSpec((1, HEAD_DIM), lambda b, n: (0, 0))],
        out_specs=pl.BlockSpec((C, D_GDN), lambda b, n: (b * N + n, 0)),
        out_shape=jax.ShapeDtypeStruct((B * S, D_GDN), BF16),
        scratch_shapes=[pltpu.VMEM((H, HEAD_DIM, HEAD_DIM), F32)],
        compiler_params=_params(("parallel", "arbitrary")),
        name="gdn_scan",
    )(u, wq, ak, egl, z, out_gain.reshape(1, HEAD_DIM))


def _xa_kv_kernel(mem_ref, g_ref, w_ref, kg_ref, k_ref, v_ref):
    kv = _dot(_rms(mem_ref[...], g_ref[...]).astype(BF16), w_ref[...])
    for h in range(XA_HEADS):
        k_ref[h] = _rms(kv[:, h * HEAD_DIM:(h + 1) * HEAD_DIM], kg_ref[...]).astype(BF16)
        v_ref[h] = kv[:, D_XA + h * HEAD_DIM: D_XA + (h + 1) * HEAD_DIM].astype(BF16)


def xa_kv(mem, mem_norm, wkv_bf16, k_gain):
    B, Mm, D = mem.shape
    spec = pl.BlockSpec((None, XA_HEADS, Mm, HEAD_DIM), lambda b: (b, 0, 0, 0))
    shape = jax.ShapeDtypeStruct((B, XA_HEADS, Mm, HEAD_DIM), BF16)
    return pl.pallas_call(
        _xa_kv_kernel,
        grid=(B,),
        in_specs=[pl.BlockSpec((None, Mm, D), lambda b: (b, 0, 0)), pl.BlockSpec((1, D), lambda b: (0, 0)),
                  pl.BlockSpec((D, 2 * D_XA), lambda b: (0, 0)), pl.BlockSpec((1, HEAD_DIM), lambda b: (0, 0))],
        out_specs=[spec, spec],
        out_shape=[shape, shape],
        compiler_params=_params(("parallel",)),
        name="xa_kv",
    )(mem, mem_norm.reshape(1, D), wkv_bf16, k_gain.reshape(1, HEAD_DIM))


def _xa_kernel(x_ref, g_ref, wq_ref, k_ref, v_ref, qg_ref, wo_ref, o_ref):
    x = x_ref[...]
    q = _dot(_rms(x, g_ref[...]).astype(BF16), wq_ref[...])
    outs = []
    for h in range(XA_HEADS):
        qh = _rms(q[:, h * HEAD_DIM:(h + 1) * HEAD_DIM], qg_ref[...]).astype(BF16)
        s = _dot_nt(qh, k_ref[h]) * SCALE
        p = jnp.exp(s - jnp.max(s, axis=-1, keepdims=True))
        p = p / jnp.sum(p, axis=-1, keepdims=True)
        outs.append(_dot(p.astype(BF16), v_ref[h]))
    o = jnp.concatenate(outs, axis=1).astype(BF16)
    o_ref[...] = x + _dot(o, wo_ref[...])


def cross_attention(x, xa_norm, wq_bf16, k, v, q_gain, wo_bf16, B, S, tm=256):
    M, D = x.shape
    Mm = k.shape[2]
    per_b = S // tm
    kv_spec = pl.BlockSpec((None, XA_HEADS, Mm, HEAD_DIM), lambda i: (i // per_b, 0, 0, 0))
    const = lambda i: (0, 0)
    return pl.pallas_call(
        _xa_kernel,
        grid=(M // tm,),
        in_specs=[pl.BlockSpec((tm, D), lambda i: (i, 0)), pl.BlockSpec((1, D), const),
                  pl.BlockSpec((D, D_XA), const), kv_spec, kv_spec,
                  pl.BlockSpec((1, HEAD_DIM), const), pl.BlockSpec((D_XA, D), const)],
        out_specs=pl.BlockSpec((tm, D), lambda i: (i, 0)),
        out_shape=jax.ShapeDtypeStruct((M, D), F32),
        compiler_params=_params(("parallel",)),
        name="cross_attention",
    )(x, xa_norm.reshape(1, D), wq_bf16, k, v, q_gain.reshape(1, HEAD_DIM), wo_bf16)


def _reorder_w_in(w_in):
    o = np.cumsum((D_NSA, 6 * D_KV_NSA, 3 * NSA_HEADS, 3 * D_GDN, GDN_HEADS, GDN_HEADS, D_GDN))
    q_a, kv_a, gate_a, qkv_b = (0, o[0]), (o[0], o[1]), o[1], (o[2], o[3])
    beta_b, a_b, z_b = o[3], o[4], (o[5], o[6])
    Hg = NSA_GROUP_SIZE
    src = np.zeros((256,), np.int64)
    used = np.zeros((256,), bool)
    for g in range(NSA_KV_GROUPS):
        for br in range(3):
            for h in range(Hg):
                src[g * HEAD_DIM + br * Hg + h] = gate_a + (g * Hg + h) * 3 + br
                used[g * HEAD_DIM + br * Hg + h] = True
    for h in range(GDN_HEADS):
        src[SM_BETA + h] = beta_b + h
        src[SM_A + h] = a_b + h
        used[SM_BETA + h] = used[SM_A + h] = True
    small = jnp.where(jnp.asarray(used)[None, :], jnp.take(w_in, jnp.asarray(src), axis=1), 0.0)
    parts = [w_in[:, qkv_b[0]:qkv_b[1]], w_in[:, q_a[0]:q_a[1]], w_in[:, z_b[0]:z_b[1]], w_in[:, kv_a[0]:kv_a[1]], small]
    return jnp.concatenate([p.astype(BF16) for p in parts], axis=1)


def mixer_layer(x2, B, S, attn_norm, w_in, nsa_q_gain, nsa_k_gain, nsa_cmp_pos, nsa_cmp_w1, nsa_cmp_w2, nsa_out_gain,
                gdn_conv_w, gdn_A_log, gdn_dt_bias, gdn_out_gain, w_out, tm=1024):
    h = rmsnorm_bf16(x2, attn_norm)
    z = matmul_cols(h, _reorder_w_in(w_in), tm=tm, tn=256, name="in_proj")
    raw, ks, vs, kw, vw = nsa_prep(z, nsa_k_gain, B, S)
    comp = nsa_compress(raw, nsa_cmp_pos, nsa_cmp_w1, nsa_cmp_w2, nsa_k_gain[0], B, S)
    y_a = nsa_attention(z, comp, ks, vs, kw, vw, nsa_q_gain, nsa_out_gain, B, S)
    u, wq, ak, egl = gdn_chunks(z, gdn_conv_w, gdn_A_log, gdn_dt_bias, B, S)
    y_b = gdn_scan(u, wq, ak, egl, z, gdn_out_gain, B, S)
    return matmul2_residual(y_a, y_b, w_out, x2, tm=tm, tn=256, name="out_proj")


def xa_layer(x2, mem, B, S, xa_norm, mem_norm, xa_wq, xa_wkv, xa_q_gain, xa_k_gain, xa_wo):
    k, v = xa_kv(mem, mem_norm, xa_wkv.astype(BF16), xa_k_gain)
    return cross_attention(x2, xa_norm, xa_wq.astype(BF16), k, v, xa_q_gain, xa_wo.astype(BF16), B, S)


def dense_ffn_layer(x2, ffn_norm, w13, w2, tm=1024):
    h = rmsnorm_bf16(x2, ffn_norm)
    act = swiglu_up(h, w13, tm=tm, tn=256)
    return down_proj_residual(act, w2, x2, tm=tm, tn=1024, tk=1024, tk_tail=256)


def moe_ffn_layer(x2, ffn_norm, router_w, w13, w2, tm=512, tn=512):
    h, idx, wts = rmsnorm_router(x2, ffn_norm, router_w)
    src_tok, row_w, pos, tile_expert, n_valid = moe_dispatch(idx[:, :TOP_K], wts[:, :TOP_K], tm)
    xs = jnp.take(h, src_tok, axis=0)
    act = moe_up(xs, w13, tile_expert, n_valid, tm=tm, tn=tn)
    row_w_rep = jnp.broadcast_to(row_w[:, None], (row_w.shape[0], 128))
    out = moe_down(act, w2, row_w_rep, tile_expert, n_valid, tm=tm, tn=tn)
    return x2 + (jnp.take(out, pos[:, 0], axis=0) + jnp.take(out, pos[:, 1], axis=0))


def kernel(x, mem, attn_norm, w_in, nsa_q_gain, nsa_k_gain, nsa_cmp_pos, nsa_cmp_w1, nsa_cmp_w2, nsa_out_gain, gdn_conv_w, gdn_A_log, gdn_dt_bias, gdn_out_gain, w_out, xa_norm, mem_norm, xa_wq, xa_wkv, xa_q_gain, xa_k_gain, xa_wo, ffn_norm, dense_w13, dense_w2, router_w, moe_w13, moe_w2):
    B, S, D = x.shape
    x2 = x.reshape(B * S, D)
    for l in range(attn_norm.shape[0]):
        x2 = mixer_layer(x2, B, S, attn_norm[l], w_in[l], nsa_q_gain[l], nsa_k_gain[l], nsa_cmp_pos[l], nsa_cmp_w1[l],
                         nsa_cmp_w2[l], nsa_out_gain[l], gdn_conv_w[l], gdn_A_log[l], gdn_dt_bias[l], gdn_out_gain[l],
                         w_out[l])
        x2 = xa_layer(x2, mem, B, S, xa_norm[l], mem_norm[l], xa_wq[l], xa_wkv[l], xa_q_gain[l], xa_k_gain[l], xa_wo[l])
        if l % 2 == 0:
            x2 = dense_ffn_layer(x2, ffn_norm[l], dense_w13[l // 2], dense_w2[l // 2])
        else:
            x2 = moe_ffn_layer(x2, ffn_norm[l], router_w[l // 2], moe_w13[l // 2], moe_w2[l // 2])
    return x2.reshape(B, S, D)
```

```python
import functools

import jax
import jax.numpy as jnp
import numpy as np
from jax import lax
from jax.experimental import pallas as pl
from jax.experimental.pallas import tpu as pltpu

F32 = jnp.float32
BF16 = jnp.bfloat16
I32 = jnp.int32

D_MODEL = 4096
HEAD_DIM = 128
EPS = 1e-6
SCALE = HEAD_DIM ** -0.5

NSA_HEADS = 16
NSA_KV_GROUPS = 2
NSA_GROUP_SIZE = NSA_HEADS // NSA_KV_GROUPS
CMP_BLOCK = 32
CMP_STRIDE = 16
SEL_BLOCK = 64
SEL_TOP_N = 16
SEL_LOCAL = 2
WINDOW = 512
Q_BLOCK = 128
FORCE = 1e9
NEG = -1e30
SEL_TILE = 512
LOG2E = 1.4426950408889634

GDN_HEADS = 16
GDN_CONV = 4
GDN_CHUNK = 64
GDN_PACK = 4
GDN_SUB = 16

D_NSA = NSA_HEADS * HEAD_DIM
D_KV_NSA = NSA_KV_GROUPS * HEAD_DIM
D_GDN = GDN_HEADS * HEAD_DIM
D_MIX = D_NSA + D_GDN

XA_HEADS = 4
D_XA = XA_HEADS * HEAD_DIM
N_EXPERTS = 8
TOP_K = 2

COL_QKV_B = 0
COL_Q_A = COL_QKV_B + 3 * D_GDN
COL_Z_B = COL_Q_A + D_NSA
COL_KV_A = COL_Z_B + D_GDN
COL_SMALL = COL_KV_A + 6 * D_KV_NSA
N_IN_R = COL_SMALL + 256
SM_BETA = 32
SM_A = 48

VMEM_LIMIT = 58 * 1024 * 1024

NT_DIMS = (((1,), (1,)), ((), ()))


def _rms(x, gain):
    return x * lax.rsqrt(jnp.mean(x * x, axis=-1, keepdims=True) + EPS) * gain


def _dot(a, b):
    return jnp.dot(a, b, preferred_element_type=F32)


def _dot_nt(a, b):
    return lax.dot_general(a, b, NT_DIMS, preferred_element_type=F32)


def _dot3(x, sel):
    hi = x.astype(BF16)
    rem = x - hi.astype(F32)
    mid = rem.astype(BF16)
    lo = (rem - mid.astype(F32)).astype(BF16)
    return _dot(hi, sel) + _dot(mid, sel) + _dot(lo, sel)


def _params(sem, vmem=VMEM_LIMIT):
    return pltpu.CompilerParams(dimension_semantics=sem, vmem_limit_bytes=vmem)


def _norm_kernel(x_ref, g_ref, o_ref):
    o_ref[...] = _rms(x_ref[...], g_ref[...]).astype(o_ref.dtype)


def rmsnorm_bf16(x, gain, tm=512):
    M, D = x.shape
    return pl.pallas_call(
        _norm_kernel,
        grid=(M // tm,),
        in_specs=[pl.BlockSpec((tm, D), lambda i: (i, 0)), pl.BlockSpec((1, D), lambda i: (0, 0))],
        out_specs=pl.BlockSpec((tm, D), lambda i: (i, 0)),
        out_shape=jax.ShapeDtypeStruct((M, D), BF16),
        compiler_params=_params(("parallel",)),
        name="rmsnorm",
    )(x, gain.reshape(1, D))


def _norm_router_kernel(x_ref, g_ref, rw_ref, o_ref, idx_ref, wt_ref):
    h = _rms(x_ref[...], g_ref[...])
    o_ref[...] = h.astype(o_ref.dtype)
    logits = jnp.dot(h, rw_ref[...], preferred_element_type=F32, precision=lax.Precision.HIGHEST)
    lane = lax.broadcasted_iota(I32, logits.shape, 1).astype(F32)
    logits = jnp.where(lane < N_EXPERTS, logits, -jnp.inf)
    m1 = jnp.max(logits, axis=-1, keepdims=True)
    i1 = jnp.min(jnp.where(logits == m1, lane, 128.0), axis=-1, keepdims=True)
    rest = jnp.where(lane == i1, -jnp.inf, logits)
    m2 = jnp.max(rest, axis=-1, keepdims=True)
    i2 = jnp.min(jnp.where(rest == m2, lane, 128.0), axis=-1, keepdims=True)
    e2 = jnp.exp(m2 - m1)
    den = 1.0 + e2
    idx_ref[...] = jnp.where(lane == 0, i1, jnp.where(lane == 1, i2, 0.0)).astype(I32)
    wt_ref[...] = jnp.where(lane == 0, 1.0 / den, jnp.where(lane == 1, e2 / den, 0.0))


def rmsnorm_router(x, gain, router_w, tm=256):
    M, D = x.shape
    rw = jnp.pad(router_w, ((0, 0), (0, 128 - N_EXPERTS)))
    row = lambda i: (i, 0)
    return pl.pallas_call(
        _norm_router_kernel,
        grid=(M // tm,),
        in_specs=[pl.BlockSpec((tm, D), row), pl.BlockSpec((1, D), lambda i: (0, 0)),
                  pl.BlockSpec((D, 128), lambda i: (0, 0))],
        out_specs=[pl.BlockSpec((tm, D), row), pl.BlockSpec((tm, 128), row), pl.BlockSpec((tm, 128), row)],
        out_shape=[jax.ShapeDtypeStruct((M, D), BF16), jax.ShapeDtypeStruct((M, 128), I32),
                   jax.ShapeDtypeStruct((M, 128), F32)],
        compiler_params=_params(("parallel",)),
        name="rmsnorm_router",
    )(x, gain.reshape(1, D), rw)


def _mm_kernel(x_ref, w_ref, o_ref):
    o_ref[...] = _dot(x_ref[...], w_ref[...].astype(BF16)).astype(o_ref.dtype)


def _mm_res_kernel(x_ref, w_ref, r_ref, o_ref):
    o_ref[...] = r_ref[...] + _dot(x_ref[...], w_ref[...].astype(BF16))


def matmul_cols(x, w, *, tm, tn, out_dtype=F32, residual=None, name="matmul"):
    M, K = x.shape
    N = w.shape[1]
    in_specs = [pl.BlockSpec((tm, K), lambda i, j: (i, 0)), pl.BlockSpec((K, tn), lambda i, j: (0, j))]
    args = [x, w]
    kern = _mm_kernel
    if residual is not None:
        in_specs.append(pl.BlockSpec((tm, tn), lambda i, j: (i, j)))
        args.append(residual)
        kern = _mm_res_kernel
    return pl.pallas_call(
        kern,
        grid=(M // tm, N // tn),
        in_specs=in_specs,
        out_specs=pl.BlockSpec((tm, tn), lambda i, j: (i, j)),
        out_shape=jax.ShapeDtypeStruct((M, N), out_dtype),
        compiler_params=_params(("parallel", "parallel")),
        name=name,
    )(*args)


def _mm2_res_kernel(xa_ref, xb_ref, wa_ref, wb_ref, r_ref, o_ref):
    acc = _dot(xa_ref[...], wa_ref[...].astype(BF16)) + _dot(xb_ref[...], wb_ref[...].astype(BF16))
    o_ref[...] = r_ref[...] + acc


def matmul2_residual(xa, xb, w, residual, *, tm, tn, name):
    M, Ka = xa.shape
    Kb = xb.shape[1]
    assert Ka == Kb and w.shape[0] == Ka + Kb
    N = w.shape[1]
    return pl.pallas_call(
        _mm2_res_kernel,
        grid=(M // tm, N // tn),
        in_specs=[pl.BlockSpec((tm, Ka), lambda i, j: (i, 0)), pl.BlockSpec((tm, Kb), lambda i, j: (i, 0)),
                  pl.BlockSpec((Ka, tn), lambda i, j: (0, j)), pl.BlockSpec((Kb, tn), lambda i, j: (1, j)),
                  pl.BlockSpec((tm, tn), lambda i, j: (i, j))],
        out_specs=pl.BlockSpec((tm, tn), lambda i, j: (i, j)),
        out_shape=jax.ShapeDtypeStruct((M, N), F32),
        compiler_params=_params(("parallel", "parallel")),
        name=name,
    )(xa, xb, w, w, residual)


def _swiglu_up_kernel(x_ref, wg_ref, wu_ref, o_ref):
    x = x_ref[...]
    gate = _dot(x, wg_ref[...].astype(BF16))
    up = _dot(x, wu_ref[...].astype(BF16))
    o_ref[...] = (gate * jax.nn.sigmoid(gate) * up).astype(o_ref.dtype)


def swiglu_up(x, w13, *, tm, tn):
    M, K = x.shape
    F = w13.shape[1] // 2
    nj = F // tn
    return pl.pallas_call(
        _swiglu_up_kernel,
        grid=(M // tm, nj),
        in_specs=[pl.BlockSpec((tm, K), lambda i, j: (i, 0)),
                  pl.BlockSpec((K, tn), lambda i, j: (0, j)),
                  pl.BlockSpec((K, tn), lambda i, j: (0, j + nj))],
        out_specs=pl.BlockSpec((tm, tn), lambda i, j: (i, j)),
        out_shape=jax.ShapeDtypeStruct((M, F), BF16),
        compiler_params=_params(("parallel", "parallel")),
        name="swiglu_up",
    )(x, w13, w13)


def _down_res_kernel(a_ref, w_ref, at_ref, wt_ref, r_ref, o_ref, *, n_main):
    k = pl.program_id(2)

    @pl.when(k == 0)
    def _():
        o_ref[...] = r_ref[...]

    @pl.when(k < n_main)
    def _():
        o_ref[...] += _dot(a_ref[...], w_ref[...].astype(BF16))

    @pl.when(k >= n_main)
    def _():
        o_ref[...] += _dot(at_ref[...], wt_ref[...].astype(BF16))


def down_proj_residual(a, w2, residual, *, tm, tn, tk, tk_tail):
    M, F = a.shape
    N = w2.shape[1]
    n_main = F // tk
    tail = F - n_main * tk
    assert tail % tk_tail == 0 and (n_main * tk) % tk_tail == 0
    n_tail = tail // tk_tail
    first_tail = (n_main * tk) // tk_tail if n_tail else 0
    main_k = lambda k: jnp.minimum(k, n_main - 1)
    tail_k = lambda k: first_tail + jnp.maximum(k - n_main, 0)
    return pl.pallas_call(
        functools.partial(_down_res_kernel, n_main=n_main),
        grid=(M // tm, N // tn, n_main + n_tail),
        in_specs=[pl.BlockSpec((tm, tk), lambda i, j, k: (i, main_k(k))),
                  pl.BlockSpec((tk, tn), lambda i, j, k: (main_k(k), j)),
                  pl.BlockSpec((tm, tk_tail), lambda i, j, k: (i, tail_k(k))),
                  pl.BlockSpec((tk_tail, tn), lambda i, j, k: (tail_k(k), j)),
                  pl.BlockSpec((tm, tn), lambda i, j, k: (i, j))],
        out_specs=pl.BlockSpec((tm, tn), lambda i, j, k: (i, j)),
        out_shape=jax.ShapeDtypeStruct((M, N), F32),
        compiler_params=_params(("parallel", "parallel", "arbitrary")),
        name="down_proj",
    )(a, w2, a, w2, residual)


def _new_weights(te_ref, i):
    return jnp.logical_or(i == 0, te_ref[i] != te_ref[jnp.maximum(i - 1, 0)])


def _moe_up_kernel(te_ref, nv_ref, x_ref, wg_ref, wu_ref, o_ref, wgb_ref, wub_ref):
    i = pl.program_id(1)
    valid = i < nv_ref[0]

    @pl.when(_new_weights(te_ref, i))
    def _():
        wgb_ref[...] = wg_ref[...].astype(BF16)
        wub_ref[...] = wu_ref[...].astype(BF16)

    @pl.when(valid)
    def _():
        x = x_ref[...]
        gate = _dot(x, wgb_ref[...])
        up = _dot(x, wub_ref[...])
        o_ref[...] = (gate * jax.nn.sigmoid(gate) * up).astype(o_ref.dtype)

    @pl.when(jnp.logical_not(valid))
    def _():
        o_ref[...] = jnp.zeros_like(o_ref)


def moe_up(xs, w13, tile_expert, n_valid, *, tm, tn):
    P, K = xs.shape
    F = w13.shape[2] // 2
    nj = F // tn
    return pl.pallas_call(
        _moe_up_kernel,
        grid_spec=pltpu.PrefetchScalarGridSpec(
            num_scalar_prefetch=2,
            grid=(nj, P // tm),
            in_specs=[pl.BlockSpec((tm, K), lambda j, i, te, nv: (jnp.minimum(i, nv[0] - 1), 0)),
                      pl.BlockSpec((None, K, tn), lambda j, i, te, nv: (te[i], 0, j)),
                      pl.BlockSpec((None, K, tn), lambda j, i, te, nv: (te[i], 0, j + nj))],
            out_specs=pl.BlockSpec((tm, tn), lambda j, i, te, nv: (i, j)),
            scratch_shapes=[pltpu.VMEM((K, tn), BF16), pltpu.VMEM((K, tn), BF16)],
        ),
        out_shape=jax.ShapeDtypeStruct((P, F), BF16),
        compiler_params=_params(("arbitrary", "arbitrary")),
        name="moe_up",
    )(tile_expert, n_valid, xs, w13, w13)


def _moe_down_kernel(te_ref, nv_ref, a_ref, w_ref, rw_ref, o_ref, wb_ref):
    i = pl.program_id(1)
    valid = i < nv_ref[0]

    @pl.when(_new_weights(te_ref, i))
    def _():
        wb_ref[...] = w_ref[...].astype(BF16)

    @pl.when(valid)
    def _():
        o_ref[...] = _dot(a_ref[...], wb_ref[...]) * rw_ref[:, 0:1]

    @pl.when(jnp.logical_not(valid))
    def _():
        o_ref[...] = jnp.zeros_like(o_ref)


def moe_down(act, w2, row_w, tile_expert, n_valid, *, tm, tn):
    P, F = act.shape
    N = w2.shape[2]
    return pl.pallas_call(
        _moe_down_kernel,
        grid_spec=pltpu.PrefetchScalarGridSpec(
            num_scalar_prefetch=2,
            grid=(N // tn, P // tm),
            in_specs=[pl.BlockSpec((tm, F), lambda j, i, te, nv: (jnp.minimum(i, nv[0] - 1), 0)),
                      pl.BlockSpec((None, F, tn), lambda j, i, te, nv: (te[i], 0, j)),
                      pl.BlockSpec((tm, 128), lambda j, i, te, nv: (i, 0))],
            out_specs=pl.BlockSpec((tm, tn), lambda j, i, te, nv: (i, j)),
            scratch_shapes=[pltpu.VMEM((F, tn), BF16)],
        ),
        out_shape=jax.ShapeDtypeStruct((P, N), F32),
        compiler_params=_params(("arbitrary", "arbitrary")),
        name="moe_down",
    )(tile_expert, n_valid, act, w2, row_w)


def moe_dispatch(top_i, top_w, tm):
    M = top_i.shape[0]
    A = M * TOP_K
    P = A + N_EXPERTS * tm
    e_flat = top_i.reshape(A)
    onehot = (e_flat[:, None] == jnp.arange(N_EXPERTS, dtype=I32)[None, :]).astype(I32)
    csum = jnp.cumsum(onehot, axis=0)
    counts = csum[-1]
    padded = ((counts + tm - 1) // tm) * tm
    pend = jnp.cumsum(padded)
    pstart = pend - padded
    dest = jnp.sum(onehot * (pstart[None, :] + csum - onehot), axis=1)
    src_tok = jnp.zeros((P,), I32).at[dest].set(jnp.arange(A, dtype=I32) // TOP_K)
    row_w = jnp.zeros((P,), F32).at[dest].set(top_w.reshape(A))
    pos = dest.reshape(M, TOP_K)
    tile_start = jnp.arange(P // tm, dtype=I32) * tm
    tile_expert = jnp.minimum(jnp.sum((tile_start[:, None] >= pend[None, :]).astype(I32), axis=1), N_EXPERTS - 1)
    n_valid = (pend[-1:] // tm).astype(I32)
    last_e = tile_expert[jnp.maximum(n_valid[0] - 1, 0)]
    tile_expert = jnp.where(tile_start // tm < n_valid[0], tile_expert, last_e)
    return src_tok, row_w, pos, tile_expert, n_valid


def _nsa_prep_kernel(kc_ref, vc_ref, ks_ref, vs_ref, kw_ref, vw_ref, kg_ref, raw_ref, kso_ref, vso_ref, kwo_ref, vwo_ref):
    raw_ref[0] = kc_ref[...]
    raw_ref[1] = vc_ref[...]
    kg = kg_ref[...]
    kso_ref[...] = _rms(ks_ref[...], kg[1:2]).astype(BF16)
    vso_ref[...] = vs_ref[...].astype(BF16)
    kwo_ref[...] = _rms(kw_ref[...], kg[2:3]).astype(BF16)
    vwo_ref[...] = vw_ref[...].astype(BF16)


def nsa_prep(z, k_gain, B, S, ts=512):
    G = NSA_KV_GROUPS
    nt = S // ts
    cb = COL_KV_A // HEAD_DIM

    def col(br, kvi):
        return pl.BlockSpec((ts, HEAD_DIM), lambda b, g, t: (b * nt + t, cb + br * 4 + kvi * 2 + g))

    kv_out = pl.BlockSpec((None, None, ts, HEAD_DIM), lambda b, g, t: (b, g, t, 0))
    kv_shape = jax.ShapeDtypeStruct((B, G, S, HEAD_DIM), BF16)
    return pl.pallas_call(
        _nsa_prep_kernel,
        grid=(B, G, nt),
        in_specs=[col(0, 0), col(0, 1), col(1, 0), col(1, 1), col(2, 0), col(2, 1),
                  pl.BlockSpec((3, HEAD_DIM), lambda b, g, t: (0, 0))],
        out_specs=[pl.BlockSpec((None, 2, None, ts, HEAD_DIM), lambda b, g, t: (b, 0, g, t, 0)),
                   kv_out, kv_out, kv_out, kv_out],
        out_shape=[jax.ShapeDtypeStruct((B, 2, G, S, HEAD_DIM), F32), kv_shape, kv_shape, kv_shape, kv_shape],
        compiler_params=_params(("parallel", "parallel", "parallel")),
        name="nsa_prep",
    )(z, z, z, z, z, z, k_gain)


def _nsa_compress_kernel(r_ref, pos_ref, w1_ref, w2_ref, kg_ref, o_ref):
    kv = pl.program_id(1)
    half = (CMP_BLOCK // 2) * HEAD_DIM
    r = r_ref[...]
    pos = pos_ref[...]
    n = r.shape[0]
    first = _dot((r + pos[:, :half]).astype(BF16), w1_ref[:half, :].astype(BF16))
    second = _dot((r + pos[:, half:]).astype(BF16), w1_ref[half:, :].astype(BF16))
    pre = first + pltpu.roll(second, n - 1, axis=0)
    hid = jax.nn.gelu(pre)
    comp = _dot(hid.astype(BF16), w2_ref[...].astype(BF16))
    o_ref[...] = jnp.where(kv == 0, _rms(comp, kg_ref[...]), comp).astype(o_ref.dtype)


def nsa_compress(raw, cmp_pos, cmp_w1, cmp_w2, k_gain0, B, S):
    G = NSA_KV_GROUPS
    n = S // CMP_STRIDE
    feat = CMP_STRIDE * HEAD_DIM
    r = raw.reshape(B, 2, G, n, feat)
    pos = cmp_pos.reshape(2, 1, CMP_BLOCK * HEAD_DIM)
    return pl.pallas_call(
        _nsa_compress_kernel,
        grid=(B, 2, G),
        in_specs=[pl.BlockSpec((None, None, None, n, feat), lambda b, kv, g: (b, kv, g, 0, 0)),
                  pl.BlockSpec((None, 1, CMP_BLOCK * HEAD_DIM), lambda b, kv, g: (kv, 0, 0)),
                  pl.BlockSpec((None, CMP_BLOCK * HEAD_DIM, HEAD_DIM), lambda b, kv, g: (kv, 0, 0)),
                  pl.BlockSpec((None, HEAD_DIM, HEAD_DIM), lambda b, kv, g: (kv, 0, 0)),
                  pl.BlockSpec((1, HEAD_DIM), lambda b, kv, g: (0, 0))],
        out_specs=pl.BlockSpec((None, None, None, n, HEAD_DIM), lambda b, kv, g: (b, kv, g, 0, 0)),
        out_shape=jax.ShapeDtypeStruct((B, 2, G, n, HEAD_DIM), BF16),
        compiler_params=_params(("parallel", "parallel", "parallel")),
        name="nsa_compress",
    )(r, pos, cmp_w1, cmp_w2, k_gain0.reshape(1, HEAD_DIM))


def _rep(x, n):
    return x if n == HEAD_DIM else jnp.concatenate([x] * (n // HEAD_DIM), axis=1)


def _flash_step(qb, kt, vt, bias, m_ref, l_ref, acc_ref):
    nk = kt.shape[0]
    rows = qb.shape[0]
    s = _dot_nt(qb, kt)
    s = (s.reshape(rows // Q_BLOCK, Q_BLOCK, nk) + bias[None]).reshape(rows, nk)
    m_prev = m_ref[...]
    m_new = jnp.maximum(m_prev, jnp.max(s, axis=-1, keepdims=True))
    alpha = jnp.exp2(m_prev - m_new)
    p = jnp.exp2(s - _rep(m_new, nk))
    l_ref[...] = alpha * l_ref[...] + jnp.sum(p, axis=-1, keepdims=True)
    acc_ref[...] = alpha * acc_ref[...] + _dot(p.astype(BF16), vt)
    m_ref[...] = m_new


def _nsa_kernel(q_ref, sm_ref, kc_ref, vc_ref, ks_ref, vs_ref, kw_ref, vw_ref, qg_ref, og_ref, c2s_ref, exp_ref,
                o_ref, selb_ref, m_ref, l_ref, acc_ref, *, k_top):
    Hg = NSA_GROUP_SIZE
    rows = Hg * Q_BLOCK
    c = pl.program_id(2)
    t0 = c * Q_BLOCK
    n_cmp = kc_ref.shape[0]
    n_sel = c2s_ref.shape[0]

    q = q_ref[...]
    qs = jnp.concatenate([q[:, h * HEAD_DIM:(h + 1) * HEAD_DIM] for h in range(Hg)], axis=0)
    qb = (_rms(qs, qg_ref[...]) * (SCALE * LOG2E)).astype(BF16)

    tq = t0 + lax.broadcasted_iota(I32, (Q_BLOCK, n_cmp), 0)
    c_end = lax.broadcasted_iota(I32, (Q_BLOCK, n_cmp), 1) * CMP_STRIDE + (CMP_BLOCK - 1)
    sc = _dot_nt(qb, kc_ref[...]).reshape(Hg, Q_BLOCK, n_cmp)
    sc = jnp.where((c_end <= tq)[None], sc, -jnp.inf)
    mc = jnp.max(sc, axis=-1, keepdims=True)
    mc = jnp.where(mc > -jnp.inf, mc, 0.0)
    pc = jnp.exp2(sc - mc)
    dc = jnp.sum(pc, axis=-1, keepdims=True)
    pc = pc / jnp.where(dc > 0, dc, 1.0)
    o_cmp = _dot(pc.reshape(rows, n_cmp).astype(BF16), vc_ref[...])

    psum = jnp.sum(pc, axis=0)
    c2s = c2s_ref[...]
    p_hi = psum.astype(BF16)
    rem = psum - p_hi.astype(F32)
    p_mid = rem.astype(BF16)
    p_lo = (rem - p_mid.astype(F32)).astype(BF16)
    imp = _dot_nt(c2s, p_hi) + _dot_nt(c2s, p_mid) + _dot_nt(c2s, p_lo)

    jj = lax.broadcasted_iota(I32, (n_sel, Q_BLOCK), 0)
    cur = lax.shift_right_logical(t0 + lax.broadcasted_iota(I32, (n_sel, Q_BLOCK), 1), 6)
    valid = jj <= cur
    forced = valid & ((jj == 0) | (jj > cur - SEL_LOCAL))
    score = jnp.where(forced, FORCE, jnp.where(valid, imp, -FORCE))
    rank = jnp.zeros((n_sel, Q_BLOCK), F32)
    for j2 in range(n_sel):
        other = score[j2:j2 + 1, :]
        tie = jnp.where(jj > j2, 1.0, 0.0)
        rank = rank + jnp.where(other > score, 1.0, jnp.where(other == score, tie, 0.0))
    sel_t = jnp.where(rank < k_top, 1.0, 0.0)
    if n_sel < Q_BLOCK:
        sel_t = jnp.concatenate([sel_t, jnp.zeros((Q_BLOCK - n_sel, Q_BLOCK), F32)], axis=0)
    sel = sel_t.T[:, :n_sel].astype(BF16)
    sel_keys = _dot(sel, exp_ref[...])
    for i in range(selb_ref.shape[0]):
        selb_ref[i] = (sel_keys[:, i * SEL_TILE:(i + 1) * SEL_TILE] - 1.0) * (-NEG)

    def reset():
        m_ref[...] = jnp.full(m_ref.shape, NEG, F32)
        l_ref[...] = jnp.zeros(l_ref.shape, F32)
        acc_ref[...] = jnp.zeros(acc_ref.shape, F32)

    reset()
    tq_s = t0 + lax.broadcasted_iota(I32, (Q_BLOCK, SEL_TILE), 0)
    kk_s = lax.broadcasted_iota(I32, (Q_BLOCK, SEL_TILE), 1)

    def sel_body(i, carry):
        k0 = pl.multiple_of(i * SEL_TILE, SEL_TILE)
        bias = selb_ref[i] + jnp.where(kk_s + k0 <= tq_s, 0.0, NEG)
        _flash_step(qb, ks_ref[pl.ds(k0, SEL_TILE), :], vs_ref[pl.ds(k0, SEL_TILE), :], bias, m_ref, l_ref, acc_ref)
        return carry

    lax.fori_loop(0, (t0 + Q_BLOCK + SEL_TILE - 1) // SEL_TILE, sel_body, 0)
    o_sel = acc_ref[...] / l_ref[...]

    wk = WINDOW + Q_BLOCK
    w0 = pl.multiple_of(jnp.maximum(t0 - WINDOW, 0), Q_BLOCK)
    tq_w = t0 + lax.broadcasted_iota(I32, (Q_BLOCK, wk), 0)
    wpos = w0 + lax.broadcasted_iota(I32, (Q_BLOCK, wk), 1)
    w_bias = jnp.where((wpos <= tq_w) & (wpos > tq_w - WINDOW), 0.0, NEG)
    sw = _dot_nt(qb, kw_ref[pl.ds(w0, wk), :]).reshape(Hg, Q_BLOCK, wk) + w_bias[None]
    pw = jnp.exp2(sw - jnp.max(sw, axis=-1, keepdims=True))
    lw = jnp.sum(pw, axis=-1, keepdims=True).reshape(rows, 1)
    o_win = _dot(pw.reshape(rows, wk).astype(BF16), vw_ref[pl.ds(w0, wk), :]) / lw

    gates = jax.nn.sigmoid(sm_ref[...])
    og = og_ref[...]
    for h in range(Hg):
        r = slice(h * Q_BLOCK, (h + 1) * Q_BLOCK)
        o = (gates[:, h:h + 1] * o_cmp[r] + gates[:, Hg + h:Hg + h + 1] * o_sel[r]
             + gates[:, 2 * Hg + h:2 * Hg + h + 1] * o_win[r])
        o_ref[:, h * HEAD_DIM:(h + 1) * HEAD_DIM] = _rms(o, og).astype(o_ref.dtype)


def nsa_attention(z, comp, ks, vs, kw, vw, q_gain, out_gain, B, S):
    G, Hg = NSA_KV_GROUPS, NSA_GROUP_SIZE
    nq = S // Q_BLOCK
    n_cmp = S // CMP_STRIDE
    n_sel = S // SEL_BLOCK
    k_top = min(SEL_TOP_N, n_sel)
    c_start = np.arange(n_cmp)[:, None] * CMP_STRIDE
    s_start = np.arange(n_sel)[None, :] * SEL_BLOCK
    overlap = np.minimum(c_start + CMP_BLOCK, s_start + SEL_BLOCK) - np.maximum(c_start, s_start)
    assert n_sel <= Q_BLOCK
    c2s = jnp.asarray((np.clip(overlap, 0, None) / CMP_STRIDE).T, dtype=BF16)
    expand = jnp.asarray(np.arange(S)[None, :] // SEL_BLOCK == np.arange(n_sel)[:, None], dtype=BF16)
    rows = Hg * Q_BLOCK
    wq = Hg * HEAD_DIM
    kv_spec = pl.BlockSpec((None, None, S, HEAD_DIM), lambda b, g, c: (b, g, 0, 0))
    const = lambda b, g, c: (0, 0)
    return pl.pallas_call(
        functools.partial(_nsa_kernel, k_top=k_top),
        grid=(B, G, nq),
        in_specs=[pl.BlockSpec((Q_BLOCK, wq), lambda b, g, c: (b * nq + c, COL_Q_A // wq + g)),
                  pl.BlockSpec((Q_BLOCK, HEAD_DIM), lambda b, g, c: (b * nq + c, COL_SMALL // HEAD_DIM + g)),
                  pl.BlockSpec((None, None, None, n_cmp, HEAD_DIM), lambda b, g, c: (b, 0, g, 0, 0)),
                  pl.BlockSpec((None, None, None, n_cmp, HEAD_DIM), lambda b, g, c: (b, 1, g, 0, 0)),
                  kv_spec, kv_spec, kv_spec, kv_spec,
                  pl.BlockSpec((1, HEAD_DIM), const), pl.BlockSpec((1, HEAD_DIM), const),
                  pl.BlockSpec((n_sel, n_cmp), const), pl.BlockSpec((n_sel, S), const)],
        out_specs=pl.BlockSpec((Q_BLOCK, wq), lambda b, g, c: (b * nq + c, g)),
        out_shape=jax.ShapeDtypeStruct((B * S, D_NSA), BF16),
        scratch_shapes=[pltpu.VMEM((S // SEL_TILE, Q_BLOCK, SEL_TILE), F32),
                        pltpu.VMEM((rows, HEAD_DIM), F32), pltpu.VMEM((rows, HEAD_DIM), F32),
                        pltpu.VMEM((rows, HEAD_DIM), F32)],
        compiler_params=_params(("parallel", "parallel", "arbitrary")),
        name="nsa_attention",
    )(z, z, comp, comp, ks, vs, kw, vw, q_gain.reshape(1, HEAD_DIM), out_gain.reshape(1, HEAD_DIM), c2s, expand)


def _gdn_chunk_kernel(xc_ref, xh_ref, sm_ref, cw_ref, alog_ref, dtb_ref, rep_ref, unfold_ref, u_ref, wq_ref, ak_ref,
                      egl_ref):
    C = GDN_CHUNK
    PK = GDN_PACK
    R = PK * C
    n = pl.program_id(1)
    xp = jnp.concatenate([jnp.where(n > 0, xh_ref[...], 0.0), xc_ref[...]], axis=0)
    cw = cw_ref[...]
    y = None
    for j in range(GDN_CONV):
        shift = GDN_CONV - 1 - j
        xs = xp if shift == 0 else pltpu.roll(xp, shift, axis=0)
        term = xs[8:] * cw[j:j + 1, :]
        y = term if y is None else y + term
    y = y * jax.nn.sigmoid(y)

    sm = sm_ref[...]
    beta = jax.nn.sigmoid(sm)
    g = -jnp.exp(alog_ref[...]) * jax.nn.softplus(sm + dtb_ref[...])
    row = lax.broadcasted_iota(I32, g.shape, 0)
    gc = g
    d = 1
    while d < C:
        gc = gc + jnp.where(row >= d, pltpu.roll(gc, d, axis=0), 0.0)
        d *= 2
    g_last = gc[C - 1:C, :]

    ri = lax.broadcasted_iota(I32, (R, R), 0)
    ci = lax.broadcasted_iota(I32, (R, R), 1)
    same = lax.shift_right_logical(ri, 6) == lax.shift_right_logical(ci, 6)
    tri = same & (ri >= ci)
    strict = same & (ri > ci)
    SUB = GDN_SUB
    nb = R // SUB
    same16 = lax.shift_right_logical(ri, 4) == lax.shift_right_logical(ci, 4)
    same32 = lax.shift_right_logical(ri, 5) == lax.shift_right_logical(ci, 5)
    off32 = same32 & jnp.logical_not(same16)
    off64 = jnp.logical_not(same32)
    NG = GDN_HEADS // PK
    W4 = NG * SUB
    row_c = lax.broadcasted_iota(I32, (R, W4), 0) & (SUB - 1)
    col_s = lax.broadcasted_iota(I32, (nb, W4), 1) & (SUB - 1)

    groups = []
    for grp in range(NG):
        heads = [grp * PK + i for i in range(PK)]

        def stack(off):
            return jnp.concatenate([y[:, off + h * HEAD_DIM: off + (h + 1) * HEAD_DIM] for h in heads], axis=0)

        def col(x, base):
            return jnp.concatenate([x[:, base + h: base + h + 1] for h in heads], axis=0)

        q4 = stack(0)
        k4 = stack(D_GDN)
        v4 = stack(2 * D_GDN)
        q4 = q4 * lax.rsqrt(jnp.sum(q4 * q4, axis=-1, keepdims=True) + EPS) * SCALE
        k4 = k4 * lax.rsqrt(jnp.sum(k4 * k4, axis=-1, keepdims=True) + EPS)
        beta4 = col(beta, SM_BETA)
        gc4 = col(gc, SM_A)
        gl4 = jnp.concatenate([jnp.broadcast_to(g_last[:, SM_A + h: SM_A + h + 1], (C, 1)) for h in heads], axis=0)

        gb = jnp.broadcast_to(gc4, (R, R))
        decay = jnp.exp(jnp.where(tri, gb - gb.T, -jnp.inf))
        kb4 = k4 * beta4
        k4b = k4.astype(BF16)
        a = jnp.where(strict, _dot_nt(kb4.astype(BF16), k4b) * decay, 0.0)
        attn = _dot_nt(q4.astype(BF16), k4b) * decay
        at = a.T
        dct = jnp.concatenate([at[SUB * b:SUB * (b + 1), SUB * b:SUB * (b + 1)] for b in range(nb)], axis=0)
        groups.append((heads, q4, k4, v4, beta4, gc4, gl4, kb4, a, attn, dct))

    coef = _dot3(jnp.concatenate([g[-1] for g in groups], axis=1), rep_ref[...])
    t4 = jnp.where(row_c == (lax.broadcasted_iota(I32, (R, W4), 1) & (SUB - 1)), 1.0, 0.0)
    for i in range(1, SUB):
        s = jnp.sum((coef[:, i * HEAD_DIM: i * HEAD_DIM + W4] * t4).reshape(nb, SUB, W4), axis=1)
        new = jnp.where(col_s == i, 1.0, 0.0) - s
        t4 = jnp.where(row_c == i, jnp.broadcast_to(new[:, None, :], (nb, SUB, W4)).reshape(R, W4), t4)
    t_tiled = _dot3(t4, unfold_ref[...])

    for gi, (heads, q4, k4, v4, beta4, gc4, gl4, kb4, a, attn, _) in enumerate(groups):
        t16 = jnp.where(same16, t_tiled[:, gi * R:(gi + 1) * R], 0.0)
        t16b = t16.astype(BF16)
        a32 = jnp.where(off32, a, 0.0).astype(BF16)
        t32 = t16 - _dot(_dot(t16b, a32).astype(BF16), t16b)
        t32b = t32.astype(BF16)
        a64 = jnp.where(off64, a, 0.0).astype(BF16)
        t64 = t32 - _dot(_dot(t32b, a64).astype(BF16), t32b)
        eg = jnp.exp(gc4)
        rhs = jnp.concatenate([v4 * beta4, kb4 * eg], axis=1)
        rhs = _dot(t64.astype(BF16), rhs.astype(BF16))
        qd4 = q4 * eg
        kdt = (k4 * jnp.exp(gl4 - gc4)).T
        for i, h in enumerate(heads):
            r = slice(i * C, (i + 1) * C)
            u_ref[h] = rhs[r, :HEAD_DIM]
            wq_ref[h] = jnp.concatenate([rhs[r, HEAD_DIM:], qd4[r]], axis=0).astype(BF16)
            ak_ref[h] = jnp.concatenate([attn[r, r], kdt[:, r]], axis=0).astype(BF16)
            egl_ref[h] = jnp.broadcast_to(jnp.exp(g_last[:, SM_A + h: SM_A + h + 1]), (8, HEAD_DIM))


def gdn_chunks(z, conv_w, a_log, dt_bias, B, S):
    C, H = GDN_CHUNK, GDN_HEADS
    N = S // C
    W = 3 * D_GDN
    pad = lambda v: jnp.zeros((1, HEAD_DIM), F32).at[0, SM_A:SM_A + H].set(v)
    out5 = lambda r, cdim: pl.BlockSpec((None, None, H, r, cdim), lambda b, n: (b, n, 0, 0, 0))
    NG, SUB, R = H // GDN_PACK, GDN_SUB, GDN_PACK * C
    W4 = NG * SUB
    src = np.arange(W4)[:, None]
    dst = np.arange(SUB * HEAD_DIM)[None, :]
    lane = dst % HEAD_DIM
    rep = (lane < W4) & (src // SUB == lane // SUB) & (src % SUB == dst // HEAD_DIM)
    dst = np.arange(NG * R)[None, :]
    unfold = (src // SUB == dst // R) & (src % SUB == dst % SUB)
    return pl.pallas_call(
        _gdn_chunk_kernel,
        grid=(B, N),
        in_specs=[pl.BlockSpec((C, W), lambda b, n: (b * N + n, 0)),
                  pl.BlockSpec((8, W), lambda b, n: (jnp.maximum((b * N + n) * (C // 8) - 1, 0), 0)),
                  pl.BlockSpec((C, HEAD_DIM), lambda b, n: (b * N + n, COL_SMALL // HEAD_DIM)),
                  pl.BlockSpec((GDN_CONV, W), lambda b, n: (0, 0)),
                  pl.BlockSpec((1, HEAD_DIM), lambda b, n: (0, 0)),
                  pl.BlockSpec((1, HEAD_DIM), lambda b, n: (0, 0)),
                  pl.BlockSpec((W4, SUB * HEAD_DIM), lambda b, n: (0, 0)),
                  pl.BlockSpec((W4, NG * R), lambda b, n: (0, 0))],
        out_specs=[out5(C, HEAD_DIM), out5(2 * C, HEAD_DIM), out5(C + HEAD_DIM, C), out5(8, HEAD_DIM)],
        out_shape=[jax.ShapeDtypeStruct((B, N, H, C, HEAD_DIM), F32),
                   jax.ShapeDtypeStruct((B, N, H, 2 * C, HEAD_DIM), BF16),
                   jax.ShapeDtypeStruct((B, N, H, C + HEAD_DIM, C), BF16),
                   jax.ShapeDtypeStruct((B, N, H, 8, HEAD_DIM), F32)],
        compiler_params=_params(("parallel", "parallel")),
        name="gdn_chunks",
    )(z, z, z, conv_w, pad(a_log), pad(dt_bias), jnp.asarray(rep, dtype=BF16), jnp.asarray(unfold, dtype=BF16))


def _gdn_scan_kernel(u_ref, wq_ref, ak_ref, egl_ref, z_ref, og_ref, o_ref, s_ref):
    C = GDN_CHUNK

    @pl.when(pl.program_id(1) == 0)
    def _():
        s_ref[...] = jnp.zeros(s_ref.shape, F32)

    og = og_ref[...]
    for h in range(GDN_HEADS):
        state = s_ref[h]
        ws = _dot(wq_ref[h], state.astype(BF16))
        v_new = u_ref[h] - ws[:C]
        av = _dot(ak_ref[h], v_new.astype(BF16))
        o = ws[C:] + av[:C]
        decayed = (state.reshape(HEAD_DIM // 8, 8, HEAD_DIM) * egl_ref[h][None]).reshape(HEAD_DIM, HEAD_DIM)
        s_ref[h] = decayed + av[C:]
        zh = z_ref[:, h * HEAD_DIM:(h + 1) * HEAD_DIM]
        o_ref[:, h * HEAD_DIM:(h + 1) * HEAD_DIM] = (_rms(o, og) * (zh * jax.nn.sigmoid(zh))).astype(o_ref.dtype)


def gdn_scan(u, wq, ak, egl, z, out_gain, B, S):
    C, H = GDN_CHUNK, GDN_HEADS
    N = S // C
    in5 = lambda r, cdim: pl.BlockSpec((None, None, H, r, cdim), lambda b, n: (b, n, 0, 0, 0))
    return pl.pallas_call(
        _gdn_scan_kernel,
        grid=(B, N),
        in_specs=[in5(C, HEAD_DIM), in5(2 * C, HEAD_DIM), in5(C + HEAD_DIM, C), in5(8, HEAD_DIM),
                  pl.BlockSpec((C, D_GDN), lambda b, n: (b * N + n, COL_Z_B // D_GDN)),
                  pl.BlockSpec((1, HEAD_DIM), lambda b, n: (0, 0))],
        out_specs=pl.BlockSpec((C, D_GDN), lambda b, n: (b * N + n, 0)),
        out_shape=jax.ShapeDtypeStruct((B * S, D_GDN), BF16),
        scratch_shapes=[pltpu.VMEM((H, HEAD_DIM, HEAD_DIM), F32)],
        compiler_params=_params(("parallel", "arbitrary")),
        name="gdn_scan",
    )(u, wq, ak, egl, z, out_gain.reshape(1, HEAD_DIM))


def _xa_kv_kernel(mem_ref, g_ref, w_ref, kg_ref, k_ref, v_ref):
    kv = _dot(_rms(mem_ref[...], g_ref[...]).astype(BF16), w_ref[...])
    for h in range(XA_HEADS):
        k_ref[h] = _rms(kv[:, h * HEAD_DIM:(h + 1) * HEAD_DIM], kg_ref[...]).astype(BF16)
        v_ref[h] = kv[:, D_XA + h * HEAD_DIM: D_XA + (h + 1) * HEAD_DIM].astype(BF16)


def xa_kv(mem, mem_norm, wkv_bf16, k_gain):
    B, Mm, D = mem.shape
    spec = pl.BlockSpec((None, XA_HEADS, Mm, HEAD_DIM), lambda b: (b, 0, 0, 0))
    shape = jax.ShapeDtypeStruct((B, XA_HEADS, Mm, HEAD_DIM), BF16)
    return pl.pallas_call(
        _xa_kv_kernel,
        grid=(B,),
        in_specs=[pl.BlockSpec((None, Mm, D), lambda b: (b, 0, 0)), pl.BlockSpec((1, D), lambda b: (0, 0)),
                  pl.BlockSpec((D, 2 * D_XA), lambda b: (0, 0)), pl.BlockSpec((1, HEAD_DIM), lambda b: (0, 0))],
        out_specs=[spec, spec],
        out_shape=[shape, shape],
        compiler_params=_params(("parallel",)),
        name="xa_kv",
    )(mem, mem_norm.reshape(1, D), wkv_bf16, k_gain.reshape(1, HEAD_DIM))


def _xa_kernel(x_ref, g_ref, wq_ref, k_ref, v_ref, qg_ref, wo_ref, o_ref):
    x = x_ref[...]
    q = _dot(_rms(x, g_ref[...]).astype(BF16), wq_ref[...])
    outs = []
    for h in range(XA_HEADS):
        qh = _rms(q[:, h * HEAD_DIM:(h + 1) * HEAD_DIM], qg_ref[...]).astype(BF16)
        s = _dot_nt(qh, k_ref[h]) * SCALE
        p = jnp.exp(s - jnp.max(s, axis=-1, keepdims=True))
        p = p / jnp.sum(p, axis=-1, keepdims=True)
        outs.append(_dot(p.astype(BF16), v_ref[h]))
    o = jnp.concatenate(outs, axis=1).astype(BF16)
    o_ref[...] = x + _dot(o, wo_ref[...])


def cross_attention(x, xa_norm, wq_bf16, k, v, q_gain, wo_bf16, B, S, tm=256):
    M, D = x.shape
    Mm = k.shape[2]
    per_b = S // tm
    kv_spec = pl.BlockSpec((None, XA_HEADS, Mm, HEAD_DIM), lambda i: (i // per_b, 0, 0, 0))
    const = lambda i: (0, 0)
    return pl.pallas_call(
        _xa_kernel,
        grid=(M // tm,),
        in_specs=[pl.BlockSpec((tm, D), lambda i: (i, 0)), pl.BlockSpec((1, D), const),
                  pl.BlockSpec((D, D_XA), const), kv_spec, kv_spec,
                  pl.BlockSpec((1, HEAD_DIM), const), pl.BlockSpec((D_XA, D), const)],
        out_specs=pl.BlockSpec((tm, D), lambda i: (i, 0)),
        out_shape=jax.ShapeDtypeStruct((M, D), F32),
        compiler_params=_params(("parallel",)),
        name="cross_attention",
    )(x, xa_norm.reshape(1, D), wq_bf16, k, v, q_gain.reshape(1, HEAD_DIM), wo_bf16)


def _reorder_w_in(w_in):
    o = np.cumsum((D_NSA, 6 * D_KV_NSA, 3 * NSA_HEADS, 3 * D_GDN, GDN_HEADS, GDN_HEADS, D_GDN))
    q_a, kv_a, gate_a, qkv_b = (0, o[0]), (o[0], o[1]), o[1], (o[2], o[3])
    beta_b, a_b, z_b = o[3], o[4], (o[5], o[6])
    Hg = NSA_GROUP_SIZE
    src = np.zeros((256,), np.int64)
    used = np.zeros((256,), bool)
    for g in range(NSA_KV_GROUPS):
        for br in range(3):
            for h in range(Hg):
                src[g * HEAD_DIM + br * Hg + h] = gate_a + (g * Hg + h) * 3 + br
                used[g * HEAD_DIM + br * Hg + h] = True
    for h in range(GDN_HEADS):
        src[SM_BETA + h] = beta_b + h
        src[SM_A + h] = a_b + h
        used[SM_BETA + h] = used[SM_A + h] = True
    small = jnp.where(jnp.asarray(used)[None, :], jnp.take(w_in, jnp.asarray(src), axis=1), 0.0)
    parts = [w_in[:, qkv_b[0]:qkv_b[1]], w_in[:, q_a[0]:q_a[1]], w_in[:, z_b[0]:z_b[1]], w_in[:, kv_a[0]:kv_a[1]], small]
    return jnp.concatenate([p.astype(BF16) for p in parts], axis=1)


def mixer_layer(x2, B, S, attn_norm, w_in, nsa_q_gain, nsa_k_gain, nsa_cmp_pos, nsa_cmp_w1, nsa_cmp_w2, nsa_out_gain,
                gdn_conv_w, gdn_A_log, gdn_dt_bias, gdn_out_gain, w_out, tm=1024):
    h = rmsnorm_bf16(x2, attn_norm)
    z = matmul_cols(h, _reorder_w_in(w_in), tm=tm, tn=256, name="in_proj")
    raw, ks, vs, kw, vw = nsa_prep(z, nsa_k_gain, B, S)
    comp = nsa_compress(raw, nsa_cmp_pos, nsa_cmp_w1, nsa_cmp_w2, nsa_k_gain[0], B, S)
    y_a = nsa_attention(z, comp, ks, vs, kw, vw, nsa_q_gain, nsa_out_gain, B, S)
    u, wq, ak, egl = gdn_chunks(z, gdn_conv_w, gdn_A_log, gdn_dt_bias, B, S)
    y_b = gdn_scan(u, wq, ak, egl, z, gdn_out_gain, B, S)
    return matmul2_residual(y_a, y_b, w_out, x2, tm=tm, tn=256, name="out_proj")


def xa_layer(x2, mem, B, S, xa_norm, mem_norm, xa_wq, xa_wkv, xa_q_gain, xa_k_gain, xa_wo):
    k, v = xa_kv(mem, mem_norm, xa_wkv.astype(BF16), xa_k_gain)
    return cross_attention(x2, xa_norm, xa_wq.astype(BF16), k, v, xa_q_gain, xa_wo.astype(BF16), B, S)


def dense_ffn_layer(x2, ffn_norm, w13, w2, tm=1024):
    h = rmsnorm_bf16(x2, ffn_norm)
    act = swiglu_up(h, w13, tm=tm, tn=256)
    return down_proj_residual(act, w2, x2, tm=min(2 * tm, x2.shape[0]), tn=1024, tk=512, tk_tail=256)


def moe_ffn_layer(x2, ffn_norm, router_w, w13, w2, tm=512, tn=512):
    h, idx, wts = rmsnorm_router(x2, ffn_norm, router_w)
    src_tok, row_w, pos, tile_expert, n_valid = moe_dispatch(idx[:, :TOP_K], wts[:, :TOP_K], tm)
    xs = jnp.take(h, src_tok, axis=0)
    act = moe_up(xs, w13, tile_expert, n_valid, tm=tm, tn=tn)
    row_w_rep = jnp.broadcast_to(row_w[:, None], (row_w.shape[0], 128))
    out = moe_down(act, w2, row_w_rep, tile_expert, n_valid, tm=tm, tn=tn)
    return x2 + (jnp.take(out, pos[:, 0], axis=0) + jnp.take(out, pos[:, 1], axis=0))


def kernel(x, mem, attn_norm, w_in, nsa_q_gain, nsa_k_gain, nsa_cmp_pos, nsa_cmp_w1, nsa_cmp_w2, nsa_out_gain, gdn_conv_w, gdn_A_log, gdn_dt_bias, gdn_out_gain, w_out, xa_norm, mem_norm, xa_wq, xa_wkv, xa_q_gain, xa_k_gain, xa_wo, ffn_norm, dense_w13, dense_w2, router_w, moe_w13, moe_w2):
    B, S, D = x.shape
    x2 = x.reshape(B * S, D)
    for l in range(attn_norm.shape[0]):
        x2 = mixer_layer(x2, B, S, attn_norm[l], w_in[l], nsa_q_gain[l], nsa_k_gain[l], nsa_cmp_pos[l], nsa_cmp_w1[l],
                         nsa_cmp_w2[l], nsa_out_gain[l], gdn_conv_w[l], gdn_A_log[l], gdn_dt_bias[l], gdn_out_gain[l],
                         w_out[l])
        x2 = xa_layer(x2, mem, B, S, xa_norm[l], mem_norm[l], xa_wq[l], xa_wkv[l], xa_q_gain[l], xa_k_gain[l], xa_wo[l])
        if l % 2 == 0:
            x2 = dense_ffn_layer(x2, ffn_norm[l], dense_w13[l // 2], dense_w2[l // 2])
        else:
            x2 = moe_ffn_layer(x2, ffn_norm[l], router_w[l // 2], moe_w13[l // 2], moe_w2[l // 2])
    return x2.reshape(B, S, D)
```

```python
import functools

import jax
import jax.numpy as jnp
import numpy as np
from jax import lax
from jax.experimental import pallas as pl
from jax.experimental.pallas import tpu as pltpu

F32 = jnp.float32
BF16 = jnp.bfloat16
I32 = jnp.int32

D_MODEL = 4096
HEAD_DIM = 128
EPS = 1e-6
SCALE = HEAD_DIM ** -0.5

NSA_HEADS = 16
NSA_KV_GROUPS = 2
NSA_GROUP_SIZE = NSA_HEADS // NSA_KV_GROUPS
CMP_BLOCK = 32
CMP_STRIDE = 16
SEL_BLOCK = 64
SEL_TOP_N = 16
SEL_LOCAL = 2
WINDOW = 512
Q_BLOCK = 128
FORCE = 1e9
NEG = -1e30
SEL_TILE = 512
LOG2E = 1.4426950408889634

GDN_HEADS = 16
GDN_CONV = 4
GDN_CHUNK = 64
GDN_PACK = 4
GDN_SUB = 16

D_NSA = NSA_HEADS * HEAD_DIM
D_KV_NSA = NSA_KV_GROUPS * HEAD_DIM
D_GDN = GDN_HEADS * HEAD_DIM
D_MIX = D_NSA + D_GDN

XA_HEADS = 4
D_XA = XA_HEADS * HEAD_DIM
N_EXPERTS = 8
TOP_K = 2

COL_QKV_B = 0
COL_Q_A = COL_QKV_B + 3 * D_GDN
COL_Z_B = COL_Q_A + D_NSA
COL_KV_A = COL_Z_B + D_GDN
COL_SMALL = COL_KV_A + 6 * D_KV_NSA
N_IN_R = COL_SMALL + 256
SM_BETA = 32
SM_A = 48

VMEM_LIMIT = 58 * 1024 * 1024

NT_DIMS = (((1,), (1,)), ((), ()))


def _rms(x, gain):
    return x * lax.rsqrt(jnp.mean(x * x, axis=-1, keepdims=True) + EPS) * gain


def _dot(a, b):
    return jnp.dot(a, b, preferred_element_type=F32)


def _dot_nt(a, b):
    return lax.dot_general(a, b, NT_DIMS, preferred_element_type=F32)


def _dot3(x, sel):
    hi = x.astype(BF16)
    rem = x - hi.astype(F32)
    mid = rem.astype(BF16)
    lo = (rem - mid.astype(F32)).astype(BF16)
    return _dot(hi, sel) + _dot(mid, sel) + _dot(lo, sel)


def _params(sem, vmem=VMEM_LIMIT):
    return pltpu.CompilerParams(dimension_semantics=sem, vmem_limit_bytes=vmem)


def _norm_kernel(x_ref, g_ref, o_ref):
    o_ref[...] = _rms(x_ref[...], g_ref[...]).astype(o_ref.dtype)


def rmsnorm_bf16(x, gain, tm=512):
    M, D = x.shape
    return pl.pallas_call(
        _norm_kernel,
        grid=(M // tm,),
        in_specs=[pl.BlockSpec((tm, D), lambda i: (i, 0)), pl.BlockSpec((1, D), lambda i: (0, 0))],
        out_specs=pl.BlockSpec((tm, D), lambda i: (i, 0)),
        out_shape=jax.ShapeDtypeStruct((M, D), BF16),
        compiler_params=_params(("parallel",)),
        name="rmsnorm",
    )(x, gain.reshape(1, D))


def _route(h, rw):
    logits = jnp.dot(h, rw, preferred_element_type=F32, precision=lax.Precision.HIGHEST)
    lane = lax.broadcasted_iota(I32, logits.shape, 1).astype(F32)
    logits = jnp.where(lane < N_EXPERTS, logits, -jnp.inf)
    m1 = jnp.max(logits, axis=-1, keepdims=True)
    i1 = jnp.min(jnp.where(logits == m1, lane, 128.0), axis=-1, keepdims=True)
    rest = jnp.where(lane == i1, -jnp.inf, logits)
    m2 = jnp.max(rest, axis=-1, keepdims=True)
    i2 = jnp.min(jnp.where(rest == m2, lane, 128.0), axis=-1, keepdims=True)
    e2 = jnp.exp(m2 - m1)
    den = 1.0 + e2
    idx = jnp.where(lane == 0, i1, jnp.where(lane == 1, i2, 0.0)).astype(I32)
    wts = jnp.where(lane == 0, 1.0 / den, jnp.where(lane == 1, e2 / den, 0.0))
    return idx, wts


def _mm_kernel(x_ref, w_ref, o_ref):
    o_ref[...] = _dot(x_ref[...], w_ref[...].astype(BF16)).astype(o_ref.dtype)


def _mm_res_kernel(x_ref, w_ref, r_ref, o_ref):
    o_ref[...] = r_ref[...] + _dot(x_ref[...], w_ref[...].astype(BF16))


def matmul_cols(x, w, *, tm, tn, out_dtype=F32, residual=None, name="matmul"):
    M, K = x.shape
    N = w.shape[1]
    in_specs = [pl.BlockSpec((tm, K), lambda i, j: (i, 0)), pl.BlockSpec((K, tn), lambda i, j: (0, j))]
    args = [x, w]
    kern = _mm_kernel
    if residual is not None:
        in_specs.append(pl.BlockSpec((tm, tn), lambda i, j: (i, j)))
        args.append(residual)
        kern = _mm_res_kernel
    return pl.pallas_call(
        kern,
        grid=(M // tm, N // tn),
        in_specs=in_specs,
        out_specs=pl.BlockSpec((tm, tn), lambda i, j: (i, j)),
        out_shape=jax.ShapeDtypeStruct((M, N), out_dtype),
        compiler_params=_params(("parallel", "parallel")),
        name=name,
    )(*args)


def _mm2_res_kernel(xa_ref, xb_ref, wa_ref, wb_ref, r_ref, o_ref):
    acc = _dot(xa_ref[...], wa_ref[...].astype(BF16)) + _dot(xb_ref[...], wb_ref[...].astype(BF16))
    o_ref[...] = r_ref[...] + acc


def matmul2_residual(xa, xb, w, residual, *, tm, tn, name):
    M, Ka = xa.shape
    Kb = xb.shape[1]
    assert Ka == Kb and w.shape[0] == Ka + Kb
    N = w.shape[1]
    return pl.pallas_call(
        _mm2_res_kernel,
        grid=(M // tm, N // tn),
        in_specs=[pl.BlockSpec((tm, Ka), lambda i, j: (i, 0)), pl.BlockSpec((tm, Kb), lambda i, j: (i, 0)),
                  pl.BlockSpec((Ka, tn), lambda i, j: (0, j)), pl.BlockSpec((Kb, tn), lambda i, j: (1, j)),
                  pl.BlockSpec((tm, tn), lambda i, j: (i, j))],
        out_specs=pl.BlockSpec((tm, tn), lambda i, j: (i, j)),
        out_shape=jax.ShapeDtypeStruct((M, N), F32),
        compiler_params=_params(("parallel", "parallel")),
        name=name,
    )(xa, xb, w, w, residual)


def _swiglu_up_kernel(x_ref, wg_ref, wu_ref, o_ref):
    x = x_ref[...]
    gate = _dot(x, wg_ref[...].astype(BF16))
    up = _dot(x, wu_ref[...].astype(BF16))
    o_ref[...] = (gate * jax.nn.sigmoid(gate) * up).astype(o_ref.dtype)


def swiglu_up(x, w13, *, tm, tn):
    M, K = x.shape
    F = w13.shape[1] // 2
    nj = F // tn
    return pl.pallas_call(
        _swiglu_up_kernel,
        grid=(M // tm, nj),
        in_specs=[pl.BlockSpec((tm, K), lambda i, j: (i, 0)),
                  pl.BlockSpec((K, tn), lambda i, j: (0, j)),
                  pl.BlockSpec((K, tn), lambda i, j: (0, j + nj))],
        out_specs=pl.BlockSpec((tm, tn), lambda i, j: (i, j)),
        out_shape=jax.ShapeDtypeStruct((M, F), BF16),
        compiler_params=_params(("parallel", "parallel")),
        name="swiglu_up",
    )(x, w13, w13)


def _down_res_kernel(a_ref, w_ref, at_ref, wt_ref, r_ref, o_ref, *, n_main):
    k = pl.program_id(2)

    @pl.when(k == 0)
    def _():
        o_ref[...] = r_ref[...]

    @pl.when(k < n_main)
    def _():
        o_ref[...] += _dot(a_ref[...], w_ref[...].astype(BF16))

    @pl.when(k >= n_main)
    def _():
        o_ref[...] += _dot(at_ref[...], wt_ref[...].astype(BF16))


def down_proj_residual(a, w2, residual, *, tm, tn, tk, tk_tail):
    M, F = a.shape
    N = w2.shape[1]
    n_main = F // tk
    tail = F - n_main * tk
    assert tail % tk_tail == 0 and (n_main * tk) % tk_tail == 0
    n_tail = tail // tk_tail
    first_tail = (n_main * tk) // tk_tail if n_tail else 0
    main_k = lambda k: jnp.minimum(k, n_main - 1)
    tail_k = lambda k: first_tail + jnp.maximum(k - n_main, 0)
    return pl.pallas_call(
        functools.partial(_down_res_kernel, n_main=n_main),
        grid=(M // tm, N // tn, n_main + n_tail),
        in_specs=[pl.BlockSpec((tm, tk), lambda i, j, k: (i, main_k(k))),
                  pl.BlockSpec((tk, tn), lambda i, j, k: (main_k(k), j)),
                  pl.BlockSpec((tm, tk_tail), lambda i, j, k: (i, tail_k(k))),
                  pl.BlockSpec((tk_tail, tn), lambda i, j, k: (tail_k(k), j)),
                  pl.BlockSpec((tm, tn), lambda i, j, k: (i, j))],
        out_specs=pl.BlockSpec((tm, tn), lambda i, j, k: (i, j)),
        out_shape=jax.ShapeDtypeStruct((M, N), F32),
        compiler_params=_params(("parallel", "parallel", "arbitrary")),
        name="down_proj",
    )(a, w2, a, w2, residual)


def _new_weights(te_ref, i):
    return jnp.logical_or(i == 0, te_ref[i] != te_ref[jnp.maximum(i - 1, 0)])


def _moe_up_kernel(te_ref, nv_ref, x_ref, wg_ref, wu_ref, o_ref, wgb_ref, wub_ref):
    i = pl.program_id(1)
    valid = i < nv_ref[0]

    @pl.when(_new_weights(te_ref, i))
    def _():
        wgb_ref[...] = wg_ref[...].astype(BF16)
        wub_ref[...] = wu_ref[...].astype(BF16)

    @pl.when(valid)
    def _():
        x = x_ref[...]
        gate = _dot(x, wgb_ref[...])
        up = _dot(x, wub_ref[...])
        o_ref[...] = (gate * jax.nn.sigmoid(gate) * up).astype(o_ref.dtype)

    @pl.when(jnp.logical_not(valid))
    def _():
        o_ref[...] = jnp.zeros_like(o_ref)


def moe_up(xs, w13, tile_expert, n_valid, *, tm, tn):
    P, K = xs.shape
    F = w13.shape[2] // 2
    nj = F // tn
    return pl.pallas_call(
        _moe_up_kernel,
        grid_spec=pltpu.PrefetchScalarGridSpec(
            num_scalar_prefetch=2,
            grid=(nj, P // tm),
            in_specs=[pl.BlockSpec((tm, K), lambda j, i, te, nv: (jnp.minimum(i, nv[0] - 1), 0)),
                      pl.BlockSpec((None, K, tn), lambda j, i, te, nv: (te[i], 0, j)),
                      pl.BlockSpec((None, K, tn), lambda j, i, te, nv: (te[i], 0, j + nj))],
            out_specs=pl.BlockSpec((tm, tn), lambda j, i, te, nv: (i, j)),
            scratch_shapes=[pltpu.VMEM((K, tn), BF16), pltpu.VMEM((K, tn), BF16)],
        ),
        out_shape=jax.ShapeDtypeStruct((P, F), BF16),
        compiler_params=_params(("arbitrary", "arbitrary")),
        name="moe_up",
    )(tile_expert, n_valid, xs, w13, w13)


def _moe_down_kernel(te_ref, nv_ref, a_ref, w_ref, rw_ref, o_ref, wb_ref):
    i = pl.program_id(1)
    valid = i < nv_ref[0]

    @pl.when(_new_weights(te_ref, i))
    def _():
        wb_ref[...] = w_ref[...].astype(BF16)

    @pl.when(valid)
    def _():
        o_ref[...] = _dot(a_ref[...], wb_ref[...]) * rw_ref[:, 0:1]

    @pl.when(jnp.logical_not(valid))
    def _():
        o_ref[...] = jnp.zeros_like(o_ref)


def moe_down(act, w2, row_w, tile_expert, n_valid, *, tm, tn):
    P, F = act.shape
    N = w2.shape[2]
    return pl.pallas_call(
        _moe_down_kernel,
        grid_spec=pltpu.PrefetchScalarGridSpec(
            num_scalar_prefetch=2,
            grid=(N // tn, P // tm),
            in_specs=[pl.BlockSpec((tm, F), lambda j, i, te, nv: (jnp.minimum(i, nv[0] - 1), 0)),
                      pl.BlockSpec((None, F, tn), lambda j, i, te, nv: (te[i], 0, j)),
                      pl.BlockSpec((tm, 128), lambda j, i, te, nv: (i, 0))],
            out_specs=pl.BlockSpec((tm, tn), lambda j, i, te, nv: (i, j)),
            scratch_shapes=[pltpu.VMEM((F, tn), BF16)],
        ),
        out_shape=jax.ShapeDtypeStruct((P, N), F32),
        compiler_params=_params(("arbitrary", "arbitrary")),
        name="moe_down",
    )(tile_expert, n_valid, act, w2, row_w)


def moe_dispatch(top_i, top_w, tm):
    M = top_i.shape[0]
    A = M * TOP_K
    P = A + N_EXPERTS * tm
    e_flat = top_i.reshape(A)
    onehot = (e_flat[:, None] == jnp.arange(N_EXPERTS, dtype=I32)[None, :]).astype(I32)
    csum = jnp.cumsum(onehot, axis=0)
    counts = csum[-1]
    padded = ((counts + tm - 1) // tm) * tm
    pend = jnp.cumsum(padded)
    pstart = pend - padded
    dest = jnp.sum(onehot * (pstart[None, :] + csum - onehot), axis=1)
    src_tok = jnp.zeros((P,), I32).at[dest].set(jnp.arange(A, dtype=I32) // TOP_K)
    row_w = jnp.zeros((P,), F32).at[dest].set(top_w.reshape(A))
    pos = dest.reshape(M, TOP_K)
    tile_start = jnp.arange(P // tm, dtype=I32) * tm
    tile_expert = jnp.minimum(jnp.sum((tile_start[:, None] >= pend[None, :]).astype(I32), axis=1), N_EXPERTS - 1)
    n_valid = (pend[-1:] // tm).astype(I32)
    last_e = tile_expert[jnp.maximum(n_valid[0] - 1, 0)]
    tile_expert = jnp.where(tile_start // tm < n_valid[0], tile_expert, last_e)
    return src_tok, row_w, pos, tile_expert, n_valid


def _nsa_prep_kernel(kc_ref, vc_ref, ks_ref, vs_ref, kw_ref, vw_ref, kg_ref, raw_ref, kso_ref, vso_ref, kwo_ref, vwo_ref):
    raw_ref[0] = kc_ref[...]
    raw_ref[1] = vc_ref[...]
    kg = kg_ref[...]
    kso_ref[...] = _rms(ks_ref[...], kg[1:2]).astype(BF16)
    vso_ref[...] = vs_ref[...].astype(BF16)
    kwo_ref[...] = _rms(kw_ref[...], kg[2:3]).astype(BF16)
    vwo_ref[...] = vw_ref[...].astype(BF16)


def nsa_prep(z, k_gain, B, S, ts=512):
    G = NSA_KV_GROUPS
    nt = S // ts
    cb = COL_KV_A // HEAD_DIM

    def col(br, kvi):
        return pl.BlockSpec((ts, HEAD_DIM), lambda b, g, t: (b * nt + t, cb + br * 4 + kvi * 2 + g))

    kv_out = pl.BlockSpec((None, None, ts, HEAD_DIM), lambda b, g, t: (b, g, t, 0))
    kv_shape = jax.ShapeDtypeStruct((B, G, S, HEAD_DIM), BF16)
    return pl.pallas_call(
        _nsa_prep_kernel,
        grid=(B, G, nt),
        in_specs=[col(0, 0), col(0, 1), col(1, 0), col(1, 1), col(2, 0), col(2, 1),
                  pl.BlockSpec((3, HEAD_DIM), lambda b, g, t: (0, 0))],
        out_specs=[pl.BlockSpec((None, 2, None, ts, HEAD_DIM), lambda b, g, t: (b, 0, g, t, 0)),
                   kv_out, kv_out, kv_out, kv_out],
        out_shape=[jax.ShapeDtypeStruct((B, 2, G, S, HEAD_DIM), F32), kv_shape, kv_shape, kv_shape, kv_shape],
        compiler_params=_params(("parallel", "parallel", "parallel")),
        name="nsa_prep",
    )(z, z, z, z, z, z, k_gain)


def _nsa_compress_kernel(r_ref, pos_ref, w1_ref, w2_ref, kg_ref, o_ref):
    kv = pl.program_id(1)
    half = (CMP_BLOCK // 2) * HEAD_DIM
    r = r_ref[...]
    pos = pos_ref[...]
    n = r.shape[0]
    first = _dot((r + pos[:, :half]).astype(BF16), w1_ref[:half, :].astype(BF16))
    second = _dot((r + pos[:, half:]).astype(BF16), w1_ref[half:, :].astype(BF16))
    pre = first + pltpu.roll(second, n - 1, axis=0)
    hid = jax.nn.gelu(pre)
    comp = _dot(hid.astype(BF16), w2_ref[...].astype(BF16))
    o_ref[...] = jnp.where(kv == 0, _rms(comp, kg_ref[...]), comp).astype(o_ref.dtype)


def nsa_compress(raw, cmp_pos, cmp_w1, cmp_w2, k_gain0, B, S):
    G = NSA_KV_GROUPS
    n = S // CMP_STRIDE
    feat = CMP_STRIDE * HEAD_DIM
    r = raw.reshape(B, 2, G, n, feat)
    pos = cmp_pos.reshape(2, 1, CMP_BLOCK * HEAD_DIM)
    return pl.pallas_call(
        _nsa_compress_kernel,
        grid=(B, 2, G),
        in_specs=[pl.BlockSpec((None, None, None, n, feat), lambda b, kv, g: (b, kv, g, 0, 0)),
                  pl.BlockSpec((None, 1, CMP_BLOCK * HEAD_DIM), lambda b, kv, g: (kv, 0, 0)),
                  pl.BlockSpec((None, CMP_BLOCK * HEAD_DIM, HEAD_DIM), lambda b, kv, g: (kv, 0, 0)),
                  pl.BlockSpec((None, HEAD_DIM, HEAD_DIM), lambda b, kv, g: (kv, 0, 0)),
                  pl.BlockSpec((1, HEAD_DIM), lambda b, kv, g: (0, 0))],
        out_specs=pl.BlockSpec((None, None, None, n, HEAD_DIM), lambda b, kv, g: (b, kv, g, 0, 0)),
        out_shape=jax.ShapeDtypeStruct((B, 2, G, n, HEAD_DIM), BF16),
        compiler_params=_params(("parallel", "parallel", "parallel")),
        name="nsa_compress",
    )(r, pos, cmp_w1, cmp_w2, k_gain0.reshape(1, HEAD_DIM))


def _rep(x, n):
    return x if n == HEAD_DIM else jnp.concatenate([x] * (n // HEAD_DIM), axis=1)


def _flash_step(qb, kt, vt, bias, m_ref, l_ref, acc_ref):
    nk = kt.shape[0]
    rows = qb.shape[0]
    s = _dot_nt(qb, kt)
    s = (s.reshape(rows // Q_BLOCK, Q_BLOCK, nk) + bias[None]).reshape(rows, nk)
    m_prev = m_ref[...]
    m_new = jnp.maximum(m_prev, jnp.max(s, axis=-1, keepdims=True))
    alpha = jnp.exp2(m_prev - m_new)
    p = jnp.exp2(s - _rep(m_new, nk))
    l_ref[...] = alpha * l_ref[...] + jnp.sum(p, axis=-1, keepdims=True)
    acc_ref[...] = alpha * acc_ref[...] + _dot(p.astype(BF16), vt)
    m_ref[...] = m_new


def _nsa_kernel(q_ref, sm_ref, kc_ref, vc_ref, ks_ref, vs_ref, kw_ref, vw_ref, qg_ref, og_ref, c2s_ref, exp_ref,
                o_ref, selb_ref, m_ref, l_ref, acc_ref, *, k_top):
    Hg = NSA_GROUP_SIZE
    rows = Hg * Q_BLOCK
    c = pl.program_id(2)
    t0 = c * Q_BLOCK
    n_cmp = kc_ref.shape[0]
    n_sel = c2s_ref.shape[0]

    q = q_ref[...]
    qs = jnp.concatenate([q[:, h * HEAD_DIM:(h + 1) * HEAD_DIM] for h in range(Hg)], axis=0)
    qb = (_rms(qs, qg_ref[...]) * (SCALE * LOG2E)).astype(BF16)

    tq = t0 + lax.broadcasted_iota(I32, (Q_BLOCK, n_cmp), 0)
    c_end = lax.broadcasted_iota(I32, (Q_BLOCK, n_cmp), 1) * CMP_STRIDE + (CMP_BLOCK - 1)
    sc = _dot_nt(qb, kc_ref[...]).reshape(Hg, Q_BLOCK, n_cmp)
    sc = jnp.where((c_end <= tq)[None], sc, -jnp.inf)
    mc = jnp.max(sc, axis=-1, keepdims=True)
    mc = jnp.where(mc > -jnp.inf, mc, 0.0)
    pc = jnp.exp2(sc - mc)
    dc = jnp.sum(pc, axis=-1, keepdims=True)
    pc = pc / jnp.where(dc > 0, dc, 1.0)
    o_cmp = _dot(pc.reshape(rows, n_cmp).astype(BF16), vc_ref[...])

    psum = jnp.sum(pc, axis=0)
    c2s = c2s_ref[...]
    p_hi = psum.astype(BF16)
    rem = psum - p_hi.astype(F32)
    p_mid = rem.astype(BF16)
    p_lo = (rem - p_mid.astype(F32)).astype(BF16)
    imp = _dot_nt(c2s, p_hi) + _dot_nt(c2s, p_mid) + _dot_nt(c2s, p_lo)

    jj = lax.broadcasted_iota(I32, (n_sel, Q_BLOCK), 0)
    cur = lax.shift_right_logical(t0 + lax.broadcasted_iota(I32, (n_sel, Q_BLOCK), 1), 6)
    valid = jj <= cur
    forced = valid & ((jj == 0) | (jj > cur - SEL_LOCAL))
    score = jnp.where(forced, FORCE, jnp.where(valid, imp, -FORCE))
    rank = jnp.zeros((n_sel, Q_BLOCK), F32)
    for j2 in range(n_sel):
        other = score[j2:j2 + 1, :]
        tie = jnp.where(jj > j2, 1.0, 0.0)
        rank = rank + jnp.where(other > score, 1.0, jnp.where(other == score, tie, 0.0))
    sel_t = jnp.where(rank < k_top, 1.0, 0.0)
    if n_sel < Q_BLOCK:
        sel_t = jnp.concatenate([sel_t, jnp.zeros((Q_BLOCK - n_sel, Q_BLOCK), F32)], axis=0)
    sel = sel_t.T[:, :n_sel].astype(BF16)
    sel_keys = _dot(sel, exp_ref[...])
    for i in range(selb_ref.shape[0]):
        selb_ref[i] = (sel_keys[:, i * SEL_TILE:(i + 1) * SEL_TILE] - 1.0) * (-NEG)

    def reset():
        m_ref[...] = jnp.full(m_ref.shape, NEG, F32)
        l_ref[...] = jnp.zeros(l_ref.shape, F32)
        acc_ref[...] = jnp.zeros(acc_ref.shape, F32)

    reset()
    tq_s = t0 + lax.broadcasted_iota(I32, (Q_BLOCK, SEL_TILE), 0)
    kk_s = lax.broadcasted_iota(I32, (Q_BLOCK, SEL_TILE), 1)

    def sel_body(i, carry):
        k0 = pl.multiple_of(i * SEL_TILE, SEL_TILE)
        bias = selb_ref[i] + jnp.where(kk_s + k0 <= tq_s, 0.0, NEG)
        _flash_step(qb, ks_ref[pl.ds(k0, SEL_TILE), :], vs_ref[pl.ds(k0, SEL_TILE), :], bias, m_ref, l_ref, acc_ref)
        return carry

    lax.fori_loop(0, (t0 + Q_BLOCK + SEL_TILE - 1) // SEL_TILE, sel_body, 0)
    o_sel = acc_ref[...] / l_ref[...]

    wk = WINDOW + Q_BLOCK
    w0 = pl.multiple_of(jnp.maximum(t0 - WINDOW, 0), Q_BLOCK)
    tq_w = t0 + lax.broadcasted_iota(I32, (Q_BLOCK, wk), 0)
    wpos = w0 + lax.broadcasted_iota(I32, (Q_BLOCK, wk), 1)
    w_bias = jnp.where((wpos <= tq_w) & (wpos > tq_w - WINDOW), 0.0, NEG)
    sw = _dot_nt(qb, kw_ref[pl.ds(w0, wk), :]).reshape(Hg, Q_BLOCK, wk) + w_bias[None]
    pw = jnp.exp2(sw - jnp.max(sw, axis=-1, keepdims=True))
    lw = jnp.sum(pw, axis=-1, keepdims=True).reshape(rows, 1)
    o_win = _dot(pw.reshape(rows, wk).astype(BF16), vw_ref[pl.ds(w0, wk), :]) / lw

    gates = jax.nn.sigmoid(sm_ref[...])
    og = og_ref[...]
    for h in range(Hg):
        r = slice(h * Q_BLOCK, (h + 1) * Q_BLOCK)
        o = (gates[:, h:h + 1] * o_cmp[r] + gates[:, Hg + h:Hg + h + 1] * o_sel[r]
             + gates[:, 2 * Hg + h:2 * Hg + h + 1] * o_win[r])
        o_ref[:, h * HEAD_DIM:(h + 1) * HEAD_DIM] = _rms(o, og).astype(o_ref.dtype)


def nsa_attention(z, comp, ks, vs, kw, vw, q_gain, out_gain, B, S):
    G, Hg = NSA_KV_GROUPS, NSA_GROUP_SIZE
    nq = S // Q_BLOCK
    n_cmp = S // CMP_STRIDE
    n_sel = S // SEL_BLOCK
    k_top = min(SEL_TOP_N, n_sel)
    c_start = np.arange(n_cmp)[:, None] * CMP_STRIDE
    s_start = np.arange(n_sel)[None, :] * SEL_BLOCK
    overlap = np.minimum(c_start + CMP_BLOCK, s_start + SEL_BLOCK) - np.maximum(c_start, s_start)
    assert n_sel <= Q_BLOCK
    c2s = jnp.asarray((np.clip(overlap, 0, None) / CMP_STRIDE).T, dtype=BF16)
    expand = jnp.asarray(np.arange(S)[None, :] // SEL_BLOCK == np.arange(n_sel)[:, None], dtype=BF16)
    rows = Hg * Q_BLOCK
    wq = Hg * HEAD_DIM
    kv_spec = pl.BlockSpec((None, None, S, HEAD_DIM), lambda b, g, c: (b, g, 0, 0))
    const = lambda b, g, c: (0, 0)
    return pl.pallas_call(
        functools.partial(_nsa_kernel, k_top=k_top),
        grid=(B, G, nq),
        in_specs=[pl.BlockSpec((Q_BLOCK, wq), lambda b, g, c: (b * nq + c, COL_Q_A // wq + g)),
                  pl.BlockSpec((Q_BLOCK, HEAD_DIM), lambda b, g, c: (b * nq + c, COL_SMALL // HEAD_DIM + g)),
                  pl.BlockSpec((None, None, None, n_cmp, HEAD_DIM), lambda b, g, c: (b, 0, g, 0, 0)),
                  pl.BlockSpec((None, None, None, n_cmp, HEAD_DIM), lambda b, g, c: (b, 1, g, 0, 0)),
                  kv_spec, kv_spec, kv_spec, kv_spec,
                  pl.BlockSpec((1, HEAD_DIM), const), pl.BlockSpec((1, HEAD_DIM), const),
                  pl.BlockSpec((n_sel, n_cmp), const), pl.BlockSpec((n_sel, S), const)],
        out_specs=pl.BlockSpec((Q_BLOCK, wq), lambda b, g, c: (b * nq + c, g)),
        out_shape=jax.ShapeDtypeStruct((B * S, D_NSA), BF16),
        scratch_shapes=[pltpu.VMEM((S // SEL_TILE, Q_BLOCK, SEL_TILE), F32),
                        pltpu.VMEM((rows, HEAD_DIM), F32), pltpu.VMEM((rows, HEAD_DIM), F32),
                        pltpu.VMEM((rows, HEAD_DIM), F32)],
        compiler_params=_params(("parallel", "parallel", "arbitrary")),
        name="nsa_attention",
    )(z, z, comp, comp, ks, vs, kw, vw, q_gain.reshape(1, HEAD_DIM), out_gain.reshape(1, HEAD_DIM), c2s, expand)


def _gdn_chunk_kernel(xc_ref, xh_ref, sm_ref, cw_ref, alog_ref, dtb_ref, rep_ref, unfold_ref, u_ref, wq_ref, ak_ref,
                      egl_ref):
    C = GDN_CHUNK
    PK = GDN_PACK
    R = PK * C
    n = pl.program_id(1)
    xp = jnp.concatenate([jnp.where(n > 0, xh_ref[...], 0.0), xc_ref[...]], axis=0)
    cw = cw_ref[...]
    y = None
    for j in range(GDN_CONV):
        shift = GDN_CONV - 1 - j
        xs = xp if shift == 0 else pltpu.roll(xp, shift, axis=0)
        term = xs[8:] * cw[j:j + 1, :]
        y = term if y is None else y + term
    y = y * jax.nn.sigmoid(y)

    sm = sm_ref[...]
    beta = jax.nn.sigmoid(sm)
    g = -jnp.exp(alog_ref[...]) * jax.nn.softplus(sm + dtb_ref[...])
    row = lax.broadcasted_iota(I32, g.shape, 0)
    gc = g
    d = 1
    while d < C:
        gc = gc + jnp.where(row >= d, pltpu.roll(gc, d, axis=0), 0.0)
        d *= 2
    g_last = gc[C - 1:C, :]

    ri = lax.broadcasted_iota(I32, (R, R), 0)
    ci = lax.broadcasted_iota(I32, (R, R), 1)
    same = lax.shift_right_logical(ri, 6) == lax.shift_right_logical(ci, 6)
    tri = same & (ri >= ci)
    strict = same & (ri > ci)
    SUB = GDN_SUB
    nb = R // SUB
    same16 = lax.shift_right_logical(ri, 4) == lax.shift_right_logical(ci, 4)
    same32 = lax.shift_right_logical(ri, 5) == lax.shift_right_logical(ci, 5)
    off32 = same32 & jnp.logical_not(same16)
    off64 = jnp.logical_not(same32)
    NG = GDN_HEADS // PK
    W4 = NG * SUB
    row_c = lax.broadcasted_iota(I32, (R, W4), 0) & (SUB - 1)
    col_s = lax.broadcasted_iota(I32, (nb, W4), 1) & (SUB - 1)

    groups = []
    for grp in range(NG):
        heads = [grp * PK + i for i in range(PK)]

        def stack(off):
            return jnp.concatenate([y[:, off + h * HEAD_DIM: off + (h + 1) * HEAD_DIM] for h in heads], axis=0)

        def col(x, base):
            return jnp.concatenate([x[:, base + h: base + h + 1] for h in heads], axis=0)

        q4 = stack(0)
        k4 = stack(D_GDN)
        v4 = stack(2 * D_GDN)
        q4 = q4 * lax.rsqrt(jnp.sum(q4 * q4, axis=-1, keepdims=True) + EPS) * SCALE
        k4 = k4 * lax.rsqrt(jnp.sum(k4 * k4, axis=-1, keepdims=True) + EPS)
        beta4 = col(beta, SM_BETA)
        gc4 = col(gc, SM_A)
        gl4 = jnp.concatenate([jnp.broadcast_to(g_last[:, SM_A + h: SM_A + h + 1], (C, 1)) for h in heads], axis=0)

        gb = jnp.broadcast_to(gc4, (R, R))
        decay = jnp.exp(jnp.where(tri, gb - gb.T, -jnp.inf))
        kb4 = k4 * beta4
        k4b = k4.astype(BF16)
        a = jnp.where(strict, _dot_nt(kb4.astype(BF16), k4b) * decay, 0.0)
        attn = _dot_nt(q4.astype(BF16), k4b) * decay
        at = a.T
        dct = jnp.concatenate([at[SUB * b:SUB * (b + 1), SUB * b:SUB * (b + 1)] for b in range(nb)], axis=0)
        groups.append((heads, q4, k4, v4, beta4, gc4, gl4, kb4, a, attn, dct))

    coef = _dot3(jnp.concatenate([g[-1] for g in groups], axis=1), rep_ref[...])
    t4 = jnp.where(row_c == (lax.broadcasted_iota(I32, (R, W4), 1) & (SUB - 1)), 1.0, 0.0)
    for i in range(1, SUB):
        s = jnp.sum((coef[:, i * HEAD_DIM: i * HEAD_DIM + W4] * t4).reshape(nb, SUB, W4), axis=1)
        new = jnp.where(col_s == i, 1.0, 0.0) - s
        t4 = jnp.where(row_c == i, jnp.broadcast_to(new[:, None, :], (nb, SUB, W4)).reshape(R, W4), t4)
    t_tiled = _dot3(t4, unfold_ref[...])

    for gi, (heads, q4, k4, v4, beta4, gc4, gl4, kb4, a, attn, _) in enumerate(groups):
        t16 = jnp.where(same16, t_tiled[:, gi * R:(gi + 1) * R], 0.0)
        t16b = t16.astype(BF16)
        a32 = jnp.where(off32, a, 0.0).astype(BF16)
        t32 = t16 - _dot(_dot(t16b, a32).astype(BF16), t16b)
        t32b = t32.astype(BF16)
        a64 = jnp.where(off64, a, 0.0).astype(BF16)
        t64 = t32 - _dot(_dot(t32b, a64).astype(BF16), t32b)
        eg = jnp.exp(gc4)
        rhs = jnp.concatenate([v4 * beta4, kb4 * eg], axis=1)
        rhs = _dot(t64.astype(BF16), rhs.astype(BF16))
        qd4 = q4 * eg
        kdt = (k4 * jnp.exp(gl4 - gc4)).T
        for i, h in enumerate(heads):
            r = slice(i * C, (i + 1) * C)
            u_ref[h] = rhs[r, :HEAD_DIM]
            wq_ref[h] = jnp.concatenate([rhs[r, HEAD_DIM:], qd4[r]], axis=0).astype(BF16)
            ak_ref[h] = jnp.concatenate([attn[r, r], kdt[:, r]], axis=0).astype(BF16)
            egl_ref[h] = jnp.broadcast_to(jnp.exp(g_last[:, SM_A + h: SM_A + h + 1]), (8, HEAD_DIM))


def gdn_chunks(z, conv_w, a_log, dt_bias, B, S):
    C, H = GDN_CHUNK, GDN_HEADS
    N = S // C
    W = 3 * D_GDN
    pad = lambda v: jnp.zeros((1, HEAD_DIM), F32).at[0, SM_A:SM_A + H].set(v)
    out5 = lambda r, cdim: pl.BlockSpec((None, None, H, r, cdim), lambda b, n: (b, n, 0, 0, 0))
    NG, SUB, R = H // GDN_PACK, GDN_SUB, GDN_PACK * C
    W4 = NG * SUB
    src = np.arange(W4)[:, None]
    dst = np.arange(SUB * HEAD_DIM)[None, :]
    lane = dst % HEAD_DIM
    rep = (lane < W4) & (src // SUB == lane // SUB) & (src % SUB == dst // HEAD_DIM)
    dst = np.arange(NG * R)[None, :]
    unfold = (src // SUB == dst // R) & (src % SUB == dst % SUB)
    return pl.pallas_call(
        _gdn_chunk_kernel,
        grid=(B, N),
        in_specs=[pl.BlockSpec((C, W), lambda b, n: (b * N + n, 0)),
                  pl.BlockSpec((8, W), lambda b, n: (jnp.maximum((b * N + n) * (C // 8) - 1, 0), 0)),
                  pl.BlockSpec((C, HEAD_DIM), lambda b, n: (b * N + n, COL_SMALL // HEAD_DIM)),
                  pl.BlockSpec((GDN_CONV, W), lambda b, n: (0, 0)),
                  pl.BlockSpec((1, HEAD_DIM), lambda b, n: (0, 0)),
                  pl.BlockSpec((1, HEAD_DIM), lambda b, n: (0, 0)),
                  pl.BlockSpec((W4, SUB * HEAD_DIM), lambda b, n: (0, 0)),
                  pl.BlockSpec((W4, NG * R), lambda b, n: (0, 0))],
        out_specs=[out5(C, HEAD_DIM), out5(2 * C, HEAD_DIM), out5(C + HEAD_DIM, C), out5(8, HEAD_DIM)],
        out_shape=[jax.ShapeDtypeStruct((B, N, H, C, HEAD_DIM), F32),
                   jax.ShapeDtypeStruct((B, N, H, 2 * C, HEAD_DIM), BF16),
                   jax.ShapeDtypeStruct((B, N, H, C + HEAD_DIM, C), BF16),
                   jax.ShapeDtypeStruct((B, N, H, 8, HEAD_DIM), F32)],
        compiler_params=_params(("parallel", "parallel")),
        name="gdn_chunks",
    )(z, z, z, conv_w, pad(a_log), pad(dt_bias), jnp.asarray(rep, dtype=BF16), jnp.asarray(unfold, dtype=BF16))


def _gdn_scan_kernel(u_ref, wq_ref, ak_ref, egl_ref, z_ref, og_ref, o_ref, s_ref):
    C = GDN_CHUNK

    @pl.when(pl.program_id(1) == 0)
    def _():
        s_ref[...] = jnp.zeros(s_ref.shape, F32)

    og = og_ref[...]
    for h in range(GDN_HEADS):
        state = s_ref[h]
        ws = _dot(wq_ref[h], state.astype(BF16))
        v_new = u_ref[h] - ws[:C]
        av = _dot(ak_ref[h], v_new.astype(BF16))
        o = ws[C:] + av[:C]
        decayed = (state.reshape(HEAD_DIM // 8, 8, HEAD_DIM) * egl_ref[h][None]).reshape(HEAD_DIM, HEAD_DIM)
        s_ref[h] = decayed + av[C:]
        zh = z_ref[:, h * HEAD_DIM:(h + 1) * HEAD_DIM]
        o_ref[:, h * HEAD_DIM:(h + 1) * HEAD_DIM] = (_rms(o, og) * (zh * jax.nn.sigmoid(zh))).astype(o_ref.dtype)


def gdn_scan(u, wq, ak, egl, z, out_gain, B, S):
    C, H = GDN_CHUNK, GDN_HEADS
    N = S // C
    in5 = lambda r, cdim: pl.BlockSpec((None, None, H, r, cdim), lambda b, n: (b, n, 0, 0, 0))
    return pl.pallas_call(
        _gdn_scan_kernel,
        grid=(B, N),
        in_specs=[in5(C, HEAD_DIM), in5(2 * C, HEAD_DIM), in5(C + HEAD_DIM, C), in5(8, HEAD_DIM),
                  pl.BlockSpec((C, D_GDN), lambda b, n: (b * N + n, COL_Z_B // D_GDN)),
                  pl.BlockSpec((1, HEAD_DIM), lambda b, n: (0, 0))],
        out_specs=pl.BlockSpec((C, D_GDN), lambda b, n: (b * N + n, 0)),
        out_shape=jax.ShapeDtypeStruct((B * S, D_GDN), BF16),
        scratch_shapes=[pltpu.VMEM((H, HEAD_DIM, HEAD_DIM), F32)],
        compiler_params=_params(("parallel", "arbitrary")),
        name="gdn_scan",
    )(u, wq, ak, egl, z, out_gain.reshape(1, HEAD_DIM))


def _xa_kv_kernel(mem_ref, g_ref, w_ref, kg_ref, k_ref, v_ref):
    kv = _dot(_rms(mem_ref[...], g_ref[...]).astype(BF16), w_ref[...])
    for h in range(XA_HEADS):
        k_ref[h] = _rms(kv[:, h * HEAD_DIM:(h + 1) * HEAD_DIM], kg_ref[...]).astype(BF16)
        v_ref[h] = kv[:, D_XA + h * HEAD_DIM: D_XA + (h + 1) * HEAD_DIM].astype(BF16)


def xa_kv(mem, mem_norm, wkv_bf16, k_gain):
    B, Mm, D = mem.shape
    spec = pl.BlockSpec((None, XA_HEADS, Mm, HEAD_DIM), lambda b: (b, 0, 0, 0))
    shape = jax.ShapeDtypeStruct((B, XA_HEADS, Mm, HEAD_DIM), BF16)
    return pl.pallas_call(
        _xa_kv_kernel,
        grid=(B,),
        in_specs=[pl.BlockSpec((None, Mm, D), lambda b: (b, 0, 0)), pl.BlockSpec((1, D), lambda b: (0, 0)),
                  pl.BlockSpec((D, 2 * D_XA), lambda b: (0, 0)), pl.BlockSpec((1, HEAD_DIM), lambda b: (0, 0))],
        out_specs=[spec, spec],
        out_shape=[shape, shape],
        compiler_params=_params(("parallel",)),
        name="xa_kv",
    )(mem, mem_norm.reshape(1, D), wkv_bf16, k_gain.reshape(1, HEAD_DIM))


def _xa_kernel(x_ref, g_ref, wq_ref, k_ref, v_ref, qg_ref, wo_ref, fg_ref, *rest, route):
    if route:
        rw_ref, o_ref, h_ref, idx_ref, wt_ref = rest
    else:
        o_ref, h_ref = rest
    x = x_ref[...]
    q = _dot(_rms(x, g_ref[...]).astype(BF16), wq_ref[...])
    outs = []
    for h in range(XA_HEADS):
        qh = _rms(q[:, h * HEAD_DIM:(h + 1) * HEAD_DIM], qg_ref[...]).astype(BF16)
        s = _dot_nt(qh, k_ref[h]) * SCALE
        p = jnp.exp(s - jnp.max(s, axis=-1, keepdims=True))
        p = p / jnp.sum(p, axis=-1, keepdims=True)
        outs.append(_dot(p.astype(BF16), v_ref[h]))
    o = jnp.concatenate(outs, axis=1).astype(BF16)
    x_new = x + _dot(o, wo_ref[...])
    o_ref[...] = x_new
    hn = _rms(x_new, fg_ref[...])
    h_ref[...] = hn.astype(h_ref.dtype)
    if route:
        idx_ref[...], wt_ref[...] = _route(hn, rw_ref[...])


def cross_attention(x, xa_norm, wq_bf16, k, v, q_gain, wo_bf16, ffn_norm, router_w, B, S, tm=256):
    M, D = x.shape
    Mm = k.shape[2]
    per_b = S // tm
    kv_spec = pl.BlockSpec((None, XA_HEADS, Mm, HEAD_DIM), lambda i: (i // per_b, 0, 0, 0))
    const = lambda i: (0, 0)
    row = lambda i: (i, 0)
    route = router_w is not None
    in_specs = [pl.BlockSpec((tm, D), row), pl.BlockSpec((1, D), const),
                pl.BlockSpec((D, D_XA), const), kv_spec, kv_spec,
                pl.BlockSpec((1, HEAD_DIM), const), pl.BlockSpec((D_XA, D), const), pl.BlockSpec((1, D), const)]
    args = [x, xa_norm.reshape(1, D), wq_bf16, k, v, q_gain.reshape(1, HEAD_DIM), wo_bf16, ffn_norm.reshape(1, D)]
    out_specs = [pl.BlockSpec((tm, D), row), pl.BlockSpec((tm, D), row)]
    out_shape = [jax.ShapeDtypeStruct((M, D), F32), jax.ShapeDtypeStruct((M, D), BF16)]
    if route:
        in_specs.append(pl.BlockSpec((D, 128), const))
        args.append(jnp.pad(router_w, ((0, 0), (0, 128 - N_EXPERTS))))
        out_specs += [pl.BlockSpec((tm, 128), row), pl.BlockSpec((tm, 128), row)]
        out_shape += [jax.ShapeDtypeStruct((M, 128), I32), jax.ShapeDtypeStruct((M, 128), F32)]
    return pl.pallas_call(
        functools.partial(_xa_kernel, route=route),
        grid=(M // tm,),
        in_specs=in_specs,
        out_specs=out_specs,
        out_shape=out_shape,
        compiler_params=_params(("parallel",)),
        name="cross_attention",
    )(*args)


def _reorder_w_in(w_in):
    o = np.cumsum((D_NSA, 6 * D_KV_NSA, 3 * NSA_HEADS, 3 * D_GDN, GDN_HEADS, GDN_HEADS, D_GDN))
    q_a, kv_a, gate_a, qkv_b = (0, o[0]), (o[0], o[1]), o[1], (o[2], o[3])
    beta_b, a_b, z_b = o[3], o[4], (o[5], o[6])
    Hg = NSA_GROUP_SIZE
    src = np.zeros((256,), np.int64)
    used = np.zeros((256,), bool)
    for g in range(NSA_KV_GROUPS):
        for br in range(3):
            for h in range(Hg):
                src[g * HEAD_DIM + br * Hg + h] = gate_a + (g * Hg + h) * 3 + br
                used[g * HEAD_DIM + br * Hg + h] = True
    for h in range(GDN_HEADS):
        src[SM_BETA + h] = beta_b + h
        src[SM_A + h] = a_b + h
        used[SM_BETA + h] = used[SM_A + h] = True
    small = jnp.where(jnp.asarray(used)[None, :], jnp.take(w_in, jnp.asarray(src), axis=1), 0.0)
    parts = [w_in[:, qkv_b[0]:qkv_b[1]], w_in[:, q_a[0]:q_a[1]], w_in[:, z_b[0]:z_b[1]], w_in[:, kv_a[0]:kv_a[1]], small]
    return jnp.concatenate([p.astype(BF16) for p in parts], axis=1)


def mixer_layer(x2, B, S, attn_norm, w_in, nsa_q_gain, nsa_k_gain, nsa_cmp_pos, nsa_cmp_w1, nsa_cmp_w2, nsa_out_gain,
                gdn_conv_w, gdn_A_log, gdn_dt_bias, gdn_out_gain, w_out, tm=1024):
    h = rmsnorm_bf16(x2, attn_norm)
    z = matmul_cols(h, _reorder_w_in(w_in), tm=tm, tn=256, name="in_proj")
    raw, ks, vs, kw, vw = nsa_prep(z, nsa_k_gain, B, S)
    comp = nsa_compress(raw, nsa_cmp_pos, nsa_cmp_w1, nsa_cmp_w2, nsa_k_gain[0], B, S)
    y_a = nsa_attention(z, comp, ks, vs, kw, vw, nsa_q_gain, nsa_out_gain, B, S)
    u, wq, ak, egl = gdn_chunks(z, gdn_conv_w, gdn_A_log, gdn_dt_bias, B, S)
    y_b = gdn_scan(u, wq, ak, egl, z, gdn_out_gain, B, S)
    return matmul2_residual(y_a, y_b, w_out, x2, tm=tm, tn=256, name="out_proj")


def xa_layer(x2, mem, B, S, xa_norm, mem_norm, xa_wq, xa_wkv, xa_q_gain, xa_k_gain, xa_wo, ffn_norm, router_w=None):
    k, v = xa_kv(mem, mem_norm, xa_wkv.astype(BF16), xa_k_gain)
    return cross_attention(x2, xa_norm, xa_wq.astype(BF16), k, v, xa_q_gain, xa_wo.astype(BF16), ffn_norm, router_w,
                           B, S)


def dense_ffn_layer(x2, h, w13, w2, tm=1024):
    act = swiglu_up(h, w13, tm=tm, tn=256)
    return down_proj_residual(act, w2, x2, tm=min(2 * tm, x2.shape[0]), tn=1024, tk=512, tk_tail=256)


def moe_ffn_layer(x2, h, idx, wts, w13, w2, tm=512, tn=512):
    src_tok, row_w, pos, tile_expert, n_valid = moe_dispatch(idx[:, :TOP_K], wts[:, :TOP_K], tm)
    xs = jnp.take(h, src_tok, axis=0)
    act = moe_up(xs, w13, tile_expert, n_valid, tm=tm, tn=tn)
    row_w_rep = jnp.broadcast_to(row_w[:, None], (row_w.shape[0], 128))
    out = moe_down(act, w2, row_w_rep, tile_expert, n_valid, tm=tm, tn=tn)
    return x2 + (jnp.take(out, pos[:, 0], axis=0) + jnp.take(out, pos[:, 1], axis=0))


def kernel(x, mem, attn_norm, w_in, nsa_q_gain, nsa_k_gain, nsa_cmp_pos, nsa_cmp_w1, nsa_cmp_w2, nsa_out_gain, gdn_conv_w, gdn_A_log, gdn_dt_bias, gdn_out_gain, w_out, xa_norm, mem_norm, xa_wq, xa_wkv, xa_q_gain, xa_k_gain, xa_wo, ffn_norm, dense_w13, dense_w2, router_w, moe_w13, moe_w2):
    B, S, D = x.shape
    x2 = x.reshape(B * S, D)
    for l in range(attn_norm.shape[0]):
        x2 = mixer_layer(x2, B, S, attn_norm[l], w_in[l], nsa_q_gain[l], nsa_k_gain[l], nsa_cmp_pos[l], nsa_cmp_w1[l],
                         nsa_cmp_w2[l], nsa_out_gain[l], gdn_conv_w[l], gdn_A_log[l], gdn_dt_bias[l], gdn_out_gain[l],
                         w_out[l])
        xa_args = (x2, mem, B, S, xa_norm[l], mem_norm[l], xa_wq[l], xa_wkv[l], xa_q_gain[l], xa_k_gain[l], xa_wo[l],
                   ffn_norm[l])
        if l % 2 == 0:
            x2, h = xa_layer(*xa_args)
            x2 = dense_ffn_layer(x2, h, dense_w13[l // 2], dense_w2[l // 2])
        else:
            x2, h, idx, wts = xa_layer(*xa_args, router_w[l // 2])
            x2 = moe_ffn_layer(x2, h, idx, wts, moe_w13[l // 2], moe_w2[l // 2])
    return x2.reshape(B, S, D)
```

```python
import functools

import jax
import jax.numpy as jnp
import numpy as np
from jax import lax
from jax.experimental import pallas as pl
from jax.experimental.pallas import tpu as pltpu

F32 = jnp.float32
BF16 = jnp.bfloat16
I32 = jnp.int32

D_MODEL = 4096
HEAD_DIM = 128
EPS = 1e-6
SCALE = HEAD_DIM ** -0.5

NSA_HEADS = 16
NSA_KV_GROUPS = 2
NSA_GROUP_SIZE = NSA_HEADS // NSA_KV_GROUPS
CMP_BLOCK = 32
CMP_STRIDE = 16
SEL_BLOCK = 64
SEL_TOP_N = 16
SEL_LOCAL = 2
WINDOW = 512
Q_BLOCK = 128
FORCE = 1e9
NEG = -1e30
SEL_TILE = 512
LOG2E = 1.4426950408889634

GDN_HEADS = 16
GDN_CONV = 4
GDN_CHUNK = 64
GDN_PACK = 4
GDN_SUB = 16

D_NSA = NSA_HEADS * HEAD_DIM
D_KV_NSA = NSA_KV_GROUPS * HEAD_DIM
D_GDN = GDN_HEADS * HEAD_DIM
D_MIX = D_NSA + D_GDN

XA_HEADS = 4
D_XA = XA_HEADS * HEAD_DIM
N_EXPERTS = 8
TOP_K = 2

COL_QKV_B = 0
COL_Q_A = COL_QKV_B + 3 * D_GDN
COL_Z_B = COL_Q_A + D_NSA
COL_KV_A = COL_Z_B + D_GDN
COL_SMALL = COL_KV_A + 6 * D_KV_NSA
SMALL_W = 512
N_IN_R = COL_SMALL + SMALL_W
SM_BETA = 32
SM_A = 48

VMEM_LIMIT = 58 * 1024 * 1024

NT_DIMS = (((1,), (1,)), ((), ()))


def _rms(x, gain):
    return x * lax.rsqrt(jnp.mean(x * x, axis=-1, keepdims=True) + EPS) * gain


def _dot(a, b):
    return jnp.dot(a, b, preferred_element_type=F32)


def _dot_nt(a, b):
    return lax.dot_general(a, b, NT_DIMS, preferred_element_type=F32)


def _dot3(x, sel):
    hi = x.astype(BF16)
    rem = x - hi.astype(F32)
    mid = rem.astype(BF16)
    lo = (rem - mid.astype(F32)).astype(BF16)
    return _dot(hi, sel) + _dot(mid, sel) + _dot(lo, sel)


def _params(sem, vmem=VMEM_LIMIT):
    return pltpu.CompilerParams(dimension_semantics=sem, vmem_limit_bytes=vmem)


def _norm_kernel(x_ref, g_ref, o_ref):
    o_ref[...] = _rms(x_ref[...], g_ref[...]).astype(o_ref.dtype)


def rmsnorm_bf16(x, gain, tm=512):
    M, D = x.shape
    return pl.pallas_call(
        _norm_kernel,
        grid=(M // tm,),
        in_specs=[pl.BlockSpec((tm, D), lambda i: (i, 0)), pl.BlockSpec((1, D), lambda i: (0, 0))],
        out_specs=pl.BlockSpec((tm, D), lambda i: (i, 0)),
        out_shape=jax.ShapeDtypeStruct((M, D), BF16),
        compiler_params=_params(("parallel",)),
        name="rmsnorm",
    )(x, gain.reshape(1, D))


def _route(h, rw_hi, rw_lo):
    h_hi = h.astype(BF16)
    h_lo = (h - h_hi.astype(F32)).astype(BF16)
    logits = _dot(h_hi, rw_hi) + _dot(h_hi, rw_lo) + _dot(h_lo, rw_hi)
    lane = lax.broadcasted_iota(I32, logits.shape, 1).astype(F32)
    logits = jnp.where(lane < N_EXPERTS, logits, -jnp.inf)
    m1 = jnp.max(logits, axis=-1, keepdims=True)
    i1 = jnp.min(jnp.where(logits == m1, lane, 128.0), axis=-1, keepdims=True)
    rest = jnp.where(lane == i1, -jnp.inf, logits)
    m2 = jnp.max(rest, axis=-1, keepdims=True)
    i2 = jnp.min(jnp.where(rest == m2, lane, 128.0), axis=-1, keepdims=True)
    e2 = jnp.exp(m2 - m1)
    den = 1.0 + e2
    idx = jnp.where(lane == 0, i1, jnp.where(lane == 1, i2, 0.0)).astype(I32)
    wts = jnp.where(lane == 0, 1.0 / den, jnp.where(lane == 1, e2 / den, 0.0))
    return idx, wts


def _mm_kernel(x_ref, w_ref, o_ref):
    o_ref[...] = _dot(x_ref[...], w_ref[...].astype(BF16)).astype(o_ref.dtype)


def _mm_res_kernel(x_ref, w_ref, r_ref, o_ref):
    o_ref[...] = r_ref[...] + _dot(x_ref[...], w_ref[...].astype(BF16))


def matmul_cols(x, w, *, tm, tn, out_dtype=F32, residual=None, name="matmul"):
    M, K = x.shape
    N = w.shape[1]
    in_specs = [pl.BlockSpec((tm, K), lambda i, j: (i, 0)), pl.BlockSpec((K, tn), lambda i, j: (0, j))]
    args = [x, w]
    kern = _mm_kernel
    if residual is not None:
        in_specs.append(pl.BlockSpec((tm, tn), lambda i, j: (i, j)))
        args.append(residual)
        kern = _mm_res_kernel
    return pl.pallas_call(
        kern,
        grid=(M // tm, N // tn),
        in_specs=in_specs,
        out_specs=pl.BlockSpec((tm, tn), lambda i, j: (i, j)),
        out_shape=jax.ShapeDtypeStruct((M, N), out_dtype),
        compiler_params=_params(("parallel", "parallel")),
        name=name,
    )(*args)


def _mm2_res_kernel(xa_ref, xb_ref, wa_ref, wb_ref, r_ref, o_ref):
    acc = _dot(xa_ref[...], wa_ref[...].astype(BF16)) + _dot(xb_ref[...], wb_ref[...].astype(BF16))
    o_ref[...] = r_ref[...] + acc


def matmul2_residual(xa, xb, w, residual, *, tm, tn, name):
    M, Ka = xa.shape
    Kb = xb.shape[1]
    assert Ka == Kb and w.shape[0] == Ka + Kb
    N = w.shape[1]
    return pl.pallas_call(
        _mm2_res_kernel,
        grid=(M // tm, N // tn),
        in_specs=[pl.BlockSpec((tm, Ka), lambda i, j: (i, 0)), pl.BlockSpec((tm, Kb), lambda i, j: (i, 0)),
                  pl.BlockSpec((Ka, tn), lambda i, j: (0, j)), pl.BlockSpec((Kb, tn), lambda i, j: (1, j)),
                  pl.BlockSpec((tm, tn), lambda i, j: (i, j))],
        out_specs=pl.BlockSpec((tm, tn), lambda i, j: (i, j)),
        out_shape=jax.ShapeDtypeStruct((M, N), F32),
        compiler_params=_params(("parallel", "parallel")),
        name=name,
    )(xa, xb, w, w, residual)


def _swiglu_up_kernel(x_ref, wg_ref, wu_ref, o_ref):
    x = x_ref[...]
    gate = _dot(x, wg_ref[...].astype(BF16))
    up = _dot(x, wu_ref[...].astype(BF16))
    o_ref[...] = (gate * jax.nn.sigmoid(gate) * up).astype(o_ref.dtype)


def swiglu_up(x, w13, *, tm, tn):
    M, K = x.shape
    F = w13.shape[1] // 2
    nj = F // tn
    return pl.pallas_call(
        _swiglu_up_kernel,
        grid=(M // tm, nj),
        in_specs=[pl.BlockSpec((tm, K), lambda i, j: (i, 0)),
                  pl.BlockSpec((K, tn), lambda i, j: (0, j)),
                  pl.BlockSpec((K, tn), lambda i, j: (0, j + nj))],
        out_specs=pl.BlockSpec((tm, tn), lambda i, j: (i, j)),
        out_shape=jax.ShapeDtypeStruct((M, F), BF16),
        compiler_params=_params(("parallel", "parallel")),
        name="swiglu_up",
    )(x, w13, w13)


def _down_res_kernel(a_ref, w_ref, at_ref, wt_ref, r_ref, o_ref, *, n_main):
    k = pl.program_id(2)

    @pl.when(k == 0)
    def _():
        o_ref[...] = r_ref[...]

    @pl.when(k < n_main)
    def _():
        o_ref[...] += _dot(a_ref[...], w_ref[...].astype(BF16))

    @pl.when(k >= n_main)
    def _():
        o_ref[...] += _dot(at_ref[...], wt_ref[...].astype(BF16))


def down_proj_residual(a, w2, residual, *, tm, tn, tk, tk_tail):
    M, F = a.shape
    N = w2.shape[1]
    n_main = F // tk
    tail = F - n_main * tk
    assert tail % tk_tail == 0 and (n_main * tk) % tk_tail == 0
    n_tail = tail // tk_tail
    first_tail = (n_main * tk) // tk_tail if n_tail else 0
    main_k = lambda k: jnp.minimum(k, n_main - 1)
    tail_k = lambda k: first_tail + jnp.maximum(k - n_main, 0)
    return pl.pallas_call(
        functools.partial(_down_res_kernel, n_main=n_main),
        grid=(M // tm, N // tn, n_main + n_tail),
        in_specs=[pl.BlockSpec((tm, tk), lambda i, j, k: (i, main_k(k))),
                  pl.BlockSpec((tk, tn), lambda i, j, k: (main_k(k), j)),
                  pl.BlockSpec((tm, tk_tail), lambda i, j, k: (i, tail_k(k))),
                  pl.BlockSpec((tk_tail, tn), lambda i, j, k: (tail_k(k), j)),
                  pl.BlockSpec((tm, tn), lambda i, j, k: (i, j))],
        out_specs=pl.BlockSpec((tm, tn), lambda i, j, k: (i, j)),
        out_shape=jax.ShapeDtypeStruct((M, N), F32),
        compiler_params=_params(("parallel", "parallel", "arbitrary")),
        name="down_proj",
    )(a, w2, a, w2, residual)


def _new_weights(te_ref, i):
    return jnp.logical_or(i == 0, te_ref[i] != te_ref[jnp.maximum(i - 1, 0)])


def _moe_up_kernel(te_ref, nv_ref, x_ref, wg_ref, wu_ref, o_ref, wgb_ref, wub_ref):
    i = pl.program_id(1)
    valid = i < nv_ref[0]

    @pl.when(_new_weights(te_ref, i))
    def _():
        wgb_ref[...] = wg_ref[...].astype(BF16)
        wub_ref[...] = wu_ref[...].astype(BF16)

    @pl.when(valid)
    def _():
        x = x_ref[...]
        gate = _dot(x, wgb_ref[...])
        up = _dot(x, wub_ref[...])
        o_ref[...] = (gate * jax.nn.sigmoid(gate) * up).astype(o_ref.dtype)

    @pl.when(jnp.logical_not(valid))
    def _():
        o_ref[...] = jnp.zeros_like(o_ref)


def moe_up(xs, w13, tile_expert, n_valid, *, tm, tn):
    P, K = xs.shape
    F = w13.shape[2] // 2
    nj = F // tn
    return pl.pallas_call(
        _moe_up_kernel,
        grid_spec=pltpu.PrefetchScalarGridSpec(
            num_scalar_prefetch=2,
            grid=(nj, P // tm),
            in_specs=[pl.BlockSpec((tm, K), lambda j, i, te, nv: (jnp.minimum(i, nv[0] - 1), 0)),
                      pl.BlockSpec((None, K, tn), lambda j, i, te, nv: (te[i], 0, j)),
                      pl.BlockSpec((None, K, tn), lambda j, i, te, nv: (te[i], 0, j + nj))],
            out_specs=pl.BlockSpec((tm, tn), lambda j, i, te, nv: (i, j)),
            scratch_shapes=[pltpu.VMEM((K, tn), BF16), pltpu.VMEM((K, tn), BF16)],
        ),
        out_shape=jax.ShapeDtypeStruct((P, F), BF16),
        compiler_params=_params(("arbitrary", "arbitrary")),
        name="moe_up",
    )(tile_expert, n_valid, xs, w13, w13)


def _moe_down_kernel(te_ref, nv_ref, a_ref, w_ref, rw_ref, o_ref, wb_ref):
    i = pl.program_id(1)
    valid = i < nv_ref[0]

    @pl.when(_new_weights(te_ref, i))
    def _():
        wb_ref[...] = w_ref[...].astype(BF16)

    @pl.when(valid)
    def _():
        o_ref[...] = _dot(a_ref[...], wb_ref[...]) * rw_ref[:, 0:1]

    @pl.when(jnp.logical_not(valid))
    def _():
        o_ref[...] = jnp.zeros_like(o_ref)


def moe_down(act, w2, row_w, tile_expert, n_valid, *, tm, tn):
    P, F = act.shape
    N = w2.shape[2]
    return pl.pallas_call(
        _moe_down_kernel,
        grid_spec=pltpu.PrefetchScalarGridSpec(
            num_scalar_prefetch=2,
            grid=(N // tn, P // tm),
            in_specs=[pl.BlockSpec((tm, F), lambda j, i, te, nv: (jnp.minimum(i, nv[0] - 1), 0)),
                      pl.BlockSpec((None, F, tn), lambda j, i, te, nv: (te[i], 0, j)),
                      pl.BlockSpec((tm, 128), lambda j, i, te, nv: (i, 0))],
            out_specs=pl.BlockSpec((tm, tn), lambda j, i, te, nv: (i, j)),
            scratch_shapes=[pltpu.VMEM((F, tn), BF16)],
        ),
        out_shape=jax.ShapeDtypeStruct((P, N), F32),
        compiler_params=_params(("arbitrary", "arbitrary")),
        name="moe_down",
    )(tile_expert, n_valid, act, w2, row_w)


def moe_dispatch(top_i, top_w, tm):
    M = top_i.shape[0]
    A = M * TOP_K
    P = A + N_EXPERTS * tm
    e_flat = top_i.reshape(A)
    onehot = (e_flat[:, None] == jnp.arange(N_EXPERTS, dtype=I32)[None, :]).astype(I32)
    csum = jnp.cumsum(onehot, axis=0)
    counts = csum[-1]
    padded = ((counts + tm - 1) // tm) * tm
    pend = jnp.cumsum(padded)
    pstart = pend - padded
    dest = jnp.sum(onehot * (pstart[None, :] + csum - onehot), axis=1)
    src_tok = jnp.zeros((P,), I32).at[dest].set(jnp.arange(A, dtype=I32) // TOP_K)
    row_w = jnp.zeros((P,), F32).at[dest].set(top_w.reshape(A))
    pos = dest.reshape(M, TOP_K)
    tile_start = jnp.arange(P // tm, dtype=I32) * tm
    tile_expert = jnp.minimum(jnp.sum((tile_start[:, None] >= pend[None, :]).astype(I32), axis=1), N_EXPERTS - 1)
    n_valid = (pend[-1:] // tm).astype(I32)
    last_e = tile_expert[jnp.maximum(n_valid[0] - 1, 0)]
    tile_expert = jnp.where(tile_start // tm < n_valid[0], tile_expert, last_e)
    return src_tok, row_w, pos, tile_expert, n_valid


def _nsa_prep_kernel(kc_ref, vc_ref, ks_ref, vs_ref, kw_ref, vw_ref, kg_ref, raw_ref, kso_ref, vso_ref, kwo_ref, vwo_ref):
    raw_ref[0] = kc_ref[...]
    raw_ref[1] = vc_ref[...]
    kg = kg_ref[...]
    kso_ref[...] = _rms(ks_ref[...], kg[1:2]).astype(BF16)
    vso_ref[...] = vs_ref[...].astype(BF16)
    kwo_ref[...] = _rms(kw_ref[...], kg[2:3]).astype(BF16)
    vwo_ref[...] = vw_ref[...].astype(BF16)


def nsa_prep(z, k_gain, B, S, ts=512):
    G = NSA_KV_GROUPS
    nt = S // ts
    cb = COL_KV_A // HEAD_DIM

    def col(br, kvi):
        return pl.BlockSpec((ts, HEAD_DIM), lambda b, g, t: (b * nt + t, cb + br * 4 + kvi * 2 + g))

    kv_out = pl.BlockSpec((None, None, ts, HEAD_DIM), lambda b, g, t: (b, g, t, 0))
    kv_shape = jax.ShapeDtypeStruct((B, G, S, HEAD_DIM), BF16)
    return pl.pallas_call(
        _nsa_prep_kernel,
        grid=(B, G, nt),
        in_specs=[col(0, 0), col(0, 1), col(1, 0), col(1, 1), col(2, 0), col(2, 1),
                  pl.BlockSpec((3, HEAD_DIM), lambda b, g, t: (0, 0))],
        out_specs=[pl.BlockSpec((None, 2, None, ts, HEAD_DIM), lambda b, g, t: (b, 0, g, t, 0)),
                   kv_out, kv_out, kv_out, kv_out],
        out_shape=[jax.ShapeDtypeStruct((B, 2, G, S, HEAD_DIM), F32), kv_shape, kv_shape, kv_shape, kv_shape],
        compiler_params=_params(("parallel", "parallel", "parallel")),
        name="nsa_prep",
    )(z, z, z, z, z, z, k_gain)


def _nsa_compress_kernel(r_ref, pos_ref, w1_ref, w2_ref, kg_ref, o_ref):
    kv = pl.program_id(1)
    half = (CMP_BLOCK // 2) * HEAD_DIM
    r = r_ref[...]
    pos = pos_ref[...]
    n = r.shape[0]
    first = _dot((r + pos[:, :half]).astype(BF16), w1_ref[:half, :].astype(BF16))
    second = _dot((r + pos[:, half:]).astype(BF16), w1_ref[half:, :].astype(BF16))
    pre = first + pltpu.roll(second, n - 1, axis=0)
    hid = jax.nn.gelu(pre)
    comp = _dot(hid.astype(BF16), w2_ref[...].astype(BF16))
    o_ref[...] = jnp.where(kv == 0, _rms(comp, kg_ref[...]), comp).astype(o_ref.dtype)


def nsa_compress(raw, cmp_pos, cmp_w1, cmp_w2, k_gain0, B, S):
    G = NSA_KV_GROUPS
    n = S // CMP_STRIDE
    feat = CMP_STRIDE * HEAD_DIM
    r = raw.reshape(B, 2, G, n, feat)
    pos = cmp_pos.reshape(2, 1, CMP_BLOCK * HEAD_DIM)
    return pl.pallas_call(
        _nsa_compress_kernel,
        grid=(B, 2, G),
        in_specs=[pl.BlockSpec((None, None, None, n, feat), lambda b, kv, g: (b, kv, g, 0, 0)),
                  pl.BlockSpec((None, 1, CMP_BLOCK * HEAD_DIM), lambda b, kv, g: (kv, 0, 0)),
                  pl.BlockSpec((None, CMP_BLOCK * HEAD_DIM, HEAD_DIM), lambda b, kv, g: (kv, 0, 0)),
                  pl.BlockSpec((None, HEAD_DIM, HEAD_DIM), lambda b, kv, g: (kv, 0, 0)),
                  pl.BlockSpec((1, HEAD_DIM), lambda b, kv, g: (0, 0))],
        out_specs=pl.BlockSpec((None, None, None, n, HEAD_DIM), lambda b, kv, g: (b, kv, g, 0, 0)),
        out_shape=jax.ShapeDtypeStruct((B, 2, G, n, HEAD_DIM), BF16),
        compiler_params=_params(("parallel", "parallel", "parallel")),
        name="nsa_compress",
    )(r, pos, cmp_w1, cmp_w2, k_gain0.reshape(1, HEAD_DIM))


def _rep(x, n):
    return x if n == HEAD_DIM else jnp.concatenate([x] * (n // HEAD_DIM), axis=1)


def _flash_step(qb, kt, vt, bias, m_ref, l_ref, acc_ref):
    nk = kt.shape[0]
    rows = qb.shape[0]
    s = _dot_nt(qb, kt)
    s = (s.reshape(rows // Q_BLOCK, Q_BLOCK, nk) + bias[None]).reshape(rows, nk)
    m_prev = m_ref[...]
    m_new = jnp.maximum(m_prev, jnp.max(s, axis=-1, keepdims=True))
    alpha = jnp.exp2(m_prev - m_new)
    p = jnp.exp2(s - _rep(m_new, nk))
    l_ref[...] = alpha * l_ref[...] + jnp.sum(p, axis=-1, keepdims=True)
    acc_ref[...] = alpha * acc_ref[...] + _dot(p.astype(BF16), vt)
    m_ref[...] = m_new


def _nsa_kernel(q_ref, sm_ref, kc_ref, vc_ref, ks_ref, vs_ref, kw_ref, vw_ref, qg_ref, og_ref, c2s_ref, exp_ref,
                o_ref, selb_ref, m_ref, l_ref, acc_ref, *, k_top):
    Hg = NSA_GROUP_SIZE
    rows = Hg * Q_BLOCK
    c = pl.program_id(2)
    t0 = c * Q_BLOCK
    n_cmp = kc_ref.shape[0]
    n_sel = c2s_ref.shape[0]

    q = q_ref[...]
    qs = jnp.concatenate([q[:, h * HEAD_DIM:(h + 1) * HEAD_DIM] for h in range(Hg)], axis=0)
    qb = (_rms(qs, qg_ref[...]) * (SCALE * LOG2E)).astype(BF16)

    tq = t0 + lax.broadcasted_iota(I32, (Q_BLOCK, n_cmp), 0)
    c_end = lax.broadcasted_iota(I32, (Q_BLOCK, n_cmp), 1) * CMP_STRIDE + (CMP_BLOCK - 1)
    sc = _dot_nt(qb, kc_ref[...]).reshape(Hg, Q_BLOCK, n_cmp)
    sc = jnp.where((c_end <= tq)[None], sc, -jnp.inf)
    mc = jnp.max(sc, axis=-1, keepdims=True)
    mc = jnp.where(mc > -jnp.inf, mc, 0.0)
    pc = jnp.exp2(sc - mc)
    dc = jnp.sum(pc, axis=-1, keepdims=True)
    pc = pc / jnp.where(dc > 0, dc, 1.0)
    o_cmp = _dot(pc.reshape(rows, n_cmp).astype(BF16), vc_ref[...])

    psum = jnp.sum(pc, axis=0)
    c2s = c2s_ref[...]
    p_hi = psum.astype(BF16)
    rem = psum - p_hi.astype(F32)
    p_mid = rem.astype(BF16)
    p_lo = (rem - p_mid.astype(F32)).astype(BF16)
    imp = _dot_nt(c2s, p_hi) + _dot_nt(c2s, p_mid) + _dot_nt(c2s, p_lo)

    jj = lax.broadcasted_iota(I32, (n_sel, Q_BLOCK), 0)
    cur = lax.shift_right_logical(t0 + lax.broadcasted_iota(I32, (n_sel, Q_BLOCK), 1), 6)
    valid = jj <= cur
    forced = valid & ((jj == 0) | (jj > cur - SEL_LOCAL))
    score = jnp.where(forced, FORCE, jnp.where(valid, imp, -FORCE))
    rank = jnp.zeros((n_sel, Q_BLOCK), F32)
    for j2 in range(n_sel):
        other = score[j2:j2 + 1, :]
        tie = jnp.where(jj > j2, 1.0, 0.0)
        rank = rank + jnp.where(other > score, 1.0, jnp.where(other == score, tie, 0.0))
    sel_t = jnp.where(rank < k_top, 1.0, 0.0)
    if n_sel < Q_BLOCK:
        sel_t = jnp.concatenate([sel_t, jnp.zeros((Q_BLOCK - n_sel, Q_BLOCK), F32)], axis=0)
    sel = sel_t.T[:, :n_sel].astype(BF16)
    sel_keys = _dot(sel, exp_ref[...])
    for i in range(selb_ref.shape[0]):
        selb_ref[i] = (sel_keys[:, i * SEL_TILE:(i + 1) * SEL_TILE] - 1.0) * (-NEG)

    def reset():
        m_ref[...] = jnp.full(m_ref.shape, NEG, F32)
        l_ref[...] = jnp.zeros(l_ref.shape, F32)
        acc_ref[...] = jnp.zeros(acc_ref.shape, F32)

    reset()
    tq_s = t0 + lax.broadcasted_iota(I32, (Q_BLOCK, SEL_TILE), 0)
    kk_s = lax.broadcasted_iota(I32, (Q_BLOCK, SEL_TILE), 1)

    def sel_body(i, carry):
        k0 = pl.multiple_of(i * SEL_TILE, SEL_TILE)
        bias = selb_ref[i] + jnp.where(kk_s + k0 <= tq_s, 0.0, NEG)
        _flash_step(qb, ks_ref[pl.ds(k0, SEL_TILE), :], vs_ref[pl.ds(k0, SEL_TILE), :], bias, m_ref, l_ref, acc_ref)
        return carry

    lax.fori_loop(0, (t0 + Q_BLOCK + SEL_TILE - 1) // SEL_TILE, sel_body, 0)
    o_sel = acc_ref[...] / l_ref[...]

    wk = WINDOW + Q_BLOCK
    w0 = pl.multiple_of(jnp.maximum(t0 - WINDOW, 0), Q_BLOCK)
    tq_w = t0 + lax.broadcasted_iota(I32, (Q_BLOCK, wk), 0)
    wpos = w0 + lax.broadcasted_iota(I32, (Q_BLOCK, wk), 1)
    w_bias = jnp.where((wpos <= tq_w) & (wpos > tq_w - WINDOW), 0.0, NEG)
    sw = _dot_nt(qb, kw_ref[pl.ds(w0, wk), :]).reshape(Hg, Q_BLOCK, wk) + w_bias[None]
    pw = jnp.exp2(sw - jnp.max(sw, axis=-1, keepdims=True))
    lw = jnp.sum(pw, axis=-1, keepdims=True).reshape(rows, 1)
    o_win = _dot(pw.reshape(rows, wk).astype(BF16), vw_ref[pl.ds(w0, wk), :]) / lw

    gates = jax.nn.sigmoid(sm_ref[...])
    og = og_ref[...]
    for h in range(Hg):
        r = slice(h * Q_BLOCK, (h + 1) * Q_BLOCK)
        o = (gates[:, h:h + 1] * o_cmp[r] + gates[:, Hg + h:Hg + h + 1] * o_sel[r]
             + gates[:, 2 * Hg + h:2 * Hg + h + 1] * o_win[r])
        o_ref[:, h * HEAD_DIM:(h + 1) * HEAD_DIM] = _rms(o, og).astype(o_ref.dtype)


def nsa_attention(z, comp, ks, vs, kw, vw, q_gain, out_gain, B, S):
    G, Hg = NSA_KV_GROUPS, NSA_GROUP_SIZE
    nq = S // Q_BLOCK
    n_cmp = S // CMP_STRIDE
    n_sel = S // SEL_BLOCK
    k_top = min(SEL_TOP_N, n_sel)
    c_start = np.arange(n_cmp)[:, None] * CMP_STRIDE
    s_start = np.arange(n_sel)[None, :] * SEL_BLOCK
    overlap = np.minimum(c_start + CMP_BLOCK, s_start + SEL_BLOCK) - np.maximum(c_start, s_start)
    assert n_sel <= Q_BLOCK
    c2s = jnp.asarray((np.clip(overlap, 0, None) / CMP_STRIDE).T, dtype=BF16)
    expand = jnp.asarray(np.arange(S)[None, :] // SEL_BLOCK == np.arange(n_sel)[:, None], dtype=BF16)
    rows = Hg * Q_BLOCK
    wq = Hg * HEAD_DIM
    kv_spec = pl.BlockSpec((None, None, S, HEAD_DIM), lambda b, g, c: (b, g, 0, 0))
    const = lambda b, g, c: (0, 0)
    return pl.pallas_call(
        functools.partial(_nsa_kernel, k_top=k_top),
        grid=(B, G, nq),
        in_specs=[pl.BlockSpec((Q_BLOCK, wq), lambda b, g, c: (b * nq + c, COL_Q_A // wq + g)),
                  pl.BlockSpec((Q_BLOCK, HEAD_DIM), lambda b, g, c: (b * nq + c, COL_SMALL // HEAD_DIM + g)),
                  pl.BlockSpec((None, None, None, n_cmp, HEAD_DIM), lambda b, g, c: (b, 0, g, 0, 0)),
                  pl.BlockSpec((None, None, None, n_cmp, HEAD_DIM), lambda b, g, c: (b, 1, g, 0, 0)),
                  kv_spec, kv_spec, kv_spec, kv_spec,
                  pl.BlockSpec((1, HEAD_DIM), const), pl.BlockSpec((1, HEAD_DIM), const),
                  pl.BlockSpec((n_sel, n_cmp), const), pl.BlockSpec((n_sel, S), const)],
        out_specs=pl.BlockSpec((Q_BLOCK, wq), lambda b, g, c: (b * nq + c, g)),
        out_shape=jax.ShapeDtypeStruct((B * S, D_NSA), BF16),
        scratch_shapes=[pltpu.VMEM((S // SEL_TILE, Q_BLOCK, SEL_TILE), F32),
                        pltpu.VMEM((rows, HEAD_DIM), F32), pltpu.VMEM((rows, HEAD_DIM), F32),
                        pltpu.VMEM((rows, HEAD_DIM), F32)],
        compiler_params=_params(("parallel", "parallel", "arbitrary")),
        name="nsa_attention",
    )(z, z, comp, comp, ks, vs, kw, vw, q_gain.reshape(1, HEAD_DIM), out_gain.reshape(1, HEAD_DIM), c2s, expand)


def _gdn_chunk_kernel(xc_ref, xh_ref, sm_ref, cw_ref, alog_ref, dtb_ref, rep_ref, unfold_ref, u_ref, wq_ref, ak_ref,
                      egl_ref):
    C = GDN_CHUNK
    PK = GDN_PACK
    R = PK * C
    n = pl.program_id(1)
    xp = jnp.concatenate([jnp.where(n > 0, xh_ref[...], 0.0), xc_ref[...]], axis=0)
    cw = cw_ref[...]
    y = None
    for j in range(GDN_CONV):
        shift = GDN_CONV - 1 - j
        xs = xp if shift == 0 else pltpu.roll(xp, shift, axis=0)
        term = xs[8:] * cw[j:j + 1, :]
        y = term if y is None else y + term
    y = y * jax.nn.sigmoid(y)

    sm = sm_ref[...]
    beta = jax.nn.sigmoid(sm)
    g = -jnp.exp(alog_ref[...]) * jax.nn.softplus(sm + dtb_ref[...])
    row = lax.broadcasted_iota(I32, g.shape, 0)
    gc = g
    d = 1
    while d < C:
        gc = gc + jnp.where(row >= d, pltpu.roll(gc, d, axis=0), 0.0)
        d *= 2
    g_last = gc[C - 1:C, :]

    ri = lax.broadcasted_iota(I32, (R, R), 0)
    ci = lax.broadcasted_iota(I32, (R, R), 1)
    same = lax.shift_right_logical(ri, 6) == lax.shift_right_logical(ci, 6)
    tri = same & (ri >= ci)
    strict = same & (ri > ci)
    SUB = GDN_SUB
    nb = R // SUB
    same16 = lax.shift_right_logical(ri, 4) == lax.shift_right_logical(ci, 4)
    same32 = lax.shift_right_logical(ri, 5) == lax.shift_right_logical(ci, 5)
    off32 = same32 & jnp.logical_not(same16)
    off64 = jnp.logical_not(same32)
    NG = GDN_HEADS // PK
    W4 = NG * SUB
    row_c = lax.broadcasted_iota(I32, (R, W4), 0) & (SUB - 1)
    col_s = lax.broadcasted_iota(I32, (nb, W4), 1) & (SUB - 1)

    groups = []
    for grp in range(NG):
        heads = [grp * PK + i for i in range(PK)]

        def stack(off):
            return jnp.concatenate([y[:, off + h * HEAD_DIM: off + (h + 1) * HEAD_DIM] for h in heads], axis=0)

        def col(x, base):
            return jnp.concatenate([x[:, base + h: base + h + 1] for h in heads], axis=0)

        q4 = stack(0)
        k4 = stack(D_GDN)
        v4 = stack(2 * D_GDN)
        q4 = q4 * lax.rsqrt(jnp.sum(q4 * q4, axis=-1, keepdims=True) + EPS) * SCALE
        k4 = k4 * lax.rsqrt(jnp.sum(k4 * k4, axis=-1, keepdims=True) + EPS)
        beta4 = col(beta, SM_BETA)
        gc4 = col(gc, SM_A)
        gl4 = jnp.concatenate([jnp.broadcast_to(g_last[:, SM_A + h: SM_A + h + 1], (C, 1)) for h in heads], axis=0)

        gb = jnp.broadcast_to(gc4, (R, R))
        decay = jnp.exp(jnp.where(tri, gb - gb.T, -jnp.inf))
        kb4 = k4 * beta4
        k4b = k4.astype(BF16)
        a = jnp.where(strict, _dot_nt(kb4.astype(BF16), k4b) * decay, 0.0)
        attn = _dot_nt(q4.astype(BF16), k4b) * decay
        at = a.T
        dct = jnp.concatenate([at[SUB * b:SUB * (b + 1), SUB * b:SUB * (b + 1)] for b in range(nb)], axis=0)
        groups.append((heads, q4, k4, v4, beta4, gc4, gl4, kb4, a, attn, dct))

    coef = _dot3(jnp.concatenate([g[-1] for g in groups], axis=1), rep_ref[...])
    t4 = jnp.where(row_c == (lax.broadcasted_iota(I32, (R, W4), 1) & (SUB - 1)), 1.0, 0.0)
    for i in range(1, SUB):
        s = jnp.sum((coef[:, i * HEAD_DIM: i * HEAD_DIM + W4] * t4).reshape(nb, SUB, W4), axis=1)
        new = jnp.where(col_s == i, 1.0, 0.0) - s
        t4 = jnp.where(row_c == i, jnp.broadcast_to(new[:, None, :], (nb, SUB, W4)).reshape(R, W4), t4)
    t_tiled = _dot3(t4, unfold_ref[...])

    for gi, (heads, q4, k4, v4, beta4, gc4, gl4, kb4, a, attn, _) in enumerate(groups):
        t16 = jnp.where(same16, t_tiled[:, gi * R:(gi + 1) * R], 0.0)
        t16b = t16.astype(BF16)
        a32 = jnp.where(off32, a, 0.0).astype(BF16)
        t32 = t16 - _dot(_dot(t16b, a32).astype(BF16), t16b)
        t32b = t32.astype(BF16)
        a64 = jnp.where(off64, a, 0.0).astype(BF16)
        t64 = t32 - _dot(_dot(t32b, a64).astype(BF16), t32b)
        eg = jnp.exp(gc4)
        rhs = jnp.concatenate([v4 * beta4, kb4 * eg], axis=1)
        rhs = _dot(t64.astype(BF16), rhs.astype(BF16))
        qd4 = q4 * eg
        kdt = (k4 * jnp.exp(gl4 - gc4)).T
        for i, h in enumerate(heads):
            r = slice(i * C, (i + 1) * C)
            u_ref[h] = rhs[r, :HEAD_DIM]
            wq_ref[h] = jnp.concatenate([rhs[r, HEAD_DIM:], qd4[r]], axis=0).astype(BF16)
            ak_ref[h] = jnp.concatenate([attn[r, r], kdt[:, r]], axis=0).astype(BF16)
            egl_ref[h] = jnp.broadcast_to(jnp.exp(g_last[:, SM_A + h: SM_A + h + 1]), (8, HEAD_DIM))


def gdn_chunks(z, conv_w, a_log, dt_bias, B, S):
    C, H = GDN_CHUNK, GDN_HEADS
    N = S // C
    W = 3 * D_GDN
    pad = lambda v: jnp.zeros((1, HEAD_DIM), F32).at[0, SM_A:SM_A + H].set(v)
    out5 = lambda r, cdim: pl.BlockSpec((None, None, H, r, cdim), lambda b, n: (b, n, 0, 0, 0))
    NG, SUB, R = H // GDN_PACK, GDN_SUB, GDN_PACK * C
    W4 = NG * SUB
    src = np.arange(W4)[:, None]
    dst = np.arange(SUB * HEAD_DIM)[None, :]
    lane = dst % HEAD_DIM
    rep = (lane < W4) & (src // SUB == lane // SUB) & (src % SUB == dst // HEAD_DIM)
    dst = np.arange(NG * R)[None, :]
    unfold = (src // SUB == dst // R) & (src % SUB == dst % SUB)
    return pl.pallas_call(
        _gdn_chunk_kernel,
        grid=(B, N),
        in_specs=[pl.BlockSpec((C, W), lambda b, n: (b * N + n, 0)),
                  pl.BlockSpec((8, W), lambda b, n: (jnp.maximum((b * N + n) * (C // 8) - 1, 0), 0)),
                  pl.BlockSpec((C, HEAD_DIM), lambda b, n: (b * N + n, COL_SMALL // HEAD_DIM)),
                  pl.BlockSpec((GDN_CONV, W), lambda b, n: (0, 0)),
                  pl.BlockSpec((1, HEAD_DIM), lambda b, n: (0, 0)),
                  pl.BlockSpec((1, HEAD_DIM), lambda b, n: (0, 0)),
                  pl.BlockSpec((W4, SUB * HEAD_DIM), lambda b, n: (0, 0)),
                  pl.BlockSpec((W4, NG * R), lambda b, n: (0, 0))],
        out_specs=[out5(C, HEAD_DIM), out5(2 * C, HEAD_DIM), out5(C + HEAD_DIM, C), out5(8, HEAD_DIM)],
        out_shape=[jax.ShapeDtypeStruct((B, N, H, C, HEAD_DIM), F32),
                   jax.ShapeDtypeStruct((B, N, H, 2 * C, HEAD_DIM), BF16),
                   jax.ShapeDtypeStruct((B, N, H, C + HEAD_DIM, C), BF16),
                   jax.ShapeDtypeStruct((B, N, H, 8, HEAD_DIM), F32)],
        compiler_params=_params(("parallel", "parallel")),
        name="gdn_chunks",
    )(z, z, z, conv_w, pad(a_log), pad(dt_bias), jnp.asarray(rep, dtype=BF16), jnp.asarray(unfold, dtype=BF16))


def _gdn_scan_kernel(u_ref, wq_ref, ak_ref, egl_ref, z_ref, og_ref, o_ref, s_ref):
    C = GDN_CHUNK

    @pl.when(pl.program_id(1) == 0)
    def _():
        s_ref[...] = jnp.zeros(s_ref.shape, F32)

    og = og_ref[...]
    for h in range(GDN_HEADS):
        state = s_ref[h]
        ws = _dot(wq_ref[h], state.astype(BF16))
        v_new = u_ref[h] - ws[:C]
        av = _dot(ak_ref[h], v_new.astype(BF16))
        o = ws[C:] + av[:C]
        decayed = (state.reshape(HEAD_DIM // 8, 8, HEAD_DIM) * egl_ref[h][None]).reshape(HEAD_DIM, HEAD_DIM)
        s_ref[h] = decayed + av[C:]
        zh = z_ref[:, h * HEAD_DIM:(h + 1) * HEAD_DIM]
        o_ref[:, h * HEAD_DIM:(h + 1) * HEAD_DIM] = (_rms(o, og) * (zh * jax.nn.sigmoid(zh))).astype(o_ref.dtype)


def gdn_scan(u, wq, ak, egl, z, out_gain, B, S):
    C, H = GDN_CHUNK, GDN_HEADS
    N = S // C
    in5 = lambda r, cdim: pl.BlockSpec((None, None, H, r, cdim), lambda b, n: (b, n, 0, 0, 0))
    return pl.pallas_call(
        _gdn_scan_kernel,
        grid=(B, N),
        in_specs=[in5(C, HEAD_DIM), in5(2 * C, HEAD_DIM), in5(C + HEAD_DIM, C), in5(8, HEAD_DIM),
                  pl.BlockSpec((C, D_GDN), lambda b, n: (b * N + n, COL_Z_B // D_GDN)),
                  pl.BlockSpec((1, HEAD_DIM), lambda b, n: (0, 0))],
        out_specs=pl.BlockSpec((C, D_GDN), lambda b, n: (b * N + n, 0)),
        out_shape=jax.ShapeDtypeStruct((B * S, D_GDN), BF16),
        scratch_shapes=[pltpu.VMEM((H, HEAD_DIM, HEAD_DIM), F32)],
        compiler_params=_params(("parallel", "arbitrary")),
        name="gdn_scan",
    )(u, wq, ak, egl, z, out_gain.reshape(1, HEAD_DIM))


def _xa_kv_kernel(mem_ref, g_ref, w_ref, kg_ref, k_ref, v_ref):
    kv = _dot(_rms(mem_ref[...], g_ref[...]).astype(BF16), w_ref[...])
    for h in range(XA_HEADS):
        k_ref[h] = _rms(kv[:, h * HEAD_DIM:(h + 1) * HEAD_DIM], kg_ref[...]).astype(BF16)
        v_ref[h] = kv[:, D_XA + h * HEAD_DIM: D_XA + (h + 1) * HEAD_DIM].astype(BF16)


def xa_kv(mem, mem_norm, wkv_bf16, k_gain):
    B, Mm, D = mem.shape
    spec = pl.BlockSpec((None, XA_HEADS, Mm, HEAD_DIM), lambda b: (b, 0, 0, 0))
    shape = jax.ShapeDtypeStruct((B, XA_HEADS, Mm, HEAD_DIM), BF16)
    return pl.pallas_call(
        _xa_kv_kernel,
        grid=(B,),
        in_specs=[pl.BlockSpec((None, Mm, D), lambda b: (b, 0, 0)), pl.BlockSpec((1, D), lambda b: (0, 0)),
                  pl.BlockSpec((D, 2 * D_XA), lambda b: (0, 0)), pl.BlockSpec((1, HEAD_DIM), lambda b: (0, 0))],
        out_specs=[spec, spec],
        out_shape=[shape, shape],
        compiler_params=_params(("parallel",)),
        name="xa_kv",
    )(mem, mem_norm.reshape(1, D), wkv_bf16, k_gain.reshape(1, HEAD_DIM))


def _xa_kernel(x_ref, g_ref, wq_ref, k_ref, v_ref, qg_ref, wo_ref, fg_ref, *rest, route):
    if route:
        rwh_ref, rwl_ref, o_ref, h_ref, idx_ref, wt_ref = rest
    else:
        o_ref, h_ref = rest
    x = x_ref[...]
    q = _dot(_rms(x, g_ref[...]).astype(BF16), wq_ref[...])
    outs = []
    for h in range(XA_HEADS):
        qh = _rms(q[:, h * HEAD_DIM:(h + 1) * HEAD_DIM], qg_ref[...]).astype(BF16)
        s = _dot_nt(qh, k_ref[h]) * SCALE
        p = jnp.exp(s - jnp.max(s, axis=-1, keepdims=True))
        p = p / jnp.sum(p, axis=-1, keepdims=True)
        outs.append(_dot(p.astype(BF16), v_ref[h]))
    o = jnp.concatenate(outs, axis=1).astype(BF16)
    x_new = x + _dot(o, wo_ref[...])
    o_ref[...] = x_new
    hn = _rms(x_new, fg_ref[...])
    h_ref[...] = hn.astype(h_ref.dtype)
    if route:
        idx_ref[...], wt_ref[...] = _route(hn, rwh_ref[...], rwl_ref[...])


def cross_attention(x, xa_norm, wq_bf16, k, v, q_gain, wo_bf16, ffn_norm, router_w, B, S, tm=256):
    M, D = x.shape
    Mm = k.shape[2]
    per_b = S // tm
    kv_spec = pl.BlockSpec((None, XA_HEADS, Mm, HEAD_DIM), lambda i: (i // per_b, 0, 0, 0))
    const = lambda i: (0, 0)
    row = lambda i: (i, 0)
    route = router_w is not None
    in_specs = [pl.BlockSpec((tm, D), row), pl.BlockSpec((1, D), const),
                pl.BlockSpec((D, D_XA), const), kv_spec, kv_spec,
                pl.BlockSpec((1, HEAD_DIM), const), pl.BlockSpec((D_XA, D), const), pl.BlockSpec((1, D), const)]
    args = [x, xa_norm.reshape(1, D), wq_bf16, k, v, q_gain.reshape(1, HEAD_DIM), wo_bf16, ffn_norm.reshape(1, D)]
    out_specs = [pl.BlockSpec((tm, D), row), pl.BlockSpec((tm, D), row)]
    out_shape = [jax.ShapeDtypeStruct((M, D), F32), jax.ShapeDtypeStruct((M, D), BF16)]
    if route:
        rw = jnp.pad(router_w, ((0, 0), (0, 128 - N_EXPERTS)))
        rw_hi = rw.astype(BF16)
        in_specs += [pl.BlockSpec((D, 128), const), pl.BlockSpec((D, 128), const)]
        args += [rw_hi, (rw - rw_hi.astype(F32)).astype(BF16)]
        out_specs += [pl.BlockSpec((tm, 128), row), pl.BlockSpec((tm, 128), row)]
        out_shape += [jax.ShapeDtypeStruct((M, 128), I32), jax.ShapeDtypeStruct((M, 128), F32)]
    return pl.pallas_call(
        functools.partial(_xa_kernel, route=route),
        grid=(M // tm,),
        in_specs=in_specs,
        out_specs=out_specs,
        out_shape=out_shape,
        compiler_params=_params(("parallel",)),
        name="cross_attention",
    )(*args)


def _reorder_w_in(w_in):
    o = np.cumsum((D_NSA, 6 * D_KV_NSA, 3 * NSA_HEADS, 3 * D_GDN, GDN_HEADS, GDN_HEADS, D_GDN))
    q_a, kv_a, gate_a, qkv_b = (0, o[0]), (o[0], o[1]), o[1], (o[2], o[3])
    beta_b, a_b, z_b = o[3], o[4], (o[5], o[6])
    Hg = NSA_GROUP_SIZE
    src = np.zeros((SMALL_W,), np.int64)
    used = np.zeros((SMALL_W,), bool)
    for g in range(NSA_KV_GROUPS):
        for br in range(3):
            for h in range(Hg):
                src[g * HEAD_DIM + br * Hg + h] = gate_a + (g * Hg + h) * 3 + br
                used[g * HEAD_DIM + br * Hg + h] = True
    for h in range(GDN_HEADS):
        src[SM_BETA + h] = beta_b + h
        src[SM_A + h] = a_b + h
        used[SM_BETA + h] = used[SM_A + h] = True
    small = jnp.where(jnp.asarray(used)[None, :], jnp.take(w_in, jnp.asarray(src), axis=1), 0.0)
    parts = [w_in[:, qkv_b[0]:qkv_b[1]], w_in[:, q_a[0]:q_a[1]], w_in[:, z_b[0]:z_b[1]], w_in[:, kv_a[0]:kv_a[1]], small]
    return jnp.concatenate([p.astype(BF16) for p in parts], axis=1)


def mixer_layer(x2, B, S, attn_norm, w_in, nsa_q_gain, nsa_k_gain, nsa_cmp_pos, nsa_cmp_w1, nsa_cmp_w2, nsa_out_gain,
                gdn_conv_w, gdn_A_log, gdn_dt_bias, gdn_out_gain, w_out, tm=1024):
    h = rmsnorm_bf16(x2, attn_norm)
    z = matmul_cols(h, _reorder_w_in(w_in), tm=tm, tn=512, name="in_proj")
    raw, ks, vs, kw, vw = nsa_prep(z, nsa_k_gain, B, S)
    comp = nsa_compress(raw, nsa_cmp_pos, nsa_cmp_w1, nsa_cmp_w2, nsa_k_gain[0], B, S)
    y_a = nsa_attention(z, comp, ks, vs, kw, vw, nsa_q_gain, nsa_out_gain, B, S)
    u, wq, ak, egl = gdn_chunks(z, gdn_conv_w, gdn_A_log, gdn_dt_bias, B, S)
    y_b = gdn_scan(u, wq, ak, egl, z, gdn_out_gain, B, S)
    return matmul2_residual(y_a, y_b, w_out, x2, tm=tm, tn=512, name="out_proj")


def xa_layer(x2, mem, B, S, xa_norm, mem_norm, xa_wq, xa_wkv, xa_q_gain, xa_k_gain, xa_wo, ffn_norm, router_w=None):
    k, v = xa_kv(mem, mem_norm, xa_wkv.astype(BF16), xa_k_gain)
    return cross_attention(x2, xa_norm, xa_wq.astype(BF16), k, v, xa_q_gain, xa_wo.astype(BF16), ffn_norm, router_w,
                           B, S)


def dense_ffn_layer(x2, h, w13, w2, tm=1024):
    act = swiglu_up(h, w13, tm=tm, tn=256)
    return down_proj_residual(act, w2, x2, tm=min(2 * tm, x2.shape[0]), tn=1024, tk=512, tk_tail=256)


def moe_ffn_layer(x2, h, idx, wts, w13, w2, tm=512, tn=512):
    src_tok, row_w, pos, tile_expert, n_valid = moe_dispatch(idx[:, :TOP_K], wts[:, :TOP_K], tm)
    xs = jnp.take(h, src_tok, axis=0)
    act = moe_up(xs, w13, tile_expert, n_valid, tm=tm, tn=tn)
    row_w_rep = jnp.broadcast_to(row_w[:, None], (row_w.shape[0], 128))
    out = moe_down(act, w2, row_w_rep, tile_expert, n_valid, tm=tm, tn=tn)
    return x2 + (jnp.take(out, pos[:, 0], axis=0) + jnp.take(out, pos[:, 1], axis=0))


def kernel(x, mem, attn_norm, w_in, nsa_q_gain, nsa_k_gain, nsa_cmp_pos, nsa_cmp_w1, nsa_cmp_w2, nsa_out_gain, gdn_conv_w, gdn_A_log, gdn_dt_bias, gdn_out_gain, w_out, xa_norm, mem_norm, xa_wq, xa_wkv, xa_q_gain, xa_k_gain, xa_wo, ffn_norm, dense_w13, dense_w2, router_w, moe_w13, moe_w2):
    B, S, D = x.shape
    x2 = x.reshape(B * S, D)
    for l in range(attn_norm.shape[0]):
        x2 = mixer_layer(x2, B, S, attn_norm[l], w_in[l], nsa_q_gain[l], nsa_k_gain[l], nsa_cmp_pos[l], nsa_cmp_w1[l],
                         nsa_cmp_w2[l], nsa_out_gain[l], gdn_conv_w[l], gdn_A_log[l], gdn_dt_bias[l], gdn_out_gain[l],
                         w_out[l])
        xa_args = (x2, mem, B, S, xa_norm[l], mem_norm[l], xa_wq[l], xa_wkv[l], xa_q_gain[l], xa_k_gain[l], xa_wo[l],
                   ffn_norm[l])
        if l % 2 == 0:
            x2, h = xa_layer(*xa_args)
            x2 = dense_ffn_layer(x2, h, dense_w13[l // 2], dense_w2[l // 2])
        else:
            x2, h, idx, wts = xa_layer(*xa_args, router_w[l // 2])
            x2 = moe_ffn_layer(x2, h, idx, wts, moe_w13[l // 2], moe_w2[l // 2])
    return x2.reshape(B, S, D)
```

```python
import functools

import jax
import jax.numpy as jnp
import numpy as np
from jax import lax
from jax.experimental import pallas as pl
from jax.experimental.pallas import tpu as pltpu

F32 = jnp.float32
BF16 = jnp.bfloat16
I32 = jnp.int32

D_MODEL = 4096
HEAD_DIM = 128
EPS = 1e-6
SCALE = HEAD_DIM ** -0.5

NSA_HEADS = 16
NSA_KV_GROUPS = 2
NSA_GROUP_SIZE = NSA_HEADS // NSA_KV_GROUPS
CMP_BLOCK = 32
CMP_STRIDE = 16
SEL_BLOCK = 64
SEL_TOP_N = 16
SEL_LOCAL = 2
WINDOW = 512
Q_BLOCK = 128
FORCE = 1e9
NEG = -1e30
SEL_TILE = 512
LOG2E = 1.4426950408889634

GDN_HEADS = 16
GDN_CONV = 4
GDN_CHUNK = 64
GDN_PACK = 4
GDN_SUB = 16

D_NSA = NSA_HEADS * HEAD_DIM
D_KV_NSA = NSA_KV_GROUPS * HEAD_DIM
D_GDN = GDN_HEADS * HEAD_DIM
D_MIX = D_NSA + D_GDN

XA_HEADS = 4
D_XA = XA_HEADS * HEAD_DIM
N_EXPERTS = 8
TOP_K = 2

COL_QKV_B = 0
COL_Q_A = COL_QKV_B + 3 * D_GDN
COL_Z_B = COL_Q_A + D_NSA
COL_KV_A = COL_Z_B + D_GDN
COL_SMALL = COL_KV_A + 6 * D_KV_NSA
SMALL_W = 512
N_IN_R = COL_SMALL + SMALL_W
SM_BETA = 32
SM_A = 48

VMEM_LIMIT = 58 * 1024 * 1024

NT_DIMS = (((1,), (1,)), ((), ()))


def _rms(x, gain):
    return x * lax.rsqrt(jnp.mean(x * x, axis=-1, keepdims=True) + EPS) * gain


def _dot(a, b):
    return jnp.dot(a, b, preferred_element_type=F32)


def _dot_nt(a, b):
    return lax.dot_general(a, b, NT_DIMS, preferred_element_type=F32)


def _dot3(x, sel):
    hi = x.astype(BF16)
    rem = x - hi.astype(F32)
    mid = rem.astype(BF16)
    lo = (rem - mid.astype(F32)).astype(BF16)
    return _dot(hi, sel) + _dot(mid, sel) + _dot(lo, sel)


def _params(sem, vmem=VMEM_LIMIT):
    return pltpu.CompilerParams(dimension_semantics=sem, vmem_limit_bytes=vmem)


def _norm_kernel(x_ref, g_ref, o_ref):
    o_ref[...] = _rms(x_ref[...], g_ref[...]).astype(o_ref.dtype)


def rmsnorm_bf16(x, gain, tm=512):
    M, D = x.shape
    return pl.pallas_call(
        _norm_kernel,
        grid=(M // tm,),
        in_specs=[pl.BlockSpec((tm, D), lambda i: (i, 0)), pl.BlockSpec((1, D), lambda i: (0, 0))],
        out_specs=pl.BlockSpec((tm, D), lambda i: (i, 0)),
        out_shape=jax.ShapeDtypeStruct((M, D), BF16),
        compiler_params=_params(("parallel",)),
        name="rmsnorm",
    )(x, gain.reshape(1, D))


def _route(h, rw_hi, rw_lo):
    h_hi = h.astype(BF16)
    h_lo = (h - h_hi.astype(F32)).astype(BF16)
    logits = _dot(h_hi, rw_hi) + _dot(h_hi, rw_lo) + _dot(h_lo, rw_hi)
    lane = lax.broadcasted_iota(I32, logits.shape, 1).astype(F32)
    logits = jnp.where(lane < N_EXPERTS, logits, -jnp.inf)
    m1 = jnp.max(logits, axis=-1, keepdims=True)
    i1 = jnp.min(jnp.where(logits == m1, lane, 128.0), axis=-1, keepdims=True)
    rest = jnp.where(lane == i1, -jnp.inf, logits)
    m2 = jnp.max(rest, axis=-1, keepdims=True)
    i2 = jnp.min(jnp.where(rest == m2, lane, 128.0), axis=-1, keepdims=True)
    e2 = jnp.exp(m2 - m1)
    den = 1.0 + e2
    idx = jnp.where(lane == 0, i1, jnp.where(lane == 1, i2, 0.0)).astype(I32)
    wts = jnp.where(lane == 0, 1.0 / den, jnp.where(lane == 1, e2 / den, 0.0))
    return idx, wts


def _mm_kernel(x_ref, w_ref, o_ref):
    o_ref[...] = _dot(x_ref[...], w_ref[...].astype(BF16)).astype(o_ref.dtype)


def _mm_res_kernel(x_ref, w_ref, r_ref, o_ref):
    o_ref[...] = r_ref[...] + _dot(x_ref[...], w_ref[...].astype(BF16))


def matmul_cols(x, w, *, tm, tn, out_dtype=F32, residual=None, name="matmul"):
    M, K = x.shape
    N = w.shape[1]
    in_specs = [pl.BlockSpec((tm, K), lambda i, j: (i, 0)), pl.BlockSpec((K, tn), lambda i, j: (0, j))]
    args = [x, w]
    kern = _mm_kernel
    if residual is not None:
        in_specs.append(pl.BlockSpec((tm, tn), lambda i, j: (i, j)))
        args.append(residual)
        kern = _mm_res_kernel
    return pl.pallas_call(
        kern,
        grid=(M // tm, N // tn),
        in_specs=in_specs,
        out_specs=pl.BlockSpec((tm, tn), lambda i, j: (i, j)),
        out_shape=jax.ShapeDtypeStruct((M, N), out_dtype),
        compiler_params=_params(("parallel", "parallel")),
        name=name,
    )(*args)


def _mm2_res_kernel(xa_ref, xb_ref, wa_ref, wb_ref, r_ref, o_ref):
    acc = _dot(xa_ref[...], wa_ref[...].astype(BF16)) + _dot(xb_ref[...], wb_ref[...].astype(BF16))
    o_ref[...] = r_ref[...] + acc


def matmul2_residual(xa, xb, w, residual, *, tm, tn, name):
    M, Ka = xa.shape
    Kb = xb.shape[1]
    assert Ka == Kb and w.shape[0] == Ka + Kb
    N = w.shape[1]
    return pl.pallas_call(
        _mm2_res_kernel,
        grid=(M // tm, N // tn),
        in_specs=[pl.BlockSpec((tm, Ka), lambda i, j: (i, 0)), pl.BlockSpec((tm, Kb), lambda i, j: (i, 0)),
                  pl.BlockSpec((Ka, tn), lambda i, j: (0, j)), pl.BlockSpec((Kb, tn), lambda i, j: (1, j)),
                  pl.BlockSpec((tm, tn), lambda i, j: (i, j))],
        out_specs=pl.BlockSpec((tm, tn), lambda i, j: (i, j)),
        out_shape=jax.ShapeDtypeStruct((M, N), F32),
        compiler_params=_params(("parallel", "parallel")),
        name=name,
    )(xa, xb, w, w, residual)


def _swiglu_up_kernel(x_ref, wg_ref, wu_ref, o_ref):
    x = x_ref[...]
    gate = _dot(x, wg_ref[...].astype(BF16))
    up = _dot(x, wu_ref[...].astype(BF16))
    o_ref[...] = (gate * jax.nn.sigmoid(gate) * up).astype(o_ref.dtype)


def swiglu_up(x, w13, *, tm, tn):
    M, K = x.shape
    F = w13.shape[1] // 2
    nj = F // tn
    return pl.pallas_call(
        _swiglu_up_kernel,
        grid=(M // tm, nj),
        in_specs=[pl.BlockSpec((tm, K), lambda i, j: (i, 0)),
                  pl.BlockSpec((K, tn), lambda i, j: (0, j)),
                  pl.BlockSpec((K, tn), lambda i, j: (0, j + nj))],
        out_specs=pl.BlockSpec((tm, tn), lambda i, j: (i, j)),
        out_shape=jax.ShapeDtypeStruct((M, F), BF16),
        compiler_params=_params(("parallel", "parallel")),
        name="swiglu_up",
    )(x, w13, w13)


def _down_res_kernel(a_ref, w_ref, at_ref, wt_ref, r_ref, o_ref, *, n_main):
    k = pl.program_id(2)

    @pl.when(k == 0)
    def _():
        o_ref[...] = r_ref[...]

    @pl.when(k < n_main)
    def _():
        o_ref[...] += _dot(a_ref[...], w_ref[...].astype(BF16))

    @pl.when(k >= n_main)
    def _():
        o_ref[...] += _dot(at_ref[...], wt_ref[...].astype(BF16))


def down_proj_residual(a, w2, residual, *, tm, tn, tk, tk_tail):
    M, F = a.shape
    N = w2.shape[1]
    n_main = F // tk
    tail = F - n_main * tk
    assert tail % tk_tail == 0 and (n_main * tk) % tk_tail == 0
    n_tail = tail // tk_tail
    first_tail = (n_main * tk) // tk_tail if n_tail else 0
    main_k = lambda k: jnp.minimum(k, n_main - 1)
    tail_k = lambda k: first_tail + jnp.maximum(k - n_main, 0)
    return pl.pallas_call(
        functools.partial(_down_res_kernel, n_main=n_main),
        grid=(M // tm, N // tn, n_main + n_tail),
        in_specs=[pl.BlockSpec((tm, tk), lambda i, j, k: (i, main_k(k))),
                  pl.BlockSpec((tk, tn), lambda i, j, k: (main_k(k), j)),
                  pl.BlockSpec((tm, tk_tail), lambda i, j, k: (i, tail_k(k))),
                  pl.BlockSpec((tk_tail, tn), lambda i, j, k: (tail_k(k), j)),
                  pl.BlockSpec((tm, tn), lambda i, j, k: (i, j))],
        out_specs=pl.BlockSpec((tm, tn), lambda i, j, k: (i, j)),
        out_shape=jax.ShapeDtypeStruct((M, N), F32),
        compiler_params=_params(("parallel", "parallel", "arbitrary")),
        name="down_proj",
    )(a, w2, a, w2, residual)


def _new_weights(te_ref, i):
    return jnp.logical_or(i == 0, te_ref[i] != te_ref[jnp.maximum(i - 1, 0)])


def _moe_up_kernel(te_ref, nv_ref, x_ref, wg_ref, wu_ref, o_ref, wgb_ref, wub_ref):
    i = pl.program_id(1)
    valid = i < nv_ref[0]

    @pl.when(_new_weights(te_ref, i))
    def _():
        wgb_ref[...] = wg_ref[...].astype(BF16)
        wub_ref[...] = wu_ref[...].astype(BF16)

    @pl.when(valid)
    def _():
        x = x_ref[...]
        gate = _dot(x, wgb_ref[...])
        up = _dot(x, wub_ref[...])
        o_ref[...] = (gate * jax.nn.sigmoid(gate) * up).astype(o_ref.dtype)

    @pl.when(jnp.logical_not(valid))
    def _():
        o_ref[...] = jnp.zeros_like(o_ref)


def moe_up(xs, w13, tile_expert, n_valid, *, tm, tn):
    P, K = xs.shape
    F = w13.shape[2] // 2
    nj = F // tn
    return pl.pallas_call(
        _moe_up_kernel,
        grid_spec=pltpu.PrefetchScalarGridSpec(
            num_scalar_prefetch=2,
            grid=(nj, P // tm),
            in_specs=[pl.BlockSpec((tm, K), lambda j, i, te, nv: (jnp.minimum(i, nv[0] - 1), 0)),
                      pl.BlockSpec((None, K, tn), lambda j, i, te, nv: (te[i], 0, j)),
                      pl.BlockSpec((None, K, tn), lambda j, i, te, nv: (te[i], 0, j + nj))],
            out_specs=pl.BlockSpec((tm, tn), lambda j, i, te, nv: (i, j)),
            scratch_shapes=[pltpu.VMEM((K, tn), BF16), pltpu.VMEM((K, tn), BF16)],
        ),
        out_shape=jax.ShapeDtypeStruct((P, F), BF16),
        compiler_params=_params(("arbitrary", "arbitrary")),
        name="moe_up",
    )(tile_expert, n_valid, xs, w13, w13)


def _moe_down_kernel(te_ref, nv_ref, a_ref, w_ref, rw_ref, o_ref, wb_ref):
    i = pl.program_id(1)
    valid = i < nv_ref[0]

    @pl.when(_new_weights(te_ref, i))
    def _():
        wb_ref[...] = w_ref[...].astype(BF16)

    @pl.when(valid)
    def _():
        o_ref[...] = _dot(a_ref[...], wb_ref[...]) * rw_ref[:, 0:1]

    @pl.when(jnp.logical_not(valid))
    def _():
        o_ref[...] = jnp.zeros_like(o_ref)


def moe_down(act, w2, row_w, tile_expert, n_valid, *, tm, tn):
    P, F = act.shape
    N = w2.shape[2]
    return pl.pallas_call(
        _moe_down_kernel,
        grid_spec=pltpu.PrefetchScalarGridSpec(
            num_scalar_prefetch=2,
            grid=(N // tn, P // tm),
            in_specs=[pl.BlockSpec((tm, F), lambda j, i, te, nv: (jnp.minimum(i, nv[0] - 1), 0)),
                      pl.BlockSpec((None, F, tn), lambda j, i, te, nv: (te[i], 0, j)),
                      pl.BlockSpec((tm, 128), lambda j, i, te, nv: (i, 0))],
            out_specs=pl.BlockSpec((tm, tn), lambda j, i, te, nv: (i, j)),
            scratch_shapes=[pltpu.VMEM((F, tn), BF16)],
        ),
        out_shape=jax.ShapeDtypeStruct((P, N), F32),
        compiler_params=_params(("arbitrary", "arbitrary")),
        name="moe_down",
    )(tile_expert, n_valid, act, w2, row_w)


def moe_dispatch(top_i, top_w, tm):
    M = top_i.shape[0]
    A = M * TOP_K
    P = A + N_EXPERTS * tm
    e_flat = top_i.reshape(A)
    onehot = (e_flat[:, None] == jnp.arange(N_EXPERTS, dtype=I32)[None, :]).astype(I32)
    csum = jnp.cumsum(onehot, axis=0)
    counts = csum[-1]
    padded = ((counts + tm - 1) // tm) * tm
    pend = jnp.cumsum(padded)
    pstart = pend - padded
    dest = jnp.sum(onehot * (pstart[None, :] + csum - onehot), axis=1)
    src_tok = jnp.zeros((P,), I32).at[dest].set(jnp.arange(A, dtype=I32) // TOP_K)
    row_w = jnp.zeros((P,), F32).at[dest].set(top_w.reshape(A))
    pos = dest.reshape(M, TOP_K)
    tile_start = jnp.arange(P // tm, dtype=I32) * tm
    tile_expert = jnp.minimum(jnp.sum((tile_start[:, None] >= pend[None, :]).astype(I32), axis=1), N_EXPERTS - 1)
    n_valid = (pend[-1:] // tm).astype(I32)
    last_e = tile_expert[jnp.maximum(n_valid[0] - 1, 0)]
    tile_expert = jnp.where(tile_start // tm < n_valid[0], tile_expert, last_e)
    return src_tok, row_w, pos, tile_expert, n_valid


def _nsa_prep_kernel(kc_ref, vc_ref, ks_ref, vs_ref, kw_ref, vw_ref, kg_ref, raw_ref, kso_ref, vso_ref, kwo_ref, vwo_ref):
    raw_ref[0] = kc_ref[...]
    raw_ref[1] = vc_ref[...]
    kg = kg_ref[...]
    kso_ref[...] = _rms(ks_ref[...], kg[1:2]).astype(BF16)
    vso_ref[...] = vs_ref[...].astype(BF16)
    kwo_ref[...] = _rms(kw_ref[...], kg[2:3]).astype(BF16)
    vwo_ref[...] = vw_ref[...].astype(BF16)


def nsa_prep(z, k_gain, B, S, ts=512):
    G = NSA_KV_GROUPS
    nt = S // ts
    cb = COL_KV_A // HEAD_DIM

    def col(br, kvi):
        return pl.BlockSpec((ts, HEAD_DIM), lambda b, g, t: (b * nt + t, cb + br * 4 + kvi * 2 + g))

    kv_out = pl.BlockSpec((None, None, ts, HEAD_DIM), lambda b, g, t: (b, g, t, 0))
    kv_shape = jax.ShapeDtypeStruct((B, G, S, HEAD_DIM), BF16)
    return pl.pallas_call(
        _nsa_prep_kernel,
        grid=(B, G, nt),
        in_specs=[col(0, 0), col(0, 1), col(1, 0), col(1, 1), col(2, 0), col(2, 1),
                  pl.BlockSpec((3, HEAD_DIM), lambda b, g, t: (0, 0))],
        out_specs=[pl.BlockSpec((None, 2, None, ts, HEAD_DIM), lambda b, g, t: (b, 0, g, t, 0)),
                   kv_out, kv_out, kv_out, kv_out],
        out_shape=[jax.ShapeDtypeStruct((B, 2, G, S, HEAD_DIM), F32), kv_shape, kv_shape, kv_shape, kv_shape],
        compiler_params=_params(("parallel", "parallel", "parallel")),
        name="nsa_prep",
    )(z, z, z, z, z, z, k_gain)


def _nsa_compress_kernel(r_ref, pos_ref, w1_ref, w2_ref, kg_ref, o_ref):
    kv = pl.program_id(1)
    half = (CMP_BLOCK // 2) * HEAD_DIM
    r = r_ref[...]
    pos = pos_ref[...]
    n = r.shape[0]
    first = _dot((r + pos[:, :half]).astype(BF16), w1_ref[:half, :].astype(BF16))
    second = _dot((r + pos[:, half:]).astype(BF16), w1_ref[half:, :].astype(BF16))
    pre = first + pltpu.roll(second, n - 1, axis=0)
    hid = jax.nn.gelu(pre)
    comp = _dot(hid.astype(BF16), w2_ref[...].astype(BF16))
    o_ref[...] = jnp.where(kv == 0, _rms(comp, kg_ref[...]), comp).astype(o_ref.dtype)


def nsa_compress(raw, cmp_pos, cmp_w1, cmp_w2, k_gain0, B, S):
    G = NSA_KV_GROUPS
    n = S // CMP_STRIDE
    feat = CMP_STRIDE * HEAD_DIM
    r = raw.reshape(B, 2, G, n, feat)
    pos = cmp_pos.reshape(2, 1, CMP_BLOCK * HEAD_DIM)
    return pl.pallas_call(
        _nsa_compress_kernel,
        grid=(B, 2, G),
        in_specs=[pl.BlockSpec((None, None, None, n, feat), lambda b, kv, g: (b, kv, g, 0, 0)),
                  pl.BlockSpec((None, 1, CMP_BLOCK * HEAD_DIM), lambda b, kv, g: (kv, 0, 0)),
                  pl.BlockSpec((None, CMP_BLOCK * HEAD_DIM, HEAD_DIM), lambda b, kv, g: (kv, 0, 0)),
                  pl.BlockSpec((None, HEAD_DIM, HEAD_DIM), lambda b, kv, g: (kv, 0, 0)),
                  pl.BlockSpec((1, HEAD_DIM), lambda b, kv, g: (0, 0))],
        out_specs=pl.BlockSpec((None, None, None, n, HEAD_DIM), lambda b, kv, g: (b, kv, g, 0, 0)),
        out_shape=jax.ShapeDtypeStruct((B, 2, G, n, HEAD_DIM), BF16),
        compiler_params=_params(("parallel", "parallel", "parallel")),
        name="nsa_compress",
    )(r, pos, cmp_w1, cmp_w2, k_gain0.reshape(1, HEAD_DIM))


def _rep(x, n):
    return x if n == HEAD_DIM else jnp.concatenate([x] * (n // HEAD_DIM), axis=1)


def _flash_step(qb, kt, vt, bias, m_ref, l_ref, acc_ref):
    nk = kt.shape[0]
    rows = qb.shape[0]
    s = _dot_nt(qb, kt)
    s = (s.reshape(rows // Q_BLOCK, Q_BLOCK, nk) + bias[None]).reshape(rows, nk)
    m_prev = m_ref[...]
    m_new = jnp.maximum(m_prev, jnp.max(s, axis=-1, keepdims=True))
    alpha = jnp.exp2(m_prev - m_new)
    p = jnp.exp2(s - _rep(m_new, nk))
    l_ref[...] = alpha * l_ref[...] + jnp.sum(p, axis=-1, keepdims=True)
    acc_ref[...] = alpha * acc_ref[...] + _dot(p.astype(BF16), vt)
    m_ref[...] = m_new


def _nsa_kernel(q_ref, sm_ref, kc_ref, vc_ref, ks_ref, vs_ref, kw_ref, vw_ref, qg_ref, og_ref, c2s_ref, exp_ref,
                o_ref, selb_ref, m_ref, l_ref, acc_ref, *, k_top):
    Hg = NSA_GROUP_SIZE
    rows = Hg * Q_BLOCK
    c = pl.program_id(2)
    t0 = c * Q_BLOCK
    n_cmp = kc_ref.shape[0]
    n_sel = c2s_ref.shape[0]

    q = q_ref[...]
    qs = jnp.concatenate([q[:, h * HEAD_DIM:(h + 1) * HEAD_DIM] for h in range(Hg)], axis=0)
    qb = (_rms(qs, qg_ref[...]) * (SCALE * LOG2E)).astype(BF16)

    tq = t0 + lax.broadcasted_iota(I32, (Q_BLOCK, n_cmp), 0)
    c_end = lax.broadcasted_iota(I32, (Q_BLOCK, n_cmp), 1) * CMP_STRIDE + (CMP_BLOCK - 1)
    sc = _dot_nt(qb, kc_ref[...]).reshape(Hg, Q_BLOCK, n_cmp)
    sc = jnp.where((c_end <= tq)[None], sc, -jnp.inf)
    mc = jnp.max(sc, axis=-1, keepdims=True)
    mc = jnp.where(mc > -jnp.inf, mc, 0.0)
    pc = jnp.exp2(sc - mc)
    dc = jnp.sum(pc, axis=-1, keepdims=True)
    pc = pc / jnp.where(dc > 0, dc, 1.0)
    o_cmp = _dot(pc.reshape(rows, n_cmp).astype(BF16), vc_ref[...])

    psum = jnp.sum(pc, axis=0)
    c2s = c2s_ref[...]
    p_hi = psum.astype(BF16)
    rem = psum - p_hi.astype(F32)
    p_mid = rem.astype(BF16)
    p_lo = (rem - p_mid.astype(F32)).astype(BF16)
    imp = _dot_nt(c2s, p_hi) + _dot_nt(c2s, p_mid) + _dot_nt(c2s, p_lo)

    jj = lax.broadcasted_iota(I32, (n_sel, Q_BLOCK), 0)
    cur = lax.shift_right_logical(t0 + lax.broadcasted_iota(I32, (n_sel, Q_BLOCK), 1), 6)
    valid = jj <= cur
    forced = valid & ((jj == 0) | (jj > cur - SEL_LOCAL))
    score = jnp.where(forced, FORCE, jnp.where(valid, imp, -FORCE))
    rank = jnp.zeros((n_sel, Q_BLOCK), F32)
    for j2 in range(n_sel):
        other = score[j2:j2 + 1, :]
        tie = jnp.where(jj > j2, 1.0, 0.0)
        rank = rank + jnp.where(other > score, 1.0, jnp.where(other == score, tie, 0.0))
    sel_t = jnp.where(rank < k_top, 1.0, 0.0)
    if n_sel < Q_BLOCK:
        sel_t = jnp.concatenate([sel_t, jnp.zeros((Q_BLOCK - n_sel, Q_BLOCK), F32)], axis=0)
    sel = sel_t.T[:, :n_sel].astype(BF16)
    sel_keys = _dot(sel, exp_ref[...])
    for i in range(selb_ref.shape[0]):
        selb_ref[i] = (sel_keys[:, i * SEL_TILE:(i + 1) * SEL_TILE] - 1.0) * (-NEG)

    def reset():
        m_ref[...] = jnp.full(m_ref.shape, NEG, F32)
        l_ref[...] = jnp.zeros(l_ref.shape, F32)
        acc_ref[...] = jnp.zeros(acc_ref.shape, F32)

    reset()
    tq_s = t0 + lax.broadcasted_iota(I32, (Q_BLOCK, SEL_TILE), 0)
    kk_s = lax.broadcasted_iota(I32, (Q_BLOCK, SEL_TILE), 1)

    def sel_body(i, carry):
        k0 = pl.multiple_of(i * SEL_TILE, SEL_TILE)
        bias = selb_ref[i] + jnp.where(kk_s + k0 <= tq_s, 0.0, NEG)
        _flash_step(qb, ks_ref[pl.ds(k0, SEL_TILE), :], vs_ref[pl.ds(k0, SEL_TILE), :], bias, m_ref, l_ref, acc_ref)
        return carry

    lax.fori_loop(0, (t0 + Q_BLOCK + SEL_TILE - 1) // SEL_TILE, sel_body, 0)
    o_sel = acc_ref[...] / l_ref[...]

    wk = WINDOW + Q_BLOCK
    w0 = pl.multiple_of(jnp.maximum(t0 - WINDOW, 0), Q_BLOCK)
    tq_w = t0 + lax.broadcasted_iota(I32, (Q_BLOCK, wk), 0)
    wpos = w0 + lax.broadcasted_iota(I32, (Q_BLOCK, wk), 1)
    w_bias = jnp.where((wpos <= tq_w) & (wpos > tq_w - WINDOW), 0.0, NEG)
    sw = _dot_nt(qb, kw_ref[pl.ds(w0, wk), :]).reshape(Hg, Q_BLOCK, wk) + w_bias[None]
    pw = jnp.exp2(sw - jnp.max(sw, axis=-1, keepdims=True))
    lw = jnp.sum(pw, axis=-1, keepdims=True).reshape(rows, 1)
    o_win = _dot(pw.reshape(rows, wk).astype(BF16), vw_ref[pl.ds(w0, wk), :]) / lw

    gates = jax.nn.sigmoid(sm_ref[...])
    og = og_ref[...]
    for h in range(Hg):
        r = slice(h * Q_BLOCK, (h + 1) * Q_BLOCK)
        o = (gates[:, h:h + 1] * o_cmp[r] + gates[:, Hg + h:Hg + h + 1] * o_sel[r]
             + gates[:, 2 * Hg + h:2 * Hg + h + 1] * o_win[r])
        o_ref[:, h * HEAD_DIM:(h + 1) * HEAD_DIM] = _rms(o, og).astype(o_ref.dtype)


def nsa_attention(z, comp, ks, vs, kw, vw, q_gain, out_gain, B, S):
    G, Hg = NSA_KV_GROUPS, NSA_GROUP_SIZE
    nq = S // Q_BLOCK
    n_cmp = S // CMP_STRIDE
    n_sel = S // SEL_BLOCK
    k_top = min(SEL_TOP_N, n_sel)
    c_start = np.arange(n_cmp)[:, None] * CMP_STRIDE
    s_start = np.arange(n_sel)[None, :] * SEL_BLOCK
    overlap = np.minimum(c_start + CMP_BLOCK, s_start + SEL_BLOCK) - np.maximum(c_start, s_start)
    assert n_sel <= Q_BLOCK
    c2s = jnp.asarray((np.clip(overlap, 0, None) / CMP_STRIDE).T, dtype=BF16)
    expand = jnp.asarray(np.arange(S)[None, :] // SEL_BLOCK == np.arange(n_sel)[:, None], dtype=BF16)
    rows = Hg * Q_BLOCK
    wq = Hg * HEAD_DIM
    kv_spec = pl.BlockSpec((None, None, S, HEAD_DIM), lambda b, g, c: (b, g, 0, 0))
    const = lambda b, g, c: (0, 0)
    return pl.pallas_call(
        functools.partial(_nsa_kernel, k_top=k_top),
        grid=(B, G, nq),
        in_specs=[pl.BlockSpec((Q_BLOCK, wq), lambda b, g, c: (b * nq + c, COL_Q_A // wq + g)),
                  pl.BlockSpec((Q_BLOCK, HEAD_DIM), lambda b, g, c: (b * nq + c, COL_SMALL // HEAD_DIM + g)),
                  pl.BlockSpec((None, None, None, n_cmp, HEAD_DIM), lambda b, g, c: (b, 0, g, 0, 0)),
                  pl.BlockSpec((None, None, None, n_cmp, HEAD_DIM), lambda b, g, c: (b, 1, g, 0, 0)),
                  kv_spec, kv_spec, kv_spec, kv_spec,
                  pl.BlockSpec((1, HEAD_DIM), const), pl.BlockSpec((1, HEAD_DIM), const),
                  pl.BlockSpec((n_sel, n_cmp), const), pl.BlockSpec((n_sel, S), const)],
        out_specs=pl.BlockSpec((Q_BLOCK, wq), lambda b, g, c: (b * nq + c, g)),
        out_shape=jax.ShapeDtypeStruct((B * S, D_NSA), BF16),
        scratch_shapes=[pltpu.VMEM((S // SEL_TILE, Q_BLOCK, SEL_TILE), F32),
                        pltpu.VMEM((rows, HEAD_DIM), F32), pltpu.VMEM((rows, HEAD_DIM), F32),
                        pltpu.VMEM((rows, HEAD_DIM), F32)],
        compiler_params=_params(("parallel", "parallel", "arbitrary")),
        name="nsa_attention",
    )(z, z, comp, comp, ks, vs, kw, vw, q_gain.reshape(1, HEAD_DIM), out_gain.reshape(1, HEAD_DIM), c2s, expand)


def _gdn_chunk_kernel(xc_ref, xh_ref, sm_ref, cw_ref, alog_ref, dtb_ref, rep_ref, unfold_ref, u_ref, wq_ref, ak_ref,
                      egl_ref):
    for b in range(xc_ref.shape[0]):
        _gdn_chunk_body(xc_ref.at[b], xh_ref.at[b], sm_ref.at[b], cw_ref, alog_ref, dtb_ref, rep_ref, unfold_ref,
                        u_ref.at[b], wq_ref.at[b], ak_ref.at[b], egl_ref.at[b])


def _gdn_chunk_body(xc_ref, xh_ref, sm_ref, cw_ref, alog_ref, dtb_ref, rep_ref, unfold_ref, u_ref, wq_ref, ak_ref,
                    egl_ref):
    C = GDN_CHUNK
    PK = GDN_PACK
    R = PK * C
    n = pl.program_id(0)
    xp = jnp.concatenate([jnp.where(n > 0, xh_ref[...], 0.0), xc_ref[...]], axis=0)
    cw = cw_ref[...]
    y = None
    for j in range(GDN_CONV):
        shift = GDN_CONV - 1 - j
        xs = xp if shift == 0 else pltpu.roll(xp, shift, axis=0)
        term = xs[8:] * cw[j:j + 1, :]
        y = term if y is None else y + term
    y = y * jax.nn.sigmoid(y)

    sm = sm_ref[...]
    beta = jax.nn.sigmoid(sm)
    g = -jnp.exp(alog_ref[...]) * jax.nn.softplus(sm + dtb_ref[...])
    row = lax.broadcasted_iota(I32, g.shape, 0)
    gc = g
    d = 1
    while d < C:
        gc = gc + jnp.where(row >= d, pltpu.roll(gc, d, axis=0), 0.0)
        d *= 2
    g_last = gc[C - 1:C, :]

    ri = lax.broadcasted_iota(I32, (R, R), 0)
    ci = lax.broadcasted_iota(I32, (R, R), 1)
    same = lax.shift_right_logical(ri, 6) == lax.shift_right_logical(ci, 6)
    tri = same & (ri >= ci)
    strict = same & (ri > ci)
    SUB = GDN_SUB
    nb = R // SUB
    same16 = lax.shift_right_logical(ri, 4) == lax.shift_right_logical(ci, 4)
    same32 = lax.shift_right_logical(ri, 5) == lax.shift_right_logical(ci, 5)
    off32 = same32 & jnp.logical_not(same16)
    off64 = jnp.logical_not(same32)
    NG = GDN_HEADS // PK
    W4 = NG * SUB
    row_c = lax.broadcasted_iota(I32, (R, W4), 0) & (SUB - 1)
    col_s = lax.broadcasted_iota(I32, (nb, W4), 1) & (SUB - 1)

    groups = []
    for grp in range(NG):
        heads = [grp * PK + i for i in range(PK)]

        def stack(off):
            return jnp.concatenate([y[:, off + h * HEAD_DIM: off + (h + 1) * HEAD_DIM] for h in heads], axis=0)

        def col(x, base):
            return jnp.concatenate([x[:, base + h: base + h + 1] for h in heads], axis=0)

        q4 = stack(0)
        k4 = stack(D_GDN)
        v4 = stack(2 * D_GDN)
        q4 = q4 * lax.rsqrt(jnp.sum(q4 * q4, axis=-1, keepdims=True) + EPS) * SCALE
        k4 = k4 * lax.rsqrt(jnp.sum(k4 * k4, axis=-1, keepdims=True) + EPS)
        beta4 = col(beta, SM_BETA)
        gc4 = col(gc, SM_A)
        gl4 = jnp.concatenate([jnp.broadcast_to(g_last[:, SM_A + h: SM_A + h + 1], (C, 1)) for h in heads], axis=0)

        gb = jnp.broadcast_to(gc4, (R, R))
        decay = jnp.exp(jnp.where(tri, gb - gb.T, -jnp.inf))
        kb4 = k4 * beta4
        k4b = k4.astype(BF16)
        a = jnp.where(strict, _dot_nt(kb4.astype(BF16), k4b) * decay, 0.0)
        attn = _dot_nt(q4.astype(BF16), k4b) * decay
        at = a.T
        dct = jnp.concatenate([at[SUB * b:SUB * (b + 1), SUB * b:SUB * (b + 1)] for b in range(nb)], axis=0)
        groups.append((heads, q4, k4, v4, beta4, gc4, gl4, kb4, a, attn, dct))

    coef = _dot3(jnp.concatenate([g[-1] for g in groups], axis=1), rep_ref[...])
    t4 = jnp.where(row_c == (lax.broadcasted_iota(I32, (R, W4), 1) & (SUB - 1)), 1.0, 0.0)
    for i in range(1, SUB):
        s = jnp.sum((coef[:, i * HEAD_DIM: i * HEAD_DIM + W4] * t4).reshape(nb, SUB, W4), axis=1)
        new = jnp.where(col_s == i, 1.0, 0.0) - s
        t4 = jnp.where(row_c == i, jnp.broadcast_to(new[:, None, :], (nb, SUB, W4)).reshape(R, W4), t4)
    t_tiled = _dot3(t4, unfold_ref[...])

    for gi, (heads, q4, k4, v4, beta4, gc4, gl4, kb4, a, attn, _) in enumerate(groups):
        t16 = jnp.where(same16, t_tiled[:, gi * R:(gi + 1) * R], 0.0)
        t16b = t16.astype(BF16)
        a32 = jnp.where(off32, a, 0.0).astype(BF16)
        t32 = t16 - _dot(_dot(t16b, a32).astype(BF16), t16b)
        t32b = t32.astype(BF16)
        a64 = jnp.where(off64, a, 0.0).astype(BF16)
        t64 = t32 - _dot(_dot(t32b, a64).astype(BF16), t32b)
        eg = jnp.exp(gc4)
        rhs = jnp.concatenate([v4 * beta4, kb4 * eg], axis=1)
        rhs = _dot(t64.astype(BF16), rhs.astype(BF16))
        qd4 = q4 * eg
        kdt = (k4 * jnp.exp(gl4 - gc4)).T
        for i, h in enumerate(heads):
            r = slice(i * C, (i + 1) * C)
            u_ref[h] = rhs[r, :HEAD_DIM]
            wq_ref[h] = jnp.concatenate([rhs[r, HEAD_DIM:], qd4[r]], axis=0).astype(BF16)
            ak_ref[h] = jnp.concatenate([attn[r, r], kdt[:, r]], axis=0).astype(BF16)
            egl_ref[h] = jnp.broadcast_to(jnp.exp(g_last[:, SM_A + h: SM_A + h + 1]), (8, HEAD_DIM))


def gdn_chunks(z, conv_w, a_log, dt_bias, B, S):
    C, H = GDN_CHUNK, GDN_HEADS
    N = S // C
    W = 3 * D_GDN
    pad = lambda v: jnp.zeros((1, HEAD_DIM), F32).at[0, SM_A:SM_A + H].set(v)
    out5 = lambda r, cdim: pl.BlockSpec((B, None, H, r, cdim), lambda n: (0, n, 0, 0, 0))
    NG, SUB, R = H // GDN_PACK, GDN_SUB, GDN_PACK * C
    W4 = NG * SUB
    src = np.arange(W4)[:, None]
    dst = np.arange(SUB * HEAD_DIM)[None, :]
    lane = dst % HEAD_DIM
    rep = (lane < W4) & (src // SUB == lane // SUB) & (src % SUB == dst // HEAD_DIM)
    dst = np.arange(NG * R)[None, :]
    unfold = (src // SUB == dst // R) & (src % SUB == dst % SUB)
    z3 = z.reshape(B, S, z.shape[1])
    const = lambda n: (0, 0)
    return pl.pallas_call(
        _gdn_chunk_kernel,
        grid=(N,),
        in_specs=[pl.BlockSpec((B, C, W), lambda n: (0, n, 0)),
                  pl.BlockSpec((B, 8, W), lambda n: (0, jnp.maximum(n * (C // 8) - 1, 0), 0)),
                  pl.BlockSpec((B, C, HEAD_DIM), lambda n: (0, n, COL_SMALL // HEAD_DIM)),
                  pl.BlockSpec((GDN_CONV, W), const),
                  pl.BlockSpec((1, HEAD_DIM), const),
                  pl.BlockSpec((1, HEAD_DIM), const),
                  pl.BlockSpec((W4, SUB * HEAD_DIM), const),
                  pl.BlockSpec((W4, NG * R), const)],
        out_specs=[out5(C, HEAD_DIM), out5(2 * C, HEAD_DIM), out5(C + HEAD_DIM, C), out5(8, HEAD_DIM)],
        out_shape=[jax.ShapeDtypeStruct((B, N, H, C, HEAD_DIM), F32),
                   jax.ShapeDtypeStruct((B, N, H, 2 * C, HEAD_DIM), BF16),
                   jax.ShapeDtypeStruct((B, N, H, C + HEAD_DIM, C), BF16),
                   jax.ShapeDtypeStruct((B, N, H, 8, HEAD_DIM), F32)],
        compiler_params=_params(("parallel",)),
        name="gdn_chunks",
    )(z3, z3, z3, conv_w, pad(a_log), pad(dt_bias), jnp.asarray(rep, dtype=BF16), jnp.asarray(unfold, dtype=BF16))


def _gdn_scan_kernel(u_ref, wq_ref, ak_ref, egl_ref, z_ref, og_ref, o_ref, s_ref):
    C = GDN_CHUNK

    @pl.when(pl.program_id(0) == 0)
    def _():
        s_ref[...] = jnp.zeros(s_ref.shape, F32)

    og = og_ref[...]
    for b in range(u_ref.shape[0]):
        for h in range(GDN_HEADS):
            state = s_ref[b, h]
            ws = _dot(wq_ref[b, h], state.astype(BF16))
            v_new = u_ref[b, h] - ws[:C]
            av = _dot(ak_ref[b, h], v_new.astype(BF16))
            o = ws[C:] + av[:C]
            decayed = (state.reshape(HEAD_DIM // 8, 8, HEAD_DIM) * egl_ref[b, h][None]).reshape(HEAD_DIM, HEAD_DIM)
            s_ref[b, h] = decayed + av[C:]
            zh = z_ref[b, :, h * HEAD_DIM:(h + 1) * HEAD_DIM]
            o_ref[b, :, h * HEAD_DIM:(h + 1) * HEAD_DIM] = (_rms(o, og) * (zh * jax.nn.sigmoid(zh))).astype(o_ref.dtype)


def gdn_scan(u, wq, ak, egl, z, out_gain, B, S):
    C, H = GDN_CHUNK, GDN_HEADS
    N = S // C
    in5 = lambda r, cdim: pl.BlockSpec((B, None, H, r, cdim), lambda n: (0, n, 0, 0, 0))
    y = pl.pallas_call(
        _gdn_scan_kernel,
        grid=(N,),
        in_specs=[in5(C, HEAD_DIM), in5(2 * C, HEAD_DIM), in5(C + HEAD_DIM, C), in5(8, HEAD_DIM),
                  pl.BlockSpec((B, C, D_GDN), lambda n: (0, n, COL_Z_B // D_GDN)),
                  pl.BlockSpec((1, HEAD_DIM), lambda n: (0, 0))],
        out_specs=pl.BlockSpec((B, C, D_GDN), lambda n: (0, n, 0)),
        out_shape=jax.ShapeDtypeStruct((B, S, D_GDN), BF16),
        scratch_shapes=[pltpu.VMEM((B, H, HEAD_DIM, HEAD_DIM), F32)],
        compiler_params=_params(("arbitrary",)),
        name="gdn_scan",
    )(u, wq, ak, egl, z.reshape(B, S, z.shape[1]), out_gain.reshape(1, HEAD_DIM))
    return y.reshape(B * S, D_GDN)


def _xa_kv_kernel(mem_ref, g_ref, w_ref, kg_ref, k_ref, v_ref):
    kv = _dot(_rms(mem_ref[...], g_ref[...]).astype(BF16), w_ref[...])
    for h in range(XA_HEADS):
        k_ref[h] = _rms(kv[:, h * HEAD_DIM:(h + 1) * HEAD_DIM], kg_ref[...]).astype(BF16)
        v_ref[h] = kv[:, D_XA + h * HEAD_DIM: D_XA + (h + 1) * HEAD_DIM].astype(BF16)


def xa_kv(mem, mem_norm, wkv_bf16, k_gain):
    B, Mm, D = mem.shape
    spec = pl.BlockSpec((None, XA_HEADS, Mm, HEAD_DIM), lambda b: (b, 0, 0, 0))
    shape = jax.ShapeDtypeStruct((B, XA_HEADS, Mm, HEAD_DIM), BF16)
    return pl.pallas_call(
        _xa_kv_kernel,
        grid=(B,),
        in_specs=[pl.BlockSpec((None, Mm, D), lambda b: (b, 0, 0)), pl.BlockSpec((1, D), lambda b: (0, 0)),
                  pl.BlockSpec((D, 2 * D_XA), lambda b: (0, 0)), pl.BlockSpec((1, HEAD_DIM), lambda b: (0, 0))],
        out_specs=[spec, spec],
        out_shape=[shape, shape],
        compiler_params=_params(("parallel",)),
        name="xa_kv",
    )(mem, mem_norm.reshape(1, D), wkv_bf16, k_gain.reshape(1, HEAD_DIM))


def _xa_kernel(x_ref, g_ref, wq_ref, k_ref, v_ref, qg_ref, wo_ref, fg_ref, *rest, route):
    if route:
        rwh_ref, rwl_ref, o_ref, h_ref, idx_ref, wt_ref = rest
    else:
        o_ref, h_ref = rest
    x = x_ref[...]
    q = _dot(_rms(x, g_ref[...]).astype(BF16), wq_ref[...])
    outs = []
    for h in range(XA_HEADS):
        qh = _rms(q[:, h * HEAD_DIM:(h + 1) * HEAD_DIM], qg_ref[...]).astype(BF16)
        s = _dot_nt(qh, k_ref[h]) * SCALE
        p = jnp.exp(s - jnp.max(s, axis=-1, keepdims=True))
        p = p / jnp.sum(p, axis=-1, keepdims=True)
        outs.append(_dot(p.astype(BF16), v_ref[h]))
    o = jnp.concatenate(outs, axis=1).astype(BF16)
    x_new = x + _dot(o, wo_ref[...])
    o_ref[...] = x_new
    hn = _rms(x_new, fg_ref[...])
    h_ref[...] = hn.astype(h_ref.dtype)
    if route:
        idx_ref[...], wt_ref[...] = _route(hn, rwh_ref[...], rwl_ref[...])


def cross_attention(x, xa_norm, wq_bf16, k, v, q_gain, wo_bf16, ffn_norm, router_w, B, S, tm=256):
    M, D = x.shape
    Mm = k.shape[2]
    per_b = S // tm
    kv_spec = pl.BlockSpec((None, XA_HEADS, Mm, HEAD_DIM), lambda i: (i // per_b, 0, 0, 0))
    const = lambda i: (0, 0)
    row = lambda i: (i, 0)
    route = router_w is not None
    in_specs = [pl.BlockSpec((tm, D), row), pl.BlockSpec((1, D), const),
                pl.BlockSpec((D, D_XA), const), kv_spec, kv_spec,
                pl.BlockSpec((1, HEAD_DIM), const), pl.BlockSpec((D_XA, D), const), pl.BlockSpec((1, D), const)]
    args = [x, xa_norm.reshape(1, D), wq_bf16, k, v, q_gain.reshape(1, HEAD_DIM), wo_bf16, ffn_norm.reshape(1, D)]
    out_specs = [pl.BlockSpec((tm, D), row), pl.BlockSpec((tm, D), row)]
    out_shape = [jax.ShapeDtypeStruct((M, D), F32), jax.ShapeDtypeStruct((M, D), BF16)]
    if route:
        rw = jnp.pad(router_w, ((0, 0), (0, 128 - N_EXPERTS)))
        rw_hi = rw.astype(BF16)
        in_specs += [pl.BlockSpec((D, 128), const), pl.BlockSpec((D, 128), const)]
        args += [rw_hi, (rw - rw_hi.astype(F32)).astype(BF16)]
        out_specs += [pl.BlockSpec((tm, 128), row), pl.BlockSpec((tm, 128), row)]
        out_shape += [jax.ShapeDtypeStruct((M, 128), I32), jax.ShapeDtypeStruct((M, 128), F32)]
    return pl.pallas_call(
        functools.partial(_xa_kernel, route=route),
        grid=(M // tm,),
        in_specs=in_specs,
        out_specs=out_specs,
        out_shape=out_shape,
        compiler_params=_params(("parallel",)),
        name="cross_attention",
    )(*args)


def _reorder_w_in(w_in):
    o = np.cumsum((D_NSA, 6 * D_KV_NSA, 3 * NSA_HEADS, 3 * D_GDN, GDN_HEADS, GDN_HEADS, D_GDN))
    q_a, kv_a, gate_a, qkv_b = (0, o[0]), (o[0], o[1]), o[1], (o[2], o[3])
    beta_b, a_b, z_b = o[3], o[4], (o[5], o[6])
    Hg = NSA_GROUP_SIZE
    src = np.zeros((SMALL_W,), np.int64)
    used = np.zeros((SMALL_W,), bool)
    for g in range(NSA_KV_GROUPS):
        for br in range(3):
            for h in range(Hg):
                src[g * HEAD_DIM + br * Hg + h] = gate_a + (g * Hg + h) * 3 + br
                used[g * HEAD_DIM + br * Hg + h] = True
    for h in range(GDN_HEADS):
        src[SM_BETA + h] = beta_b + h
        src[SM_A + h] = a_b + h
        used[SM_BETA + h] = used[SM_A + h] = True
    small = jnp.where(jnp.asarray(used)[None, :], jnp.take(w_in, jnp.asarray(src), axis=1), 0.0)
    parts = [w_in[:, qkv_b[0]:qkv_b[1]], w_in[:, q_a[0]:q_a[1]], w_in[:, z_b[0]:z_b[1]], w_in[:, kv_a[0]:kv_a[1]], small]
    return jnp.concatenate([p.astype(BF16) for p in parts], axis=1)


def mixer_layer(x2, B, S, attn_norm, w_in, nsa_q_gain, nsa_k_gain, nsa_cmp_pos, nsa_cmp_w1, nsa_cmp_w2, nsa_out_gain,
                gdn_conv_w, gdn_A_log, gdn_dt_bias, gdn_out_gain, w_out, tm=1024):
    h = rmsnorm_bf16(x2, attn_norm)
    z = matmul_cols(h, _reorder_w_in(w_in), tm=tm, tn=1024, name="in_proj")
    raw, ks, vs, kw, vw = nsa_prep(z, nsa_k_gain, B, S)
    comp = nsa_compress(raw, nsa_cmp_pos, nsa_cmp_w1, nsa_cmp_w2, nsa_k_gain[0], B, S)
    y_a = nsa_attention(z, comp, ks, vs, kw, vw, nsa_q_gain, nsa_out_gain, B, S)
    u, wq, ak, egl = gdn_chunks(z, gdn_conv_w, gdn_A_log, gdn_dt_bias, B, S)
    y_b = gdn_scan(u, wq, ak, egl, z, gdn_out_gain, B, S)
    return matmul2_residual(y_a, y_b, w_out, x2, tm=tm, tn=512, name="out_proj")


def xa_layer(x2, mem, B, S, xa_norm, mem_norm, xa_wq, xa_wkv, xa_q_gain, xa_k_gain, xa_wo, ffn_norm, router_w=None):
    k, v = xa_kv(mem, mem_norm, xa_wkv.astype(BF16), xa_k_gain)
    return cross_attention(x2, xa_norm, xa_wq.astype(BF16), k, v, xa_q_gain, xa_wo.astype(BF16), ffn_norm, router_w,
                           B, S)


def dense_ffn_layer(x2, h, w13, w2, tm=1024):
    act = swiglu_up(h, w13, tm=tm, tn=256)
    return down_proj_residual(act, w2, x2, tm=min(2 * tm, x2.shape[0]), tn=1024, tk=512, tk_tail=256)


def moe_ffn_layer(x2, h, idx, wts, w13, w2, tm=512, tn=512):
    src_tok, row_w, pos, tile_expert, n_valid = moe_dispatch(idx[:, :TOP_K], wts[:, :TOP_K], tm)
    xs = jnp.take(h, src_tok, axis=0)
    act = moe_up(xs, w13, tile_expert, n_valid, tm=tm, tn=tn)
    row_w_rep = jnp.broadcast_to(row_w[:, None], (row_w.shape[0], 128))
    out = moe_down(act, w2, row_w_rep, tile_expert, n_valid, tm=tm, tn=tn)
    return x2 + (jnp.take(out, pos[:, 0], axis=0) + jnp.take(out, pos[:, 1], axis=0))


def kernel(x, mem, attn_norm, w_in, nsa_q_gain, nsa_k_gain, nsa_cmp_pos, nsa_cmp_w1, nsa_cmp_w2, nsa_out_gain, gdn_conv_w, gdn_A_log, gdn_dt_bias, gdn_out_gain, w_out, xa_norm, mem_norm, xa_wq, xa_wkv, xa_q_gain, xa_k_gain, xa_wo, ffn_norm, dense_w13, dense_w2, router_w, moe_w13, moe_w2):
    B, S, D = x.shape
    x2 = x.reshape(B * S, D)
    for l in range(attn_norm.shape[0]):
        x2 = mixer_layer(x2, B, S, attn_norm[l], w_in[l], nsa_q_gain[l], nsa_k_gain[l], nsa_cmp_pos[l], nsa_cmp_w1[l],
                         nsa_cmp_w2[l], nsa_out_gain[l], gdn_conv_w[l], gdn_A_log[l], gdn_dt_bias[l], gdn_out_gain[l],
                         w_out[l])
        xa_args = (x2, mem, B, S, xa_norm[l], mem_norm[l], xa_wq[l], xa_wkv[l], xa_q_gain[l], xa_k_gain[l], xa_wo[l],
                   ffn_norm[l])
        if l % 2 == 0:
            x2, h = xa_layer(*xa_args)
            x2 = dense_ffn_layer(x2, h, dense_w13[l // 2], dense_w2[l // 2])
        else:
            x2, h, idx, wts = xa_layer(*xa_args, router_w[l // 2])
            x2 = moe_ffn_layer(x2, h, idx, wts, moe_w13[l // 2], moe_w2[l // 2])
    return x2.reshape(B, S, D)
```

```python
import functools

import jax
import jax.numpy as jnp
import numpy as np
from jax import lax
from jax.experimental import pallas as pl
from jax.experimental.pallas import tpu as pltpu

F32 = jnp.float32
BF16 = jnp.bfloat16
I32 = jnp.int32

D_MODEL = 4096
HEAD_DIM = 128
EPS = 1e-6
SCALE = HEAD_DIM ** -0.5

NSA_HEADS = 16
NSA_KV_GROUPS = 2
NSA_GROUP_SIZE = NSA_HEADS // NSA_KV_GROUPS
CMP_BLOCK = 32
CMP_STRIDE = 16
SEL_BLOCK = 64
SEL_TOP_N = 16
SEL_LOCAL = 2
WINDOW = 512
Q_BLOCK = 128
FORCE = 1e9
NEG = -1e30
SEL_TILE = 512
LOG2E = 1.4426950408889634

GDN_HEADS = 16
GDN_CONV = 4
GDN_CHUNK = 64
GDN_PACK = 4
GDN_SUB = 16

D_NSA = NSA_HEADS * HEAD_DIM
D_KV_NSA = NSA_KV_GROUPS * HEAD_DIM
D_GDN = GDN_HEADS * HEAD_DIM
D_MIX = D_NSA + D_GDN

XA_HEADS = 4
D_XA = XA_HEADS * HEAD_DIM
N_EXPERTS = 8
TOP_K = 2

COL_QKV_B = 0
COL_Q_A = COL_QKV_B + 3 * D_GDN
COL_Z_B = COL_Q_A + D_NSA
COL_KV_A = COL_Z_B + D_GDN
COL_SMALL = COL_KV_A + 6 * D_KV_NSA
SMALL_W = 512
N_IN_R = COL_SMALL + SMALL_W
RELAYOUT_TILE = 512
SM_BETA = 32
SM_A = 48

VMEM_LIMIT = 58 * 1024 * 1024

NT_DIMS = (((1,), (1,)), ((), ()))


def _rms(x, gain):
    return x * lax.rsqrt(jnp.mean(x * x, axis=-1, keepdims=True) + EPS) * gain


def _dot(a, b):
    return jnp.dot(a, b, preferred_element_type=F32)


def _dot_nt(a, b):
    return lax.dot_general(a, b, NT_DIMS, preferred_element_type=F32)


def _dot3(x, sel):
    hi = x.astype(BF16)
    rem = x - hi.astype(F32)
    mid = rem.astype(BF16)
    lo = (rem - mid.astype(F32)).astype(BF16)
    return _dot(hi, sel) + _dot(mid, sel) + _dot(lo, sel)


def _params(sem, vmem=VMEM_LIMIT):
    return pltpu.CompilerParams(dimension_semantics=sem, vmem_limit_bytes=vmem)


def _norm_kernel(x_ref, g_ref, o_ref):
    o_ref[...] = _rms(x_ref[...], g_ref[...]).astype(o_ref.dtype)


def rmsnorm_bf16(x, gain, tm=512):
    M, D = x.shape
    return pl.pallas_call(
        _norm_kernel,
        grid=(M // tm,),
        in_specs=[pl.BlockSpec((tm, D), lambda i: (i, 0)), pl.BlockSpec((1, D), lambda i: (0, 0))],
        out_specs=pl.BlockSpec((tm, D), lambda i: (i, 0)),
        out_shape=jax.ShapeDtypeStruct((M, D), BF16),
        compiler_params=_params(("parallel",)),
        name="rmsnorm",
    )(x, gain.reshape(1, D))


def _route(h, rw_hi, rw_lo):
    h_hi = h.astype(BF16)
    h_lo = (h - h_hi.astype(F32)).astype(BF16)
    logits = _dot(h_hi, rw_hi) + _dot(h_hi, rw_lo) + _dot(h_lo, rw_hi)
    lane = lax.broadcasted_iota(I32, logits.shape, 1).astype(F32)
    logits = jnp.where(lane < N_EXPERTS, logits, -jnp.inf)
    m1 = jnp.max(logits, axis=-1, keepdims=True)
    i1 = jnp.min(jnp.where(logits == m1, lane, 128.0), axis=-1, keepdims=True)
    rest = jnp.where(lane == i1, -jnp.inf, logits)
    m2 = jnp.max(rest, axis=-1, keepdims=True)
    i2 = jnp.min(jnp.where(rest == m2, lane, 128.0), axis=-1, keepdims=True)
    e2 = jnp.exp(m2 - m1)
    den = 1.0 + e2
    idx = jnp.where(lane == 0, i1, jnp.where(lane == 1, i2, 0.0)).astype(I32)
    wts = jnp.where(lane == 0, 1.0 / den, jnp.where(lane == 1, e2 / den, 0.0))
    return idx, wts


def _mm_kernel(x_ref, w_ref, o_ref):
    o_ref[...] = _dot(x_ref[...], w_ref[...].astype(BF16)).astype(o_ref.dtype)


def _mm_res_kernel(x_ref, w_ref, r_ref, o_ref):
    o_ref[...] = r_ref[...] + _dot(x_ref[...], w_ref[...].astype(BF16))


def matmul_cols(x, w, *, tm, tn, out_dtype=F32, residual=None, name="matmul"):
    M, K = x.shape
    N = w.shape[1]
    in_specs = [pl.BlockSpec((tm, K), lambda i, j: (i, 0)), pl.BlockSpec((K, tn), lambda i, j: (0, j))]
    args = [x, w]
    kern = _mm_kernel
    if residual is not None:
        in_specs.append(pl.BlockSpec((tm, tn), lambda i, j: (i, j)))
        args.append(residual)
        kern = _mm_res_kernel
    return pl.pallas_call(
        kern,
        grid=(M // tm, N // tn),
        in_specs=in_specs,
        out_specs=pl.BlockSpec((tm, tn), lambda i, j: (i, j)),
        out_shape=jax.ShapeDtypeStruct((M, N), out_dtype),
        compiler_params=_params(("parallel", "parallel")),
        name=name,
    )(*args)


def _mm2_res_kernel(xa_ref, xb_ref, wa_ref, wb_ref, r_ref, o_ref):
    acc = _dot(xa_ref[...], wa_ref[...].astype(BF16)) + _dot(xb_ref[...], wb_ref[...].astype(BF16))
    o_ref[...] = r_ref[...] + acc


def matmul2_residual(xa, xb, w, residual, *, tm, tn, name):
    M, Ka = xa.shape
    Kb = xb.shape[1]
    assert Ka == Kb and w.shape[0] == Ka + Kb
    N = w.shape[1]
    return pl.pallas_call(
        _mm2_res_kernel,
        grid=(M // tm, N // tn),
        in_specs=[pl.BlockSpec((tm, Ka), lambda i, j: (i, 0)), pl.BlockSpec((tm, Kb), lambda i, j: (i, 0)),
                  pl.BlockSpec((Ka, tn), lambda i, j: (0, j)), pl.BlockSpec((Kb, tn), lambda i, j: (1, j)),
                  pl.BlockSpec((tm, tn), lambda i, j: (i, j))],
        out_specs=pl.BlockSpec((tm, tn), lambda i, j: (i, j)),
        out_shape=jax.ShapeDtypeStruct((M, N), F32),
        compiler_params=_params(("parallel", "parallel")),
        name=name,
    )(xa, xb, w, w, residual)


def _swiglu_up_kernel(x_ref, wg_ref, wu_ref, o_ref):
    x = x_ref[...]
    gate = _dot(x, wg_ref[...].astype(BF16))
    up = _dot(x, wu_ref[...].astype(BF16))
    o_ref[...] = (gate * jax.nn.sigmoid(gate) * up).astype(o_ref.dtype)


def swiglu_up(x, w13, *, tm, tn):
    M, K = x.shape
    F = w13.shape[1] // 2
    nj = F // tn
    return pl.pallas_call(
        _swiglu_up_kernel,
        grid=(M // tm, nj),
        in_specs=[pl.BlockSpec((tm, K), lambda i, j: (i, 0)),
                  pl.BlockSpec((K, tn), lambda i, j: (0, j)),
                  pl.BlockSpec((K, tn), lambda i, j: (0, j + nj))],
        out_specs=pl.BlockSpec((tm, tn), lambda i, j: (i, j)),
        out_shape=jax.ShapeDtypeStruct((M, F), BF16),
        compiler_params=_params(("parallel", "parallel")),
        name="swiglu_up",
    )(x, w13, w13)


def _down_res_kernel(a_ref, w_ref, at_ref, wt_ref, r_ref, o_ref, *, n_main):
    k = pl.program_id(2)

    @pl.when(k == 0)
    def _():
        o_ref[...] = r_ref[...]

    @pl.when(k < n_main)
    def _():
        o_ref[...] += _dot(a_ref[...], w_ref[...].astype(BF16))

    @pl.when(k >= n_main)
    def _():
        o_ref[...] += _dot(at_ref[...], wt_ref[...].astype(BF16))


def down_proj_residual(a, w2, residual, *, tm, tn, tk, tk_tail):
    M, F = a.shape
    N = w2.shape[1]
    n_main = F // tk
    tail = F - n_main * tk
    assert tail % tk_tail == 0 and (n_main * tk) % tk_tail == 0
    n_tail = tail // tk_tail
    first_tail = (n_main * tk) // tk_tail if n_tail else 0
    main_k = lambda k: jnp.minimum(k, n_main - 1)
    tail_k = lambda k: first_tail + jnp.maximum(k - n_main, 0)
    return pl.pallas_call(
        functools.partial(_down_res_kernel, n_main=n_main),
        grid=(M // tm, N // tn, n_main + n_tail),
        in_specs=[pl.BlockSpec((tm, tk), lambda i, j, k: (i, main_k(k))),
                  pl.BlockSpec((tk, tn), lambda i, j, k: (main_k(k), j)),
                  pl.BlockSpec((tm, tk_tail), lambda i, j, k: (i, tail_k(k))),
                  pl.BlockSpec((tk_tail, tn), lambda i, j, k: (tail_k(k), j)),
                  pl.BlockSpec((tm, tn), lambda i, j, k: (i, j))],
        out_specs=pl.BlockSpec((tm, tn), lambda i, j, k: (i, j)),
        out_shape=jax.ShapeDtypeStruct((M, N), F32),
        compiler_params=_params(("parallel", "parallel", "arbitrary")),
        name="down_proj",
    )(a, w2, a, w2, residual)


def _new_weights(te_ref, i):
    return jnp.logical_or(i == 0, te_ref[i] != te_ref[jnp.maximum(i - 1, 0)])


def _moe_up_kernel(te_ref, nv_ref, x_ref, wg_ref, wu_ref, o_ref, wgb_ref, wub_ref):
    i = pl.program_id(1)
    valid = i < nv_ref[0]

    @pl.when(_new_weights(te_ref, i))
    def _():
        wgb_ref[...] = wg_ref[...].astype(BF16)
        wub_ref[...] = wu_ref[...].astype(BF16)

    @pl.when(valid)
    def _():
        x = x_ref[...]
        gate = _dot(x, wgb_ref[...])
        up = _dot(x, wub_ref[...])
        o_ref[...] = (gate * jax.nn.sigmoid(gate) * up).astype(o_ref.dtype)

    @pl.when(jnp.logical_not(valid))
    def _():
        o_ref[...] = jnp.zeros_like(o_ref)


def moe_up(xs, w13, tile_expert, n_valid, *, tm, tn):
    P, K = xs.shape
    F = w13.shape[2] // 2
    nj = F // tn
    return pl.pallas_call(
        _moe_up_kernel,
        grid_spec=pltpu.PrefetchScalarGridSpec(
            num_scalar_prefetch=2,
            grid=(nj, P // tm),
            in_specs=[pl.BlockSpec((tm, K), lambda j, i, te, nv: (jnp.minimum(i, nv[0] - 1), 0)),
                      pl.BlockSpec((None, K, tn), lambda j, i, te, nv: (te[i], 0, j)),
                      pl.BlockSpec((None, K, tn), lambda j, i, te, nv: (te[i], 0, j + nj))],
            out_specs=pl.BlockSpec((tm, tn), lambda j, i, te, nv: (i, j)),
            scratch_shapes=[pltpu.VMEM((K, tn), BF16), pltpu.VMEM((K, tn), BF16)],
        ),
        out_shape=jax.ShapeDtypeStruct((P, F), BF16),
        compiler_params=_params(("arbitrary", "arbitrary")),
        name="moe_up",
    )(tile_expert, n_valid, xs, w13, w13)


def _moe_down_kernel(te_ref, nv_ref, a_ref, w_ref, rw_ref, o_ref, wb_ref):
    i = pl.program_id(1)
    valid = i < nv_ref[0]

    @pl.when(_new_weights(te_ref, i))
    def _():
        wb_ref[...] = w_ref[...].astype(BF16)

    @pl.when(valid)
    def _():
        o_ref[...] = _dot(a_ref[...], wb_ref[...]) * rw_ref[:, 0:1]

    @pl.when(jnp.logical_not(valid))
    def _():
        o_ref[...] = jnp.zeros_like(o_ref)


def moe_down(act, w2, row_w, tile_expert, n_valid, *, tm, tn):
    P, F = act.shape
    N = w2.shape[2]
    return pl.pallas_call(
        _moe_down_kernel,
        grid_spec=pltpu.PrefetchScalarGridSpec(
            num_scalar_prefetch=2,
            grid=(N // tn, P // tm),
            in_specs=[pl.BlockSpec((tm, F), lambda j, i, te, nv: (jnp.minimum(i, nv[0] - 1), 0)),
                      pl.BlockSpec((None, F, tn), lambda j, i, te, nv: (te[i], 0, j)),
                      pl.BlockSpec((tm, 128), lambda j, i, te, nv: (i, 0))],
            out_specs=pl.BlockSpec((tm, tn), lambda j, i, te, nv: (i, j)),
            scratch_shapes=[pltpu.VMEM((F, tn), BF16)],
        ),
        out_shape=jax.ShapeDtypeStruct((P, N), F32),
        compiler_params=_params(("arbitrary", "arbitrary")),
        name="moe_down",
    )(tile_expert, n_valid, act, w2, row_w)


def moe_dispatch(top_i, top_w, tm):
    M = top_i.shape[0]
    A = M * TOP_K
    P = A + N_EXPERTS * tm
    e_flat = top_i.reshape(A)
    onehot = (e_flat[:, None] == jnp.arange(N_EXPERTS, dtype=I32)[None, :]).astype(I32)
    csum = jnp.cumsum(onehot, axis=0)
    counts = csum[-1]
    padded = ((counts + tm - 1) // tm) * tm
    pend = jnp.cumsum(padded)
    pstart = pend - padded
    dest = jnp.sum(onehot * (pstart[None, :] + csum - onehot), axis=1)
    src_tok = jnp.zeros((P,), I32).at[dest].set(jnp.arange(A, dtype=I32) // TOP_K)
    row_w = jnp.zeros((P,), F32).at[dest].set(top_w.reshape(A))
    pos = dest.reshape(M, TOP_K)
    tile_start = jnp.arange(P // tm, dtype=I32) * tm
    tile_expert = jnp.minimum(jnp.sum((tile_start[:, None] >= pend[None, :]).astype(I32), axis=1), N_EXPERTS - 1)
    n_valid = (pend[-1:] // tm).astype(I32)
    last_e = tile_expert[jnp.maximum(n_valid[0] - 1, 0)]
    tile_expert = jnp.where(tile_start // tm < n_valid[0], tile_expert, last_e)
    return src_tok, row_w, pos, tile_expert, n_valid


def _nsa_prep_kernel(kc_ref, vc_ref, ks_ref, vs_ref, kw_ref, vw_ref, kg_ref, raw_ref, kso_ref, vso_ref, kwo_ref, vwo_ref):
    raw_ref[0] = kc_ref[...]
    raw_ref[1] = vc_ref[...]
    kg = kg_ref[...]
    kso_ref[...] = _rms(ks_ref[...], kg[1:2]).astype(BF16)
    vso_ref[...] = vs_ref[...].astype(BF16)
    kwo_ref[...] = _rms(kw_ref[...], kg[2:3]).astype(BF16)
    vwo_ref[...] = vw_ref[...].astype(BF16)


def nsa_prep(z, k_gain, B, S, ts=512):
    G = NSA_KV_GROUPS
    nt = S // ts
    cb = COL_KV_A // HEAD_DIM

    def col(br, kvi):
        return pl.BlockSpec((ts, HEAD_DIM), lambda b, g, t: (b * nt + t, cb + br * 4 + kvi * 2 + g))

    kv_out = pl.BlockSpec((None, None, ts, HEAD_DIM), lambda b, g, t: (b, g, t, 0))
    kv_shape = jax.ShapeDtypeStruct((B, G, S, HEAD_DIM), BF16)
    return pl.pallas_call(
        _nsa_prep_kernel,
        grid=(B, G, nt),
        in_specs=[col(0, 0), col(0, 1), col(1, 0), col(1, 1), col(2, 0), col(2, 1),
                  pl.BlockSpec((3, HEAD_DIM), lambda b, g, t: (0, 0))],
        out_specs=[pl.BlockSpec((None, 2, None, ts, HEAD_DIM), lambda b, g, t: (b, 0, g, t, 0)),
                   kv_out, kv_out, kv_out, kv_out],
        out_shape=[jax.ShapeDtypeStruct((B, 2, G, S, HEAD_DIM), F32), kv_shape, kv_shape, kv_shape, kv_shape],
        compiler_params=_params(("parallel", "parallel", "parallel")),
        name="nsa_prep",
    )(z, z, z, z, z, z, k_gain)


def _nsa_compress_kernel(r_ref, pos_ref, w1_ref, w2_ref, kg_ref, o_ref):
    kv = pl.program_id(1)
    half = (CMP_BLOCK // 2) * HEAD_DIM
    r = r_ref[...]
    pos = pos_ref[...]
    n = r.shape[0]
    first = _dot((r + pos[:, :half]).astype(BF16), w1_ref[:half, :].astype(BF16))
    second = _dot((r + pos[:, half:]).astype(BF16), w1_ref[half:, :].astype(BF16))
    pre = first + pltpu.roll(second, n - 1, axis=0)
    hid = jax.nn.gelu(pre)
    comp = _dot(hid.astype(BF16), w2_ref[...].astype(BF16))
    o_ref[...] = jnp.where(kv == 0, _rms(comp, kg_ref[...]), comp).astype(o_ref.dtype)


def nsa_compress(raw, cmp_pos, cmp_w1, cmp_w2, k_gain0, B, S):
    G = NSA_KV_GROUPS
    n = S // CMP_STRIDE
    feat = CMP_STRIDE * HEAD_DIM
    r = raw.reshape(B, 2, G, n, feat)
    pos = cmp_pos.reshape(2, 1, CMP_BLOCK * HEAD_DIM)
    return pl.pallas_call(
        _nsa_compress_kernel,
        grid=(B, 2, G),
        in_specs=[pl.BlockSpec((None, None, None, n, feat), lambda b, kv, g: (b, kv, g, 0, 0)),
                  pl.BlockSpec((None, 1, CMP_BLOCK * HEAD_DIM), lambda b, kv, g: (kv, 0, 0)),
                  pl.BlockSpec((None, CMP_BLOCK * HEAD_DIM, HEAD_DIM), lambda b, kv, g: (kv, 0, 0)),
                  pl.BlockSpec((None, HEAD_DIM, HEAD_DIM), lambda b, kv, g: (kv, 0, 0)),
                  pl.BlockSpec((1, HEAD_DIM), lambda b, kv, g: (0, 0))],
        out_specs=pl.BlockSpec((None, None, None, n, HEAD_DIM), lambda b, kv, g: (b, kv, g, 0, 0)),
        out_shape=jax.ShapeDtypeStruct((B, 2, G, n, HEAD_DIM), BF16),
        compiler_params=_params(("parallel", "parallel", "parallel")),
        name="nsa_compress",
    )(r, pos, cmp_w1, cmp_w2, k_gain0.reshape(1, HEAD_DIM))


def _rep(x, n):
    return x if n == HEAD_DIM else jnp.concatenate([x] * (n // HEAD_DIM), axis=1)


def _flash_step(qb, kt, vt, bias, m_ref, l_ref, acc_ref):
    nk = kt.shape[0]
    rows = qb.shape[0]
    s = _dot_nt(qb, kt)
    s = (s.reshape(rows // Q_BLOCK, Q_BLOCK, nk) + bias[None]).reshape(rows, nk)
    m_prev = m_ref[...]
    m_new = jnp.maximum(m_prev, jnp.max(s, axis=-1, keepdims=True))
    alpha = jnp.exp2(m_prev - m_new)
    p = jnp.exp2(s - _rep(m_new, nk))
    l_ref[...] = alpha * l_ref[...] + jnp.sum(p, axis=-1, keepdims=True)
    acc_ref[...] = alpha * acc_ref[...] + _dot(p.astype(BF16), vt)
    m_ref[...] = m_new


def _nsa_kernel(q_ref, sm_ref, kc_ref, vc_ref, ks_ref, vs_ref, kw_ref, vw_ref, qg_ref, og_ref, c2s_ref, exp_ref,
                o_ref, selb_ref, m_ref, l_ref, acc_ref, *, k_top):
    Hg = NSA_GROUP_SIZE
    rows = Hg * Q_BLOCK
    c = pl.program_id(2)
    t0 = c * Q_BLOCK
    n_cmp = kc_ref.shape[0]
    n_sel = c2s_ref.shape[0]

    q = q_ref[...]
    qs = jnp.concatenate([q[:, h * HEAD_DIM:(h + 1) * HEAD_DIM] for h in range(Hg)], axis=0)
    qb = (_rms(qs, qg_ref[...]) * (SCALE * LOG2E)).astype(BF16)

    tq = t0 + lax.broadcasted_iota(I32, (Q_BLOCK, n_cmp), 0)
    c_end = lax.broadcasted_iota(I32, (Q_BLOCK, n_cmp), 1) * CMP_STRIDE + (CMP_BLOCK - 1)
    sc = _dot_nt(qb, kc_ref[...]).reshape(Hg, Q_BLOCK, n_cmp)
    sc = jnp.where((c_end <= tq)[None], sc, -jnp.inf)
    mc = jnp.max(sc, axis=-1, keepdims=True)
    mc = jnp.where(mc > -jnp.inf, mc, 0.0)
    pc = jnp.exp2(sc - mc)
    dc = jnp.sum(pc, axis=-1, keepdims=True)
    pc = pc / jnp.where(dc > 0, dc, 1.0)
    o_cmp = _dot(pc.reshape(rows, n_cmp).astype(BF16), vc_ref[...])

    psum = jnp.sum(pc, axis=0)
    c2s = c2s_ref[...]
    p_hi = psum.astype(BF16)
    rem = psum - p_hi.astype(F32)
    p_mid = rem.astype(BF16)
    p_lo = (rem - p_mid.astype(F32)).astype(BF16)
    imp = _dot_nt(c2s, p_hi) + _dot_nt(c2s, p_mid) + _dot_nt(c2s, p_lo)

    jj = lax.broadcasted_iota(I32, (n_sel, Q_BLOCK), 0)
    cur = lax.shift_right_logical(t0 + lax.broadcasted_iota(I32, (n_sel, Q_BLOCK), 1), 6)
    valid = jj <= cur
    forced = valid & ((jj == 0) | (jj > cur - SEL_LOCAL))
    score = jnp.where(forced, FORCE, jnp.where(valid, imp, -FORCE))
    rank = jnp.zeros((n_sel, Q_BLOCK), F32)
    for j2 in range(n_sel):
        other = score[j2:j2 + 1, :]
        tie = jnp.where(jj > j2, 1.0, 0.0)
        rank = rank + jnp.where(other > score, 1.0, jnp.where(other == score, tie, 0.0))
    sel_t = jnp.where(rank < k_top, 1.0, 0.0)
    if n_sel < Q_BLOCK:
        sel_t = jnp.concatenate([sel_t, jnp.zeros((Q_BLOCK - n_sel, Q_BLOCK), F32)], axis=0)
    sel = sel_t.T[:, :n_sel].astype(BF16)
    sel_keys = _dot(sel, exp_ref[...])
    for i in range(selb_ref.shape[0]):
        selb_ref[i] = (sel_keys[:, i * SEL_TILE:(i + 1) * SEL_TILE] - 1.0) * (-NEG)

    def reset():
        m_ref[...] = jnp.full(m_ref.shape, NEG, F32)
        l_ref[...] = jnp.zeros(l_ref.shape, F32)
        acc_ref[...] = jnp.zeros(acc_ref.shape, F32)

    reset()
    tq_s = t0 + lax.broadcasted_iota(I32, (Q_BLOCK, SEL_TILE), 0)
    kk_s = lax.broadcasted_iota(I32, (Q_BLOCK, SEL_TILE), 1)

    def sel_body(i, carry):
        k0 = pl.multiple_of(i * SEL_TILE, SEL_TILE)
        bias = selb_ref[i] + jnp.where(kk_s + k0 <= tq_s, 0.0, NEG)
        _flash_step(qb, ks_ref[pl.ds(k0, SEL_TILE), :], vs_ref[pl.ds(k0, SEL_TILE), :], bias, m_ref, l_ref, acc_ref)
        return carry

    lax.fori_loop(0, (t0 + Q_BLOCK + SEL_TILE - 1) // SEL_TILE, sel_body, 0)
    o_sel = acc_ref[...] / l_ref[...]

    wk = WINDOW + Q_BLOCK
    w0 = pl.multiple_of(jnp.maximum(t0 - WINDOW, 0), Q_BLOCK)
    tq_w = t0 + lax.broadcasted_iota(I32, (Q_BLOCK, wk), 0)
    wpos = w0 + lax.broadcasted_iota(I32, (Q_BLOCK, wk), 1)
    w_bias = jnp.where((wpos <= tq_w) & (wpos > tq_w - WINDOW), 0.0, NEG)
    sw = _dot_nt(qb, kw_ref[pl.ds(w0, wk), :]).reshape(Hg, Q_BLOCK, wk) + w_bias[None]
    pw = jnp.exp2(sw - jnp.max(sw, axis=-1, keepdims=True))
    lw = jnp.sum(pw, axis=-1, keepdims=True).reshape(rows, 1)
    o_win = _dot(pw.reshape(rows, wk).astype(BF16), vw_ref[pl.ds(w0, wk), :]) / lw

    gates = jax.nn.sigmoid(sm_ref[...])
    og = og_ref[...]
    for h in range(Hg):
        r = slice(h * Q_BLOCK, (h + 1) * Q_BLOCK)
        o = (gates[:, h:h + 1] * o_cmp[r] + gates[:, Hg + h:Hg + h + 1] * o_sel[r]
             + gates[:, 2 * Hg + h:2 * Hg + h + 1] * o_win[r])
        o_ref[:, h * HEAD_DIM:(h + 1) * HEAD_DIM] = _rms(o, og).astype(o_ref.dtype)


def nsa_attention(z, comp, ks, vs, kw, vw, q_gain, out_gain, B, S):
    G, Hg = NSA_KV_GROUPS, NSA_GROUP_SIZE
    nq = S // Q_BLOCK
    n_cmp = S // CMP_STRIDE
    n_sel = S // SEL_BLOCK
    k_top = min(SEL_TOP_N, n_sel)
    c_start = np.arange(n_cmp)[:, None] * CMP_STRIDE
    s_start = np.arange(n_sel)[None, :] * SEL_BLOCK
    overlap = np.minimum(c_start + CMP_BLOCK, s_start + SEL_BLOCK) - np.maximum(c_start, s_start)
    assert n_sel <= Q_BLOCK
    c2s = jnp.asarray((np.clip(overlap, 0, None) / CMP_STRIDE).T, dtype=BF16)
    expand = jnp.asarray(np.arange(S)[None, :] // SEL_BLOCK == np.arange(n_sel)[:, None], dtype=BF16)
    rows = Hg * Q_BLOCK
    wq = Hg * HEAD_DIM
    kv_spec = pl.BlockSpec((None, None, S, HEAD_DIM), lambda b, g, c: (b, g, 0, 0))
    const = lambda b, g, c: (0, 0)
    return pl.pallas_call(
        functools.partial(_nsa_kernel, k_top=k_top),
        grid=(B, G, nq),
        in_specs=[pl.BlockSpec((Q_BLOCK, wq), lambda b, g, c: (b * nq + c, COL_Q_A // wq + g)),
                  pl.BlockSpec((Q_BLOCK, HEAD_DIM), lambda b, g, c: (b * nq + c, COL_SMALL // HEAD_DIM + g)),
                  pl.BlockSpec((None, None, None, n_cmp, HEAD_DIM), lambda b, g, c: (b, 0, g, 0, 0)),
                  pl.BlockSpec((None, None, None, n_cmp, HEAD_DIM), lambda b, g, c: (b, 1, g, 0, 0)),
                  kv_spec, kv_spec, kv_spec, kv_spec,
                  pl.BlockSpec((1, HEAD_DIM), const), pl.BlockSpec((1, HEAD_DIM), const),
                  pl.BlockSpec((n_sel, n_cmp), const), pl.BlockSpec((n_sel, S), const)],
        out_specs=pl.BlockSpec((Q_BLOCK, wq), lambda b, g, c: (b * nq + c, g)),
        out_shape=jax.ShapeDtypeStruct((B * S, D_NSA), BF16),
        scratch_shapes=[pltpu.VMEM((S // SEL_TILE, Q_BLOCK, SEL_TILE), F32),
                        pltpu.VMEM((rows, HEAD_DIM), F32), pltpu.VMEM((rows, HEAD_DIM), F32),
                        pltpu.VMEM((rows, HEAD_DIM), F32)],
        compiler_params=_params(("parallel", "parallel", "arbitrary")),
        name="nsa_attention",
    )(z, z, comp, comp, ks, vs, kw, vw, q_gain.reshape(1, HEAD_DIM), out_gain.reshape(1, HEAD_DIM), c2s, expand)


def _gdn_chunk_kernel(xc_ref, xh_ref, sm_ref, cw_ref, alog_ref, dtb_ref, rep_ref, unfold_ref, u_ref, wq_ref, ak_ref,
                      egl_ref):
    for b in range(xc_ref.shape[0]):
        _gdn_chunk_body(xc_ref.at[b], xh_ref.at[b], sm_ref.at[b], cw_ref, alog_ref, dtb_ref, rep_ref, unfold_ref,
                        u_ref.at[b], wq_ref.at[b], ak_ref.at[b], egl_ref.at[b])


def _gdn_chunk_body(xc_ref, xh_ref, sm_ref, cw_ref, alog_ref, dtb_ref, rep_ref, unfold_ref, u_ref, wq_ref, ak_ref,
                    egl_ref):
    C = GDN_CHUNK
    PK = GDN_PACK
    R = PK * C
    n = pl.program_id(0)
    xp = jnp.concatenate([jnp.where(n > 0, xh_ref[...], 0.0), xc_ref[...]], axis=0)
    cw = cw_ref[...]
    y = None
    for j in range(GDN_CONV):
        shift = GDN_CONV - 1 - j
        xs = xp if shift == 0 else pltpu.roll(xp, shift, axis=0)
        term = xs[8:] * cw[j:j + 1, :]
        y = term if y is None else y + term
    y = y * jax.nn.sigmoid(y)

    sm = sm_ref[...]
    beta = jax.nn.sigmoid(sm)
    g = -jnp.exp(alog_ref[...]) * jax.nn.softplus(sm + dtb_ref[...])
    row = lax.broadcasted_iota(I32, g.shape, 0)
    gc = g
    d = 1
    while d < C:
        gc = gc + jnp.where(row >= d, pltpu.roll(gc, d, axis=0), 0.0)
        d *= 2
    g_last = gc[C - 1:C, :]

    ri = lax.broadcasted_iota(I32, (R, R), 0)
    ci = lax.broadcasted_iota(I32, (R, R), 1)
    same = lax.shift_right_logical(ri, 6) == lax.shift_right_logical(ci, 6)
    tri = same & (ri >= ci)
    strict = same & (ri > ci)
    SUB = GDN_SUB
    nb = R // SUB
    same16 = lax.shift_right_logical(ri, 4) == lax.shift_right_logical(ci, 4)
    same32 = lax.shift_right_logical(ri, 5) == lax.shift_right_logical(ci, 5)
    off32 = same32 & jnp.logical_not(same16)
    off64 = jnp.logical_not(same32)
    NG = GDN_HEADS // PK
    W4 = NG * SUB
    row_c = lax.broadcasted_iota(I32, (R, W4), 0) & (SUB - 1)
    col_s = lax.broadcasted_iota(I32, (nb, W4), 1) & (SUB - 1)

    groups = []
    for grp in range(NG):
        heads = [grp * PK + i for i in range(PK)]

        def stack(off):
            return jnp.concatenate([y[:, off + h * HEAD_DIM: off + (h + 1) * HEAD_DIM] for h in heads], axis=0)

        def col(x, base):
            return jnp.concatenate([x[:, base + h: base + h + 1] for h in heads], axis=0)

        q4 = stack(0)
        k4 = stack(D_GDN)
        v4 = stack(2 * D_GDN)
        q4 = q4 * lax.rsqrt(jnp.sum(q4 * q4, axis=-1, keepdims=True) + EPS) * SCALE
        k4 = k4 * lax.rsqrt(jnp.sum(k4 * k4, axis=-1, keepdims=True) + EPS)
        beta4 = col(beta, SM_BETA)
        gc4 = col(gc, SM_A)
        gl4 = jnp.concatenate([jnp.broadcast_to(g_last[:, SM_A + h: SM_A + h + 1], (C, 1)) for h in heads], axis=0)

        gb = jnp.broadcast_to(gc4, (R, R))
        decay = jnp.exp(jnp.where(tri, gb - gb.T, -jnp.inf))
        kb4 = k4 * beta4
        k4b = k4.astype(BF16)
        a = jnp.where(strict, _dot_nt(kb4.astype(BF16), k4b) * decay, 0.0)
        attn = _dot_nt(q4.astype(BF16), k4b) * decay
        at = a.T
        dct = jnp.concatenate([at[SUB * b:SUB * (b + 1), SUB * b:SUB * (b + 1)] for b in range(nb)], axis=0)
        groups.append((heads, q4, k4, v4, beta4, gc4, gl4, kb4, a, attn, dct))

    coef = _dot3(jnp.concatenate([g[-1] for g in groups], axis=1), rep_ref[...])
    t4 = jnp.where(row_c == (lax.broadcasted_iota(I32, (R, W4), 1) & (SUB - 1)), 1.0, 0.0)
    for i in range(1, SUB):
        s = jnp.sum((coef[:, i * HEAD_DIM: i * HEAD_DIM + W4] * t4).reshape(nb, SUB, W4), axis=1)
        new = jnp.where(col_s == i, 1.0, 0.0) - s
        t4 = jnp.where(row_c == i, jnp.broadcast_to(new[:, None, :], (nb, SUB, W4)).reshape(R, W4), t4)
    t_tiled = _dot3(t4, unfold_ref[...])

    for gi, (heads, q4, k4, v4, beta4, gc4, gl4, kb4, a, attn, _) in enumerate(groups):
        t16 = jnp.where(same16, t_tiled[:, gi * R:(gi + 1) * R], 0.0)
        t16b = t16.astype(BF16)
        a32 = jnp.where(off32, a, 0.0).astype(BF16)
        t32 = t16 - _dot(_dot(t16b, a32).astype(BF16), t16b)
        t32b = t32.astype(BF16)
        a64 = jnp.where(off64, a, 0.0).astype(BF16)
        t64 = t32 - _dot(_dot(t32b, a64).astype(BF16), t32b)
        eg = jnp.exp(gc4)
        rhs = jnp.concatenate([v4 * beta4, kb4 * eg], axis=1)
        rhs = _dot(t64.astype(BF16), rhs.astype(BF16))
        qd4 = q4 * eg
        kdt = (k4 * jnp.exp(gl4 - gc4)).T
        for i, h in enumerate(heads):
            r = slice(i * C, (i + 1) * C)
            u_ref[h] = rhs[r, :HEAD_DIM]
            wq_ref[h] = jnp.concatenate([rhs[r, HEAD_DIM:], qd4[r]], axis=0).astype(BF16)
            ak_ref[h] = jnp.concatenate([attn[r, r], kdt[:, r]], axis=0).astype(BF16)
            egl_ref[h] = jnp.broadcast_to(jnp.exp(g_last[:, SM_A + h: SM_A + h + 1]), (8, HEAD_DIM))


def gdn_chunks(z, conv_w, a_log, dt_bias, B, S):
    C, H = GDN_CHUNK, GDN_HEADS
    N = S // C
    W = 3 * D_GDN
    pad = lambda v: jnp.zeros((1, HEAD_DIM), F32).at[0, SM_A:SM_A + H].set(v)
    out5 = lambda r, cdim: pl.BlockSpec((B, None, H, r, cdim), lambda n: (0, n, 0, 0, 0))
    NG, SUB, R = H // GDN_PACK, GDN_SUB, GDN_PACK * C
    W4 = NG * SUB
    src = np.arange(W4)[:, None]
    dst = np.arange(SUB * HEAD_DIM)[None, :]
    lane = dst % HEAD_DIM
    rep = (lane < W4) & (src // SUB == lane // SUB) & (src % SUB == dst // HEAD_DIM)
    dst = np.arange(NG * R)[None, :]
    unfold = (src // SUB == dst // R) & (src % SUB == dst % SUB)
    z3 = z.reshape(B, S, z.shape[1])
    const = lambda n: (0, 0)
    return pl.pallas_call(
        _gdn_chunk_kernel,
        grid=(N,),
        in_specs=[pl.BlockSpec((B, C, W), lambda n: (0, n, 0)),
                  pl.BlockSpec((B, 8, W), lambda n: (0, jnp.maximum(n * (C // 8) - 1, 0), 0)),
                  pl.BlockSpec((B, C, HEAD_DIM), lambda n: (0, n, COL_SMALL // HEAD_DIM)),
                  pl.BlockSpec((GDN_CONV, W), const),
                  pl.BlockSpec((1, HEAD_DIM), const),
                  pl.BlockSpec((1, HEAD_DIM), const),
                  pl.BlockSpec((W4, SUB * HEAD_DIM), const),
                  pl.BlockSpec((W4, NG * R), const)],
        out_specs=[out5(C, HEAD_DIM), out5(2 * C, HEAD_DIM), out5(C + HEAD_DIM, C), out5(8, HEAD_DIM)],
        out_shape=[jax.ShapeDtypeStruct((B, N, H, C, HEAD_DIM), F32),
                   jax.ShapeDtypeStruct((B, N, H, 2 * C, HEAD_DIM), BF16),
                   jax.ShapeDtypeStruct((B, N, H, C + HEAD_DIM, C), BF16),
                   jax.ShapeDtypeStruct((B, N, H, 8, HEAD_DIM), F32)],
        compiler_params=_params(("parallel",)),
        name="gdn_chunks",
    )(z3, z3, z3, conv_w, pad(a_log), pad(dt_bias), jnp.asarray(rep, dtype=BF16), jnp.asarray(unfold, dtype=BF16))


def _gdn_scan_kernel(u_ref, wq_ref, ak_ref, egl_ref, z_ref, og_ref, o_ref, s_ref):
    C = GDN_CHUNK

    @pl.when(pl.program_id(0) == 0)
    def _():
        s_ref[...] = jnp.zeros(s_ref.shape, F32)

    og = og_ref[...]
    for b in range(u_ref.shape[0]):
        for h in range(GDN_HEADS):
            state = s_ref[b, h]
            ws = _dot(wq_ref[b, h], state.astype(BF16))
            v_new = u_ref[b, h] - ws[:C]
            av = _dot(ak_ref[b, h], v_new.astype(BF16))
            o = ws[C:] + av[:C]
            decayed = (state.reshape(HEAD_DIM // 8, 8, HEAD_DIM) * egl_ref[b, h][None]).reshape(HEAD_DIM, HEAD_DIM)
            s_ref[b, h] = decayed + av[C:]
            zh = z_ref[b, :, h * HEAD_DIM:(h + 1) * HEAD_DIM]
            o_ref[b, :, h * HEAD_DIM:(h + 1) * HEAD_DIM] = (_rms(o, og) * (zh * jax.nn.sigmoid(zh))).astype(o_ref.dtype)


def gdn_scan(u, wq, ak, egl, z, out_gain, B, S):
    C, H = GDN_CHUNK, GDN_HEADS
    N = S // C
    in5 = lambda r, cdim: pl.BlockSpec((B, None, H, r, cdim), lambda n: (0, n, 0, 0, 0))
    y = pl.pallas_call(
        _gdn_scan_kernel,
        grid=(N,),
        in_specs=[in5(C, HEAD_DIM), in5(2 * C, HEAD_DIM), in5(C + HEAD_DIM, C), in5(8, HEAD_DIM),
                  pl.BlockSpec((B, C, D_GDN), lambda n: (0, n, COL_Z_B // D_GDN)),
                  pl.BlockSpec((1, HEAD_DIM), lambda n: (0, 0))],
        out_specs=pl.BlockSpec((B, C, D_GDN), lambda n: (0, n, 0)),
        out_shape=jax.ShapeDtypeStruct((B, S, D_GDN), BF16),
        scratch_shapes=[pltpu.VMEM((B, H, HEAD_DIM, HEAD_DIM), F32)],
        compiler_params=_params(("arbitrary",)),
        name="gdn_scan",
    )(u, wq, ak, egl, z.reshape(B, S, z.shape[1]), out_gain.reshape(1, HEAD_DIM))
    return y.reshape(B * S, D_GDN)


def _xa_kv_kernel(mem_ref, g_ref, w_ref, kg_ref, k_ref, v_ref):
    kv = _dot(_rms(mem_ref[...], g_ref[...]).astype(BF16), w_ref[...])
    for h in range(XA_HEADS):
        k_ref[h] = _rms(kv[:, h * HEAD_DIM:(h + 1) * HEAD_DIM], kg_ref[...]).astype(BF16)
        v_ref[h] = kv[:, D_XA + h * HEAD_DIM: D_XA + (h + 1) * HEAD_DIM].astype(BF16)


def xa_kv(mem, mem_norm, wkv_bf16, k_gain):
    B, Mm, D = mem.shape
    spec = pl.BlockSpec((None, XA_HEADS, Mm, HEAD_DIM), lambda b: (b, 0, 0, 0))
    shape = jax.ShapeDtypeStruct((B, XA_HEADS, Mm, HEAD_DIM), BF16)
    return pl.pallas_call(
        _xa_kv_kernel,
        grid=(B,),
        in_specs=[pl.BlockSpec((None, Mm, D), lambda b: (b, 0, 0)), pl.BlockSpec((1, D), lambda b: (0, 0)),
                  pl.BlockSpec((D, 2 * D_XA), lambda b: (0, 0)), pl.BlockSpec((1, HEAD_DIM), lambda b: (0, 0))],
        out_specs=[spec, spec],
        out_shape=[shape, shape],
        compiler_params=_params(("parallel",)),
        name="xa_kv",
    )(mem, mem_norm.reshape(1, D), wkv_bf16, k_gain.reshape(1, HEAD_DIM))


def _xa_kernel(x_ref, g_ref, wq_ref, k_ref, v_ref, qg_ref, wo_ref, fg_ref, *rest, route):
    if route:
        rwh_ref, rwl_ref, o_ref, h_ref, idx_ref, wt_ref = rest
    else:
        o_ref, h_ref = rest
    x = x_ref[...]
    q = _dot(_rms(x, g_ref[...]).astype(BF16), wq_ref[...])
    outs = []
    for h in range(XA_HEADS):
        qh = _rms(q[:, h * HEAD_DIM:(h + 1) * HEAD_DIM], qg_ref[...]).astype(BF16)
        s = _dot_nt(qh, k_ref[h]) * SCALE
        p = jnp.exp(s - jnp.max(s, axis=-1, keepdims=True))
        p = p / jnp.sum(p, axis=-1, keepdims=True)
        outs.append(_dot(p.astype(BF16), v_ref[h]))
    o = jnp.concatenate(outs, axis=1).astype(BF16)
    x_new = x + _dot(o, wo_ref[...])
    o_ref[...] = x_new
    hn = _rms(x_new, fg_ref[...])
    h_ref[...] = hn.astype(h_ref.dtype)
    if route:
        idx_ref[...], wt_ref[...] = _route(hn, rwh_ref[...], rwl_ref[...])


def cross_attention(x, xa_norm, wq_bf16, k, v, q_gain, wo_bf16, ffn_norm, router_w, B, S, tm=256):
    M, D = x.shape
    Mm = k.shape[2]
    per_b = S // tm
    kv_spec = pl.BlockSpec((None, XA_HEADS, Mm, HEAD_DIM), lambda i: (i // per_b, 0, 0, 0))
    const = lambda i: (0, 0)
    row = lambda i: (i, 0)
    route = router_w is not None
    in_specs = [pl.BlockSpec((tm, D), row), pl.BlockSpec((1, D), const),
                pl.BlockSpec((D, D_XA), const), kv_spec, kv_spec,
                pl.BlockSpec((1, HEAD_DIM), const), pl.BlockSpec((D_XA, D), const), pl.BlockSpec((1, D), const)]
    args = [x, xa_norm.reshape(1, D), wq_bf16, k, v, q_gain.reshape(1, HEAD_DIM), wo_bf16, ffn_norm.reshape(1, D)]
    out_specs = [pl.BlockSpec((tm, D), row), pl.BlockSpec((tm, D), row)]
    out_shape = [jax.ShapeDtypeStruct((M, D), F32), jax.ShapeDtypeStruct((M, D), BF16)]
    if route:
        rw = jnp.pad(router_w, ((0, 0), (0, 128 - N_EXPERTS)))
        rw_hi = rw.astype(BF16)
        in_specs += [pl.BlockSpec((D, 128), const), pl.BlockSpec((D, 128), const)]
        args += [rw_hi, (rw - rw_hi.astype(F32)).astype(BF16)]
        out_specs += [pl.BlockSpec((tm, 128), row), pl.BlockSpec((tm, 128), row)]
        out_shape += [jax.ShapeDtypeStruct((M, 128), I32), jax.ShapeDtypeStruct((M, 128), F32)]
    return pl.pallas_call(
        functools.partial(_xa_kernel, route=route),
        grid=(M // tm,),
        in_specs=in_specs,
        out_specs=out_specs,
        out_shape=out_shape,
        compiler_params=_params(("parallel",)),
        name="cross_attention",
    )(*args)


def _reorder_w_in(w_in):
    o = np.cumsum((D_NSA, 6 * D_KV_NSA, 3 * NSA_HEADS, 3 * D_GDN, GDN_HEADS, GDN_HEADS, D_GDN))
    q_a, kv_a, gate_a, qkv_b = (0, o[0]), (o[0], o[1]), o[1], (o[2], o[3])
    beta_b, a_b, z_b = o[3], o[4], (o[5], o[6])
    Hg = NSA_GROUP_SIZE
    K = w_in.shape[0]
    gates = w_in[:, gate_a:gate_a + 3 * NSA_HEADS].reshape(K, NSA_KV_GROUPS, Hg, 3).transpose(0, 1, 3, 2)
    gates = gates.reshape(K, NSA_KV_GROUPS, 3 * Hg)
    zeros = lambda n: jnp.zeros((K, n), w_in.dtype)
    assert SM_BETA >= 3 * Hg and SM_A == SM_BETA + GDN_HEADS
    small = jnp.concatenate(
        [gates[:, 0], zeros(SM_BETA - 3 * Hg), w_in[:, beta_b:beta_b + 2 * GDN_HEADS], zeros(HEAD_DIM - SM_A - GDN_HEADS),
         gates[:, 1], zeros(SMALL_W - HEAD_DIM - 3 * Hg)], axis=1)
    buf = jnp.pad(small.astype(BF16), ((0, 0), (COL_SMALL, 0)))
    tw = RELAYOUT_TILE
    buf = _move_columns(w_in, buf, src_col=qkv_b[0], dst_col=COL_QKV_B, width=qkv_b[1] - qkv_b[0], tw=tw)
    buf = _move_columns(w_in, buf, src_col=z_b[0], dst_col=COL_Z_B, width=z_b[1] - z_b[0], tw=tw)
    buf = _move_columns(w_in, buf, src_col=q_a[0], dst_col=COL_Q_A, width=q_a[1] - q_a[0], tw=tw)
    return _move_columns(w_in, buf, src_col=kv_a[0], dst_col=COL_KV_A, width=kv_a[1] - kv_a[0], tw=tw)


def _move_columns_kernel(a_ref, b_ref, buf_ref, o_ref, *, shift):
    del buf_ref
    a = a_ref[...]
    if shift:
        a = jnp.concatenate([a[:, shift:], b_ref[:, :shift]], axis=1)
    o_ref[...] = a.astype(o_ref.dtype)


def _move_columns(src, buf, *, src_col, dst_col, width, tw, tr=1024):
    K = src.shape[0]
    src_col, dst_col, width = int(src_col), int(dst_col), int(width)
    assert dst_col % tw == 0 and width % tw == 0 and K % tr == 0
    blk0, shift = divmod(src_col, tw)
    d0 = dst_col // tw
    return pl.pallas_call(
        functools.partial(_move_columns_kernel, shift=shift),
        grid=(K // tr, width // tw),
        in_specs=[pl.BlockSpec((tr, tw), lambda r, j: (r, blk0 + j)),
                  pl.BlockSpec((tr, tw), lambda r, j: (r, blk0 + j + (1 if shift else 0))),
                  pl.BlockSpec(memory_space=pl.ANY)],
        out_specs=pl.BlockSpec((tr, tw), lambda r, j: (r, d0 + j)),
        out_shape=jax.ShapeDtypeStruct(buf.shape, buf.dtype),
        input_output_aliases={2: 0},
        compiler_params=_params(("parallel", "parallel")),
        name="move_columns",
    )(src, src, buf)


def mixer_layer(x2, B, S, attn_norm, w_in, nsa_q_gain, nsa_k_gain, nsa_cmp_pos, nsa_cmp_w1, nsa_cmp_w2, nsa_out_gain,
                gdn_conv_w, gdn_A_log, gdn_dt_bias, gdn_out_gain, w_out, tm=1024):
    h = rmsnorm_bf16(x2, attn_norm)
    z = matmul_cols(h, _reorder_w_in(w_in), tm=tm, tn=1024, name="in_proj")
    raw, ks, vs, kw, vw = nsa_prep(z, nsa_k_gain, B, S)
    comp = nsa_compress(raw, nsa_cmp_pos, nsa_cmp_w1, nsa_cmp_w2, nsa_k_gain[0], B, S)
    y_a = nsa_attention(z, comp, ks, vs, kw, vw, nsa_q_gain, nsa_out_gain, B, S)
    u, wq, ak, egl = gdn_chunks(z, gdn_conv_w, gdn_A_log, gdn_dt_bias, B, S)
    y_b = gdn_scan(u, wq, ak, egl, z, gdn_out_gain, B, S)
    return matmul2_residual(y_a, y_b, w_out, x2, tm=tm, tn=512, name="out_proj")


def xa_layer(x2, mem, B, S, xa_norm, mem_norm, xa_wq, xa_wkv, xa_q_gain, xa_k_gain, xa_wo, ffn_norm, router_w=None):
    k, v = xa_kv(mem, mem_norm, xa_wkv.astype(BF16), xa_k_gain)
    return cross_attention(x2, xa_norm, xa_wq.astype(BF16), k, v, xa_q_gain, xa_wo.astype(BF16), ffn_norm, router_w,
                           B, S)


def dense_ffn_layer(x2, h, w13, w2, tm=1024):
    act = swiglu_up(h, w13, tm=tm, tn=256)
    return down_proj_residual(act, w2, x2, tm=min(2 * tm, x2.shape[0]), tn=1024, tk=512, tk_tail=256)


def moe_ffn_layer(x2, h, idx, wts, w13, w2, tm=512, tn=512):
    src_tok, row_w, pos, tile_expert, n_valid = moe_dispatch(idx[:, :TOP_K], wts[:, :TOP_K], tm)
    xs = jnp.take(h, src_tok, axis=0)
    act = moe_up(xs, w13, tile_expert, n_valid, tm=tm, tn=tn)
    row_w_rep = jnp.broadcast_to(row_w[:, None], (row_w.shape[0], 128))
    out = moe_down(act, w2, row_w_rep, tile_expert, n_valid, tm=tm, tn=tn)
    return x2 + (jnp.take(out, pos[:, 0], axis=0) + jnp.take(out, pos[:, 1], axis=0))


def kernel(x, mem, attn_norm, w_in, nsa_q_gain, nsa_k_gain, nsa_cmp_pos, nsa_cmp_w1, nsa_cmp_w2, nsa_out_gain, gdn_conv_w, gdn_A_log, gdn_dt_bias, gdn_out_gain, w_out, xa_norm, mem_norm, xa_wq, xa_wkv, xa_q_gain, xa_k_gain, xa_wo, ffn_norm, dense_w13, dense_w2, router_w, moe_w13, moe_w2):
    B, S, D = x.shape
    x2 = x.reshape(B * S, D)
    for l in range(attn_norm.shape[0]):
        x2 = mixer_layer(x2, B, S, attn_norm[l], w_in[l], nsa_q_gain[l], nsa_k_gain[l], nsa_cmp_pos[l], nsa_cmp_w1[l],
                         nsa_cmp_w2[l], nsa_out_gain[l], gdn_conv_w[l], gdn_A_log[l], gdn_dt_bias[l], gdn_out_gain[l],
                         w_out[l])
        xa_args = (x2, mem, B, S, xa_norm[l], mem_norm[l], xa_wq[l], xa_wkv[l], xa_q_gain[l], xa_k_gain[l], xa_wo[l],
                   ffn_norm[l])
        if l % 2 == 0:
            x2, h = xa_layer(*xa_args)
            x2 = dense_ffn_layer(x2, h, dense_w13[l // 2], dense_w2[l // 2])
        else:
            x2, h, idx, wts = xa_layer(*xa_args, router_w[l // 2])
            x2 = moe_ffn_layer(x2, h, idx, wts, moe_w13[l // 2], moe_w2[l // 2])
    return x2.reshape(B, S, D)
```

```python
import functools

import jax
import jax.numpy as jnp
import numpy as np
from jax import lax
from jax.experimental import pallas as pl
from jax.experimental.pallas import tpu as pltpu

F32 = jnp.float32
BF16 = jnp.bfloat16
I32 = jnp.int32

D_MODEL = 4096
HEAD_DIM = 128
EPS = 1e-6
SCALE = HEAD_DIM ** -0.5

NSA_HEADS = 16
NSA_KV_GROUPS = 2
NSA_GROUP_SIZE = NSA_HEADS // NSA_KV_GROUPS
CMP_BLOCK = 32
CMP_STRIDE = 16
SEL_BLOCK = 64
SEL_TOP_N = 16
SEL_LOCAL = 2
WINDOW = 512
Q_BLOCK = 128
FORCE = 1e9
NEG = -1e30
SEL_TILE = 512
LOG2E = 1.4426950408889634

GDN_HEADS = 16
GDN_CONV = 4
GDN_CHUNK = 64
GDN_PACK = 4
GDN_SUB = 16

D_NSA = NSA_HEADS * HEAD_DIM
D_KV_NSA = NSA_KV_GROUPS * HEAD_DIM
D_GDN = GDN_HEADS * HEAD_DIM
D_MIX = D_NSA + D_GDN

XA_HEADS = 4
D_XA = XA_HEADS * HEAD_DIM
N_EXPERTS = 8
TOP_K = 2

COL_QKV_B = 0
COL_Q_A = COL_QKV_B + 3 * D_GDN
COL_Z_B = COL_Q_A + D_NSA
COL_KV_A = COL_Z_B + D_GDN
COL_SMALL = COL_KV_A + 6 * D_KV_NSA
SMALL_W = 512
N_IN_R = COL_SMALL + SMALL_W
SM_BETA = 32
SM_A = 48

VMEM_LIMIT = 58 * 1024 * 1024

NT_DIMS = (((1,), (1,)), ((), ()))


def _rms(x, gain):
    return x * lax.rsqrt(jnp.mean(x * x, axis=-1, keepdims=True) + EPS) * gain


def _dot(a, b):
    return jnp.dot(a, b, preferred_element_type=F32)


def _dot_nt(a, b):
    return lax.dot_general(a, b, NT_DIMS, preferred_element_type=F32)


def _dot3(x, sel):
    hi = x.astype(BF16)
    rem = x - hi.astype(F32)
    mid = rem.astype(BF16)
    lo = (rem - mid.astype(F32)).astype(BF16)
    return _dot(hi, sel) + _dot(mid, sel) + _dot(lo, sel)


def _params(sem, vmem=VMEM_LIMIT):
    return pltpu.CompilerParams(dimension_semantics=sem, vmem_limit_bytes=vmem)


def _norm_kernel(x_ref, g_ref, o_ref):
    o_ref[...] = _rms(x_ref[...], g_ref[...]).astype(o_ref.dtype)


def rmsnorm_bf16(x, gain, tm=512):
    M, D = x.shape
    return pl.pallas_call(
        _norm_kernel,
        grid=(M // tm,),
        in_specs=[pl.BlockSpec((tm, D), lambda i: (i, 0)), pl.BlockSpec((1, D), lambda i: (0, 0))],
        out_specs=pl.BlockSpec((tm, D), lambda i: (i, 0)),
        out_shape=jax.ShapeDtypeStruct((M, D), BF16),
        compiler_params=_params(("parallel",)),
        name="rmsnorm",
    )(x, gain.reshape(1, D))


def _route(h, rw_hi, rw_lo):
    h_hi = h.astype(BF16)
    h_lo = (h - h_hi.astype(F32)).astype(BF16)
    logits = _dot(h_hi, rw_hi) + _dot(h_hi, rw_lo) + _dot(h_lo, rw_hi)
    lane = lax.broadcasted_iota(I32, logits.shape, 1).astype(F32)
    logits = jnp.where(lane < N_EXPERTS, logits, -jnp.inf)
    m1 = jnp.max(logits, axis=-1, keepdims=True)
    i1 = jnp.min(jnp.where(logits == m1, lane, 128.0), axis=-1, keepdims=True)
    rest = jnp.where(lane == i1, -jnp.inf, logits)
    m2 = jnp.max(rest, axis=-1, keepdims=True)
    i2 = jnp.min(jnp.where(rest == m2, lane, 128.0), axis=-1, keepdims=True)
    e2 = jnp.exp(m2 - m1)
    den = 1.0 + e2
    idx = jnp.where(lane == 0, i1, jnp.where(lane == 1, i2, 0.0)).astype(I32)
    wts = jnp.where(lane == 0, 1.0 / den, jnp.where(lane == 1, e2 / den, 0.0))
    return idx, wts


def _mm_kernel(x_ref, w_ref, o_ref):
    o_ref[...] = _dot(x_ref[...], w_ref[...].astype(BF16)).astype(o_ref.dtype)


def matmul_cols(x, w, *, tm, tn, out_dtype=F32, name="matmul"):
    M, K = x.shape
    N = w.shape[1]
    return pl.pallas_call(
        _mm_kernel,
        grid=(M // tm, N // tn),
        in_specs=[pl.BlockSpec((tm, K), lambda i, j: (i, 0), pipeline_mode=pl.Buffered(1)),
                  pl.BlockSpec((K, tn), lambda i, j: (0, j))],
        out_specs=pl.BlockSpec((tm, tn), lambda i, j: (i, j)),
        out_shape=jax.ShapeDtypeStruct((M, N), out_dtype),
        compiler_params=_params(("parallel", "parallel")),
        name=name,
    )(x, w)


def _mm2_res_kernel(xa_ref, xb_ref, wa_ref, wb_ref, r_ref, o_ref):
    acc = _dot(xa_ref[...], wa_ref[...].astype(BF16)) + _dot(xb_ref[...], wb_ref[...].astype(BF16))
    o_ref[...] = r_ref[...] + acc


def matmul2_residual(xa, xb, w, layer, residual, *, tm, tn, name):
    M, Ka = xa.shape
    Kb = xb.shape[1]
    assert Ka == Kb and w.shape[1] == Ka + Kb
    N = w.shape[2]
    return pl.pallas_call(
        _mm2_res_kernel,
        grid=(M // tm, N // tn),
        in_specs=[pl.BlockSpec((tm, Ka), lambda i, j: (i, 0), pipeline_mode=pl.Buffered(1)),
                  pl.BlockSpec((tm, Kb), lambda i, j: (i, 0), pipeline_mode=pl.Buffered(1)),
                  pl.BlockSpec((None, Ka, tn), lambda i, j: (layer, 0, j)),
                  pl.BlockSpec((None, Kb, tn), lambda i, j: (layer, 1, j)),
                  pl.BlockSpec((tm, tn), lambda i, j: (i, j))],
        out_specs=pl.BlockSpec((tm, tn), lambda i, j: (i, j)),
        out_shape=jax.ShapeDtypeStruct((M, N), F32),
        compiler_params=_params(("parallel", "parallel")),
        name=name,
    )(xa, xb, w, w, residual)


def _swiglu_up_kernel(x_ref, wg_ref, wu_ref, o_ref):
    x = x_ref[...]
    gate = _dot(x, wg_ref[...].astype(BF16))
    up = _dot(x, wu_ref[...].astype(BF16))
    o_ref[...] = (gate * jax.nn.sigmoid(gate) * up).astype(o_ref.dtype)


def swiglu_up(x, w13, *, tm, tn):
    M, K = x.shape
    F = w13.shape[1] // 2
    nj = F // tn
    return pl.pallas_call(
        _swiglu_up_kernel,
        grid=(M // tm, nj),
        in_specs=[pl.BlockSpec((tm, K), lambda i, j: (i, 0), pipeline_mode=pl.Buffered(1)),
                  pl.BlockSpec((K, tn), lambda i, j: (0, j)),
                  pl.BlockSpec((K, tn), lambda i, j: (0, j + nj))],
        out_specs=pl.BlockSpec((tm, tn), lambda i, j: (i, j)),
        out_shape=jax.ShapeDtypeStruct((M, F), BF16),
        compiler_params=_params(("parallel", "parallel")),
        name="swiglu_up",
    )(x, w13, w13)


def _down_res_kernel(a_ref, w_ref, at_ref, wt_ref, r_ref, o_ref, *, n_main):
    k = pl.program_id(2)

    @pl.when(k == 0)
    def _():
        o_ref[...] = r_ref[...]

    @pl.when(k < n_main)
    def _():
        o_ref[...] += _dot(a_ref[...], w_ref[...].astype(BF16))

    @pl.when(k >= n_main)
    def _():
        o_ref[...] += _dot(at_ref[...], wt_ref[...].astype(BF16))


def down_proj_residual(a, w2, residual, *, tm, tn, tk, tk_tail):
    M, F = a.shape
    N = w2.shape[1]
    n_main = F // tk
    tail = F - n_main * tk
    assert tail % tk_tail == 0 and (n_main * tk) % tk_tail == 0
    n_tail = tail // tk_tail
    first_tail = (n_main * tk) // tk_tail if n_tail else 0
    main_k = lambda k: jnp.minimum(k, n_main - 1)
    tail_k = lambda k: first_tail + jnp.maximum(k - n_main, 0)
    return pl.pallas_call(
        functools.partial(_down_res_kernel, n_main=n_main),
        grid=(M // tm, N // tn, n_main + n_tail),
        in_specs=[pl.BlockSpec((tm, tk), lambda i, j, k: (i, main_k(k))),
                  pl.BlockSpec((tk, tn), lambda i, j, k: (main_k(k), j)),
                  pl.BlockSpec((tm, tk_tail), lambda i, j, k: (i, tail_k(k))),
                  pl.BlockSpec((tk_tail, tn), lambda i, j, k: (tail_k(k), j)),
                  pl.BlockSpec((tm, tn), lambda i, j, k: (i, j))],
        out_specs=pl.BlockSpec((tm, tn), lambda i, j, k: (i, j)),
        out_shape=jax.ShapeDtypeStruct((M, N), F32),
        compiler_params=_params(("parallel", "parallel", "arbitrary")),
        name="down_proj",
    )(a, w2, a, w2, residual)


def _new_weights(te_ref, i):
    return jnp.logical_or(i == 0, te_ref[i] != te_ref[jnp.maximum(i - 1, 0)])


def _moe_up_kernel(te_ref, nv_ref, x_ref, wg_ref, wu_ref, o_ref, wgb_ref, wub_ref):
    i = pl.program_id(1)
    valid = i < nv_ref[0]

    @pl.when(_new_weights(te_ref, i))
    def _():
        wgb_ref[...] = wg_ref[...].astype(BF16)
        wub_ref[...] = wu_ref[...].astype(BF16)

    @pl.when(valid)
    def _():
        x = x_ref[...]
        gate = _dot(x, wgb_ref[...])
        up = _dot(x, wub_ref[...])
        o_ref[...] = (gate * jax.nn.sigmoid(gate) * up).astype(o_ref.dtype)

    @pl.when(jnp.logical_not(valid))
    def _():
        o_ref[...] = jnp.zeros_like(o_ref)


def moe_up(xs, w13, tile_expert, n_valid, *, tm, tn):
    P, K = xs.shape
    F = w13.shape[2] // 2
    nj = F // tn
    return pl.pallas_call(
        _moe_up_kernel,
        grid_spec=pltpu.PrefetchScalarGridSpec(
            num_scalar_prefetch=2,
            grid=(nj, P // tm),
            in_specs=[pl.BlockSpec((tm, K), lambda j, i, te, nv: (jnp.minimum(i, nv[0] - 1), 0)),
                      pl.BlockSpec((None, K, tn), lambda j, i, te, nv: (te[i], 0, j)),
                      pl.BlockSpec((None, K, tn), lambda j, i, te, nv: (te[i], 0, j + nj))],
            out_specs=pl.BlockSpec((tm, tn), lambda j, i, te, nv: (i, j)),
            scratch_shapes=[pltpu.VMEM((K, tn), BF16), pltpu.VMEM((K, tn), BF16)],
        ),
        out_shape=jax.ShapeDtypeStruct((P, F), BF16),
        compiler_params=_params(("arbitrary", "arbitrary")),
        name="moe_up",
    )(tile_expert, n_valid, xs, w13, w13)


def _moe_down_kernel(te_ref, nv_ref, a_ref, w_ref, rw_ref, o_ref, wb_ref):
    i = pl.program_id(1)
    valid = i < nv_ref[0]

    @pl.when(_new_weights(te_ref, i))
    def _():
        wb_ref[...] = w_ref[...].astype(BF16)

    @pl.when(valid)
    def _():
        o_ref[...] = _dot(a_ref[...], wb_ref[...]) * rw_ref[:, 0:1]

    @pl.when(jnp.logical_not(valid))
    def _():
        o_ref[...] = jnp.zeros_like(o_ref)


def moe_down(act, w2, row_w, tile_expert, n_valid, *, tm, tn):
    P, F = act.shape
    N = w2.shape[2]
    return pl.pallas_call(
        _moe_down_kernel,
        grid_spec=pltpu.PrefetchScalarGridSpec(
            num_scalar_prefetch=2,
            grid=(N // tn, P // tm),
            in_specs=[pl.BlockSpec((tm, F), lambda j, i, te, nv: (jnp.minimum(i, nv[0] - 1), 0)),
                      pl.BlockSpec((None, F, tn), lambda j, i, te, nv: (te[i], 0, j)),
                      pl.BlockSpec((tm, 128), lambda j, i, te, nv: (i, 0))],
            out_specs=pl.BlockSpec((tm, tn), lambda j, i, te, nv: (i, j)),
            scratch_shapes=[pltpu.VMEM((F, tn), BF16)],
        ),
        out_shape=jax.ShapeDtypeStruct((P, N), F32),
        compiler_params=_params(("arbitrary", "arbitrary")),
        name="moe_down",
    )(tile_expert, n_valid, act, w2, row_w)


def moe_dispatch(top_i, top_w, tm):
    M = top_i.shape[0]
    A = M * TOP_K
    P = A + N_EXPERTS * tm
    e_flat = top_i.reshape(A)
    onehot = (e_flat[:, None] == jnp.arange(N_EXPERTS, dtype=I32)[None, :]).astype(I32)
    csum = jnp.cumsum(onehot, axis=0)
    counts = csum[-1]
    padded = ((counts + tm - 1) // tm) * tm
    pend = jnp.cumsum(padded)
    pstart = pend - padded
    dest = jnp.sum(onehot * (pstart[None, :] + csum - onehot), axis=1)
    src_tok = jnp.zeros((P,), I32).at[dest].set(jnp.arange(A, dtype=I32) // TOP_K)
    row_w = jnp.zeros((P,), F32).at[dest].set(top_w.reshape(A))
    pos = dest.reshape(M, TOP_K)
    tile_start = jnp.arange(P // tm, dtype=I32) * tm
    tile_expert = jnp.minimum(jnp.sum((tile_start[:, None] >= pend[None, :]).astype(I32), axis=1), N_EXPERTS - 1)
    n_valid = (pend[-1:] // tm).astype(I32)
    last_e = tile_expert[jnp.maximum(n_valid[0] - 1, 0)]
    tile_expert = jnp.where(tile_start // tm < n_valid[0], tile_expert, last_e)
    return src_tok, row_w, pos, tile_expert, n_valid


def _nsa_prep_kernel(kc_ref, vc_ref, ks_ref, vs_ref, kw_ref, vw_ref, kg_ref, raw_ref, kso_ref, vso_ref, kwo_ref, vwo_ref):
    raw_ref[0] = kc_ref[...]
    raw_ref[1] = vc_ref[...]
    kg = kg_ref[...]
    kso_ref[...] = _rms(ks_ref[...], kg[1:2]).astype(BF16)
    vso_ref[...] = vs_ref[...].astype(BF16)
    kwo_ref[...] = _rms(kw_ref[...], kg[2:3]).astype(BF16)
    vwo_ref[...] = vw_ref[...].astype(BF16)


def nsa_prep(z, k_gain, B, S, ts=512):
    G = NSA_KV_GROUPS
    nt = S // ts
    cb = COL_KV_A // HEAD_DIM

    def col(br, kvi):
        return pl.BlockSpec((ts, HEAD_DIM), lambda b, g, t: (b * nt + t, cb + br * 4 + kvi * 2 + g))

    kv_out = pl.BlockSpec((None, None, ts, HEAD_DIM), lambda b, g, t: (b, g, t, 0))
    kv_shape = jax.ShapeDtypeStruct((B, G, S, HEAD_DIM), BF16)
    return pl.pallas_call(
        _nsa_prep_kernel,
        grid=(B, G, nt),
        in_specs=[col(0, 0), col(0, 1), col(1, 0), col(1, 1), col(2, 0), col(2, 1),
                  pl.BlockSpec((3, HEAD_DIM), lambda b, g, t: (0, 0))],
        out_specs=[pl.BlockSpec((None, 2, None, ts, HEAD_DIM), lambda b, g, t: (b, 0, g, t, 0)),
                   kv_out, kv_out, kv_out, kv_out],
        out_shape=[jax.ShapeDtypeStruct((B, 2, G, S, HEAD_DIM), F32), kv_shape, kv_shape, kv_shape, kv_shape],
        compiler_params=_params(("parallel", "parallel", "parallel")),
        name="nsa_prep",
    )(z, z, z, z, z, z, k_gain)


def _nsa_compress_kernel(r_ref, pos_ref, w1_ref, w2_ref, kg_ref, o_ref):
    kv = pl.program_id(1)
    half = (CMP_BLOCK // 2) * HEAD_DIM
    r = r_ref[...]
    pos = pos_ref[...]
    n = r.shape[0]
    first = _dot((r + pos[:, :half]).astype(BF16), w1_ref[:half, :].astype(BF16))
    second = _dot((r + pos[:, half:]).astype(BF16), w1_ref[half:, :].astype(BF16))
    pre = first + pltpu.roll(second, n - 1, axis=0)
    hid = jax.nn.gelu(pre)
    comp = _dot(hid.astype(BF16), w2_ref[...].astype(BF16))
    o_ref[...] = jnp.where(kv == 0, _rms(comp, kg_ref[...]), comp).astype(o_ref.dtype)


def nsa_compress(raw, cmp_pos, cmp_w1, cmp_w2, k_gain0, B, S):
    G = NSA_KV_GROUPS
    n = S // CMP_STRIDE
    feat = CMP_STRIDE * HEAD_DIM
    r = raw.reshape(B, 2, G, n, feat)
    pos = cmp_pos.reshape(2, 1, CMP_BLOCK * HEAD_DIM)
    return pl.pallas_call(
        _nsa_compress_kernel,
        grid=(B, 2, G),
        in_specs=[pl.BlockSpec((None, None, None, n, feat), lambda b, kv, g: (b, kv, g, 0, 0)),
                  pl.BlockSpec((None, 1, CMP_BLOCK * HEAD_DIM), lambda b, kv, g: (kv, 0, 0)),
                  pl.BlockSpec((None, CMP_BLOCK * HEAD_DIM, HEAD_DIM), lambda b, kv, g: (kv, 0, 0)),
                  pl.BlockSpec((None, HEAD_DIM, HEAD_DIM), lambda b, kv, g: (kv, 0, 0)),
                  pl.BlockSpec((1, HEAD_DIM), lambda b, kv, g: (0, 0))],
        out_specs=pl.BlockSpec((None, None, None, n, HEAD_DIM), lambda b, kv, g: (b, kv, g, 0, 0)),
        out_shape=jax.ShapeDtypeStruct((B, 2, G, n, HEAD_DIM), BF16),
        compiler_params=_params(("parallel", "parallel", "parallel")),
        name="nsa_compress",
    )(r, pos, cmp_w1, cmp_w2, k_gain0.reshape(1, HEAD_DIM))


def _rep(x, n):
    return x if n == HEAD_DIM else jnp.concatenate([x] * (n // HEAD_DIM), axis=1)


def _flash_step(qb, kt, vt, bias, m_ref, l_ref, acc_ref):
    nk = kt.shape[0]
    rows = qb.shape[0]
    s = _dot_nt(qb, kt)
    s = (s.reshape(rows // Q_BLOCK, Q_BLOCK, nk) + bias[None]).reshape(rows, nk)
    m_prev = m_ref[...]
    m_new = jnp.maximum(m_prev, jnp.max(s, axis=-1, keepdims=True))
    alpha = jnp.exp2(m_prev - m_new)
    p = jnp.exp2(s - _rep(m_new, nk))
    l_ref[...] = alpha * l_ref[...] + jnp.sum(p, axis=-1, keepdims=True)
    acc_ref[...] = alpha * acc_ref[...] + _dot(p.astype(BF16), vt)
    m_ref[...] = m_new


def _nsa_kernel(q_ref, sm_ref, kc_ref, vc_ref, ks_ref, vs_ref, kw_ref, vw_ref, qg_ref, og_ref, c2s_ref, exp_ref,
                o_ref, selb_ref, m_ref, l_ref, acc_ref, *, k_top):
    Hg = NSA_GROUP_SIZE
    rows = Hg * Q_BLOCK
    c = pl.program_id(2)
    t0 = c * Q_BLOCK
    n_cmp = kc_ref.shape[0]
    n_sel = c2s_ref.shape[0]

    q = q_ref[...]
    qs = jnp.concatenate([q[:, h * HEAD_DIM:(h + 1) * HEAD_DIM] for h in range(Hg)], axis=0)
    qb = (_rms(qs, qg_ref[...]) * (SCALE * LOG2E)).astype(BF16)

    tq = t0 + lax.broadcasted_iota(I32, (Q_BLOCK, n_cmp), 0)
    c_end = lax.broadcasted_iota(I32, (Q_BLOCK, n_cmp), 1) * CMP_STRIDE + (CMP_BLOCK - 1)
    sc = _dot_nt(qb, kc_ref[...]).reshape(Hg, Q_BLOCK, n_cmp)
    sc = jnp.where((c_end <= tq)[None], sc, -jnp.inf)
    mc = jnp.max(sc, axis=-1, keepdims=True)
    mc = jnp.where(mc > -jnp.inf, mc, 0.0)
    pc = jnp.exp2(sc - mc)
    dc = jnp.sum(pc, axis=-1, keepdims=True)
    pc = pc / jnp.where(dc > 0, dc, 1.0)
    o_cmp = _dot(pc.reshape(rows, n_cmp).astype(BF16), vc_ref[...])

    psum = jnp.sum(pc, axis=0)
    c2s = c2s_ref[...]
    p_hi = psum.astype(BF16)
    rem = psum - p_hi.astype(F32)
    p_mid = rem.astype(BF16)
    p_lo = (rem - p_mid.astype(F32)).astype(BF16)
    imp = _dot_nt(c2s, p_hi) + _dot_nt(c2s, p_mid) + _dot_nt(c2s, p_lo)

    jj = lax.broadcasted_iota(I32, (n_sel, Q_BLOCK), 0)
    cur = lax.shift_right_logical(t0 + lax.broadcasted_iota(I32, (n_sel, Q_BLOCK), 1), 6)
    valid = jj <= cur
    forced = valid & ((jj == 0) | (jj > cur - SEL_LOCAL))
    score = jnp.where(forced, FORCE, jnp.where(valid, imp, -FORCE))
    rank = jnp.zeros((n_sel, Q_BLOCK), F32)
    for j2 in range(n_sel):
        other = score[j2:j2 + 1, :]
        tie = jnp.where(jj > j2, 1.0, 0.0)
        rank = rank + jnp.where(other > score, 1.0, jnp.where(other == score, tie, 0.0))
    sel_t = jnp.where(rank < k_top, 1.0, 0.0)
    if n_sel < Q_BLOCK:
        sel_t = jnp.concatenate([sel_t, jnp.zeros((Q_BLOCK - n_sel, Q_BLOCK), F32)], axis=0)
    sel = sel_t.T[:, :n_sel].astype(BF16)
    sel_keys = _dot(sel, exp_ref[...])
    for i in range(selb_ref.shape[0]):
        selb_ref[i] = (sel_keys[:, i * SEL_TILE:(i + 1) * SEL_TILE] - 1.0) * (-NEG)

    def reset():
        m_ref[...] = jnp.full(m_ref.shape, NEG, F32)
        l_ref[...] = jnp.zeros(l_ref.shape, F32)
        acc_ref[...] = jnp.zeros(acc_ref.shape, F32)

    reset()
    tq_s = t0 + lax.broadcasted_iota(I32, (Q_BLOCK, SEL_TILE), 0)
    kk_s = lax.broadcasted_iota(I32, (Q_BLOCK, SEL_TILE), 1)

    def sel_body(i, carry):
        k0 = pl.multiple_of(i * SEL_TILE, SEL_TILE)
        bias = selb_ref[i] + jnp.where(kk_s + k0 <= tq_s, 0.0, NEG)
        _flash_step(qb, ks_ref[pl.ds(k0, SEL_TILE), :], vs_ref[pl.ds(k0, SEL_TILE), :], bias, m_ref, l_ref, acc_ref)
        return carry

    lax.fori_loop(0, (t0 + Q_BLOCK + SEL_TILE - 1) // SEL_TILE, sel_body, 0)
    o_sel = acc_ref[...] / l_ref[...]

    wk = WINDOW + Q_BLOCK
    w0 = pl.multiple_of(jnp.maximum(t0 - WINDOW, 0), Q_BLOCK)
    tq_w = t0 + lax.broadcasted_iota(I32, (Q_BLOCK, wk), 0)
    wpos = w0 + lax.broadcasted_iota(I32, (Q_BLOCK, wk), 1)
    w_bias = jnp.where((wpos <= tq_w) & (wpos > tq_w - WINDOW), 0.0, NEG)
    sw = _dot_nt(qb, kw_ref[pl.ds(w0, wk), :]).reshape(Hg, Q_BLOCK, wk) + w_bias[None]
    pw = jnp.exp2(sw - jnp.max(sw, axis=-1, keepdims=True))
    lw = jnp.sum(pw, axis=-1, keepdims=True).reshape(rows, 1)
    o_win = _dot(pw.reshape(rows, wk).astype(BF16), vw_ref[pl.ds(w0, wk), :]) / lw

    gates = jax.nn.sigmoid(sm_ref[...])
    og = og_ref[...]
    for h in range(Hg):
        r = slice(h * Q_BLOCK, (h + 1) * Q_BLOCK)
        o = (gates[:, h:h + 1] * o_cmp[r] + gates[:, Hg + h:Hg + h + 1] * o_sel[r]
             + gates[:, 2 * Hg + h:2 * Hg + h + 1] * o_win[r])
        o_ref[:, h * HEAD_DIM:(h + 1) * HEAD_DIM] = _rms(o, og).astype(o_ref.dtype)


def nsa_attention(z, comp, ks, vs, kw, vw, q_gain, out_gain, B, S):
    G, Hg = NSA_KV_GROUPS, NSA_GROUP_SIZE
    nq = S // Q_BLOCK
    n_cmp = S // CMP_STRIDE
    n_sel = S // SEL_BLOCK
    k_top = min(SEL_TOP_N, n_sel)
    c_start = np.arange(n_cmp)[:, None] * CMP_STRIDE
    s_start = np.arange(n_sel)[None, :] * SEL_BLOCK
    overlap = np.minimum(c_start + CMP_BLOCK, s_start + SEL_BLOCK) - np.maximum(c_start, s_start)
    assert n_sel <= Q_BLOCK
    c2s = jnp.asarray((np.clip(overlap, 0, None) / CMP_STRIDE).T, dtype=BF16)
    expand = jnp.asarray(np.arange(S)[None, :] // SEL_BLOCK == np.arange(n_sel)[:, None], dtype=BF16)
    rows = Hg * Q_BLOCK
    wq = Hg * HEAD_DIM
    kv_spec = pl.BlockSpec((None, None, S, HEAD_DIM), lambda b, g, c: (b, g, 0, 0))
    const = lambda b, g, c: (0, 0)
    return pl.pallas_call(
        functools.partial(_nsa_kernel, k_top=k_top),
        grid=(B, G, nq),
        in_specs=[pl.BlockSpec((Q_BLOCK, wq), lambda b, g, c: (b * nq + c, COL_Q_A // wq + g)),
                  pl.BlockSpec((Q_BLOCK, HEAD_DIM), lambda b, g, c: (b * nq + c, COL_SMALL // HEAD_DIM + g)),
                  pl.BlockSpec((None, None, None, n_cmp, HEAD_DIM), lambda b, g, c: (b, 0, g, 0, 0)),
                  pl.BlockSpec((None, None, None, n_cmp, HEAD_DIM), lambda b, g, c: (b, 1, g, 0, 0)),
                  kv_spec, kv_spec, kv_spec, kv_spec,
                  pl.BlockSpec((1, HEAD_DIM), const), pl.BlockSpec((1, HEAD_DIM), const),
                  pl.BlockSpec((n_sel, n_cmp), const), pl.BlockSpec((n_sel, S), const)],
        out_specs=pl.BlockSpec((Q_BLOCK, wq), lambda b, g, c: (b * nq + c, g)),
        out_shape=jax.ShapeDtypeStruct((B * S, D_NSA), BF16),
        scratch_shapes=[pltpu.VMEM((S // SEL_TILE, Q_BLOCK, SEL_TILE), F32),
                        pltpu.VMEM((rows, HEAD_DIM), F32), pltpu.VMEM((rows, HEAD_DIM), F32),
                        pltpu.VMEM((rows, HEAD_DIM), F32)],
        compiler_params=_params(("parallel", "parallel", "arbitrary")),
        name="nsa_attention",
    )(z, z, comp, comp, ks, vs, kw, vw, q_gain.reshape(1, HEAD_DIM), out_gain.reshape(1, HEAD_DIM), c2s, expand)


def _gdn_chunk_kernel(xc_ref, xh_ref, sm_ref, cw_ref, alog_ref, dtb_ref, rep_ref, unfold_ref, u_ref, wq_ref, ak_ref,
                      egl_ref):
    for b in range(xc_ref.shape[0]):
        _gdn_chunk_body(xc_ref.at[b], xh_ref.at[b], sm_ref.at[b], cw_ref, alog_ref, dtb_ref, rep_ref, unfold_ref,
                        u_ref.at[b], wq_ref.at[b], ak_ref.at[b], egl_ref.at[b])


def _gdn_chunk_body(xc_ref, xh_ref, sm_ref, cw_ref, alog_ref, dtb_ref, rep_ref, unfold_ref, u_ref, wq_ref, ak_ref,
                    egl_ref):
    C = GDN_CHUNK
    PK = GDN_PACK
    R = PK * C
    n = pl.program_id(0)
    xp = jnp.concatenate([jnp.where(n > 0, xh_ref[...], 0.0), xc_ref[...]], axis=0)
    cw = cw_ref[...]
    y = None
    for j in range(GDN_CONV):
        shift = GDN_CONV - 1 - j
        xs = xp if shift == 0 else pltpu.roll(xp, shift, axis=0)
        term = xs[8:] * cw[j:j + 1, :]
        y = term if y is None else y + term
    y = y * jax.nn.sigmoid(y)

    sm = sm_ref[...]
    beta = jax.nn.sigmoid(sm)
    g = -jnp.exp(alog_ref[...]) * jax.nn.softplus(sm + dtb_ref[...])
    row = lax.broadcasted_iota(I32, g.shape, 0)
    gc = g
    d = 1
    while d < C:
        gc = gc + jnp.where(row >= d, pltpu.roll(gc, d, axis=0), 0.0)
        d *= 2
    g_last = gc[C - 1:C, :]

    ri = lax.broadcasted_iota(I32, (R, R), 0)
    ci = lax.broadcasted_iota(I32, (R, R), 1)
    same = lax.shift_right_logical(ri, 6) == lax.shift_right_logical(ci, 6)
    tri = same & (ri >= ci)
    strict = same & (ri > ci)
    SUB = GDN_SUB
    nb = R // SUB
    same16 = lax.shift_right_logical(ri, 4) == lax.shift_right_logical(ci, 4)
    same32 = lax.shift_right_logical(ri, 5) == lax.shift_right_logical(ci, 5)
    off32 = same32 & jnp.logical_not(same16)
    off64 = jnp.logical_not(same32)
    NG = GDN_HEADS // PK
    W4 = NG * SUB
    row_c = lax.broadcasted_iota(I32, (R, W4), 0) & (SUB - 1)
    col_s = lax.broadcasted_iota(I32, (nb, W4), 1) & (SUB - 1)

    groups = []
    for grp in range(NG):
        heads = [grp * PK + i for i in range(PK)]

        def stack(off):
            return jnp.concatenate([y[:, off + h * HEAD_DIM: off + (h + 1) * HEAD_DIM] for h in heads], axis=0)

        def col(x, base):
            return jnp.concatenate([x[:, base + h: base + h + 1] for h in heads], axis=0)

        q4 = stack(0)
        k4 = stack(D_GDN)
        v4 = stack(2 * D_GDN)
        q4 = q4 * lax.rsqrt(jnp.sum(q4 * q4, axis=-1, keepdims=True) + EPS) * SCALE
        k4 = k4 * lax.rsqrt(jnp.sum(k4 * k4, axis=-1, keepdims=True) + EPS)
        beta4 = col(beta, SM_BETA)
        gc4 = col(gc, SM_A)
        gl4 = jnp.concatenate([jnp.broadcast_to(g_last[:, SM_A + h: SM_A + h + 1], (C, 1)) for h in heads], axis=0)

        gb = jnp.broadcast_to(gc4, (R, R))
        decay = jnp.exp(jnp.where(tri, gb - gb.T, -jnp.inf))
        kb4 = k4 * beta4
        k4b = k4.astype(BF16)
        a = jnp.where(strict, _dot_nt(kb4.astype(BF16), k4b) * decay, 0.0)
        attn = _dot_nt(q4.astype(BF16), k4b) * decay
        at = a.T
        dct = jnp.concatenate([at[SUB * b:SUB * (b + 1), SUB * b:SUB * (b + 1)] for b in range(nb)], axis=0)
        groups.append((heads, q4, k4, v4, beta4, gc4, gl4, kb4, a, attn, dct))

    coef = _dot3(jnp.concatenate([g[-1] for g in groups], axis=1), rep_ref[...])
    t4 = jnp.where(row_c == (lax.broadcasted_iota(I32, (R, W4), 1) & (SUB - 1)), 1.0, 0.0)
    for i in range(1, SUB):
        s = jnp.sum((coef[:, i * HEAD_DIM: i * HEAD_DIM + W4] * t4).reshape(nb, SUB, W4), axis=1)
        new = jnp.where(col_s == i, 1.0, 0.0) - s
        t4 = jnp.where(row_c == i, jnp.broadcast_to(new[:, None, :], (nb, SUB, W4)).reshape(R, W4), t4)
    t_tiled = _dot3(t4, unfold_ref[...])

    for gi, (heads, q4, k4, v4, beta4, gc4, gl4, kb4, a, attn, _) in enumerate(groups):
        t16 = jnp.where(same16, t_tiled[:, gi * R:(gi + 1) * R], 0.0)
        t16b = t16.astype(BF16)
        a32 = jnp.where(off32, a, 0.0).astype(BF16)
        t32 = t16 - _dot(_dot(t16b, a32).astype(BF16), t16b)
        t32b = t32.astype(BF16)
        a64 = jnp.where(off64, a, 0.0).astype(BF16)
        t64 = t32 - _dot(_dot(t32b, a64).astype(BF16), t32b)
        eg = jnp.exp(gc4)
        rhs = jnp.concatenate([v4 * beta4, kb4 * eg], axis=1)
        rhs = _dot(t64.astype(BF16), rhs.astype(BF16))
        qd4 = q4 * eg
        kdt = (k4 * jnp.exp(gl4 - gc4)).T
        for i, h in enumerate(heads):
            r = slice(i * C, (i + 1) * C)
            u_ref[h] = rhs[r, :HEAD_DIM]
            wq_ref[h] = jnp.concatenate([rhs[r, HEAD_DIM:], qd4[r]], axis=0).astype(BF16)
            ak_ref[h] = jnp.concatenate([attn[r, r], kdt[:, r]], axis=0).astype(BF16)
            egl_ref[h] = jnp.broadcast_to(jnp.exp(g_last[:, SM_A + h: SM_A + h + 1]), (8, HEAD_DIM))


def gdn_chunks(z, conv_w, a_log, dt_bias, B, S):
    C, H = GDN_CHUNK, GDN_HEADS
    N = S // C
    W = 3 * D_GDN
    pad = lambda v: jnp.zeros((1, HEAD_DIM), F32).at[0, SM_A:SM_A + H].set(v)
    out5 = lambda r, cdim: pl.BlockSpec((B, None, H, r, cdim), lambda n: (0, n, 0, 0, 0))
    NG, SUB, R = H // GDN_PACK, GDN_SUB, GDN_PACK * C
    W4 = NG * SUB
    src = np.arange(W4)[:, None]
    dst = np.arange(SUB * HEAD_DIM)[None, :]
    lane = dst % HEAD_DIM
    rep = (lane < W4) & (src // SUB == lane // SUB) & (src % SUB == dst // HEAD_DIM)
    dst = np.arange(NG * R)[None, :]
    unfold = (src // SUB == dst // R) & (src % SUB == dst % SUB)
    z3 = z.reshape(B, S, z.shape[1])
    const = lambda n: (0, 0)
    return pl.pallas_call(
        _gdn_chunk_kernel,
        grid=(N,),
        in_specs=[pl.BlockSpec((B, C, W), lambda n: (0, n, 0)),
                  pl.BlockSpec((B, 8, W), lambda n: (0, jnp.maximum(n * (C // 8) - 1, 0), 0)),
                  pl.BlockSpec((B, C, HEAD_DIM), lambda n: (0, n, COL_SMALL // HEAD_DIM)),
                  pl.BlockSpec((GDN_CONV, W), const),
                  pl.BlockSpec((1, HEAD_DIM), const),
                  pl.BlockSpec((1, HEAD_DIM), const),
                  pl.BlockSpec((W4, SUB * HEAD_DIM), const),
                  pl.BlockSpec((W4, NG * R), const)],
        out_specs=[out5(C, HEAD_DIM), out5(2 * C, HEAD_DIM), out5(C + HEAD_DIM, C), out5(8, HEAD_DIM)],
        out_shape=[jax.ShapeDtypeStruct((B, N, H, C, HEAD_DIM), F32),
                   jax.ShapeDtypeStruct((B, N, H, 2 * C, HEAD_DIM), BF16),
                   jax.ShapeDtypeStruct((B, N, H, C + HEAD_DIM, C), BF16),
                   jax.ShapeDtypeStruct((B, N, H, 8, HEAD_DIM), F32)],
        compiler_params=_params(("parallel",)),
        name="gdn_chunks",
    )(z3, z3, z3, conv_w, pad(a_log), pad(dt_bias), jnp.asarray(rep, dtype=BF16), jnp.asarray(unfold, dtype=BF16))


def _gdn_scan_kernel(u_ref, wq_ref, ak_ref, egl_ref, z_ref, og_ref, o_ref, s_ref):
    C = GDN_CHUNK

    @pl.when(pl.program_id(0) == 0)
    def _():
        s_ref[...] = jnp.zeros(s_ref.shape, F32)

    og = og_ref[...]
    for b in range(u_ref.shape[0]):
        for h in range(GDN_HEADS):
            state = s_ref[b, h]
            ws = _dot(wq_ref[b, h], state.astype(BF16))
            v_new = u_ref[b, h] - ws[:C]
            av = _dot(ak_ref[b, h], v_new.astype(BF16))
            o = ws[C:] + av[:C]
            decayed = (state.reshape(HEAD_DIM // 8, 8, HEAD_DIM) * egl_ref[b, h][None]).reshape(HEAD_DIM, HEAD_DIM)
            s_ref[b, h] = decayed + av[C:]
            zh = z_ref[b, :, h * HEAD_DIM:(h + 1) * HEAD_DIM]
            o_ref[b, :, h * HEAD_DIM:(h + 1) * HEAD_DIM] = (_rms(o, og) * (zh * jax.nn.sigmoid(zh))).astype(o_ref.dtype)


def gdn_scan(u, wq, ak, egl, z, out_gain, B, S):
    C, H = GDN_CHUNK, GDN_HEADS
    N = S // C
    in5 = lambda r, cdim: pl.BlockSpec((B, None, H, r, cdim), lambda n: (0, n, 0, 0, 0))
    y = pl.pallas_call(
        _gdn_scan_kernel,
        grid=(N,),
        in_specs=[in5(C, HEAD_DIM), in5(2 * C, HEAD_DIM), in5(C + HEAD_DIM, C), in5(8, HEAD_DIM),
                  pl.BlockSpec((B, C, D_GDN), lambda n: (0, n, COL_Z_B // D_GDN)),
                  pl.BlockSpec((1, HEAD_DIM), lambda n: (0, 0))],
        out_specs=pl.BlockSpec((B, C, D_GDN), lambda n: (0, n, 0)),
        out_shape=jax.ShapeDtypeStruct((B, S, D_GDN), BF16),
        scratch_shapes=[pltpu.VMEM((B, H, HEAD_DIM, HEAD_DIM), F32)],
        compiler_params=_params(("arbitrary",)),
        name="gdn_scan",
    )(u, wq, ak, egl, z.reshape(B, S, z.shape[1]), out_gain.reshape(1, HEAD_DIM))
    return y.reshape(B * S, D_GDN)


def _xa_kv_kernel(mem_ref, g_ref, w_ref, kg_ref, k_ref, v_ref):
    kv = _dot(_rms(mem_ref[...], g_ref[...]).astype(BF16), w_ref[...])
    for h in range(XA_HEADS):
        k_ref[h] = _rms(kv[:, h * HEAD_DIM:(h + 1) * HEAD_DIM], kg_ref[...]).astype(BF16)
        v_ref[h] = kv[:, D_XA + h * HEAD_DIM: D_XA + (h + 1) * HEAD_DIM].astype(BF16)


def xa_kv(mem, mem_norm, wkv_bf16, k_gain):
    B, Mm, D = mem.shape
    spec = pl.BlockSpec((None, XA_HEADS, Mm, HEAD_DIM), lambda b: (b, 0, 0, 0))
    shape = jax.ShapeDtypeStruct((B, XA_HEADS, Mm, HEAD_DIM), BF16)
    return pl.pallas_call(
        _xa_kv_kernel,
        grid=(B,),
        in_specs=[pl.BlockSpec((None, Mm, D), lambda b: (b, 0, 0)), pl.BlockSpec((1, D), lambda b: (0, 0)),
                  pl.BlockSpec((D, 2 * D_XA), lambda b: (0, 0)), pl.BlockSpec((1, HEAD_DIM), lambda b: (0, 0))],
        out_specs=[spec, spec],
        out_shape=[shape, shape],
        compiler_params=_params(("parallel",)),
        name="xa_kv",
    )(mem, mem_norm.reshape(1, D), wkv_bf16, k_gain.reshape(1, HEAD_DIM))


def _xa_kernel(x_ref, g_ref, wq_ref, k_ref, v_ref, qg_ref, wo_ref, fg_ref, *rest, route):
    if route:
        rwh_ref, rwl_ref, o_ref, h_ref, idx_ref, wt_ref = rest
    else:
        o_ref, h_ref = rest
    x = x_ref[...]
    q = _dot(_rms(x, g_ref[...]).astype(BF16), wq_ref[...])
    outs = []
    for h in range(XA_HEADS):
        qh = _rms(q[:, h * HEAD_DIM:(h + 1) * HEAD_DIM], qg_ref[...]).astype(BF16)
        s = _dot_nt(qh, k_ref[h]) * SCALE
        p = jnp.exp(s - jnp.max(s, axis=-1, keepdims=True))
        p = p / jnp.sum(p, axis=-1, keepdims=True)
        outs.append(_dot(p.astype(BF16), v_ref[h]))
    o = jnp.concatenate(outs, axis=1).astype(BF16)
    x_new = x + _dot(o, wo_ref[...])
    o_ref[...] = x_new
    hn = _rms(x_new, fg_ref[...])
    h_ref[...] = hn.astype(h_ref.dtype)
    if route:
        idx_ref[...], wt_ref[...] = _route(hn, rwh_ref[...], rwl_ref[...])


def cross_attention(x, xa_norm, wq_bf16, k, v, q_gain, wo_bf16, ffn_norm, router_w, B, S, tm=256):
    M, D = x.shape
    Mm = k.shape[2]
    per_b = S // tm
    kv_spec = pl.BlockSpec((None, XA_HEADS, Mm, HEAD_DIM), lambda i: (i // per_b, 0, 0, 0))
    const = lambda i: (0, 0)
    row = lambda i: (i, 0)
    route = router_w is not None
    in_specs = [pl.BlockSpec((tm, D), row), pl.BlockSpec((1, D), const),
                pl.BlockSpec((D, D_XA), const), kv_spec, kv_spec,
                pl.BlockSpec((1, HEAD_DIM), const), pl.BlockSpec((D_XA, D), const), pl.BlockSpec((1, D), const)]
    args = [x, xa_norm.reshape(1, D), wq_bf16, k, v, q_gain.reshape(1, HEAD_DIM), wo_bf16, ffn_norm.reshape(1, D)]
    out_specs = [pl.BlockSpec((tm, D), row), pl.BlockSpec((tm, D), row)]
    out_shape = [jax.ShapeDtypeStruct((M, D), F32), jax.ShapeDtypeStruct((M, D), BF16)]
    if route:
        rw = jnp.pad(router_w, ((0, 0), (0, 128 - N_EXPERTS)))
        rw_hi = rw.astype(BF16)
        in_specs += [pl.BlockSpec((D, 128), const), pl.BlockSpec((D, 128), const)]
        args += [rw_hi, (rw - rw_hi.astype(F32)).astype(BF16)]
        out_specs += [pl.BlockSpec((tm, 128), row), pl.BlockSpec((tm, 128), row)]
        out_shape += [jax.ShapeDtypeStruct((M, 128), I32), jax.ShapeDtypeStruct((M, 128), F32)]
    return pl.pallas_call(
        functools.partial(_xa_kernel, route=route),
        grid=(M // tm,),
        in_specs=in_specs,
        out_specs=out_specs,
        out_shape=out_shape,
        compiler_params=_params(("parallel",)),
        name="cross_attention",
    )(*args)


def _reorder_w_in(w_in):
    o = np.cumsum((D_NSA, 6 * D_KV_NSA, 3 * NSA_HEADS, 3 * D_GDN, GDN_HEADS, GDN_HEADS, D_GDN))
    q_a, kv_a, gate_a, qkv_b = (0, o[0]), (o[0], o[1]), o[1], (o[2], o[3])
    beta_b, a_b, z_b = o[3], o[4], (o[5], o[6])
    Hg = NSA_GROUP_SIZE
    K = w_in.shape[0]
    gates = w_in[:, gate_a:gate_a + 3 * NSA_HEADS].reshape(K, NSA_KV_GROUPS, Hg, 3).transpose(0, 1, 3, 2)
    gates = gates.reshape(K, NSA_KV_GROUPS, 3 * Hg)
    zeros = lambda n: jnp.zeros((K, n), w_in.dtype)
    assert SM_BETA >= 3 * Hg and SM_A == SM_BETA + GDN_HEADS
    small = jnp.concatenate(
        [gates[:, 0], zeros(SM_BETA - 3 * Hg), w_in[:, beta_b:beta_b + 2 * GDN_HEADS], zeros(HEAD_DIM - SM_A - GDN_HEADS),
         gates[:, 1], zeros(SMALL_W - HEAD_DIM - 3 * Hg)], axis=1)
    parts = [w_in[:, qkv_b[0]:qkv_b[1]], w_in[:, q_a[0]:q_a[1]], w_in[:, z_b[0]:z_b[1]], w_in[:, kv_a[0]:kv_a[1]], small]
    return jnp.concatenate([p.astype(BF16) for p in parts], axis=1)


def mixer_layer(x2, B, S, attn_norm, w_in, nsa_q_gain, nsa_k_gain, nsa_cmp_pos, nsa_cmp_w1, nsa_cmp_w2, nsa_out_gain,
                gdn_conv_w, gdn_A_log, gdn_dt_bias, gdn_out_gain, w_out_stack, layer, tm=1024):
    h = rmsnorm_bf16(x2, attn_norm)
    z = matmul_cols(h, _reorder_w_in(w_in), tm=min(2 * tm, x2.shape[0]), tn=512, name="in_proj")
    raw, ks, vs, kw, vw = nsa_prep(z, nsa_k_gain, B, S)
    comp = nsa_compress(raw, nsa_cmp_pos, nsa_cmp_w1, nsa_cmp_w2, nsa_k_gain[0], B, S)
    y_a = nsa_attention(z, comp, ks, vs, kw, vw, nsa_q_gain, nsa_out_gain, B, S)
    u, wq, ak, egl = gdn_chunks(z, gdn_conv_w, gdn_A_log, gdn_dt_bias, B, S)
    y_b = gdn_scan(u, wq, ak, egl, z, gdn_out_gain, B, S)
    return matmul2_residual(y_a, y_b, w_out_stack, layer, x2, tm=min(2 * tm, x2.shape[0]), tn=512, name="out_proj")


def xa_layer(x2, mem, B, S, xa_norm, mem_norm, xa_wq, xa_wkv, xa_q_gain, xa_k_gain, xa_wo, ffn_norm, router_w=None):
    k, v = xa_kv(mem, mem_norm, xa_wkv.astype(BF16), xa_k_gain)
    return cross_attention(x2, xa_norm, xa_wq.astype(BF16), k, v, xa_q_gain, xa_wo.astype(BF16), ffn_norm, router_w,
                           B, S)


def dense_ffn_layer(x2, h, w13, w2, tm=1024):
    act = swiglu_up(h, w13, tm=min(2 * tm, x2.shape[0]), tn=256)
    return down_proj_residual(act, w2, x2, tm=min(2 * tm, x2.shape[0]), tn=1024, tk=512, tk_tail=256)


def moe_ffn_layer(x2, h, idx, wts, w13, w2, tm=512, tn=512):
    src_tok, row_w, pos, tile_expert, n_valid = moe_dispatch(idx[:, :TOP_K], wts[:, :TOP_K], tm)
    xs = jnp.take(h, src_tok, axis=0)
    act = moe_up(xs, w13, tile_expert, n_valid, tm=tm, tn=tn)
    row_w_rep = jnp.broadcast_to(row_w[:, None], (row_w.shape[0], 128))
    out = moe_down(act, w2, row_w_rep, tile_expert, n_valid, tm=tm, tn=tn)
    return x2 + (jnp.take(out, pos[:, 0], axis=0) + jnp.take(out, pos[:, 1], axis=0))


def kernel(x, mem, attn_norm, w_in, nsa_q_gain, nsa_k_gain, nsa_cmp_pos, nsa_cmp_w1, nsa_cmp_w2, nsa_out_gain, gdn_conv_w, gdn_A_log, gdn_dt_bias, gdn_out_gain, w_out, xa_norm, mem_norm, xa_wq, xa_wkv, xa_q_gain, xa_k_gain, xa_wo, ffn_norm, dense_w13, dense_w2, router_w, moe_w13, moe_w2):
    B, S, D = x.shape
    x2 = x.reshape(B * S, D)
    for l in range(attn_norm.shape[0]):
        x2 = mixer_layer(x2, B, S, attn_norm[l], w_in[l], nsa_q_gain[l], nsa_k_gain[l], nsa_cmp_pos[l], nsa_cmp_w1[l],
                         nsa_cmp_w2[l], nsa_out_gain[l], gdn_conv_w[l], gdn_A_log[l], gdn_dt_bias[l], gdn_out_gain[l],
                         w_out, l)
        xa_args = (x2, mem, B, S, xa_norm[l], mem_norm[l], xa_wq[l], xa_wkv[l], xa_q_gain[l], xa_k_gain[l], xa_wo[l],
                   ffn_norm[l])
        if l % 2 == 0:
            x2, h = xa_layer(*xa_args)
            x2 = dense_ffn_layer(x2, h, dense_w13[l // 2], dense_w2[l // 2])
        else:
            x2, h, idx, wts = xa_layer(*xa_args, router_w[l // 2])
            x2 = moe_ffn_layer(x2, h, idx, wts, moe_w13[l // 2], moe_w2[l // 2])
    return x2.reshape(B, S, D)
```

```python
import functools

import jax
import jax.numpy as jnp
import numpy as np
from jax import lax
from jax.experimental import pallas as pl
from jax.experimental.pallas import tpu as pltpu

F32 = jnp.float32
BF16 = jnp.bfloat16
I32 = jnp.int32

D_MODEL = 4096
HEAD_DIM = 128
EPS = 1e-6
SCALE = HEAD_DIM ** -0.5

NSA_HEADS = 16
NSA_KV_GROUPS = 2
NSA_GROUP_SIZE = NSA_HEADS // NSA_KV_GROUPS
CMP_BLOCK = 32
CMP_STRIDE = 16
SEL_BLOCK = 64
SEL_TOP_N = 16
SEL_LOCAL = 2
WINDOW = 512
Q_BLOCK = 128
FORCE = 1e9
NEG = -1e30
SEL_TILE = 512
LOG2E = 1.4426950408889634

GDN_HEADS = 16
GDN_CONV = 4
GDN_CHUNK = 64
GDN_PACK = 4
GDN_SUB = 16

D_NSA = NSA_HEADS * HEAD_DIM
D_KV_NSA = NSA_KV_GROUPS * HEAD_DIM
D_GDN = GDN_HEADS * HEAD_DIM
D_MIX = D_NSA + D_GDN

XA_HEADS = 4
D_XA = XA_HEADS * HEAD_DIM
N_EXPERTS = 8
TOP_K = 2

COL_QKV_B = 0
COL_Q_A = COL_QKV_B + 3 * D_GDN
COL_Z_B = COL_Q_A + D_NSA
COL_KV_A = COL_Z_B + D_GDN
COL_SMALL = COL_KV_A + 6 * D_KV_NSA
SMALL_W = 512
N_IN_R = COL_SMALL + SMALL_W
SM_BETA = 32
SM_A = 48

VMEM_LIMIT = 58 * 1024 * 1024

NT_DIMS = (((1,), (1,)), ((), ()))


def _rms(x, gain):
    return x * lax.rsqrt(jnp.mean(x * x, axis=-1, keepdims=True) + EPS) * gain


def _dot(a, b):
    return jnp.dot(a, b, preferred_element_type=F32)


def _dot_nt(a, b):
    return lax.dot_general(a, b, NT_DIMS, preferred_element_type=F32)


def _dot3(x, sel):
    hi = x.astype(BF16)
    rem = x - hi.astype(F32)
    mid = rem.astype(BF16)
    lo = (rem - mid.astype(F32)).astype(BF16)
    return _dot(hi, sel) + _dot(mid, sel) + _dot(lo, sel)


def _params(sem, vmem=VMEM_LIMIT):
    return pltpu.CompilerParams(dimension_semantics=sem, vmem_limit_bytes=vmem)


def _norm_kernel(x_ref, g_ref, o_ref):
    o_ref[...] = _rms(x_ref[...], g_ref[...]).astype(o_ref.dtype)


def rmsnorm_bf16(x, gain, tm=512):
    M, D = x.shape
    return pl.pallas_call(
        _norm_kernel,
        grid=(M // tm,),
        in_specs=[pl.BlockSpec((tm, D), lambda i: (i, 0)), pl.BlockSpec((1, D), lambda i: (0, 0))],
        out_specs=pl.BlockSpec((tm, D), lambda i: (i, 0)),
        out_shape=jax.ShapeDtypeStruct((M, D), BF16),
        compiler_params=_params(("parallel",)),
        name="rmsnorm",
    )(x, gain.reshape(1, D))


def _route(h, rw_hi, rw_lo):
    h_hi = h.astype(BF16)
    h_lo = (h - h_hi.astype(F32)).astype(BF16)
    logits = _dot(h_hi, rw_hi) + _dot(h_hi, rw_lo) + _dot(h_lo, rw_hi)
    lane = lax.broadcasted_iota(I32, logits.shape, 1).astype(F32)
    logits = jnp.where(lane < N_EXPERTS, logits, -jnp.inf)
    m1 = jnp.max(logits, axis=-1, keepdims=True)
    i1 = jnp.min(jnp.where(logits == m1, lane, 128.0), axis=-1, keepdims=True)
    rest = jnp.where(lane == i1, -jnp.inf, logits)
    m2 = jnp.max(rest, axis=-1, keepdims=True)
    i2 = jnp.min(jnp.where(rest == m2, lane, 128.0), axis=-1, keepdims=True)
    e2 = jnp.exp(m2 - m1)
    den = 1.0 + e2
    idx = jnp.where(lane == 0, i1, jnp.where(lane == 1, i2, 0.0)).astype(I32)
    wts = jnp.where(lane == 0, 1.0 / den, jnp.where(lane == 1, e2 / den, 0.0))
    return idx, wts


def _mm_kernel(x_ref, w_ref, o_ref):
    o_ref[...] = _dot(x_ref[...], w_ref[...].astype(BF16)).astype(o_ref.dtype)


def matmul_cols(x, w, *, tm, tn, out_dtype=F32, name="matmul"):
    M, K = x.shape
    N = w.shape[1]
    return pl.pallas_call(
        _mm_kernel,
        grid=(M // tm, N // tn),
        in_specs=[pl.BlockSpec((tm, K), lambda i, j: (i, 0)), pl.BlockSpec((K, tn), lambda i, j: (0, j))],
        out_specs=pl.BlockSpec((tm, tn), lambda i, j: (i, j)),
        out_shape=jax.ShapeDtypeStruct((M, N), out_dtype),
        compiler_params=_params(("parallel", "parallel")),
        name=name,
    )(x, w)


def _mm2_res_kernel(xa_ref, xb_ref, wa_ref, wb_ref, r_ref, o_ref):
    acc = _dot(xa_ref[...], wa_ref[...].astype(BF16)) + _dot(xb_ref[...], wb_ref[...].astype(BF16))
    o_ref[...] = r_ref[...] + acc


def matmul2_residual(xa, xb, w, layer, residual, *, tm, tn, name):
    M, Ka = xa.shape
    Kb = xb.shape[1]
    assert Ka == Kb and w.shape[1] == Ka + Kb
    N = w.shape[2]
    return pl.pallas_call(
        _mm2_res_kernel,
        grid=(M // tm, N // tn),
        in_specs=[pl.BlockSpec((tm, Ka), lambda i, j: (i, 0)), pl.BlockSpec((tm, Kb), lambda i, j: (i, 0)),
                  pl.BlockSpec((None, Ka, tn), lambda i, j: (layer, 0, j)),
                  pl.BlockSpec((None, Kb, tn), lambda i, j: (layer, 1, j)),
                  pl.BlockSpec((tm, tn), lambda i, j: (i, j))],
        out_specs=pl.BlockSpec((tm, tn), lambda i, j: (i, j)),
        out_shape=jax.ShapeDtypeStruct((M, N), F32),
        compiler_params=_params(("parallel", "parallel")),
        name=name,
    )(xa, xb, w, w, residual)


def _swiglu_up_kernel(x_ref, wg_ref, wu_ref, o_ref):
    x = x_ref[...]
    gate = _dot(x, wg_ref[...].astype(BF16))
    up = _dot(x, wu_ref[...].astype(BF16))
    o_ref[...] = (gate * jax.nn.sigmoid(gate) * up).astype(o_ref.dtype)


def swiglu_up(x, w13, *, tm, tn):
    M, K = x.shape
    F = w13.shape[1] // 2
    nj = F // tn
    return pl.pallas_call(
        _swiglu_up_kernel,
        grid=(M // tm, nj),
        in_specs=[pl.BlockSpec((tm, K), lambda i, j: (i, 0), pipeline_mode=pl.Buffered(1)),
                  pl.BlockSpec((K, tn), lambda i, j: (0, j)),
                  pl.BlockSpec((K, tn), lambda i, j: (0, j + nj))],
        out_specs=pl.BlockSpec((tm, tn), lambda i, j: (i, j)),
        out_shape=jax.ShapeDtypeStruct((M, F), BF16),
        compiler_params=_params(("parallel", "parallel")),
        name="swiglu_up",
    )(x, w13, w13)


def _down_res_kernel(a_ref, w_ref, at_ref, wt_ref, r_ref, o_ref, *, n_main):
    k = pl.program_id(2)

    @pl.when(k == 0)
    def _():
        o_ref[...] = r_ref[...]

    @pl.when(k < n_main)
    def _():
        o_ref[...] += _dot(a_ref[...], w_ref[...].astype(BF16))

    @pl.when(k >= n_main)
    def _():
        o_ref[...] += _dot(at_ref[...], wt_ref[...].astype(BF16))


def down_proj_residual(a, w2, residual, *, tm, tn, tk, tk_tail):
    M, F = a.shape
    N = w2.shape[1]
    n_main = F // tk
    tail = F - n_main * tk
    assert tail % tk_tail == 0 and (n_main * tk) % tk_tail == 0
    n_tail = tail // tk_tail
    first_tail = (n_main * tk) // tk_tail if n_tail else 0
    main_k = lambda k: jnp.minimum(k, n_main - 1)
    tail_k = lambda k: first_tail + jnp.maximum(k - n_main, 0)
    return pl.pallas_call(
        functools.partial(_down_res_kernel, n_main=n_main),
        grid=(M // tm, N // tn, n_main + n_tail),
        in_specs=[pl.BlockSpec((tm, tk), lambda i, j, k: (i, main_k(k))),
                  pl.BlockSpec((tk, tn), lambda i, j, k: (main_k(k), j)),
                  pl.BlockSpec((tm, tk_tail), lambda i, j, k: (i, tail_k(k))),
                  pl.BlockSpec((tk_tail, tn), lambda i, j, k: (tail_k(k), j)),
                  pl.BlockSpec((tm, tn), lambda i, j, k: (i, j))],
        out_specs=pl.BlockSpec((tm, tn), lambda i, j, k: (i, j)),
        out_shape=jax.ShapeDtypeStruct((M, N), F32),
        compiler_params=_params(("parallel", "parallel", "arbitrary")),
        name="down_proj",
    )(a, w2, a, w2, residual)


def _new_weights(te_ref, i):
    return jnp.logical_or(i == 0, te_ref[i] != te_ref[jnp.maximum(i - 1, 0)])


def _moe_up_kernel(te_ref, nv_ref, x_ref, wg_ref, wu_ref, o_ref, wgb_ref, wub_ref):
    i = pl.program_id(1)
    valid = i < nv_ref[0]

    @pl.when(_new_weights(te_ref, i))
    def _():
        wgb_ref[...] = wg_ref[...].astype(BF16)
        wub_ref[...] = wu_ref[...].astype(BF16)

    @pl.when(valid)
    def _():
        x = x_ref[...]
        gate = _dot(x, wgb_ref[...])
        up = _dot(x, wub_ref[...])
        o_ref[...] = (gate * jax.nn.sigmoid(gate) * up).astype(o_ref.dtype)

    @pl.when(jnp.logical_not(valid))
    def _():
        o_ref[...] = jnp.zeros_like(o_ref)


def moe_up(xs, w13, tile_expert, n_valid, *, tm, tn):
    P, K = xs.shape
    F = w13.shape[2] // 2
    nj = F // tn
    return pl.pallas_call(
        _moe_up_kernel,
        grid_spec=pltpu.PrefetchScalarGridSpec(
            num_scalar_prefetch=2,
            grid=(nj, P // tm),
            in_specs=[pl.BlockSpec((tm, K), lambda j, i, te, nv: (jnp.minimum(i, nv[0] - 1), 0)),
                      pl.BlockSpec((None, K, tn), lambda j, i, te, nv: (te[i], 0, j)),
                      pl.BlockSpec((None, K, tn), lambda j, i, te, nv: (te[i], 0, j + nj))],
            out_specs=pl.BlockSpec((tm, tn), lambda j, i, te, nv: (i, j)),
            scratch_shapes=[pltpu.VMEM((K, tn), BF16), pltpu.VMEM((K, tn), BF16)],
        ),
        out_shape=jax.ShapeDtypeStruct((P, F), BF16),
        compiler_params=_params(("arbitrary", "arbitrary")),
        name="moe_up",
    )(tile_expert, n_valid, xs, w13, w13)


def _moe_down_kernel(te_ref, nv_ref, a_ref, w_ref, rw_ref, o_ref, wb_ref):
    i = pl.program_id(1)
    valid = i < nv_ref[0]

    @pl.when(_new_weights(te_ref, i))
    def _():
        wb_ref[...] = w_ref[...].astype(BF16)

    @pl.when(valid)
    def _():
        o_ref[...] = _dot(a_ref[...], wb_ref[...]) * rw_ref[:, 0:1]

    @pl.when(jnp.logical_not(valid))
    def _():
        o_ref[...] = jnp.zeros_like(o_ref)


def moe_down(act, w2, row_w, tile_expert, n_valid, *, tm, tn):
    P, F = act.shape
    N = w2.shape[2]
    return pl.pallas_call(
        _moe_down_kernel,
        grid_spec=pltpu.PrefetchScalarGridSpec(
            num_scalar_prefetch=2,
            grid=(N // tn, P // tm),
            in_specs=[pl.BlockSpec((tm, F), lambda j, i, te, nv: (jnp.minimum(i, nv[0] - 1), 0)),
                      pl.BlockSpec((None, F, tn), lambda j, i, te, nv: (te[i], 0, j)),
                      pl.BlockSpec((tm, 128), lambda j, i, te, nv: (i, 0))],
            out_specs=pl.BlockSpec((tm, tn), lambda j, i, te, nv: (i, j)),
            scratch_shapes=[pltpu.VMEM((F, tn), BF16)],
        ),
        out_shape=jax.ShapeDtypeStruct((P, N), F32),
        compiler_params=_params(("arbitrary", "arbitrary")),
        name="moe_down",
    )(tile_expert, n_valid, act, w2, row_w)


def moe_dispatch(top_i, top_w, tm):
    M = top_i.shape[0]
    A = M * TOP_K
    P = A + N_EXPERTS * tm
    e_flat = top_i.reshape(A)
    onehot = (e_flat[:, None] == jnp.arange(N_EXPERTS, dtype=I32)[None, :]).astype(I32)
    csum = jnp.cumsum(onehot, axis=0)
    counts = csum[-1]
    padded = ((counts + tm - 1) // tm) * tm
    pend = jnp.cumsum(padded)
    pstart = pend - padded
    dest = jnp.sum(onehot * (pstart[None, :] + csum - onehot), axis=1)
    src_tok = jnp.zeros((P,), I32).at[dest].set(jnp.arange(A, dtype=I32) // TOP_K)
    row_w = jnp.zeros((P,), F32).at[dest].set(top_w.reshape(A))
    pos = dest.reshape(M, TOP_K)
    tile_start = jnp.arange(P // tm, dtype=I32) * tm
    tile_expert = jnp.minimum(jnp.sum((tile_start[:, None] >= pend[None, :]).astype(I32), axis=1), N_EXPERTS - 1)
    n_valid = (pend[-1:] // tm).astype(I32)
    last_e = tile_expert[jnp.maximum(n_valid[0] - 1, 0)]
    tile_expert = jnp.where(tile_start // tm < n_valid[0], tile_expert, last_e)
    return src_tok, row_w, pos, tile_expert, n_valid


def _nsa_prep_kernel(kc_ref, vc_ref, ks_ref, vs_ref, kw_ref, vw_ref, kg_ref, raw_ref, kso_ref, vso_ref, kwo_ref, vwo_ref):
    raw_ref[0] = kc_ref[...]
    raw_ref[1] = vc_ref[...]
    kg = kg_ref[...]
    kso_ref[...] = _rms(ks_ref[...], kg[1:2]).astype(BF16)
    vso_ref[...] = vs_ref[...].astype(BF16)
    kwo_ref[...] = _rms(kw_ref[...], kg[2:3]).astype(BF16)
    vwo_ref[...] = vw_ref[...].astype(BF16)


def nsa_prep(z, k_gain, B, S, ts=512):
    G = NSA_KV_GROUPS
    nt = S // ts
    cb = COL_KV_A // HEAD_DIM

    def col(br, kvi):
        return pl.BlockSpec((ts, HEAD_DIM), lambda b, g, t: (b * nt + t, cb + br * 4 + kvi * 2 + g))

    kv_out = pl.BlockSpec((None, None, ts, HEAD_DIM), lambda b, g, t: (b, g, t, 0))
    kv_shape = jax.ShapeDtypeStruct((B, G, S, HEAD_DIM), BF16)
    return pl.pallas_call(
        _nsa_prep_kernel,
        grid=(B, G, nt),
        in_specs=[col(0, 0), col(0, 1), col(1, 0), col(1, 1), col(2, 0), col(2, 1),
                  pl.BlockSpec((3, HEAD_DIM), lambda b, g, t: (0, 0))],
        out_specs=[pl.BlockSpec((None, 2, None, ts, HEAD_DIM), lambda b, g, t: (b, 0, g, t, 0)),
                   kv_out, kv_out, kv_out, kv_out],
        out_shape=[jax.ShapeDtypeStruct((B, 2, G, S, HEAD_DIM), F32), kv_shape, kv_shape, kv_shape, kv_shape],
        compiler_params=_params(("parallel", "parallel", "parallel")),
        name="nsa_prep",
    )(z, z, z, z, z, z, k_gain)


def _nsa_compress_kernel(r_ref, pos_ref, w1_ref, w2_ref, kg_ref, o_ref):
    kv = pl.program_id(1)
    half = (CMP_BLOCK // 2) * HEAD_DIM
    r = r_ref[...]
    pos = pos_ref[...]
    n = r.shape[0]
    first = _dot((r + pos[:, :half]).astype(BF16), w1_ref[:half, :].astype(BF16))
    second = _dot((r + pos[:, half:]).astype(BF16), w1_ref[half:, :].astype(BF16))
    pre = first + pltpu.roll(second, n - 1, axis=0)
    hid = jax.nn.gelu(pre)
    comp = _dot(hid.astype(BF16), w2_ref[...].astype(BF16))
    o_ref[...] = jnp.where(kv == 0, _rms(comp, kg_ref[...]), comp).astype(o_ref.dtype)


def nsa_compress(raw, cmp_pos, cmp_w1, cmp_w2, k_gain0, B, S):
    G = NSA_KV_GROUPS
    n = S // CMP_STRIDE
    feat = CMP_STRIDE * HEAD_DIM
    r = raw.reshape(B, 2, G, n, feat)
    pos = cmp_pos.reshape(2, 1, CMP_BLOCK * HEAD_DIM)
    return pl.pallas_call(
        _nsa_compress_kernel,
        grid=(B, 2, G),
        in_specs=[pl.BlockSpec((None, None, None, n, feat), lambda b, kv, g: (b, kv, g, 0, 0)),
                  pl.BlockSpec((None, 1, CMP_BLOCK * HEAD_DIM), lambda b, kv, g: (kv, 0, 0)),
                  pl.BlockSpec((None, CMP_BLOCK * HEAD_DIM, HEAD_DIM), lambda b, kv, g: (kv, 0, 0)),
                  pl.BlockSpec((None, HEAD_DIM, HEAD_DIM), lambda b, kv, g: (kv, 0, 0)),
                  pl.BlockSpec((1, HEAD_DIM), lambda b, kv, g: (0, 0))],
        out_specs=pl.BlockSpec((None, None, None, n, HEAD_DIM), lambda b, kv, g: (b, kv, g, 0, 0)),
        out_shape=jax.ShapeDtypeStruct((B, 2, G, n, HEAD_DIM), BF16),
        compiler_params=_params(("parallel", "parallel", "parallel")),
        name="nsa_compress",
    )(r, pos, cmp_w1, cmp_w2, k_gain0.reshape(1, HEAD_DIM))


def _rep(x, n):
    return x if n == HEAD_DIM else jnp.concatenate([x] * (n // HEAD_DIM), axis=1)


def _flash_step(qb, kt, vt, bias, m_ref, l_ref, acc_ref):
    nk = kt.shape[0]
    rows = qb.shape[0]
    s = _dot_nt(qb, kt)
    s = (s.reshape(rows // Q_BLOCK, Q_BLOCK, nk) + bias[None]).reshape(rows, nk)
    m_prev = m_ref[...]
    m_new = jnp.maximum(m_prev, jnp.max(s, axis=-1, keepdims=True))
    alpha = jnp.exp2(m_prev - m_new)
    p = jnp.exp2(s - _rep(m_new, nk))
    l_ref[...] = alpha * l_ref[...] + jnp.sum(p, axis=-1, keepdims=True)
    acc_ref[...] = alpha * acc_ref[...] + _dot(p.astype(BF16), vt)
    m_ref[...] = m_new


def _nsa_kernel(q_ref, sm_ref, kc_ref, vc_ref, ks_ref, vs_ref, kw_ref, vw_ref, qg_ref, og_ref, c2s_ref, exp_ref,
                o_ref, selb_ref, m_ref, l_ref, acc_ref, *, k_top):
    Hg = NSA_GROUP_SIZE
    rows = Hg * Q_BLOCK
    c = pl.program_id(2)
    t0 = c * Q_BLOCK
    n_cmp = kc_ref.shape[0]
    n_sel = c2s_ref.shape[0]

    q = q_ref[...]
    qs = jnp.concatenate([q[:, h * HEAD_DIM:(h + 1) * HEAD_DIM] for h in range(Hg)], axis=0)
    qb = (_rms(qs, qg_ref[...]) * (SCALE * LOG2E)).astype(BF16)

    tq = t0 + lax.broadcasted_iota(I32, (Q_BLOCK, n_cmp), 0)
    c_end = lax.broadcasted_iota(I32, (Q_BLOCK, n_cmp), 1) * CMP_STRIDE + (CMP_BLOCK - 1)
    sc = _dot_nt(qb, kc_ref[...]).reshape(Hg, Q_BLOCK, n_cmp)
    sc = jnp.where((c_end <= tq)[None], sc, -jnp.inf)
    mc = jnp.max(sc, axis=-1, keepdims=True)
    mc = jnp.where(mc > -jnp.inf, mc, 0.0)
    pc = jnp.exp2(sc - mc)
    dc = jnp.sum(pc, axis=-1, keepdims=True)
    pc = pc / jnp.where(dc > 0, dc, 1.0)
    o_cmp = _dot(pc.reshape(rows, n_cmp).astype(BF16), vc_ref[...])

    psum = jnp.sum(pc, axis=0)
    c2s = c2s_ref[...]
    p_hi = psum.astype(BF16)
    rem = psum - p_hi.astype(F32)
    p_mid = rem.astype(BF16)
    p_lo = (rem - p_mid.astype(F32)).astype(BF16)
    imp = _dot_nt(c2s, p_hi) + _dot_nt(c2s, p_mid) + _dot_nt(c2s, p_lo)

    jj = lax.broadcasted_iota(I32, (n_sel, Q_BLOCK), 0)
    cur = lax.shift_right_logical(t0 + lax.broadcasted_iota(I32, (n_sel, Q_BLOCK), 1), 6)
    valid = jj <= cur
    forced = valid & ((jj == 0) | (jj > cur - SEL_LOCAL))
    score = jnp.where(forced, FORCE, jnp.where(valid, imp, -FORCE))
    rank = jnp.zeros((n_sel, Q_BLOCK), F32)
    for j2 in range(n_sel):
        other = score[j2:j2 + 1, :]
        tie = jnp.where(jj > j2, 1.0, 0.0)
        rank = rank + jnp.where(other > score, 1.0, jnp.where(other == score, tie, 0.0))
    sel_t = jnp.where(rank < k_top, 1.0, 0.0)
    if n_sel < Q_BLOCK:
        sel_t = jnp.concatenate([sel_t, jnp.zeros((Q_BLOCK - n_sel, Q_BLOCK), F32)], axis=0)
    sel = sel_t.T[:, :n_sel].astype(BF16)
    sel_keys = _dot(sel, exp_ref[...])
    for i in range(selb_ref.shape[0]):
        selb_ref[i] = (sel_keys[:, i * SEL_TILE:(i + 1) * SEL_TILE] - 1.0) * (-NEG)

    def reset():
        m_ref[...] = jnp.full(m_ref.shape, NEG, F32)
        l_ref[...] = jnp.zeros(l_ref.shape, F32)
        acc_ref[...] = jnp.zeros(acc_ref.shape, F32)

    reset()
    tq_s = t0 + lax.broadcasted_iota(I32, (Q_BLOCK, SEL_TILE), 0)
    kk_s = lax.broadcasted_iota(I32, (Q_BLOCK, SEL_TILE), 1)

    def sel_body(i, carry):
        k0 = pl.multiple_of(i * SEL_TILE, SEL_TILE)
        bias = selb_ref[i] + jnp.where(kk_s + k0 <= tq_s, 0.0, NEG)
        _flash_step(qb, ks_ref[pl.ds(k0, SEL_TILE), :], vs_ref[pl.ds(k0, SEL_TILE), :], bias, m_ref, l_ref, acc_ref)
        return carry

    lax.fori_loop(0, (t0 + Q_BLOCK + SEL_TILE - 1) // SEL_TILE, sel_body, 0)
    o_sel = acc_ref[...] / l_ref[...]

    wk = WINDOW + Q_BLOCK
    w0 = pl.multiple_of(jnp.maximum(t0 - WINDOW, 0), Q_BLOCK)
    tq_w = t0 + lax.broadcasted_iota(I32, (Q_BLOCK, wk), 0)
    wpos = w0 + lax.broadcasted_iota(I32, (Q_BLOCK, wk), 1)
    w_bias = jnp.where((wpos <= tq_w) & (wpos > tq_w - WINDOW), 0.0, NEG)
    sw = _dot_nt(qb, kw_ref[pl.ds(w0, wk), :]).reshape(Hg, Q_BLOCK, wk) + w_bias[None]
    pw = jnp.exp2(sw - jnp.max(sw, axis=-1, keepdims=True))
    lw = jnp.sum(pw, axis=-1, keepdims=True).reshape(rows, 1)
    o_win = _dot(pw.reshape(rows, wk).astype(BF16), vw_ref[pl.ds(w0, wk), :]) / lw

    gates = jax.nn.sigmoid(sm_ref[...])
    og = og_ref[...]
    for h in range(Hg):
        r = slice(h * Q_BLOCK, (h + 1) * Q_BLOCK)
        o = (gates[:, h:h + 1] * o_cmp[r] + gates[:, Hg + h:Hg + h + 1] * o_sel[r]
             + gates[:, 2 * Hg + h:2 * Hg + h + 1] * o_win[r])
        o_ref[:, h * HEAD_DIM:(h + 1) * HEAD_DIM] = _rms(o, og).astype(o_ref.dtype)


def nsa_attention(z, comp, ks, vs, kw, vw, q_gain, out_gain, B, S):
    G, Hg = NSA_KV_GROUPS, NSA_GROUP_SIZE
    nq = S // Q_BLOCK
    n_cmp = S // CMP_STRIDE
    n_sel = S // SEL_BLOCK
    k_top = min(SEL_TOP_N, n_sel)
    c_start = np.arange(n_cmp)[:, None] * CMP_STRIDE
    s_start = np.arange(n_sel)[None, :] * SEL_BLOCK
    overlap = np.minimum(c_start + CMP_BLOCK, s_start + SEL_BLOCK) - np.maximum(c_start, s_start)
    assert n_sel <= Q_BLOCK
    c2s = jnp.asarray((np.clip(overlap, 0, None) / CMP_STRIDE).T, dtype=BF16)
    expand = jnp.asarray(np.arange(S)[None, :] // SEL_BLOCK == np.arange(n_sel)[:, None], dtype=BF16)
    rows = Hg * Q_BLOCK
    wq = Hg * HEAD_DIM
    kv_spec = pl.BlockSpec((None, None, S, HEAD_DIM), lambda b, g, c: (b, g, 0, 0))
    const = lambda b, g, c: (0, 0)
    return pl.pallas_call(
        functools.partial(_nsa_kernel, k_top=k_top),
        grid=(B, G, nq),
        in_specs=[pl.BlockSpec((Q_BLOCK, wq), lambda b, g, c: (b * nq + c, COL_Q_A // wq + g)),
                  pl.BlockSpec((Q_BLOCK, HEAD_DIM), lambda b, g, c: (b * nq + c, COL_SMALL // HEAD_DIM + g)),
                  pl.BlockSpec((None, None, None, n_cmp, HEAD_DIM), lambda b, g, c: (b, 0, g, 0, 0)),
                  pl.BlockSpec((None, None, None, n_cmp, HEAD_DIM), lambda b, g, c: (b, 1, g, 0, 0)),
                  kv_spec, kv_spec, kv_spec, kv_spec,
                  pl.BlockSpec((1, HEAD_DIM), const), pl.BlockSpec((1, HEAD_DIM), const),
                  pl.BlockSpec((n_sel, n_cmp), const), pl.BlockSpec((n_sel, S), const)],
        out_specs=pl.BlockSpec((Q_BLOCK, wq), lambda b, g, c: (b * nq + c, g)),
        out_shape=jax.ShapeDtypeStruct((B * S, D_NSA), BF16),
        scratch_shapes=[pltpu.VMEM((S // SEL_TILE, Q_BLOCK, SEL_TILE), F32),
                        pltpu.VMEM((rows, HEAD_DIM), F32), pltpu.VMEM((rows, HEAD_DIM), F32),
                        pltpu.VMEM((rows, HEAD_DIM), F32)],
        compiler_params=_params(("parallel", "parallel", "arbitrary")),
        name="nsa_attention",
    )(z, z, comp, comp, ks, vs, kw, vw, q_gain.reshape(1, HEAD_DIM), out_gain.reshape(1, HEAD_DIM), c2s, expand)


def _gdn_chunk_kernel(xc_ref, xh_ref, sm_ref, cw_ref, alog_ref, dtb_ref, rep_ref, unfold_ref, u_ref, wq_ref, ak_ref,
                      egl_ref):
    for b in range(xc_ref.shape[0]):
        _gdn_chunk_body(xc_ref.at[b], xh_ref.at[b], sm_ref.at[b], cw_ref, alog_ref, dtb_ref, rep_ref, unfold_ref,
                        u_ref.at[b], wq_ref.at[b], ak_ref.at[b], egl_ref.at[b])


def _gdn_chunk_body(xc_ref, xh_ref, sm_ref, cw_ref, alog_ref, dtb_ref, rep_ref, unfold_ref, u_ref, wq_ref, ak_ref,
                    egl_ref):
    C = GDN_CHUNK
    PK = GDN_PACK
    R = PK * C
    n = pl.program_id(0)
    xp = jnp.concatenate([jnp.where(n > 0, xh_ref[...], 0.0), xc_ref[...]], axis=0)
    cw = cw_ref[...]
    y = None
    for j in range(GDN_CONV):
        shift = GDN_CONV - 1 - j
        xs = xp if shift == 0 else pltpu.roll(xp, shift, axis=0)
        term = xs[8:] * cw[j:j + 1, :]
        y = term if y is None else y + term
    y = y * jax.nn.sigmoid(y)

    sm = sm_ref[...]
    beta = jax.nn.sigmoid(sm)
    g = -jnp.exp(alog_ref[...]) * jax.nn.softplus(sm + dtb_ref[...])
    row = lax.broadcasted_iota(I32, g.shape, 0)
    gc = g
    d = 1
    while d < C:
        gc = gc + jnp.where(row >= d, pltpu.roll(gc, d, axis=0), 0.0)
        d *= 2
    g_last = gc[C - 1:C, :]

    ri = lax.broadcasted_iota(I32, (R, R), 0)
    ci = lax.broadcasted_iota(I32, (R, R), 1)
    same = lax.shift_right_logical(ri, 6) == lax.shift_right_logical(ci, 6)
    tri = same & (ri >= ci)
    strict = same & (ri > ci)
    SUB = GDN_SUB
    nb = R // SUB
    same16 = lax.shift_right_logical(ri, 4) == lax.shift_right_logical(ci, 4)
    same32 = lax.shift_right_logical(ri, 5) == lax.shift_right_logical(ci, 5)
    off32 = same32 & jnp.logical_not(same16)
    off64 = jnp.logical_not(same32)
    NG = GDN_HEADS // PK
    W4 = NG * SUB
    row_c = lax.broadcasted_iota(I32, (R, W4), 0) & (SUB - 1)
    col_s = lax.broadcasted_iota(I32, (nb, W4), 1) & (SUB - 1)

    groups = []
    for grp in range(NG):
        heads = [grp * PK + i for i in range(PK)]

        def stack(off):
            return jnp.concatenate([y[:, off + h * HEAD_DIM: off + (h + 1) * HEAD_DIM] for h in heads], axis=0)

        def col(x, base):
            return jnp.concatenate([x[:, base + h: base + h + 1] for h in heads], axis=0)

        q4 = stack(0)
        k4 = stack(D_GDN)
        v4 = stack(2 * D_GDN)
        q4 = q4 * lax.rsqrt(jnp.sum(q4 * q4, axis=-1, keepdims=True) + EPS) * SCALE
        k4 = k4 * lax.rsqrt(jnp.sum(k4 * k4, axis=-1, keepdims=True) + EPS)
        beta4 = col(beta, SM_BETA)
        gc4 = col(gc, SM_A)
        gl4 = jnp.concatenate([jnp.broadcast_to(g_last[:, SM_A + h: SM_A + h + 1], (C, 1)) for h in heads], axis=0)

        gb = jnp.broadcast_to(gc4, (R, R))
        decay = jnp.exp(jnp.where(tri, gb - gb.T, -jnp.inf))
        kb4 = k4 * beta4
        k4b = k4.astype(BF16)
        a = jnp.where(strict, _dot_nt(kb4.astype(BF16), k4b) * decay, 0.0)
        attn = _dot_nt(q4.astype(BF16), k4b) * decay
        at = a.T
        dct = jnp.concatenate([at[SUB * b:SUB * (b + 1), SUB * b:SUB * (b + 1)] for b in range(nb)], axis=0)
        groups.append((heads, q4, k4, v4, beta4, gc4, gl4, kb4, a, attn, dct))

    coef = _dot3(jnp.concatenate([g[-1] for g in groups], axis=1), rep_ref[...])
    t4 = jnp.where(row_c == (lax.broadcasted_iota(I32, (R, W4), 1) & (SUB - 1)), 1.0, 0.0)
    for i in range(1, SUB):
        s = jnp.sum((coef[:, i * HEAD_DIM: i * HEAD_DIM + W4] * t4).reshape(nb, SUB, W4), axis=1)
        new = jnp.where(col_s == i, 1.0, 0.0) - s
        t4 = jnp.where(row_c == i, jnp.broadcast_to(new[:, None, :], (nb, SUB, W4)).reshape(R, W4), t4)
    t_tiled = _dot3(t4, unfold_ref[...])

    for gi, (heads, q4, k4, v4, beta4, gc4, gl4, kb4, a, attn, _) in enumerate(groups):
        t16 = jnp.where(same16, t_tiled[:, gi * R:(gi + 1) * R], 0.0)
        t16b = t16.astype(BF16)
        a32 = jnp.where(off32, a, 0.0).astype(BF16)
        t32 = t16 - _dot(_dot(t16b, a32).astype(BF16), t16b)
        t32b = t32.astype(BF16)
        a64 = jnp.where(off64, a, 0.0).astype(BF16)
        t64 = t32 - _dot(_dot(t32b, a64).astype(BF16), t32b)
        eg = jnp.exp(gc4)
        rhs = jnp.concatenate([v4 * beta4, kb4 * eg], axis=1)
        rhs = _dot(t64.astype(BF16), rhs.astype(BF16))
        qd4 = q4 * eg
        kdt = (k4 * jnp.exp(gl4 - gc4)).T
        for i, h in enumerate(heads):
            r = slice(i * C, (i + 1) * C)
            u_ref[h] = rhs[r, :HEAD_DIM]
            wq_ref[h] = jnp.concatenate([rhs[r, HEAD_DIM:], qd4[r]], axis=0).astype(BF16)
            ak_ref[h] = jnp.concatenate([attn[r, r], kdt[:, r]], axis=0).astype(BF16)
            egl_ref[h] = jnp.broadcast_to(jnp.exp(g_last[:, SM_A + h: SM_A + h + 1]), (8, HEAD_DIM))


def gdn_chunks(z, conv_w, a_log, dt_bias, B, S):
    C, H = GDN_CHUNK, GDN_HEADS
    N = S // C
    W = 3 * D_GDN
    pad = lambda v: jnp.zeros((1, HEAD_DIM), F32).at[0, SM_A:SM_A + H].set(v)
    out5 = lambda r, cdim: pl.BlockSpec((B, None, H, r, cdim), lambda n: (0, n, 0, 0, 0))
    NG, SUB, R = H // GDN_PACK, GDN_SUB, GDN_PACK * C
    W4 = NG * SUB
    src = np.arange(W4)[:, None]
    dst = np.arange(SUB * HEAD_DIM)[None, :]
    lane = dst % HEAD_DIM
    rep = (lane < W4) & (src // SUB == lane // SUB) & (src % SUB == dst // HEAD_DIM)
    dst = np.arange(NG * R)[None, :]
    unfold = (src // SUB == dst // R) & (src % SUB == dst % SUB)
    z3 = z.reshape(B, S, z.shape[1])
    const = lambda n: (0, 0)
    return pl.pallas_call(
        _gdn_chunk_kernel,
        grid=(N,),
        in_specs=[pl.BlockSpec((B, C, W), lambda n: (0, n, 0)),
                  pl.BlockSpec((B, 8, W), lambda n: (0, jnp.maximum(n * (C // 8) - 1, 0), 0)),
                  pl.BlockSpec((B, C, HEAD_DIM), lambda n: (0, n, COL_SMALL // HEAD_DIM)),
                  pl.BlockSpec((GDN_CONV, W), const),
                  pl.BlockSpec((1, HEAD_DIM), const),
                  pl.BlockSpec((1, HEAD_DIM), const),
                  pl.BlockSpec((W4, SUB * HEAD_DIM), const),
                  pl.BlockSpec((W4, NG * R), const)],
        out_specs=[out5(C, HEAD_DIM), out5(2 * C, HEAD_DIM), out5(C + HEAD_DIM, C), out5(8, HEAD_DIM)],
        out_shape=[jax.ShapeDtypeStruct((B, N, H, C, HEAD_DIM), F32),
                   jax.ShapeDtypeStruct((B, N, H, 2 * C, HEAD_DIM), BF16),
                   jax.ShapeDtypeStruct((B, N, H, C + HEAD_DIM, C), BF16),
                   jax.ShapeDtypeStruct((B, N, H, 8, HEAD_DIM), F32)],
        compiler_params=_params(("parallel",)),
        name="gdn_chunks",
    )(z3, z3, z3, conv_w, pad(a_log), pad(dt_bias), jnp.asarray(rep, dtype=BF16), jnp.asarray(unfold, dtype=BF16))


def _gdn_scan_kernel(u_ref, wq_ref, ak_ref, egl_ref, z_ref, og_ref, o_ref, s_ref):
    C = GDN_CHUNK

    @pl.when(pl.program_id(0) == 0)
    def _():
        s_ref[...] = jnp.zeros(s_ref.shape, F32)

    og = og_ref[...]
    for b in range(u_ref.shape[0]):
        for h in range(GDN_HEADS):
            state = s_ref[b, h]
            ws = _dot(wq_ref[b, h], state.astype(BF16))
            v_new = u_ref[b, h] - ws[:C]
            av = _dot(ak_ref[b, h], v_new.astype(BF16))
            o = ws[C:] + av[:C]
            decayed = (state.reshape(HEAD_DIM // 8, 8, HEAD_DIM) * egl_ref[b, h][None]).reshape(HEAD_DIM, HEAD_DIM)
            s_ref[b, h] = decayed + av[C:]
            zh = z_ref[b, :, h * HEAD_DIM:(h + 1) * HEAD_DIM]
            o_ref[b, :, h * HEAD_DIM:(h + 1) * HEAD_DIM] = (_rms(o, og) * (zh * jax.nn.sigmoid(zh))).astype(o_ref.dtype)


def gdn_scan(u, wq, ak, egl, z, out_gain, B, S):
    C, H = GDN_CHUNK, GDN_HEADS
    N = S // C
    in5 = lambda r, cdim: pl.BlockSpec((B, None, H, r, cdim), lambda n: (0, n, 0, 0, 0))
    y = pl.pallas_call(
        _gdn_scan_kernel,
        grid=(N,),
        in_specs=[in5(C, HEAD_DIM), in5(2 * C, HEAD_DIM), in5(C + HEAD_DIM, C), in5(8, HEAD_DIM),
                  pl.BlockSpec((B, C, D_GDN), lambda n: (0, n, COL_Z_B // D_GDN)),
                  pl.BlockSpec((1, HEAD_DIM), lambda n: (0, 0))],
        out_specs=pl.BlockSpec((B, C, D_GDN), lambda n: (0, n, 0)),
        out_shape=jax.ShapeDtypeStruct((B, S, D_GDN), BF16),
        scratch_shapes=[pltpu.VMEM((B, H, HEAD_DIM, HEAD_DIM), F32)],
        compiler_params=_params(("arbitrary",)),
        name="gdn_scan",
    )(u, wq, ak, egl, z.reshape(B, S, z.shape[1]), out_gain.reshape(1, HEAD_DIM))
    return y.reshape(B * S, D_GDN)


def _xa_kv_kernel(mem_ref, g_ref, w_ref, kg_ref, k_ref, v_ref):
    kv = _dot(_rms(mem_ref[...], g_ref[...]).astype(BF16), w_ref[...])
    for h in range(XA_HEADS):
        k_ref[h] = _rms(kv[:, h * HEAD_DIM:(h + 1) * HEAD_DIM], kg_ref[...]).astype(BF16)
        v_ref[h] = kv[:, D_XA + h * HEAD_DIM: D_XA + (h + 1) * HEAD_DIM].astype(BF16)


def xa_kv(mem, mem_norm, wkv_bf16, k_gain):
    B, Mm, D = mem.shape
    spec = pl.BlockSpec((None, XA_HEADS, Mm, HEAD_DIM), lambda b: (b, 0, 0, 0))
    shape = jax.ShapeDtypeStruct((B, XA_HEADS, Mm, HEAD_DIM), BF16)
    return pl.pallas_call(
        _xa_kv_kernel,
        grid=(B,),
        in_specs=[pl.BlockSpec((None, Mm, D), lambda b: (b, 0, 0)), pl.BlockSpec((1, D), lambda b: (0, 0)),
                  pl.BlockSpec((D, 2 * D_XA), lambda b: (0, 0)), pl.BlockSpec((1, HEAD_DIM), lambda b: (0, 0))],
        out_specs=[spec, spec],
        out_shape=[shape, shape],
        compiler_params=_params(("parallel",)),
        name="xa_kv",
    )(mem, mem_norm.reshape(1, D), wkv_bf16, k_gain.reshape(1, HEAD_DIM))


def _xa_kernel(x_ref, g_ref, wq_ref, k_ref, v_ref, qg_ref, wo_ref, fg_ref, *rest, route):
    if route:
        rwh_ref, rwl_ref, o_ref, h_ref, idx_ref, wt_ref = rest
    else:
        o_ref, h_ref = rest
    x = x_ref[...]
    q = _dot(_rms(x, g_ref[...]).astype(BF16), wq_ref[...])
    outs = []
    for h in range(XA_HEADS):
        qh = _rms(q[:, h * HEAD_DIM:(h + 1) * HEAD_DIM], qg_ref[...]).astype(BF16)
        s = _dot_nt(qh, k_ref[h]) * SCALE
        p = jnp.exp(s - jnp.max(s, axis=-1, keepdims=True))
        p = p / jnp.sum(p, axis=-1, keepdims=True)
        outs.append(_dot(p.astype(BF16), v_ref[h]))
    o = jnp.concatenate(outs, axis=1).astype(BF16)
    x_new = x + _dot(o, wo_ref[...])
    o_ref[...] = x_new
    hn = _rms(x_new, fg_ref[...])
    h_ref[...] = hn.astype(h_ref.dtype)
    if route:
        idx_ref[...], wt_ref[...] = _route(hn, rwh_ref[...], rwl_ref[...])


def cross_attention(x, xa_norm, wq_bf16, k, v, q_gain, wo_bf16, ffn_norm, router_w, B, S, tm=256):
    M, D = x.shape
    Mm = k.shape[2]
    per_b = S // tm
    kv_spec = pl.BlockSpec((None, XA_HEADS, Mm, HEAD_DIM), lambda i: (i // per_b, 0, 0, 0))
    const = lambda i: (0, 0)
    row = lambda i: (i, 0)
    route = router_w is not None
    in_specs = [pl.BlockSpec((tm, D), row), pl.BlockSpec((1, D), const),
                pl.BlockSpec((D, D_XA), const), kv_spec, kv_spec,
                pl.BlockSpec((1, HEAD_DIM), const), pl.BlockSpec((D_XA, D), const), pl.BlockSpec((1, D), const)]
    args = [x, xa_norm.reshape(1, D), wq_bf16, k, v, q_gain.reshape(1, HEAD_DIM), wo_bf16, ffn_norm.reshape(1, D)]
    out_specs = [pl.BlockSpec((tm, D), row), pl.BlockSpec((tm, D), row)]
    out_shape = [jax.ShapeDtypeStruct((M, D), F32), jax.ShapeDtypeStruct((M, D), BF16)]
    if route:
        rw = jnp.pad(router_w, ((0, 0), (0, 128 - N_EXPERTS)))
        rw_hi = rw.astype(BF16)
        in_specs += [pl.BlockSpec((D, 128), const), pl.BlockSpec((D, 128), const)]
        args += [rw_hi, (rw - rw_hi.astype(F32)).astype(BF16)]
        out_specs += [pl.BlockSpec((tm, 128), row), pl.BlockSpec((tm, 128), row)]
        out_shape += [jax.ShapeDtypeStruct((M, 128), I32), jax.ShapeDtypeStruct((M, 128), F32)]
    return pl.pallas_call(
        functools.partial(_xa_kernel, route=route),
        grid=(M // tm,),
        in_specs=in_specs,
        out_specs=out_specs,
        out_shape=out_shape,
        compiler_params=_params(("parallel",)),
        name="cross_attention",
    )(*args)


def _reorder_w_in(w_in):
    o = np.cumsum((D_NSA, 6 * D_KV_NSA, 3 * NSA_HEADS, 3 * D_GDN, GDN_HEADS, GDN_HEADS, D_GDN))
    q_a, kv_a, gate_a, qkv_b = (0, o[0]), (o[0], o[1]), o[1], (o[2], o[3])
    beta_b, a_b, z_b = o[3], o[4], (o[5], o[6])
    Hg = NSA_GROUP_SIZE
    src = np.zeros((SMALL_W,), np.int64)
    used = np.zeros((SMALL_W,), bool)
    for g in range(NSA_KV_GROUPS):
        for br in range(3):
            for h in range(Hg):
                src[g * HEAD_DIM + br * Hg + h] = gate_a + (g * Hg + h) * 3 + br
                used[g * HEAD_DIM + br * Hg + h] = True
    for h in range(GDN_HEADS):
        src[SM_BETA + h] = beta_b + h
        src[SM_A + h] = a_b + h
        used[SM_BETA + h] = used[SM_A + h] = True
    small = jnp.where(jnp.asarray(used)[None, :], jnp.take(w_in, jnp.asarray(src), axis=1), 0.0)
    parts = [w_in[:, qkv_b[0]:qkv_b[1]], w_in[:, q_a[0]:q_a[1]], w_in[:, z_b[0]:z_b[1]], w_in[:, kv_a[0]:kv_a[1]], small]
    return jnp.concatenate([p.astype(BF16) for p in parts], axis=1)


def mixer_layer(x2, B, S, attn_norm, w_in, nsa_q_gain, nsa_k_gain, nsa_cmp_pos, nsa_cmp_w1, nsa_cmp_w2, nsa_out_gain,
                gdn_conv_w, gdn_A_log, gdn_dt_bias, gdn_out_gain, w_out_stack, layer, tm=1024):
    h = rmsnorm_bf16(x2, attn_norm)
    z = matmul_cols(h, _reorder_w_in(w_in), tm=tm, tn=1024, name="in_proj")
    raw, ks, vs, kw, vw = nsa_prep(z, nsa_k_gain, B, S)
    comp = nsa_compress(raw, nsa_cmp_pos, nsa_cmp_w1, nsa_cmp_w2, nsa_k_gain[0], B, S)
    y_a = nsa_attention(z, comp, ks, vs, kw, vw, nsa_q_gain, nsa_out_gain, B, S)
    u, wq, ak, egl = gdn_chunks(z, gdn_conv_w, gdn_A_log, gdn_dt_bias, B, S)
    y_b = gdn_scan(u, wq, ak, egl, z, gdn_out_gain, B, S)
    return matmul2_residual(y_a, y_b, w_out_stack, layer, x2, tm=tm, tn=512, name="out_proj")


def xa_layer(x2, mem, B, S, xa_norm, mem_norm, xa_wq, xa_wkv, xa_q_gain, xa_k_gain, xa_wo, ffn_norm, router_w=None):
    k, v = xa_kv(mem, mem_norm, xa_wkv.astype(BF16), xa_k_gain)
    return cross_attention(x2, xa_norm, xa_wq.astype(BF16), k, v, xa_q_gain, xa_wo.astype(BF16), ffn_norm, router_w,
                           B, S)


def dense_ffn_layer(x2, h, w13, w2, tm=1024):
    act = swiglu_up(h, w13, tm=min(2 * tm, x2.shape[0]), tn=256)
    return down_proj_residual(act, w2, x2, tm=min(2 * tm, x2.shape[0]), tn=1024, tk=512, tk_tail=256)


def moe_ffn_layer(x2, h, idx, wts, w13, w2, tm=512, tn=512):
    src_tok, row_w, pos, tile_expert, n_valid = moe_dispatch(idx[:, :TOP_K], wts[:, :TOP_K], tm)
    xs = jnp.take(h, src_tok, axis=0)
    act = moe_up(xs, w13, tile_expert, n_valid, tm=tm, tn=tn)
    row_w_rep = jnp.broadcast_to(row_w[:, None], (row_w.shape[0], 128))
    out = moe_down(act, w2, row_w_rep, tile_expert, n_valid, tm=tm, tn=tn)
    return x2 + (jnp.take(out, pos[:, 0], axis=0) + jnp.take(out, pos[:, 1], axis=0))


def kernel(x, mem, attn_norm, w_in, nsa_q_gain, nsa_k_gain, nsa_cmp_pos, nsa_cmp_w1, nsa_cmp_w2, nsa_out_gain, gdn_conv_w, gdn_A_log, gdn_dt_bias, gdn_out_gain, w_out, xa_norm, mem_norm, xa_wq, xa_wkv, xa_q_gain, xa_k_gain, xa_wo, ffn_norm, dense_w13, dense_w2, router_w, moe_w13, moe_w2):
    B, S, D = x.shape
    x2 = x.reshape(B * S, D)
    for l in range(attn_norm.shape[0]):
        x2 = mixer_layer(x2, B, S, attn_norm[l], w_in[l], nsa_q_gain[l], nsa_k_gain[l], nsa_cmp_pos[l], nsa_cmp_w1[l],
                         nsa_cmp_w2[l], nsa_out_gain[l], gdn_conv_w[l], gdn_A_log[l], gdn_dt_bias[l], gdn_out_gain[l],
                         w_out, l)
        xa_args = (x2, mem, B, S, xa_norm[l], mem_norm[l], xa_wq[l], xa_wkv[l], xa_q_gain[l], xa_k_gain[l], xa_wo[l],
                   ffn_norm[l])
        if l % 2 == 0:
            x2, h = xa_layer(*xa_args)
            x2 = dense_ffn_layer(x2, h, dense_w13[l // 2], dense_w2[l // 2])
        else:
            x2, h, idx, wts = xa_layer(*xa_args, router_w[l // 2])
            x2 = moe_ffn_layer(x2, h, idx, wts, moe_w13[l // 2], moe_w2[l // 2])
    return x2.reshape(B, S, D)
```

```python
import functools

import jax
import jax.numpy as jnp
import numpy as np
from jax import lax
from jax.experimental import pallas as pl
from jax.experimental.pallas import tpu as pltpu

F32 = jnp.float32
BF16 = jnp.bfloat16
I32 = jnp.int32

D_MODEL = 4096
HEAD_DIM = 128
EPS = 1e-6
SCALE = HEAD_DIM ** -0.5

NSA_HEADS = 16
NSA_KV_GROUPS = 2
NSA_GROUP_SIZE = NSA_HEADS // NSA_KV_GROUPS
CMP_BLOCK = 32
CMP_STRIDE = 16
SEL_BLOCK = 64
SEL_TOP_N = 16
SEL_LOCAL = 2
WINDOW = 512
Q_BLOCK = 128
FORCE = 1e9
NEG = -1e30
SEL_TILE = 512
LOG2E = 1.4426950408889634

GDN_HEADS = 16
GDN_CONV = 4
GDN_CHUNK = 64
GDN_PACK = 4
GDN_SUB = 16

D_NSA = NSA_HEADS * HEAD_DIM
D_KV_NSA = NSA_KV_GROUPS * HEAD_DIM
D_GDN = GDN_HEADS * HEAD_DIM
D_MIX = D_NSA + D_GDN

XA_HEADS = 4
D_XA = XA_HEADS * HEAD_DIM
N_EXPERTS = 8
TOP_K = 2

COL_QKV_B = 0
COL_Q_A = COL_QKV_B + 3 * D_GDN
COL_Z_B = COL_Q_A + D_NSA
COL_KV_A = COL_Z_B + D_GDN
COL_SMALL = COL_KV_A + 6 * D_KV_NSA
SMALL_W = 512
N_IN_R = COL_SMALL + SMALL_W
SM_BETA = 32
SM_A = 48

VMEM_LIMIT = 58 * 1024 * 1024

NT_DIMS = (((1,), (1,)), ((), ()))


def _rms(x, gain):
    return x * lax.rsqrt(jnp.mean(x * x, axis=-1, keepdims=True) + EPS) * gain


def _dot(a, b):
    return jnp.dot(a, b, preferred_element_type=F32)


def _dot_nt(a, b):
    return lax.dot_general(a, b, NT_DIMS, preferred_element_type=F32)


def _dot3(x, sel):
    hi = x.astype(BF16)
    rem = x - hi.astype(F32)
    mid = rem.astype(BF16)
    lo = (rem - mid.astype(F32)).astype(BF16)
    return _dot(hi, sel) + _dot(mid, sel) + _dot(lo, sel)


def _params(sem, vmem=VMEM_LIMIT):
    return pltpu.CompilerParams(dimension_semantics=sem, vmem_limit_bytes=vmem)


def _norm_kernel(x_ref, g_ref, o_ref):
    o_ref[...] = _rms(x_ref[...], g_ref[...]).astype(o_ref.dtype)


def rmsnorm_bf16(x, gain, tm=512):
    M, D = x.shape
    return pl.pallas_call(
        _norm_kernel,
        grid=(M // tm,),
        in_specs=[pl.BlockSpec((tm, D), lambda i: (i, 0)), pl.BlockSpec((1, D), lambda i: (0, 0))],
        out_specs=pl.BlockSpec((tm, D), lambda i: (i, 0)),
        out_shape=jax.ShapeDtypeStruct((M, D), BF16),
        compiler_params=_params(("parallel",)),
        name="rmsnorm",
    )(x, gain.reshape(1, D))


def _route(h, rw_hi, rw_lo):
    h_hi = h.astype(BF16)
    h_lo = (h - h_hi.astype(F32)).astype(BF16)
    logits = _dot(h_hi, rw_hi) + _dot(h_hi, rw_lo) + _dot(h_lo, rw_hi)
    lane = lax.broadcasted_iota(I32, logits.shape, 1).astype(F32)
    logits = jnp.where(lane < N_EXPERTS, logits, -jnp.inf)
    m1 = jnp.max(logits, axis=-1, keepdims=True)
    i1 = jnp.min(jnp.where(logits == m1, lane, 128.0), axis=-1, keepdims=True)
    rest = jnp.where(lane == i1, -jnp.inf, logits)
    m2 = jnp.max(rest, axis=-1, keepdims=True)
    i2 = jnp.min(jnp.where(rest == m2, lane, 128.0), axis=-1, keepdims=True)
    e2 = jnp.exp(m2 - m1)
    den = 1.0 + e2
    idx = jnp.where(lane == 0, i1, jnp.where(lane == 1, i2, 0.0)).astype(I32)
    wts = jnp.where(lane == 0, 1.0 / den, jnp.where(lane == 1, e2 / den, 0.0))
    return idx, wts


def _mm_kernel(x_ref, w_ref, o_ref):
    o_ref[...] = _dot(x_ref[...], w_ref[...].astype(BF16)).astype(o_ref.dtype)


def matmul_cols(x, w, *, tm, tn, out_dtype=F32, name="matmul"):
    M, K = x.shape
    N = w.shape[1]
    return pl.pallas_call(
        _mm_kernel,
        grid=(M // tm, N // tn),
        in_specs=[pl.BlockSpec((tm, K), lambda i, j: (i, 0)), pl.BlockSpec((K, tn), lambda i, j: (0, j))],
        out_specs=pl.BlockSpec((tm, tn), lambda i, j: (i, j)),
        out_shape=jax.ShapeDtypeStruct((M, N), out_dtype),
        compiler_params=_params(("parallel", "parallel")),
        name=name,
    )(x, w)


def _mm2_res_kernel(xa_ref, xb_ref, wa_ref, wb_ref, r_ref, o_ref):
    acc = _dot(xa_ref[...], wa_ref[...].astype(BF16)) + _dot(xb_ref[...], wb_ref[...].astype(BF16))
    o_ref[...] = r_ref[...] + acc


def matmul2_residual(xa, xb, w, layer, residual, *, tm, tn, name):
    M, Ka = xa.shape
    Kb = xb.shape[1]
    assert Ka == Kb and w.shape[1] == Ka + Kb
    N = w.shape[2]
    return pl.pallas_call(
        _mm2_res_kernel,
        grid=(M // tm, N // tn),
        in_specs=[pl.BlockSpec((tm, Ka), lambda i, j: (i, 0)), pl.BlockSpec((tm, Kb), lambda i, j: (i, 0)),
                  pl.BlockSpec((None, Ka, tn), lambda i, j: (layer, 0, j)),
                  pl.BlockSpec((None, Kb, tn), lambda i, j: (layer, 1, j)),
                  pl.BlockSpec((tm, tn), lambda i, j: (i, j))],
        out_specs=pl.BlockSpec((tm, tn), lambda i, j: (i, j)),
        out_shape=jax.ShapeDtypeStruct((M, N), F32),
        compiler_params=_params(("parallel", "parallel")),
        name=name,
    )(xa, xb, w, w, residual)


def _swiglu_up_kernel(x_ref, wg_ref, wu_ref, o_ref):
    x = x_ref[...]
    gate = _dot(x, wg_ref[...].astype(BF16))
    up = _dot(x, wu_ref[...].astype(BF16))
    o_ref[...] = (gate * jax.nn.sigmoid(gate) * up).astype(o_ref.dtype)


def swiglu_up(x, w13, *, tm, tn):
    M, K = x.shape
    F = w13.shape[1] // 2
    nj = F // tn
    return pl.pallas_call(
        _swiglu_up_kernel,
        grid=(M // tm, nj),
        in_specs=[pl.BlockSpec((tm, K), lambda i, j: (i, 0), pipeline_mode=pl.Buffered(1)),
                  pl.BlockSpec((K, tn), lambda i, j: (0, j)),
                  pl.BlockSpec((K, tn), lambda i, j: (0, j + nj))],
        out_specs=pl.BlockSpec((tm, tn), lambda i, j: (i, j)),
        out_shape=jax.ShapeDtypeStruct((M, F), BF16),
        compiler_params=_params(("parallel", "parallel")),
        name="swiglu_up",
    )(x, w13, w13)


def _down_res_kernel(a_ref, w_ref, at_ref, wt_ref, r_ref, o_ref, *, n_main):
    k = pl.program_id(2)

    @pl.when(k == 0)
    def _():
        o_ref[...] = r_ref[...]

    @pl.when(k < n_main)
    def _():
        o_ref[...] += _dot(a_ref[...], w_ref[...].astype(BF16))

    @pl.when(k >= n_main)
    def _():
        o_ref[...] += _dot(at_ref[...], wt_ref[...].astype(BF16))


def down_proj_residual(a, w2, residual, *, tm, tn, tk, tk_tail):
    M, F = a.shape
    N = w2.shape[1]
    n_main = F // tk
    tail = F - n_main * tk
    assert tail % tk_tail == 0 and (n_main * tk) % tk_tail == 0
    n_tail = tail // tk_tail
    first_tail = (n_main * tk) // tk_tail if n_tail else 0
    main_k = lambda k: jnp.minimum(k, n_main - 1)
    tail_k = lambda k: first_tail + jnp.maximum(k - n_main, 0)
    return pl.pallas_call(
        functools.partial(_down_res_kernel, n_main=n_main),
        grid=(M // tm, N // tn, n_main + n_tail),
        in_specs=[pl.BlockSpec((tm, tk), lambda i, j, k: (i, main_k(k))),
                  pl.BlockSpec((tk, tn), lambda i, j, k: (main_k(k), j)),
                  pl.BlockSpec((tm, tk_tail), lambda i, j, k: (i, tail_k(k))),
                  pl.BlockSpec((tk_tail, tn), lambda i, j, k: (tail_k(k), j)),
                  pl.BlockSpec((tm, tn), lambda i, j, k: (i, j))],
        out_specs=pl.BlockSpec((tm, tn), lambda i, j, k: (i, j)),
        out_shape=jax.ShapeDtypeStruct((M, N), F32),
        compiler_params=_params(("parallel", "parallel", "arbitrary")),
        name="down_proj",
    )(a, w2, a, w2, residual)


def _new_weights(te_ref, i):
    return jnp.logical_or(i == 0, te_ref[i] != te_ref[jnp.maximum(i - 1, 0)])


def _moe_up_kernel(te_ref, nv_ref, x_ref, wg_ref, wu_ref, *rest, tile_off):
    o_ref, wgb_ref, wub_ref = rest[-3:]
    i = pl.program_id(1) + tile_off
    valid = i < nv_ref[0]

    @pl.when(jnp.logical_or(pl.program_id(1) == 0, _new_weights(te_ref, i)))
    def _():
        wgb_ref[...] = wg_ref[...].astype(BF16)
        wub_ref[...] = wu_ref[...].astype(BF16)

    @pl.when(valid)
    def _():
        x = x_ref[...]
        gate = _dot(x, wgb_ref[...])
        up = _dot(x, wub_ref[...])
        o_ref[...] = (gate * jax.nn.sigmoid(gate) * up).astype(o_ref.dtype)

    @pl.when(jnp.logical_not(valid))
    def _():
        o_ref[...] = jnp.zeros_like(o_ref)


def moe_up(xs, w13, tile_expert, n_valid, *, tm, tn, tile_off=0, total_rows=None, prev=None):
    rows, K = xs.shape
    P = rows if total_rows is None else total_rows
    F = w13.shape[2] // 2
    nj = F // tn
    in_specs = [pl.BlockSpec((tm, K), lambda j, i, te, nv: (jnp.maximum(jnp.minimum(i, nv[0] - tile_off - 1), 0), 0)),
                pl.BlockSpec((None, K, tn), lambda j, i, te, nv: (te[i + tile_off], 0, j)),
                pl.BlockSpec((None, K, tn), lambda j, i, te, nv: (te[i + tile_off], 0, j + nj))]
    args = [tile_expert, n_valid, xs, w13, w13]
    aliases = {}
    if prev is not None:
        in_specs.append(pl.BlockSpec(memory_space=pl.ANY))
        args.append(prev)
        aliases = {5: 0}
    return pl.pallas_call(
        functools.partial(_moe_up_kernel, tile_off=tile_off),
        grid_spec=pltpu.PrefetchScalarGridSpec(
            num_scalar_prefetch=2,
            grid=(nj, rows // tm),
            in_specs=in_specs,
            out_specs=pl.BlockSpec((tm, tn), lambda j, i, te, nv: (i + tile_off, j)),
            scratch_shapes=[pltpu.VMEM((K, tn), BF16), pltpu.VMEM((K, tn), BF16)],
        ),
        out_shape=jax.ShapeDtypeStruct((P, F), BF16),
        input_output_aliases=aliases,
        compiler_params=_params(("arbitrary", "arbitrary")),
        name="moe_up",
    )(*args)


def _moe_down_kernel(te_ref, nv_ref, a_ref, w_ref, rw_ref, o_ref, wb_ref):
    i = pl.program_id(1)
    valid = i < nv_ref[0]

    @pl.when(_new_weights(te_ref, i))
    def _():
        wb_ref[...] = w_ref[...].astype(BF16)

    @pl.when(valid)
    def _():
        o_ref[...] = _dot(a_ref[...], wb_ref[...]) * rw_ref[:, 0:1]

    @pl.when(jnp.logical_not(valid))
    def _():
        o_ref[...] = jnp.zeros_like(o_ref)


def moe_down(act, w2, row_w, tile_expert, n_valid, *, tm, tn):
    P, F = act.shape
    N = w2.shape[2]
    return pl.pallas_call(
        _moe_down_kernel,
        grid_spec=pltpu.PrefetchScalarGridSpec(
            num_scalar_prefetch=2,
            grid=(N // tn, P // tm),
            in_specs=[pl.BlockSpec((tm, F), lambda j, i, te, nv: (jnp.minimum(i, nv[0] - 1), 0)),
                      pl.BlockSpec((None, F, tn), lambda j, i, te, nv: (te[i], 0, j)),
                      pl.BlockSpec((tm, 128), lambda j, i, te, nv: (i, 0))],
            out_specs=pl.BlockSpec((tm, tn), lambda j, i, te, nv: (i, j)),
            scratch_shapes=[pltpu.VMEM((F, tn), BF16)],
        ),
        out_shape=jax.ShapeDtypeStruct((P, N), F32),
        compiler_params=_params(("arbitrary", "arbitrary")),
        name="moe_down",
    )(tile_expert, n_valid, act, w2, row_w)


def moe_dispatch(top_i, top_w, tm):
    M = top_i.shape[0]
    A = M * TOP_K
    P = A + N_EXPERTS * tm
    e_flat = top_i.reshape(A)
    onehot = (e_flat[:, None] == jnp.arange(N_EXPERTS, dtype=I32)[None, :]).astype(I32)
    csum = jnp.cumsum(onehot, axis=0)
    counts = csum[-1]
    padded = ((counts + tm - 1) // tm) * tm
    pend = jnp.cumsum(padded)
    pstart = pend - padded
    dest = jnp.sum(onehot * (pstart[None, :] + csum - onehot), axis=1)
    src_tok = jnp.zeros((P,), I32).at[dest].set(jnp.arange(A, dtype=I32) // TOP_K)
    row_w = jnp.zeros((P,), F32).at[dest].set(top_w.reshape(A))
    pos = dest.reshape(M, TOP_K)
    tile_start = jnp.arange(P // tm, dtype=I32) * tm
    tile_expert = jnp.minimum(jnp.sum((tile_start[:, None] >= pend[None, :]).astype(I32), axis=1), N_EXPERTS - 1)
    n_valid = (pend[-1:] // tm).astype(I32)
    last_e = tile_expert[jnp.maximum(n_valid[0] - 1, 0)]
    tile_expert = jnp.where(tile_start // tm < n_valid[0], tile_expert, last_e)
    return src_tok, row_w, pos, tile_expert, n_valid


def _nsa_prep_kernel(kc_ref, vc_ref, ks_ref, vs_ref, kw_ref, vw_ref, kg_ref, raw_ref, kso_ref, vso_ref, kwo_ref, vwo_ref):
    raw_ref[0] = kc_ref[...]
    raw_ref[1] = vc_ref[...]
    kg = kg_ref[...]
    kso_ref[...] = _rms(ks_ref[...], kg[1:2]).astype(BF16)
    vso_ref[...] = vs_ref[...].astype(BF16)
    kwo_ref[...] = _rms(kw_ref[...], kg[2:3]).astype(BF16)
    vwo_ref[...] = vw_ref[...].astype(BF16)


def nsa_prep(z, k_gain, B, S, ts=512):
    G = NSA_KV_GROUPS
    nt = S // ts
    cb = COL_KV_A // HEAD_DIM

    def col(br, kvi):
        return pl.BlockSpec((ts, HEAD_DIM), lambda b, g, t: (b * nt + t, cb + br * 4 + kvi * 2 + g))

    kv_out = pl.BlockSpec((None, None, ts, HEAD_DIM), lambda b, g, t: (b, g, t, 0))
    kv_shape = jax.ShapeDtypeStruct((B, G, S, HEAD_DIM), BF16)
    return pl.pallas_call(
        _nsa_prep_kernel,
        grid=(B, G, nt),
        in_specs=[col(0, 0), col(0, 1), col(1, 0), col(1, 1), col(2, 0), col(2, 1),
                  pl.BlockSpec((3, HEAD_DIM), lambda b, g, t: (0, 0))],
        out_specs=[pl.BlockSpec((None, 2, None, ts, HEAD_DIM), lambda b, g, t: (b, 0, g, t, 0)),
                   kv_out, kv_out, kv_out, kv_out],
        out_shape=[jax.ShapeDtypeStruct((B, 2, G, S, HEAD_DIM), F32), kv_shape, kv_shape, kv_shape, kv_shape],
        compiler_params=_params(("parallel", "parallel", "parallel")),
        name="nsa_prep",
    )(z, z, z, z, z, z, k_gain)


def _nsa_compress_kernel(r_ref, pos_ref, w1_ref, w2_ref, kg_ref, o_ref):
    kv = pl.program_id(1)
    half = (CMP_BLOCK // 2) * HEAD_DIM
    r = r_ref[...]
    pos = pos_ref[...]
    n = r.shape[0]
    first = _dot((r + pos[:, :half]).astype(BF16), w1_ref[:half, :].astype(BF16))
    second = _dot((r + pos[:, half:]).astype(BF16), w1_ref[half:, :].astype(BF16))
    pre = first + pltpu.roll(second, n - 1, axis=0)
    hid = jax.nn.gelu(pre)
    comp = _dot(hid.astype(BF16), w2_ref[...].astype(BF16))
    o_ref[...] = jnp.where(kv == 0, _rms(comp, kg_ref[...]), comp).astype(o_ref.dtype)


def nsa_compress(raw, cmp_pos, cmp_w1, cmp_w2, k_gain0, B, S):
    G = NSA_KV_GROUPS
    n = S // CMP_STRIDE
    feat = CMP_STRIDE * HEAD_DIM
    r = raw.reshape(B, 2, G, n, feat)
    pos = cmp_pos.reshape(2, 1, CMP_BLOCK * HEAD_DIM)
    return pl.pallas_call(
        _nsa_compress_kernel,
        grid=(B, 2, G),
        in_specs=[pl.BlockSpec((None, None, None, n, feat), lambda b, kv, g: (b, kv, g, 0, 0)),
                  pl.BlockSpec((None, 1, CMP_BLOCK * HEAD_DIM), lambda b, kv, g: (kv, 0, 0)),
                  pl.BlockSpec((None, CMP_BLOCK * HEAD_DIM, HEAD_DIM), lambda b, kv, g: (kv, 0, 0)),
                  pl.BlockSpec((None, HEAD_DIM, HEAD_DIM), lambda b, kv, g: (kv, 0, 0)),
                  pl.BlockSpec((1, HEAD_DIM), lambda b, kv, g: (0, 0))],
        out_specs=pl.BlockSpec((None, None, None, n, HEAD_DIM), lambda b, kv, g: (b, kv, g, 0, 0)),
        out_shape=jax.ShapeDtypeStruct((B, 2, G, n, HEAD_DIM), BF16),
        compiler_params=_params(("parallel", "parallel", "parallel")),
        name="nsa_compress",
    )(r, pos, cmp_w1, cmp_w2, k_gain0.reshape(1, HEAD_DIM))


def _rep(x, n):
    return x if n == HEAD_DIM else jnp.concatenate([x] * (n // HEAD_DIM), axis=1)


def _flash_step(qb, kt, vt, bias, m_ref, l_ref, acc_ref):
    nk = kt.shape[0]
    rows = qb.shape[0]
    s = _dot_nt(qb, kt)
    s = (s.reshape(rows // Q_BLOCK, Q_BLOCK, nk) + bias[None]).reshape(rows, nk)
    m_prev = m_ref[...]
    m_new = jnp.maximum(m_prev, jnp.max(s, axis=-1, keepdims=True))
    alpha = jnp.exp2(m_prev - m_new)
    p = jnp.exp2(s - _rep(m_new, nk))
    l_ref[...] = alpha * l_ref[...] + jnp.sum(p, axis=-1, keepdims=True)
    acc_ref[...] = alpha * acc_ref[...] + _dot(p.astype(BF16), vt)
    m_ref[...] = m_new


def _nsa_kernel(q_ref, sm_ref, kc_ref, vc_ref, ks_ref, vs_ref, kw_ref, vw_ref, qg_ref, og_ref, c2s_ref, exp_ref,
                o_ref, selb_ref, m_ref, l_ref, acc_ref, *, k_top):
    Hg = NSA_GROUP_SIZE
    rows = Hg * Q_BLOCK
    c = pl.program_id(2)
    t0 = c * Q_BLOCK
    n_cmp = kc_ref.shape[0]
    n_sel = c2s_ref.shape[0]

    q = q_ref[...]
    qs = jnp.concatenate([q[:, h * HEAD_DIM:(h + 1) * HEAD_DIM] for h in range(Hg)], axis=0)
    qb = (_rms(qs, qg_ref[...]) * (SCALE * LOG2E)).astype(BF16)

    tq = t0 + lax.broadcasted_iota(I32, (Q_BLOCK, n_cmp), 0)
    c_end = lax.broadcasted_iota(I32, (Q_BLOCK, n_cmp), 1) * CMP_STRIDE + (CMP_BLOCK - 1)
    sc = _dot_nt(qb, kc_ref[...]).reshape(Hg, Q_BLOCK, n_cmp)
    sc = jnp.where((c_end <= tq)[None], sc, -jnp.inf)
    mc = jnp.max(sc, axis=-1, keepdims=True)
    mc = jnp.where(mc > -jnp.inf, mc, 0.0)
    pc = jnp.exp2(sc - mc)
    dc = jnp.sum(pc, axis=-1, keepdims=True)
    pc = pc / jnp.where(dc > 0, dc, 1.0)
    o_cmp = _dot(pc.reshape(rows, n_cmp).astype(BF16), vc_ref[...])

    psum = jnp.sum(pc, axis=0)
    c2s = c2s_ref[...]
    p_hi = psum.astype(BF16)
    rem = psum - p_hi.astype(F32)
    p_mid = rem.astype(BF16)
    p_lo = (rem - p_mid.astype(F32)).astype(BF16)
    imp = _dot_nt(c2s, p_hi) + _dot_nt(c2s, p_mid) + _dot_nt(c2s, p_lo)

    jj = lax.broadcasted_iota(I32, (n_sel, Q_BLOCK), 0)
    cur = lax.shift_right_logical(t0 + lax.broadcasted_iota(I32, (n_sel, Q_BLOCK), 1), 6)
    valid = jj <= cur
    forced = valid & ((jj == 0) | (jj > cur - SEL_LOCAL))
    score = jnp.where(forced, FORCE, jnp.where(valid, imp, -FORCE))
    rank = jnp.zeros((n_sel, Q_BLOCK), F32)
    for j2 in range(n_sel):
        other = score[j2:j2 + 1, :]
        tie = jnp.where(jj > j2, 1.0, 0.0)
        rank = rank + jnp.where(other > score, 1.0, jnp.where(other == score, tie, 0.0))
    sel_t = jnp.where(rank < k_top, 1.0, 0.0)
    if n_sel < Q_BLOCK:
        sel_t = jnp.concatenate([sel_t, jnp.zeros((Q_BLOCK - n_sel, Q_BLOCK), F32)], axis=0)
    sel = sel_t.T[:, :n_sel].astype(BF16)
    sel_keys = _dot(sel, exp_ref[...])
    for i in range(selb_ref.shape[0]):
        selb_ref[i] = (sel_keys[:, i * SEL_TILE:(i + 1) * SEL_TILE] - 1.0) * (-NEG)

    def reset():
        m_ref[...] = jnp.full(m_ref.shape, NEG, F32)
        l_ref[...] = jnp.zeros(l_ref.shape, F32)
        acc_ref[...] = jnp.zeros(acc_ref.shape, F32)

    reset()
    tq_s = t0 + lax.broadcasted_iota(I32, (Q_BLOCK, SEL_TILE), 0)
    kk_s = lax.broadcasted_iota(I32, (Q_BLOCK, SEL_TILE), 1)

    def sel_body(i, carry):
        k0 = pl.multiple_of(i * SEL_TILE, SEL_TILE)
        bias = selb_ref[i] + jnp.where(kk_s + k0 <= tq_s, 0.0, NEG)
        _flash_step(qb, ks_ref[pl.ds(k0, SEL_TILE), :], vs_ref[pl.ds(k0, SEL_TILE), :], bias, m_ref, l_ref, acc_ref)
        return carry

    lax.fori_loop(0, (t0 + Q_BLOCK + SEL_TILE - 1) // SEL_TILE, sel_body, 0)
    o_sel = acc_ref[...] / l_ref[...]

    wk = WINDOW + Q_BLOCK
    w0 = pl.multiple_of(jnp.maximum(t0 - WINDOW, 0), Q_BLOCK)
    tq_w = t0 + lax.broadcasted_iota(I32, (Q_BLOCK, wk), 0)
    wpos = w0 + lax.broadcasted_iota(I32, (Q_BLOCK, wk), 1)
    w_bias = jnp.where((wpos <= tq_w) & (wpos > tq_w - WINDOW), 0.0, NEG)
    sw = _dot_nt(qb, kw_ref[pl.ds(w0, wk), :]).reshape(Hg, Q_BLOCK, wk) + w_bias[None]
    pw = jnp.exp2(sw - jnp.max(sw, axis=-1, keepdims=True))
    lw = jnp.sum(pw, axis=-1, keepdims=True).reshape(rows, 1)
    o_win = _dot(pw.reshape(rows, wk).astype(BF16), vw_ref[pl.ds(w0, wk), :]) / lw

    gates = jax.nn.sigmoid(sm_ref[...])
    og = og_ref[...]
    for h in range(Hg):
        r = slice(h * Q_BLOCK, (h + 1) * Q_BLOCK)
        o = (gates[:, h:h + 1] * o_cmp[r] + gates[:, Hg + h:Hg + h + 1] * o_sel[r]
             + gates[:, 2 * Hg + h:2 * Hg + h + 1] * o_win[r])
        o_ref[:, h * HEAD_DIM:(h + 1) * HEAD_DIM] = _rms(o, og).astype(o_ref.dtype)


def nsa_attention(z, comp, ks, vs, kw, vw, q_gain, out_gain, B, S):
    G, Hg = NSA_KV_GROUPS, NSA_GROUP_SIZE
    nq = S // Q_BLOCK
    n_cmp = S // CMP_STRIDE
    n_sel = S // SEL_BLOCK
    k_top = min(SEL_TOP_N, n_sel)
    c_start = np.arange(n_cmp)[:, None] * CMP_STRIDE
    s_start = np.arange(n_sel)[None, :] * SEL_BLOCK
    overlap = np.minimum(c_start + CMP_BLOCK, s_start + SEL_BLOCK) - np.maximum(c_start, s_start)
    assert n_sel <= Q_BLOCK
    c2s = jnp.asarray((np.clip(overlap, 0, None) / CMP_STRIDE).T, dtype=BF16)
    expand = jnp.asarray(np.arange(S)[None, :] // SEL_BLOCK == np.arange(n_sel)[:, None], dtype=BF16)
    rows = Hg * Q_BLOCK
    wq = Hg * HEAD_DIM
    kv_spec = pl.BlockSpec((None, None, S, HEAD_DIM), lambda b, g, c: (b, g, 0, 0))
    const = lambda b, g, c: (0, 0)
    return pl.pallas_call(
        functools.partial(_nsa_kernel, k_top=k_top),
        grid=(B, G, nq),
        in_specs=[pl.BlockSpec((Q_BLOCK, wq), lambda b, g, c: (b * nq + c, COL_Q_A // wq + g)),
                  pl.BlockSpec((Q_BLOCK, HEAD_DIM), lambda b, g, c: (b * nq + c, COL_SMALL // HEAD_DIM + g)),
                  pl.BlockSpec((None, None, None, n_cmp, HEAD_DIM), lambda b, g, c: (b, 0, g, 0, 0)),
                  pl.BlockSpec((None, None, None, n_cmp, HEAD_DIM), lambda b, g, c: (b, 1, g, 0, 0)),
                  kv_spec, kv_spec, kv_spec, kv_spec,
                  pl.BlockSpec((1, HEAD_DIM), const), pl.BlockSpec((1, HEAD_DIM), const),
                  pl.BlockSpec((n_sel, n_cmp), const), pl.BlockSpec((n_sel, S), const)],
        out_specs=pl.BlockSpec((Q_BLOCK, wq), lambda b, g, c: (b * nq + c, g)),
        out_shape=jax.ShapeDtypeStruct((B * S, D_NSA), BF16),
        scratch_shapes=[pltpu.VMEM((S // SEL_TILE, Q_BLOCK, SEL_TILE), F32),
                        pltpu.VMEM((rows, HEAD_DIM), F32), pltpu.VMEM((rows, HEAD_DIM), F32),
                        pltpu.VMEM((rows, HEAD_DIM), F32)],
        compiler_params=_params(("parallel", "parallel", "arbitrary")),
        name="nsa_attention",
    )(z, z, comp, comp, ks, vs, kw, vw, q_gain.reshape(1, HEAD_DIM), out_gain.reshape(1, HEAD_DIM), c2s, expand)


def _gdn_chunk_kernel(xc_ref, xh_ref, sm_ref, cw_ref, alog_ref, dtb_ref, rep_ref, unfold_ref, u_ref, wq_ref, ak_ref,
                      egl_ref):
    for b in range(xc_ref.shape[0]):
        _gdn_chunk_body(xc_ref.at[b], xh_ref.at[b], sm_ref.at[b], cw_ref, alog_ref, dtb_ref, rep_ref, unfold_ref,
                        u_ref.at[b], wq_ref.at[b], ak_ref.at[b], egl_ref.at[b])


def _gdn_chunk_body(xc_ref, xh_ref, sm_ref, cw_ref, alog_ref, dtb_ref, rep_ref, unfold_ref, u_ref, wq_ref, ak_ref,
                    egl_ref):
    C = GDN_CHUNK
    PK = GDN_PACK
    R = PK * C
    n = pl.program_id(0)
    xp = jnp.concatenate([jnp.where(n > 0, xh_ref[...], 0.0), xc_ref[...]], axis=0)
    cw = cw_ref[...]
    y = None
    for j in range(GDN_CONV):
        shift = GDN_CONV - 1 - j
        xs = xp if shift == 0 else pltpu.roll(xp, shift, axis=0)
        term = xs[8:] * cw[j:j + 1, :]
        y = term if y is None else y + term
    y = y * jax.nn.sigmoid(y)

    sm = sm_ref[...]
    beta = jax.nn.sigmoid(sm)
    g = -jnp.exp(alog_ref[...]) * jax.nn.softplus(sm + dtb_ref[...])
    row = lax.broadcasted_iota(I32, g.shape, 0)
    gc = g
    d = 1
    while d < C:
        gc = gc + jnp.where(row >= d, pltpu.roll(gc, d, axis=0), 0.0)
        d *= 2
    g_last = gc[C - 1:C, :]

    ri = lax.broadcasted_iota(I32, (R, R), 0)
    ci = lax.broadcasted_iota(I32, (R, R), 1)
    same = lax.shift_right_logical(ri, 6) == lax.shift_right_logical(ci, 6)
    tri = same & (ri >= ci)
    strict = same & (ri > ci)
    SUB = GDN_SUB
    nb = R // SUB
    same16 = lax.shift_right_logical(ri, 4) == lax.shift_right_logical(ci, 4)
    same32 = lax.shift_right_logical(ri, 5) == lax.shift_right_logical(ci, 5)
    off32 = same32 & jnp.logical_not(same16)
    off64 = jnp.logical_not(same32)
    NG = GDN_HEADS // PK
    W4 = NG * SUB
    row_c = lax.broadcasted_iota(I32, (R, W4), 0) & (SUB - 1)
    col_s = lax.broadcasted_iota(I32, (nb, W4), 1) & (SUB - 1)

    groups = []
    for grp in range(NG):
        heads = [grp * PK + i for i in range(PK)]

        def stack(off):
            return jnp.concatenate([y[:, off + h * HEAD_DIM: off + (h + 1) * HEAD_DIM] for h in heads], axis=0)

        def col(x, base):
            return jnp.concatenate([x[:, base + h: base + h + 1] for h in heads], axis=0)

        q4 = stack(0)
        k4 = stack(D_GDN)
        v4 = stack(2 * D_GDN)
        q4 = q4 * lax.rsqrt(jnp.sum(q4 * q4, axis=-1, keepdims=True) + EPS) * SCALE
        k4 = k4 * lax.rsqrt(jnp.sum(k4 * k4, axis=-1, keepdims=True) + EPS)
        beta4 = col(beta, SM_BETA)
        gc4 = col(gc, SM_A)
        gl4 = jnp.concatenate([jnp.broadcast_to(g_last[:, SM_A + h: SM_A + h + 1], (C, 1)) for h in heads], axis=0)

        gb = jnp.broadcast_to(gc4, (R, R))
        decay = jnp.exp(jnp.where(tri, gb - gb.T, -jnp.inf))
        kb4 = k4 * beta4
        k4b = k4.astype(BF16)
        a = jnp.where(strict, _dot_nt(kb4.astype(BF16), k4b) * decay, 0.0)
        attn = _dot_nt(q4.astype(BF16), k4b) * decay
        at = a.T
        dct = jnp.concatenate([at[SUB * b:SUB * (b + 1), SUB * b:SUB * (b + 1)] for b in range(nb)], axis=0)
        groups.append((heads, q4, k4, v4, beta4, gc4, gl4, kb4, a, attn, dct))

    coef = _dot3(jnp.concatenate([g[-1] for g in groups], axis=1), rep_ref[...])
    t4 = jnp.where(row_c == (lax.broadcasted_iota(I32, (R, W4), 1) & (SUB - 1)), 1.0, 0.0)
    for i in range(1, SUB):
        s = jnp.sum((coef[:, i * HEAD_DIM: i * HEAD_DIM + W4] * t4).reshape(nb, SUB, W4), axis=1)
        new = jnp.where(col_s == i, 1.0, 0.0) - s
        t4 = jnp.where(row_c == i, jnp.broadcast_to(new[:, None, :], (nb, SUB, W4)).reshape(R, W4), t4)
    t_tiled = _dot3(t4, unfold_ref[...])

    for gi, (heads, q4, k4, v4, beta4, gc4, gl4, kb4, a, attn, _) in enumerate(groups):
        t16 = jnp.where(same16, t_tiled[:, gi * R:(gi + 1) * R], 0.0)
        t16b = t16.astype(BF16)
        a32 = jnp.where(off32, a, 0.0).astype(BF16)
        t32 = t16 - _dot(_dot(t16b, a32).astype(BF16), t16b)
        t32b = t32.astype(BF16)
        a64 = jnp.where(off64, a, 0.0).astype(BF16)
        t64 = t32 - _dot(_dot(t32b, a64).astype(BF16), t32b)
        eg = jnp.exp(gc4)
        rhs = jnp.concatenate([v4 * beta4, kb4 * eg], axis=1)
        rhs = _dot(t64.astype(BF16), rhs.astype(BF16))
        qd4 = q4 * eg
        kdt = (k4 * jnp.exp(gl4 - gc4)).T
        for i, h in enumerate(heads):
            r = slice(i * C, (i + 1) * C)
            u_ref[h] = rhs[r, :HEAD_DIM]
            wq_ref[h] = jnp.concatenate([rhs[r, HEAD_DIM:], qd4[r]], axis=0).astype(BF16)
            ak_ref[h] = jnp.concatenate([attn[r, r], kdt[:, r]], axis=0).astype(BF16)
            egl_ref[h] = jnp.broadcast_to(jnp.exp(g_last[:, SM_A + h: SM_A + h + 1]), (8, HEAD_DIM))


def gdn_chunks(z, conv_w, a_log, dt_bias, B, S):
    C, H = GDN_CHUNK, GDN_HEADS
    N = S // C
    W = 3 * D_GDN
    pad = lambda v: jnp.zeros((1, HEAD_DIM), F32).at[0, SM_A:SM_A + H].set(v)
    out5 = lambda r, cdim: pl.BlockSpec((B, None, H, r, cdim), lambda n: (0, n, 0, 0, 0))
    NG, SUB, R = H // GDN_PACK, GDN_SUB, GDN_PACK * C
    W4 = NG * SUB
    src = np.arange(W4)[:, None]
    dst = np.arange(SUB * HEAD_DIM)[None, :]
    lane = dst % HEAD_DIM
    rep = (lane < W4) & (src // SUB == lane // SUB) & (src % SUB == dst // HEAD_DIM)
    dst = np.arange(NG * R)[None, :]
    unfold = (src // SUB == dst // R) & (src % SUB == dst % SUB)
    z3 = z.reshape(B, S, z.shape[1])
    const = lambda n: (0, 0)
    return pl.pallas_call(
        _gdn_chunk_kernel,
        grid=(N,),
        in_specs=[pl.BlockSpec((B, C, W), lambda n: (0, n, 0)),
                  pl.BlockSpec((B, 8, W), lambda n: (0, jnp.maximum(n * (C // 8) - 1, 0), 0)),
                  pl.BlockSpec((B, C, HEAD_DIM), lambda n: (0, n, COL_SMALL // HEAD_DIM)),
                  pl.BlockSpec((GDN_CONV, W), const),
                  pl.BlockSpec((1, HEAD_DIM), const),
                  pl.BlockSpec((1, HEAD_DIM), const),
                  pl.BlockSpec((W4, SUB * HEAD_DIM), const),
                  pl.BlockSpec((W4, NG * R), const)],
        out_specs=[out5(C, HEAD_DIM), out5(2 * C, HEAD_DIM), out5(C + HEAD_DIM, C), out5(8, HEAD_DIM)],
        out_shape=[jax.ShapeDtypeStruct((B, N, H, C, HEAD_DIM), F32),
                   jax.ShapeDtypeStruct((B, N, H, 2 * C, HEAD_DIM), BF16),
                   jax.ShapeDtypeStruct((B, N, H, C + HEAD_DIM, C), BF16),
                   jax.ShapeDtypeStruct((B, N, H, 8, HEAD_DIM), F32)],
        compiler_params=_params(("parallel",)),
        name="gdn_chunks",
    )(z3, z3, z3, conv_w, pad(a_log), pad(dt_bias), jnp.asarray(rep, dtype=BF16), jnp.asarray(unfold, dtype=BF16))


def _gdn_scan_kernel(u_ref, wq_ref, ak_ref, egl_ref, z_ref, og_ref, o_ref, s_ref):
    C = GDN_CHUNK

    @pl.when(pl.program_id(0) == 0)
    def _():
        s_ref[...] = jnp.zeros(s_ref.shape, F32)

    og = og_ref[...]
    for b in range(u_ref.shape[0]):
        for h in range(GDN_HEADS):
            state = s_ref[b, h]
            ws = _dot(wq_ref[b, h], state.astype(BF16))
            v_new = u_ref[b, h] - ws[:C]
            av = _dot(ak_ref[b, h], v_new.astype(BF16))
            o = ws[C:] + av[:C]
            decayed = (state.reshape(HEAD_DIM // 8, 8, HEAD_DIM) * egl_ref[b, h][None]).reshape(HEAD_DIM, HEAD_DIM)
            s_ref[b, h] = decayed + av[C:]
            zh = z_ref[b, :, h * HEAD_DIM:(h + 1) * HEAD_DIM]
            o_ref[b, :, h * HEAD_DIM:(h + 1) * HEAD_DIM] = (_rms(o, og) * (zh * jax.nn.sigmoid(zh))).astype(o_ref.dtype)


def gdn_scan(u, wq, ak, egl, z, out_gain, B, S):
    C, H = GDN_CHUNK, GDN_HEADS
    N = S // C
    in5 = lambda r, cdim: pl.BlockSpec((B, None, H, r, cdim), lambda n: (0, n, 0, 0, 0))
    y = pl.pallas_call(
        _gdn_scan_kernel,
        grid=(N,),
        in_specs=[in5(C, HEAD_DIM), in5(2 * C, HEAD_DIM), in5(C + HEAD_DIM, C), in5(8, HEAD_DIM),
                  pl.BlockSpec((B, C, D_GDN), lambda n: (0, n, COL_Z_B // D_GDN)),
                  pl.BlockSpec((1, HEAD_DIM), lambda n: (0, 0))],
        out_specs=pl.BlockSpec((B, C, D_GDN), lambda n: (0, n, 0)),
        out_shape=jax.ShapeDtypeStruct((B, S, D_GDN), BF16),
        scratch_shapes=[pltpu.VMEM((B, H, HEAD_DIM, HEAD_DIM), F32)],
        compiler_params=_params(("arbitrary",)),
        name="gdn_scan",
    )(u, wq, ak, egl, z.reshape(B, S, z.shape[1]), out_gain.reshape(1, HEAD_DIM))
    return y.reshape(B * S, D_GDN)


def _xa_kv_kernel(mem_ref, g_ref, w_ref, kg_ref, k_ref, v_ref):
    kv = _dot(_rms(mem_ref[...], g_ref[...]).astype(BF16), w_ref[...])
    for h in range(XA_HEADS):
        k_ref[h] = _rms(kv[:, h * HEAD_DIM:(h + 1) * HEAD_DIM], kg_ref[...]).astype(BF16)
        v_ref[h] = kv[:, D_XA + h * HEAD_DIM: D_XA + (h + 1) * HEAD_DIM].astype(BF16)


def xa_kv(mem, mem_norm, wkv_bf16, k_gain):
    B, Mm, D = mem.shape
    spec = pl.BlockSpec((None, XA_HEADS, Mm, HEAD_DIM), lambda b: (b, 0, 0, 0))
    shape = jax.ShapeDtypeStruct((B, XA_HEADS, Mm, HEAD_DIM), BF16)
    return pl.pallas_call(
        _xa_kv_kernel,
        grid=(B,),
        in_specs=[pl.BlockSpec((None, Mm, D), lambda b: (b, 0, 0)), pl.BlockSpec((1, D), lambda b: (0, 0)),
                  pl.BlockSpec((D, 2 * D_XA), lambda b: (0, 0)), pl.BlockSpec((1, HEAD_DIM), lambda b: (0, 0))],
        out_specs=[spec, spec],
        out_shape=[shape, shape],
        compiler_params=_params(("parallel",)),
        name="xa_kv",
    )(mem, mem_norm.reshape(1, D), wkv_bf16, k_gain.reshape(1, HEAD_DIM))


def _xa_kernel(x_ref, g_ref, wq_ref, k_ref, v_ref, qg_ref, wo_ref, fg_ref, *rest, route):
    if route:
        rwh_ref, rwl_ref, o_ref, h_ref, idx_ref, wt_ref = rest
    else:
        o_ref, h_ref = rest
    x = x_ref[...]
    q = _dot(_rms(x, g_ref[...]).astype(BF16), wq_ref[...])
    outs = []
    for h in range(XA_HEADS):
        qh = _rms(q[:, h * HEAD_DIM:(h + 1) * HEAD_DIM], qg_ref[...]).astype(BF16)
        s = _dot_nt(qh, k_ref[h]) * SCALE
        p = jnp.exp(s - jnp.max(s, axis=-1, keepdims=True))
        p = p / jnp.sum(p, axis=-1, keepdims=True)
        outs.append(_dot(p.astype(BF16), v_ref[h]))
    o = jnp.concatenate(outs, axis=1).astype(BF16)
    x_new = x + _dot(o, wo_ref[...])
    o_ref[...] = x_new
    hn = _rms(x_new, fg_ref[...])
    h_ref[...] = hn.astype(h_ref.dtype)
    if route:
        idx_ref[...], wt_ref[...] = _route(hn, rwh_ref[...], rwl_ref[...])


def cross_attention(x, xa_norm, wq_bf16, k, v, q_gain, wo_bf16, ffn_norm, router_w, B, S, tm=256):
    M, D = x.shape
    Mm = k.shape[2]
    per_b = S // tm
    kv_spec = pl.BlockSpec((None, XA_HEADS, Mm, HEAD_DIM), lambda i: (i // per_b, 0, 0, 0))
    const = lambda i: (0, 0)
    row = lambda i: (i, 0)
    route = router_w is not None
    in_specs = [pl.BlockSpec((tm, D), row), pl.BlockSpec((1, D), const),
                pl.BlockSpec((D, D_XA), const), kv_spec, kv_spec,
                pl.BlockSpec((1, HEAD_DIM), const), pl.BlockSpec((D_XA, D), const), pl.BlockSpec((1, D), const)]
    args = [x, xa_norm.reshape(1, D), wq_bf16, k, v, q_gain.reshape(1, HEAD_DIM), wo_bf16, ffn_norm.reshape(1, D)]
    out_specs = [pl.BlockSpec((tm, D), row), pl.BlockSpec((tm, D), row)]
    out_shape = [jax.ShapeDtypeStruct((M, D), F32), jax.ShapeDtypeStruct((M, D), BF16)]
    if route:
        rw = jnp.pad(router_w, ((0, 0), (0, 128 - N_EXPERTS)))
        rw_hi = rw.astype(BF16)
        in_specs += [pl.BlockSpec((D, 128), const), pl.BlockSpec((D, 128), const)]
        args += [rw_hi, (rw - rw_hi.astype(F32)).astype(BF16)]
        out_specs += [pl.BlockSpec((tm, 128), row), pl.BlockSpec((tm, 128), row)]
        out_shape += [jax.ShapeDtypeStruct((M, 128), I32), jax.ShapeDtypeStruct((M, 128), F32)]
    return pl.pallas_call(
        functools.partial(_xa_kernel, route=route),
        grid=(M // tm,),
        in_specs=in_specs,
        out_specs=out_specs,
        out_shape=out_shape,
        compiler_params=_params(("parallel",)),
        name="cross_attention",
    )(*args)


def _reorder_w_in(w_in):
    o = np.cumsum((D_NSA, 6 * D_KV_NSA, 3 * NSA_HEADS, 3 * D_GDN, GDN_HEADS, GDN_HEADS, D_GDN))
    q_a, kv_a, gate_a, qkv_b = (0, o[0]), (o[0], o[1]), o[1], (o[2], o[3])
    beta_b, a_b, z_b = o[3], o[4], (o[5], o[6])
    Hg = NSA_GROUP_SIZE
    src = np.zeros((SMALL_W,), np.int64)
    used = np.zeros((SMALL_W,), bool)
    for g in range(NSA_KV_GROUPS):
        for br in range(3):
            for h in range(Hg):
                src[g * HEAD_DIM + br * Hg + h] = gate_a + (g * Hg + h) * 3 + br
                used[g * HEAD_DIM + br * Hg + h] = True
    for h in range(GDN_HEADS):
        src[SM_BETA + h] = beta_b + h
        src[SM_A + h] = a_b + h
        used[SM_BETA + h] = used[SM_A + h] = True
    small = jnp.where(jnp.asarray(used)[None, :], jnp.take(w_in, jnp.asarray(src), axis=1), 0.0)
    parts = [w_in[:, qkv_b[0]:qkv_b[1]], w_in[:, q_a[0]:q_a[1]], w_in[:, z_b[0]:z_b[1]], w_in[:, kv_a[0]:kv_a[1]], small]
    return jnp.concatenate([p.astype(BF16) for p in parts], axis=1)


def mixer_layer(x2, B, S, attn_norm, w_in, nsa_q_gain, nsa_k_gain, nsa_cmp_pos, nsa_cmp_w1, nsa_cmp_w2, nsa_out_gain,
                gdn_conv_w, gdn_A_log, gdn_dt_bias, gdn_out_gain, w_out_stack, layer, tm=1024):
    h = rmsnorm_bf16(x2, attn_norm)
    z = matmul_cols(h, _reorder_w_in(w_in), tm=tm, tn=1024, name="in_proj")
    raw, ks, vs, kw, vw = nsa_prep(z, nsa_k_gain, B, S)
    comp = nsa_compress(raw, nsa_cmp_pos, nsa_cmp_w1, nsa_cmp_w2, nsa_k_gain[0], B, S)
    y_a = nsa_attention(z, comp, ks, vs, kw, vw, nsa_q_gain, nsa_out_gain, B, S)
    u, wq, ak, egl = gdn_chunks(z, gdn_conv_w, gdn_A_log, gdn_dt_bias, B, S)
    y_b = gdn_scan(u, wq, ak, egl, z, gdn_out_gain, B, S)
    return matmul2_residual(y_a, y_b, w_out_stack, layer, x2, tm=tm, tn=512, name="out_proj")


def xa_layer(x2, mem, B, S, xa_norm, mem_norm, xa_wq, xa_wkv, xa_q_gain, xa_k_gain, xa_wo, ffn_norm, router_w=None):
    k, v = xa_kv(mem, mem_norm, xa_wkv.astype(BF16), xa_k_gain)
    return cross_attention(x2, xa_norm, xa_wq.astype(BF16), k, v, xa_q_gain, xa_wo.astype(BF16), ffn_norm, router_w,
                           B, S)


def dense_ffn_layer(x2, h, w13, w2, tm=1024):
    act = swiglu_up(h, w13, tm=min(2 * tm, x2.shape[0]), tn=256)
    return down_proj_residual(act, w2, x2, tm=min(2 * tm, x2.shape[0]), tn=1024, tk=512, tk_tail=256)


def moe_ffn_layer(x2, h, idx, wts, w13, w2, tm=512, tn=512):
    src_tok, row_w, pos, tile_expert, n_valid = moe_dispatch(idx[:, :TOP_K], wts[:, :TOP_K], tm)
    P = src_tok.shape[0]
    half = (P // tm // 2) * tm
    act = moe_up(jnp.take(h, src_tok[:half], axis=0), w13, tile_expert, n_valid, tm=tm, tn=tn, total_rows=P)
    act = moe_up(jnp.take(h, src_tok[half:], axis=0), w13, tile_expert, n_valid, tm=tm, tn=tn, tile_off=half // tm,
                 total_rows=P, prev=act)
    row_w_rep = jnp.broadcast_to(row_w[:, None], (row_w.shape[0], 128))
    out = moe_down(act, w2, row_w_rep, tile_expert, n_valid, tm=tm, tn=tn)
    return x2 + (jnp.take(out, pos[:, 0], axis=0) + jnp.take(out, pos[:, 1], axis=0))


def kernel(x, mem, attn_norm, w_in, nsa_q_gain, nsa_k_gain, nsa_cmp_pos, nsa_cmp_w1, nsa_cmp_w2, nsa_out_gain, gdn_conv_w, gdn_A_log, gdn_dt_bias, gdn_out_gain, w_out, xa_norm, mem_norm, xa_wq, xa_wkv, xa_q_gain, xa_k_gain, xa_wo, ffn_norm, dense_w13, dense_w2, router_w, moe_w13, moe_w2):
    B, S, D = x.shape
    x2 = x.reshape(B * S, D)
    for l in range(attn_norm.shape[0]):
        x2 = mixer_layer(x2, B, S, attn_norm[l], w_in[l], nsa_q_gain[l], nsa_k_gain[l], nsa_cmp_pos[l], nsa_cmp_w1[l],
                         nsa_cmp_w2[l], nsa_out_gain[l], gdn_conv_w[l], gdn_A_log[l], gdn_dt_bias[l], gdn_out_gain[l],
                         w_out, l)
        xa_args = (x2, mem, B, S, xa_norm[l], mem_norm[l], xa_wq[l], xa_wkv[l], xa_q_gain[l], xa_k_gain[l], xa_wo[l],
                   ffn_norm[l])
        if l % 2 == 0:
            x2, h = xa_layer(*xa_args)
            x2 = dense_ffn_layer(x2, h, dense_w13[l // 2], dense_w2[l // 2])
        else:
            x2, h, idx, wts = xa_layer(*xa_args, router_w[l // 2])
            x2 = moe_ffn_layer(x2, h, idx, wts, moe_w13[l // 2], moe_w2[l // 2])
    return x2.reshape(B, S, D)
```

```python
import functools

import jax
import jax.numpy as jnp
import numpy as np
from jax import lax
from jax.experimental import pallas as pl
from jax.experimental.pallas import tpu as pltpu

F32 = jnp.float32
BF16 = jnp.bfloat16
I32 = jnp.int32

D_MODEL = 4096
HEAD_DIM = 128
EPS = 1e-6
SCALE = HEAD_DIM ** -0.5

NSA_HEADS = 16
NSA_KV_GROUPS = 2
NSA_GROUP_SIZE = NSA_HEADS // NSA_KV_GROUPS
CMP_BLOCK = 32
CMP_STRIDE = 16
SEL_BLOCK = 64
SEL_TOP_N = 16
SEL_LOCAL = 2
WINDOW = 512
Q_BLOCK = 128
FORCE = 1e9
NEG = -1e30
SEL_TILE = 512
LOG2E = 1.4426950408889634

GDN_HEADS = 16
GDN_CONV = 4
GDN_CHUNK = 64
GDN_PACK = 4
GDN_SUB = 16

D_NSA = NSA_HEADS * HEAD_DIM
D_KV_NSA = NSA_KV_GROUPS * HEAD_DIM
D_GDN = GDN_HEADS * HEAD_DIM
D_MIX = D_NSA + D_GDN

XA_HEADS = 4
D_XA = XA_HEADS * HEAD_DIM
N_EXPERTS = 8
TOP_K = 2

COL_QKV_B = 0
COL_Q_A = COL_QKV_B + 3 * D_GDN
COL_Z_B = COL_Q_A + D_NSA
COL_KV_A = COL_Z_B + D_GDN
COL_SMALL = COL_KV_A + 6 * D_KV_NSA
SMALL_W = 512
N_IN_R = COL_SMALL + SMALL_W
SM_BETA = 32
SM_A = 48

VMEM_LIMIT = 58 * 1024 * 1024

NT_DIMS = (((1,), (1,)), ((), ()))


def _rms(x, gain):
    return x * lax.rsqrt(jnp.mean(x * x, axis=-1, keepdims=True) + EPS) * gain


def _dot(a, b):
    return jnp.dot(a, b, preferred_element_type=F32)


def _dot_nt(a, b):
    return lax.dot_general(a, b, NT_DIMS, preferred_element_type=F32)


def _dot3(x, sel):
    hi = x.astype(BF16)
    rem = x - hi.astype(F32)
    mid = rem.astype(BF16)
    lo = (rem - mid.astype(F32)).astype(BF16)
    return _dot(hi, sel) + _dot(mid, sel) + _dot(lo, sel)


def _params(sem, vmem=VMEM_LIMIT):
    return pltpu.CompilerParams(dimension_semantics=sem, vmem_limit_bytes=vmem)


def _norm_kernel(x_ref, g_ref, o_ref):
    o_ref[...] = _rms(x_ref[...], g_ref[...]).astype(o_ref.dtype)


def rmsnorm_bf16(x, gain, tm=512):
    M, D = x.shape
    return pl.pallas_call(
        _norm_kernel,
        grid=(M // tm,),
        in_specs=[pl.BlockSpec((tm, D), lambda i: (i, 0)), pl.BlockSpec((1, D), lambda i: (0, 0))],
        out_specs=pl.BlockSpec((tm, D), lambda i: (i, 0)),
        out_shape=jax.ShapeDtypeStruct((M, D), BF16),
        compiler_params=_params(("parallel",)),
        name="rmsnorm",
    )(x, gain.reshape(1, D))


def _route(h, rw_hi, rw_lo):
    h_hi = h.astype(BF16)
    h_lo = (h - h_hi.astype(F32)).astype(BF16)
    logits = _dot(h_hi, rw_hi) + _dot(h_hi, rw_lo) + _dot(h_lo, rw_hi)
    lane = lax.broadcasted_iota(I32, logits.shape, 1).astype(F32)
    logits = jnp.where(lane < N_EXPERTS, logits, -jnp.inf)
    m1 = jnp.max(logits, axis=-1, keepdims=True)
    i1 = jnp.min(jnp.where(logits == m1, lane, 128.0), axis=-1, keepdims=True)
    rest = jnp.where(lane == i1, -jnp.inf, logits)
    m2 = jnp.max(rest, axis=-1, keepdims=True)
    i2 = jnp.min(jnp.where(rest == m2, lane, 128.0), axis=-1, keepdims=True)
    e2 = jnp.exp(m2 - m1)
    den = 1.0 + e2
    idx = jnp.where(lane == 0, i1, jnp.where(lane == 1, i2, 0.0)).astype(I32)
    wts = jnp.where(lane == 0, 1.0 / den, jnp.where(lane == 1, e2 / den, 0.0))
    return idx, wts


def _mm_kernel(x_ref, w_ref, o_ref):
    o_ref[...] = _dot(x_ref[...], w_ref[...].astype(BF16)).astype(o_ref.dtype)


def matmul_cols(x, w, *, tm, tn, out_dtype=F32, name="matmul"):
    M, K = x.shape
    N = w.shape[1]
    return pl.pallas_call(
        _mm_kernel,
        grid=(M // tm, N // tn),
        in_specs=[pl.BlockSpec((tm, K), lambda i, j: (i, 0)), pl.BlockSpec((K, tn), lambda i, j: (0, j))],
        out_specs=pl.BlockSpec((tm, tn), lambda i, j: (i, j)),
        out_shape=jax.ShapeDtypeStruct((M, N), out_dtype),
        compiler_params=_params(("parallel", "parallel")),
        name=name,
    )(x, w)


def _mm2_res_kernel(xa_ref, xb_ref, wa_ref, wb_ref, r_ref, o_ref):
    acc = _dot(xa_ref[...], wa_ref[...].astype(BF16)) + _dot(xb_ref[...], wb_ref[...].astype(BF16))
    o_ref[...] = r_ref[...] + acc


def matmul2_residual(xa, xb, w, layer, residual, *, tm, tn, name):
    M, Ka = xa.shape
    Kb = xb.shape[1]
    assert Ka == Kb and w.shape[1] == Ka + Kb
    N = w.shape[2]
    return pl.pallas_call(
        _mm2_res_kernel,
        grid=(M // tm, N // tn),
        in_specs=[pl.BlockSpec((tm, Ka), lambda i, j: (i, 0)), pl.BlockSpec((tm, Kb), lambda i, j: (i, 0)),
                  pl.BlockSpec((None, Ka, tn), lambda i, j: (layer, 0, j)),
                  pl.BlockSpec((None, Kb, tn), lambda i, j: (layer, 1, j)),
                  pl.BlockSpec((tm, tn), lambda i, j: (i, j))],
        out_specs=pl.BlockSpec((tm, tn), lambda i, j: (i, j)),
        out_shape=jax.ShapeDtypeStruct((M, N), F32),
        compiler_params=_params(("parallel", "parallel")),
        name=name,
    )(xa, xb, w, w, residual)


def _swiglu_up_kernel(x_ref, wg_ref, wu_ref, o_ref):
    x = x_ref[...]
    gate = _dot(x, wg_ref[...].astype(BF16))
    up = _dot(x, wu_ref[...].astype(BF16))
    o_ref[...] = (gate * jax.nn.sigmoid(gate) * up).astype(o_ref.dtype)


def swiglu_up(x, w13, *, tm, tn):
    M, K = x.shape
    F = w13.shape[1] // 2
    nj = F // tn
    return pl.pallas_call(
        _swiglu_up_kernel,
        grid=(M // tm, nj),
        in_specs=[pl.BlockSpec((tm, K), lambda i, j: (i, 0), pipeline_mode=pl.Buffered(1)),
                  pl.BlockSpec((K, tn), lambda i, j: (0, j)),
                  pl.BlockSpec((K, tn), lambda i, j: (0, j + nj))],
        out_specs=pl.BlockSpec((tm, tn), lambda i, j: (i, j)),
        out_shape=jax.ShapeDtypeStruct((M, F), BF16),
        compiler_params=_params(("parallel", "parallel")),
        name="swiglu_up",
    )(x, w13, w13)


def _down_res_kernel(a_ref, w_ref, at_ref, wt_ref, r_ref, o_ref, *, n_main):
    k = pl.program_id(2)

    @pl.when(k == 0)
    def _():
        o_ref[...] = r_ref[...]

    @pl.when(k < n_main)
    def _():
        o_ref[...] += _dot(a_ref[...], w_ref[...].astype(BF16))

    @pl.when(k >= n_main)
    def _():
        o_ref[...] += _dot(at_ref[...], wt_ref[...].astype(BF16))


def down_proj_residual(a, w2, residual, *, tm, tn, tk, tk_tail):
    M, F = a.shape
    N = w2.shape[1]
    n_main = F // tk
    tail = F - n_main * tk
    assert tail % tk_tail == 0 and (n_main * tk) % tk_tail == 0
    n_tail = tail // tk_tail
    first_tail = (n_main * tk) // tk_tail if n_tail else 0
    main_k = lambda k: jnp.minimum(k, n_main - 1)
    tail_k = lambda k: first_tail + jnp.maximum(k - n_main, 0)
    return pl.pallas_call(
        functools.partial(_down_res_kernel, n_main=n_main),
        grid=(M // tm, N // tn, n_main + n_tail),
        in_specs=[pl.BlockSpec((tm, tk), lambda i, j, k: (i, main_k(k))),
                  pl.BlockSpec((tk, tn), lambda i, j, k: (main_k(k), j)),
                  pl.BlockSpec((tm, tk_tail), lambda i, j, k: (i, tail_k(k))),
                  pl.BlockSpec((tk_tail, tn), lambda i, j, k: (tail_k(k), j)),
                  pl.BlockSpec((tm, tn), lambda i, j, k: (i, j))],
        out_specs=pl.BlockSpec((tm, tn), lambda i, j, k: (i, j)),
        out_shape=jax.ShapeDtypeStruct((M, N), F32),
        compiler_params=_params(("parallel", "parallel", "arbitrary")),
        name="down_proj",
    )(a, w2, a, w2, residual)


def _new_weights(te_ref, i):
    return jnp.logical_or(i == 0, te_ref[i] != te_ref[jnp.maximum(i - 1, 0)])


def _moe_up_kernel(te_ref, nv_ref, x_ref, wg_ref, wu_ref, o_ref, wgb_ref, wub_ref):
    i = pl.program_id(1)
    valid = i < nv_ref[0]

    @pl.when(_new_weights(te_ref, i))
    def _():
        wgb_ref[...] = wg_ref[...].astype(BF16)
        wub_ref[...] = wu_ref[...].astype(BF16)

    @pl.when(valid)
    def _():
        x = x_ref[...]
        gate = _dot(x, wgb_ref[...])
        up = _dot(x, wub_ref[...])
        o_ref[...] = (gate * jax.nn.sigmoid(gate) * up).astype(o_ref.dtype)

    @pl.when(jnp.logical_not(valid))
    def _():
        o_ref[...] = jnp.zeros_like(o_ref)


def moe_up(xs, w13, tile_expert, n_valid, *, tm, tn):
    P, K = xs.shape
    F = w13.shape[2] // 2
    nj = F // tn
    return pl.pallas_call(
        _moe_up_kernel,
        grid_spec=pltpu.PrefetchScalarGridSpec(
            num_scalar_prefetch=2,
            grid=(nj, P // tm),
            in_specs=[pl.BlockSpec((tm, K), lambda j, i, te, nv: (jnp.minimum(i, nv[0] - 1), 0)),
                      pl.BlockSpec((None, K, tn), lambda j, i, te, nv: (te[i], 0, j)),
                      pl.BlockSpec((None, K, tn), lambda j, i, te, nv: (te[i], 0, j + nj))],
            out_specs=pl.BlockSpec((tm, tn), lambda j, i, te, nv: (i, j)),
            scratch_shapes=[pltpu.VMEM((K, tn), BF16), pltpu.VMEM((K, tn), BF16)],
        ),
        out_shape=jax.ShapeDtypeStruct((P, F), BF16),
        compiler_params=_params(("arbitrary", "arbitrary")),
        name="moe_up",
    )(tile_expert, n_valid, xs, w13, w13)


def _moe_down_kernel(te_ref, nv_ref, a_ref, w_ref, rw_ref, o_ref, wb_ref):
    i = pl.program_id(1)
    valid = i < nv_ref[0]

    @pl.when(_new_weights(te_ref, i))
    def _():
        wb_ref[...] = w_ref[...].astype(BF16)

    @pl.when(valid)
    def _():
        o_ref[...] = _dot(a_ref[...], wb_ref[...]) * rw_ref[:, 0:1]

    @pl.when(jnp.logical_not(valid))
    def _():
        o_ref[...] = jnp.zeros_like(o_ref)


def moe_down(act, w2, row_w, tile_expert, n_valid, *, tm, tn):
    P, F = act.shape
    N = w2.shape[2]
    return pl.pallas_call(
        _moe_down_kernel,
        grid_spec=pltpu.PrefetchScalarGridSpec(
            num_scalar_prefetch=2,
            grid=(N // tn, P // tm),
            in_specs=[pl.BlockSpec((tm, F), lambda j, i, te, nv: (jnp.minimum(i, nv[0] - 1), 0)),
                      pl.BlockSpec((None, F, tn), lambda j, i, te, nv: (te[i], 0, j)),
                      pl.BlockSpec((tm, 128), lambda j, i, te, nv: (i, 0))],
            out_specs=pl.BlockSpec((tm, tn), lambda j, i, te, nv: (i, j)),
            scratch_shapes=[pltpu.VMEM((F, tn), BF16)],
        ),
        out_shape=jax.ShapeDtypeStruct((P, N), F32),
        compiler_params=_params(("arbitrary", "arbitrary")),
        name="moe_down",
    )(tile_expert, n_valid, act, w2, row_w)


def moe_dispatch(top_i, top_w, tm):
    M = top_i.shape[0]
    A = M * TOP_K
    P = A + N_EXPERTS * tm
    e_flat = top_i.reshape(A)
    onehot = (e_flat[:, None] == jnp.arange(N_EXPERTS, dtype=I32)[None, :]).astype(I32)
    csum = jnp.cumsum(onehot, axis=0)
    counts = csum[-1]
    padded = ((counts + tm - 1) // tm) * tm
    pend = jnp.cumsum(padded)
    pstart = pend - padded
    dest = jnp.sum(onehot * (pstart[None, :] + csum - onehot), axis=1)
    place = dict(mode="promise_in_bounds", unique_indices=True)
    src_tok = jnp.zeros((P,), I32).at[dest].set(jnp.arange(A, dtype=I32) // TOP_K, **place)
    row_w = jnp.zeros((P,), F32).at[dest].set(top_w.reshape(A), **place)
    pos = dest.reshape(M, TOP_K)
    tile_start = jnp.arange(P // tm, dtype=I32) * tm
    tile_expert = jnp.minimum(jnp.sum((tile_start[:, None] >= pend[None, :]).astype(I32), axis=1), N_EXPERTS - 1)
    n_valid = (pend[-1:] // tm).astype(I32)
    last_e = tile_expert[jnp.maximum(n_valid[0] - 1, 0)]
    tile_expert = jnp.where(tile_start // tm < n_valid[0], tile_expert, last_e)
    return src_tok, row_w, pos, tile_expert, n_valid


def _nsa_prep_kernel(kc_ref, vc_ref, ks_ref, vs_ref, kw_ref, vw_ref, kg_ref, raw_ref, kso_ref, vso_ref, kwo_ref, vwo_ref):
    raw_ref[0] = kc_ref[...]
    raw_ref[1] = vc_ref[...]
    kg = kg_ref[...]
    kso_ref[...] = _rms(ks_ref[...], kg[1:2]).astype(BF16)
    vso_ref[...] = vs_ref[...].astype(BF16)
    kwo_ref[...] = _rms(kw_ref[...], kg[2:3]).astype(BF16)
    vwo_ref[...] = vw_ref[...].astype(BF16)


def nsa_prep(z, k_gain, B, S, ts=512):
    G = NSA_KV_GROUPS
    nt = S // ts
    cb = COL_KV_A // HEAD_DIM

    def col(br, kvi):
        return pl.BlockSpec((ts, HEAD_DIM), lambda b, g, t: (b * nt + t, cb + br * 4 + kvi * 2 + g))

    kv_out = pl.BlockSpec((None, None, ts, HEAD_DIM), lambda b, g, t: (b, g, t, 0))
    kv_shape = jax.ShapeDtypeStruct((B, G, S, HEAD_DIM), BF16)
    return pl.pallas_call(
        _nsa_prep_kernel,
        grid=(B, G, nt),
        in_specs=[col(0, 0), col(0, 1), col(1, 0), col(1, 1), col(2, 0), col(2, 1),
                  pl.BlockSpec((3, HEAD_DIM), lambda b, g, t: (0, 0))],
        out_specs=[pl.BlockSpec((None, 2, None, ts, HEAD_DIM), lambda b, g, t: (b, 0, g, t, 0)),
                   kv_out, kv_out, kv_out, kv_out],
        out_shape=[jax.ShapeDtypeStruct((B, 2, G, S, HEAD_DIM), F32), kv_shape, kv_shape, kv_shape, kv_shape],
        compiler_params=_params(("parallel", "parallel", "parallel")),
        name="nsa_prep",
    )(z, z, z, z, z, z, k_gain)


def _nsa_compress_kernel(r_ref, pos_ref, w1_ref, w2_ref, kg_ref, o_ref):
    kv = pl.program_id(1)
    half = (CMP_BLOCK // 2) * HEAD_DIM
    r = r_ref[...]
    pos = pos_ref[...]
    n = r.shape[0]
    first = _dot((r + pos[:, :half]).astype(BF16), w1_ref[:half, :].astype(BF16))
    second = _dot((r + pos[:, half:]).astype(BF16), w1_ref[half:, :].astype(BF16))
    pre = first + pltpu.roll(second, n - 1, axis=0)
    hid = jax.nn.gelu(pre)
    comp = _dot(hid.astype(BF16), w2_ref[...].astype(BF16))
    o_ref[...] = jnp.where(kv == 0, _rms(comp, kg_ref[...]), comp).astype(o_ref.dtype)


def nsa_compress(raw, cmp_pos, cmp_w1, cmp_w2, k_gain0, B, S):
    G = NSA_KV_GROUPS
    n = S // CMP_STRIDE
    feat = CMP_STRIDE * HEAD_DIM
    r = raw.reshape(B, 2, G, n, feat)
    pos = cmp_pos.reshape(2, 1, CMP_BLOCK * HEAD_DIM)
    return pl.pallas_call(
        _nsa_compress_kernel,
        grid=(B, 2, G),
        in_specs=[pl.BlockSpec((None, None, None, n, feat), lambda b, kv, g: (b, kv, g, 0, 0)),
                  pl.BlockSpec((None, 1, CMP_BLOCK * HEAD_DIM), lambda b, kv, g: (kv, 0, 0)),
                  pl.BlockSpec((None, CMP_BLOCK * HEAD_DIM, HEAD_DIM), lambda b, kv, g: (kv, 0, 0)),
                  pl.BlockSpec((None, HEAD_DIM, HEAD_DIM), lambda b, kv, g: (kv, 0, 0)),
                  pl.BlockSpec((1, HEAD_DIM), lambda b, kv, g: (0, 0))],
        out_specs=pl.BlockSpec((None, None, None, n, HEAD_DIM), lambda b, kv, g: (b, kv, g, 0, 0)),
        out_shape=jax.ShapeDtypeStruct((B, 2, G, n, HEAD_DIM), BF16),
        compiler_params=_params(("parallel", "parallel", "parallel")),
        name="nsa_compress",
    )(r, pos, cmp_w1, cmp_w2, k_gain0.reshape(1, HEAD_DIM))


def _rep(x, n):
    return x if n == HEAD_DIM else jnp.concatenate([x] * (n // HEAD_DIM), axis=1)


def _flash_step(qb, kt, vt, bias, m_ref, l_ref, acc_ref):
    nk = kt.shape[0]
    rows = qb.shape[0]
    s = _dot_nt(qb, kt)
    s = (s.reshape(rows // Q_BLOCK, Q_BLOCK, nk) + bias[None]).reshape(rows, nk)
    m_prev = m_ref[...]
    m_new = jnp.maximum(m_prev, jnp.max(s, axis=-1, keepdims=True))
    alpha = jnp.exp2(m_prev - m_new)
    p = jnp.exp2(s - _rep(m_new, nk))
    l_ref[...] = alpha * l_ref[...] + jnp.sum(p, axis=-1, keepdims=True)
    acc_ref[...] = alpha * acc_ref[...] + _dot(p.astype(BF16), vt)
    m_ref[...] = m_new


def _nsa_kernel(q_ref, sm_ref, kc_ref, vc_ref, ks_ref, vs_ref, kw_ref, vw_ref, qg_ref, og_ref, c2s_ref, exp_ref,
                o_ref, selb_ref, m_ref, l_ref, acc_ref, *, k_top):
    Hg = NSA_GROUP_SIZE
    rows = Hg * Q_BLOCK
    c = pl.program_id(2)
    t0 = c * Q_BLOCK
    n_cmp = kc_ref.shape[0]
    n_sel = c2s_ref.shape[0]

    q = q_ref[...]
    qs = jnp.concatenate([q[:, h * HEAD_DIM:(h + 1) * HEAD_DIM] for h in range(Hg)], axis=0)
    qb = (_rms(qs, qg_ref[...]) * (SCALE * LOG2E)).astype(BF16)

    tq = t0 + lax.broadcasted_iota(I32, (Q_BLOCK, n_cmp), 0)
    c_end = lax.broadcasted_iota(I32, (Q_BLOCK, n_cmp), 1) * CMP_STRIDE + (CMP_BLOCK - 1)
    sc = _dot_nt(qb, kc_ref[...]).reshape(Hg, Q_BLOCK, n_cmp)
    sc = jnp.where((c_end <= tq)[None], sc, -jnp.inf)
    mc = jnp.max(sc, axis=-1, keepdims=True)
    mc = jnp.where(mc > -jnp.inf, mc, 0.0)
    pc = jnp.exp2(sc - mc)
    dc = jnp.sum(pc, axis=-1, keepdims=True)
    pc = pc / jnp.where(dc > 0, dc, 1.0)
    o_cmp = _dot(pc.reshape(rows, n_cmp).astype(BF16), vc_ref[...])

    psum = jnp.sum(pc, axis=0)
    c2s = c2s_ref[...]
    p_hi = psum.astype(BF16)
    rem = psum - p_hi.astype(F32)
    p_mid = rem.astype(BF16)
    p_lo = (rem - p_mid.astype(F32)).astype(BF16)
    imp = _dot_nt(c2s, p_hi) + _dot_nt(c2s, p_mid) + _dot_nt(c2s, p_lo)

    jj = lax.broadcasted_iota(I32, (n_sel, Q_BLOCK), 0)
    cur = lax.shift_right_logical(t0 + lax.broadcasted_iota(I32, (n_sel, Q_BLOCK), 1), 6)
    valid = jj <= cur
    forced = valid & ((jj == 0) | (jj > cur - SEL_LOCAL))
    score = jnp.where(forced, FORCE, jnp.where(valid, imp, -FORCE))
    rank = jnp.zeros((n_sel, Q_BLOCK), F32)
    for j2 in range(n_sel):
        other = score[j2:j2 + 1, :]
        tie = jnp.where(jj > j2, 1.0, 0.0)
        rank = rank + jnp.where(other > score, 1.0, jnp.where(other == score, tie, 0.0))
    sel_t = jnp.where(rank < k_top, 1.0, 0.0)
    if n_sel < Q_BLOCK:
        sel_t = jnp.concatenate([sel_t, jnp.zeros((Q_BLOCK - n_sel, Q_BLOCK), F32)], axis=0)
    sel = sel_t.T[:, :n_sel].astype(BF16)
    sel_keys = _dot(sel, exp_ref[...])
    for i in range(selb_ref.shape[0]):
        selb_ref[i] = (sel_keys[:, i * SEL_TILE:(i + 1) * SEL_TILE] - 1.0) * (-NEG)

    def reset():
        m_ref[...] = jnp.full(m_ref.shape, NEG, F32)
        l_ref[...] = jnp.zeros(l_ref.shape, F32)
        acc_ref[...] = jnp.zeros(acc_ref.shape, F32)

    reset()
    tq_s = t0 + lax.broadcasted_iota(I32, (Q_BLOCK, SEL_TILE), 0)
    kk_s = lax.broadcasted_iota(I32, (Q_BLOCK, SEL_TILE), 1)

    def sel_body(i, carry):
        k0 = pl.multiple_of(i * SEL_TILE, SEL_TILE)
        bias = selb_ref[i] + jnp.where(kk_s + k0 <= tq_s, 0.0, NEG)
        _flash_step(qb, ks_ref[pl.ds(k0, SEL_TILE), :], vs_ref[pl.ds(k0, SEL_TILE), :], bias, m_ref, l_ref, acc_ref)
        return carry

    lax.fori_loop(0, (t0 + Q_BLOCK + SEL_TILE - 1) // SEL_TILE, sel_body, 0)
    o_sel = acc_ref[...] / l_ref[...]

    wk = WINDOW + Q_BLOCK
    w0 = pl.multiple_of(jnp.maximum(t0 - WINDOW, 0), Q_BLOCK)
    tq_w = t0 + lax.broadcasted_iota(I32, (Q_BLOCK, wk), 0)
    wpos = w0 + lax.broadcasted_iota(I32, (Q_BLOCK, wk), 1)
    w_bias = jnp.where((wpos <= tq_w) & (wpos > tq_w - WINDOW), 0.0, NEG)
    sw = _dot_nt(qb, kw_ref[pl.ds(w0, wk), :]).reshape(Hg, Q_BLOCK, wk) + w_bias[None]
    pw = jnp.exp2(sw - jnp.max(sw, axis=-1, keepdims=True))
    lw = jnp.sum(pw, axis=-1, keepdims=True).reshape(rows, 1)
    o_win = _dot(pw.reshape(rows, wk).astype(BF16), vw_ref[pl.ds(w0, wk), :]) / lw

    gates = jax.nn.sigmoid(sm_ref[...])
    og = og_ref[...]
    for h in range(Hg):
        r = slice(h * Q_BLOCK, (h + 1) * Q_BLOCK)
        o = (gates[:, h:h + 1] * o_cmp[r] + gates[:, Hg + h:Hg + h + 1] * o_sel[r]
             + gates[:, 2 * Hg + h:2 * Hg + h + 1] * o_win[r])
        o_ref[:, h * HEAD_DIM:(h + 1) * HEAD_DIM] = _rms(o, og).astype(o_ref.dtype)


def nsa_attention(z, comp, ks, vs, kw, vw, q_gain, out_gain, B, S):
    G, Hg = NSA_KV_GROUPS, NSA_GROUP_SIZE
    nq = S // Q_BLOCK
    n_cmp = S // CMP_STRIDE
    n_sel = S // SEL_BLOCK
    k_top = min(SEL_TOP_N, n_sel)
    c_start = np.arange(n_cmp)[:, None] * CMP_STRIDE
    s_start = np.arange(n_sel)[None, :] * SEL_BLOCK
    overlap = np.minimum(c_start + CMP_BLOCK, s_start + SEL_BLOCK) - np.maximum(c_start, s_start)
    assert n_sel <= Q_BLOCK
    c2s = jnp.asarray((np.clip(overlap, 0, None) / CMP_STRIDE).T, dtype=BF16)
    expand = jnp.asarray(np.arange(S)[None, :] // SEL_BLOCK == np.arange(n_sel)[:, None], dtype=BF16)
    rows = Hg * Q_BLOCK
    wq = Hg * HEAD_DIM
    kv_spec = pl.BlockSpec((None, None, S, HEAD_DIM), lambda b, g, c: (b, g, 0, 0))
    const = lambda b, g, c: (0, 0)
    return pl.pallas_call(
        functools.partial(_nsa_kernel, k_top=k_top),
        grid=(B, G, nq),
        in_specs=[pl.BlockSpec((Q_BLOCK, wq), lambda b, g, c: (b * nq + c, COL_Q_A // wq + g)),
                  pl.BlockSpec((Q_BLOCK, HEAD_DIM), lambda b, g, c: (b * nq + c, COL_SMALL // HEAD_DIM + g)),
                  pl.BlockSpec((None, None, None, n_cmp, HEAD_DIM), lambda b, g, c: (b, 0, g, 0, 0)),
                  pl.BlockSpec((None, None, None, n_cmp, HEAD_DIM), lambda b, g, c: (b, 1, g, 0, 0)),
                  kv_spec, kv_spec, kv_spec, kv_spec,
                  pl.BlockSpec((1, HEAD_DIM), const), pl.BlockSpec((1, HEAD_DIM), const),
                  pl.BlockSpec((n_sel, n_cmp), const), pl.BlockSpec((n_sel, S), const)],
        out_specs=pl.BlockSpec((Q_BLOCK, wq), lambda b, g, c: (b * nq + c, g)),
        out_shape=jax.ShapeDtypeStruct((B * S, D_NSA), BF16),
        scratch_shapes=[pltpu.VMEM((S // SEL_TILE, Q_BLOCK, SEL_TILE), F32),
                        pltpu.VMEM((rows, HEAD_DIM), F32), pltpu.VMEM((rows, HEAD_DIM), F32),
                        pltpu.VMEM((rows, HEAD_DIM), F32)],
        compiler_params=_params(("parallel", "parallel", "arbitrary")),
        name="nsa_attention",
    )(z, z, comp, comp, ks, vs, kw, vw, q_gain.reshape(1, HEAD_DIM), out_gain.reshape(1, HEAD_DIM), c2s, expand)


def _gdn_chunk_kernel(xc_ref, xh_ref, sm_ref, cw_ref, alog_ref, dtb_ref, rep_ref, unfold_ref, u_ref, wq_ref, ak_ref,
                      egl_ref):
    for b in range(xc_ref.shape[0]):
        _gdn_chunk_body(xc_ref.at[b], xh_ref.at[b], sm_ref.at[b], cw_ref, alog_ref, dtb_ref, rep_ref, unfold_ref,
                        u_ref.at[b], wq_ref.at[b], ak_ref.at[b], egl_ref.at[b])


def _gdn_chunk_body(xc_ref, xh_ref, sm_ref, cw_ref, alog_ref, dtb_ref, rep_ref, unfold_ref, u_ref, wq_ref, ak_ref,
                    egl_ref):
    C = GDN_CHUNK
    PK = GDN_PACK
    R = PK * C
    n = pl.program_id(0)
    xp = jnp.concatenate([jnp.where(n > 0, xh_ref[...], 0.0), xc_ref[...]], axis=0)
    cw = cw_ref[...]
    y = None
    for j in range(GDN_CONV):
        shift = GDN_CONV - 1 - j
        xs = xp if shift == 0 else pltpu.roll(xp, shift, axis=0)
        term = xs[8:] * cw[j:j + 1, :]
        y = term if y is None else y + term
    y = y * jax.nn.sigmoid(y)

    sm = sm_ref[...]
    beta = jax.nn.sigmoid(sm)
    g = -jnp.exp(alog_ref[...]) * jax.nn.softplus(sm + dtb_ref[...])
    row = lax.broadcasted_iota(I32, g.shape, 0)
    gc = g
    d = 1
    while d < C:
        gc = gc + jnp.where(row >= d, pltpu.roll(gc, d, axis=0), 0.0)
        d *= 2
    g_last = gc[C - 1:C, :]

    ri = lax.broadcasted_iota(I32, (R, R), 0)
    ci = lax.broadcasted_iota(I32, (R, R), 1)
    same = lax.shift_right_logical(ri, 6) == lax.shift_right_logical(ci, 6)
    tri = same & (ri >= ci)
    strict = same & (ri > ci)
    SUB = GDN_SUB
    nb = R // SUB
    same16 = lax.shift_right_logical(ri, 4) == lax.shift_right_logical(ci, 4)
    same32 = lax.shift_right_logical(ri, 5) == lax.shift_right_logical(ci, 5)
    off32 = same32 & jnp.logical_not(same16)
    off64 = jnp.logical_not(same32)
    NG = GDN_HEADS // PK
    W4 = NG * SUB
    row_c = lax.broadcasted_iota(I32, (R, W4), 0) & (SUB - 1)
    col_s = lax.broadcasted_iota(I32, (nb, W4), 1) & (SUB - 1)

    groups = []
    for grp in range(NG):
        heads = [grp * PK + i for i in range(PK)]

        def stack(off):
            return jnp.concatenate([y[:, off + h * HEAD_DIM: off + (h + 1) * HEAD_DIM] for h in heads], axis=0)

        def col(x, base):
            return jnp.concatenate([x[:, base + h: base + h + 1] for h in heads], axis=0)

        q4 = stack(0)
        k4 = stack(D_GDN)
        v4 = stack(2 * D_GDN)
        q4 = q4 * lax.rsqrt(jnp.sum(q4 * q4, axis=-1, keepdims=True) + EPS) * SCALE
        k4 = k4 * lax.rsqrt(jnp.sum(k4 * k4, axis=-1, keepdims=True) + EPS)
        beta4 = col(beta, SM_BETA)
        gc4 = col(gc, SM_A)
        gl4 = jnp.concatenate([jnp.broadcast_to(g_last[:, SM_A + h: SM_A + h + 1], (C, 1)) for h in heads], axis=0)

        gb = jnp.broadcast_to(gc4, (R, R))
        decay = jnp.exp(jnp.where(tri, gb - gb.T, -jnp.inf))
        kb4 = k4 * beta4
        k4b = k4.astype(BF16)
        a = jnp.where(strict, _dot_nt(kb4.astype(BF16), k4b) * decay, 0.0)
        attn = _dot_nt(q4.astype(BF16), k4b) * decay
        at = a.T
        dct = jnp.concatenate([at[SUB * b:SUB * (b + 1), SUB * b:SUB * (b + 1)] for b in range(nb)], axis=0)
        groups.append((heads, q4, k4, v4, beta4, gc4, gl4, kb4, a, attn, dct))

    coef = _dot3(jnp.concatenate([g[-1] for g in groups], axis=1), rep_ref[...])
    t4 = jnp.where(row_c == (lax.broadcasted_iota(I32, (R, W4), 1) & (SUB - 1)), 1.0, 0.0)
    for i in range(1, SUB):
        s = jnp.sum((coef[:, i * HEAD_DIM: i * HEAD_DIM + W4] * t4).reshape(nb, SUB, W4), axis=1)
        new = jnp.where(col_s == i, 1.0, 0.0) - s
        t4 = jnp.where(row_c == i, jnp.broadcast_to(new[:, None, :], (nb, SUB, W4)).reshape(R, W4), t4)
    t_tiled = _dot3(t4, unfold_ref[...])

    for gi, (heads, q4, k4, v4, beta4, gc4, gl4, kb4, a, attn, _) in enumerate(groups):
        t16 = jnp.where(same16, t_tiled[:, gi * R:(gi + 1) * R], 0.0)
        t16b = t16.astype(BF16)
        a32 = jnp.where(off32, a, 0.0).astype(BF16)
        t32 = t16 - _dot(_dot(t16b, a32).astype(BF16), t16b)
        t32b = t32.astype(BF16)
        a64 = jnp.where(off64, a, 0.0).astype(BF16)
        t64 = t32 - _dot(_dot(t32b, a64).astype(BF16), t32b)
        eg = jnp.exp(gc4)
        rhs = jnp.concatenate([v4 * beta4, kb4 * eg], axis=1)
        rhs = _dot(t64.astype(BF16), rhs.astype(BF16))
        qd4 = q4 * eg
        kdt = (k4 * jnp.exp(gl4 - gc4)).T
        for i, h in enumerate(heads):
            r = slice(i * C, (i + 1) * C)
            u_ref[h] = rhs[r, :HEAD_DIM]
            wq_ref[h] = jnp.concatenate([rhs[r, HEAD_DIM:], qd4[r]], axis=0).astype(BF16)
            ak_ref[h] = jnp.concatenate([attn[r, r], kdt[:, r]], axis=0).astype(BF16)
            egl_ref[h] = jnp.broadcast_to(jnp.exp(g_last[:, SM_A + h: SM_A + h + 1]), (8, HEAD_DIM))


def gdn_chunks(z, conv_w, a_log, dt_bias, B, S):
    C, H = GDN_CHUNK, GDN_HEADS
    N = S // C
    W = 3 * D_GDN
    pad = lambda v: jnp.zeros((1, HEAD_DIM), F32).at[0, SM_A:SM_A + H].set(v)
    out5 = lambda r, cdim: pl.BlockSpec((B, None, H, r, cdim), lambda n: (0, n, 0, 0, 0))
    NG, SUB, R = H // GDN_PACK, GDN_SUB, GDN_PACK * C
    W4 = NG * SUB
    src = np.arange(W4)[:, None]
    dst = np.arange(SUB * HEAD_DIM)[None, :]
    lane = dst % HEAD_DIM
    rep = (lane < W4) & (src // SUB == lane // SUB) & (src % SUB == dst // HEAD_DIM)
    dst = np.arange(NG * R)[None, :]
    unfold = (src // SUB == dst // R) & (src % SUB == dst % SUB)
    z3 = z.reshape(B, S, z.shape[1])
    const = lambda n: (0, 0)
    return pl.pallas_call(
        _gdn_chunk_kernel,
        grid=(N,),
        in_specs=[pl.BlockSpec((B, C, W), lambda n: (0, n, 0)),
                  pl.BlockSpec((B, 8, W), lambda n: (0, jnp.maximum(n * (C // 8) - 1, 0), 0)),
                  pl.BlockSpec((B, C, HEAD_DIM), lambda n: (0, n, COL_SMALL // HEAD_DIM)),
                  pl.BlockSpec((GDN_CONV, W), const),
                  pl.BlockSpec((1, HEAD_DIM), const),
                  pl.BlockSpec((1, HEAD_DIM), const),
                  pl.BlockSpec((W4, SUB * HEAD_DIM), const),
                  pl.BlockSpec((W4, NG * R), const)],
        out_specs=[out5(C, HEAD_DIM), out5(2 * C, HEAD_DIM), out5(C + HEAD_DIM, C), out5(8, HEAD_DIM)],
        out_shape=[jax.ShapeDtypeStruct((B, N, H, C, HEAD_DIM), F32),
                   jax.ShapeDtypeStruct((B, N, H, 2 * C, HEAD_DIM), BF16),
                   jax.ShapeDtypeStruct((B, N, H, C + HEAD_DIM, C), BF16),
                   jax.ShapeDtypeStruct((B, N, H, 8, HEAD_DIM), F32)],
        compiler_params=_params(("parallel",)),
        name="gdn_chunks",
    )(z3, z3, z3, conv_w, pad(a_log), pad(dt_bias), jnp.asarray(rep, dtype=BF16), jnp.asarray(unfold, dtype=BF16))


def _gdn_scan_kernel(u_ref, wq_ref, ak_ref, egl_ref, z_ref, og_ref, o_ref, s_ref):
    C = GDN_CHUNK

    @pl.when(pl.program_id(0) == 0)
    def _():
        s_ref[...] = jnp.zeros(s_ref.shape, F32)

    og = og_ref[...]
    for b in range(u_ref.shape[0]):
        for h in range(GDN_HEADS):
            state = s_ref[b, h]
            ws = _dot(wq_ref[b, h], state.astype(BF16))
            v_new = u_ref[b, h] - ws[:C]
            av = _dot(ak_ref[b, h], v_new.astype(BF16))
            o = ws[C:] + av[:C]
            decayed = (state.reshape(HEAD_DIM // 8, 8, HEAD_DIM) * egl_ref[b, h][None]).reshape(HEAD_DIM, HEAD_DIM)
            s_ref[b, h] = decayed + av[C:]
            zh = z_ref[b, :, h * HEAD_DIM:(h + 1) * HEAD_DIM]
            o_ref[b, :, h * HEAD_DIM:(h + 1) * HEAD_DIM] = (_rms(o, og) * (zh * jax.nn.sigmoid(zh))).astype(o_ref.dtype)


def gdn_scan(u, wq, ak, egl, z, out_gain, B, S):
    C, H = GDN_CHUNK, GDN_HEADS
    N = S // C
    in5 = lambda r, cdim: pl.BlockSpec((B, None, H, r, cdim), lambda n: (0, n, 0, 0, 0))
    y = pl.pallas_call(
        _gdn_scan_kernel,
        grid=(N,),
        in_specs=[in5(C, HEAD_DIM), in5(2 * C, HEAD_DIM), in5(C + HEAD_DIM, C), in5(8, HEAD_DIM),
                  pl.BlockSpec((B, C, D_GDN), lambda n: (0, n, COL_Z_B // D_GDN)),
                  pl.BlockSpec((1, HEAD_DIM), lambda n: (0, 0))],
        out_specs=pl.BlockSpec((B, C, D_GDN), lambda n: (0, n, 0)),
        out_shape=jax.ShapeDtypeStruct((B, S, D_GDN), BF16),
        scratch_shapes=[pltpu.VMEM((B, H, HEAD_DIM, HEAD_DIM), F32)],
        compiler_params=_params(("arbitrary",)),
        name="gdn_scan",
    )(u, wq, ak, egl, z.reshape(B, S, z.shape[1]), out_gain.reshape(1, HEAD_DIM))
    return y.reshape(B * S, D_GDN)


def _xa_kv_kernel(mem_ref, g_ref, w_ref, kg_ref, k_ref, v_ref):
    kv = _dot(_rms(mem_ref[...], g_ref[...]).astype(BF16), w_ref[...])
    for h in range(XA_HEADS):
        k_ref[h] = _rms(kv[:, h * HEAD_DIM:(h + 1) * HEAD_DIM], kg_ref[...]).astype(BF16)
        v_ref[h] = kv[:, D_XA + h * HEAD_DIM: D_XA + (h + 1) * HEAD_DIM].astype(BF16)


def xa_kv(mem, mem_norm, wkv_bf16, k_gain):
    B, Mm, D = mem.shape
    spec = pl.BlockSpec((None, XA_HEADS, Mm, HEAD_DIM), lambda b: (b, 0, 0, 0))
    shape = jax.ShapeDtypeStruct((B, XA_HEADS, Mm, HEAD_DIM), BF16)
    return pl.pallas_call(
        _xa_kv_kernel,
        grid=(B,),
        in_specs=[pl.BlockSpec((None, Mm, D), lambda b: (b, 0, 0)), pl.BlockSpec((1, D), lambda b: (0, 0)),
                  pl.BlockSpec((D, 2 * D_XA), lambda b: (0, 0)), pl.BlockSpec((1, HEAD_DIM), lambda b: (0, 0))],
        out_specs=[spec, spec],
        out_shape=[shape, shape],
        compiler_params=_params(("parallel",)),
        name="xa_kv",
    )(mem, mem_norm.reshape(1, D), wkv_bf16, k_gain.reshape(1, HEAD_DIM))


def _xa_kernel(x_ref, g_ref, wq_ref, k_ref, v_ref, qg_ref, wo_ref, fg_ref, *rest, route):
    if route:
        rwh_ref, rwl_ref, o_ref, h_ref, idx_ref, wt_ref = rest
    else:
        o_ref, h_ref = rest
    x = x_ref[...]
    q = _dot(_rms(x, g_ref[...]).astype(BF16), wq_ref[...])
    outs = []
    for h in range(XA_HEADS):
        qh = _rms(q[:, h * HEAD_DIM:(h + 1) * HEAD_DIM], qg_ref[...]).astype(BF16)
        s = _dot_nt(qh, k_ref[h]) * SCALE
        p = jnp.exp(s - jnp.max(s, axis=-1, keepdims=True))
        p = p / jnp.sum(p, axis=-1, keepdims=True)
        outs.append(_dot(p.astype(BF16), v_ref[h]))
    o = jnp.concatenate(outs, axis=1).astype(BF16)
    x_new = x + _dot(o, wo_ref[...])
    o_ref[...] = x_new
    hn = _rms(x_new, fg_ref[...])
    h_ref[...] = hn.astype(h_ref.dtype)
    if route:
        idx_ref[...], wt_ref[...] = _route(hn, rwh_ref[...], rwl_ref[...])


def cross_attention(x, xa_norm, wq_bf16, k, v, q_gain, wo_bf16, ffn_norm, router_w, B, S, tm=256):
    M, D = x.shape
    Mm = k.shape[2]
    per_b = S // tm
    kv_spec = pl.BlockSpec((None, XA_HEADS, Mm, HEAD_DIM), lambda i: (i // per_b, 0, 0, 0))
    const = lambda i: (0, 0)
    row = lambda i: (i, 0)
    route = router_w is not None
    in_specs = [pl.BlockSpec((tm, D), row), pl.BlockSpec((1, D), const),
                pl.BlockSpec((D, D_XA), const), kv_spec, kv_spec,
                pl.BlockSpec((1, HEAD_DIM), const), pl.BlockSpec((D_XA, D), const), pl.BlockSpec((1, D), const)]
    args = [x, xa_norm.reshape(1, D), wq_bf16, k, v, q_gain.reshape(1, HEAD_DIM), wo_bf16, ffn_norm.reshape(1, D)]
    out_specs = [pl.BlockSpec((tm, D), row), pl.BlockSpec((tm, D), row)]
    out_shape = [jax.ShapeDtypeStruct((M, D), F32), jax.ShapeDtypeStruct((M, D), BF16)]
    if route:
        rw = jnp.pad(router_w, ((0, 0), (0, 128 - N_EXPERTS)))
        rw_hi = rw.astype(BF16)
        in_specs += [pl.BlockSpec((D, 128), const), pl.BlockSpec((D, 128), const)]
        args += [rw_hi, (rw - rw_hi.astype(F32)).astype(BF16)]
        out_specs += [pl.BlockSpec((tm, 128), row), pl.BlockSpec((tm, 128), row)]
        out_shape += [jax.ShapeDtypeStruct((M, 128), I32), jax.ShapeDtypeStruct((M, 128), F32)]
    return pl.pallas_call(
        functools.partial(_xa_kernel, route=route),
        grid=(M // tm,),
        in_specs=in_specs,
        out_specs=out_specs,
        out_shape=out_shape,
        compiler_params=_params(("parallel",)),
        name="cross_attention",
    )(*args)


def _reorder_w_in(w_in):
    o = np.cumsum((D_NSA, 6 * D_KV_NSA, 3 * NSA_HEADS, 3 * D_GDN, GDN_HEADS, GDN_HEADS, D_GDN))
    q_a, kv_a, gate_a, qkv_b = (0, o[0]), (o[0], o[1]), o[1], (o[2], o[3])
    beta_b, a_b, z_b = o[3], o[4], (o[5], o[6])
    Hg = NSA_GROUP_SIZE
    src = np.zeros((SMALL_W,), np.int64)
    used = np.zeros((SMALL_W,), bool)
    for g in range(NSA_KV_GROUPS):
        for br in range(3):
            for h in range(Hg):
                src[g * HEAD_DIM + br * Hg + h] = gate_a + (g * Hg + h) * 3 + br
                used[g * HEAD_DIM + br * Hg + h] = True
    for h in range(GDN_HEADS):
        src[SM_BETA + h] = beta_b + h
        src[SM_A + h] = a_b + h
        used[SM_BETA + h] = used[SM_A + h] = True
    small = jnp.where(jnp.asarray(used)[None, :], jnp.take(w_in, jnp.asarray(src), axis=1), 0.0)
    parts = [w_in[:, qkv_b[0]:qkv_b[1]], w_in[:, q_a[0]:q_a[1]], w_in[:, z_b[0]:z_b[1]], w_in[:, kv_a[0]:kv_a[1]], small]
    return jnp.concatenate([p.astype(BF16) for p in parts], axis=1)


def mixer_layer(x2, B, S, attn_norm, w_in, nsa_q_gain, nsa_k_gain, nsa_cmp_pos, nsa_cmp_w1, nsa_cmp_w2, nsa_out_gain,
                gdn_conv_w, gdn_A_log, gdn_dt_bias, gdn_out_gain, w_out_stack, layer, tm=1024):
    h = rmsnorm_bf16(x2, attn_norm)
    z = matmul_cols(h, _reorder_w_in(w_in), tm=tm, tn=1024, name="in_proj")
    raw, ks, vs, kw, vw = nsa_prep(z, nsa_k_gain, B, S)
    comp = nsa_compress(raw, nsa_cmp_pos, nsa_cmp_w1, nsa_cmp_w2, nsa_k_gain[0], B, S)
    y_a = nsa_attention(z, comp, ks, vs, kw, vw, nsa_q_gain, nsa_out_gain, B, S)
    u, wq, ak, egl = gdn_chunks(z, gdn_conv_w, gdn_A_log, gdn_dt_bias, B, S)
    y_b = gdn_scan(u, wq, ak, egl, z, gdn_out_gain, B, S)
    return matmul2_residual(y_a, y_b, w_out_stack, layer, x2, tm=tm, tn=512, name="out_proj")


def xa_layer(x2, mem, B, S, xa_norm, mem_norm, xa_wq, xa_wkv, xa_q_gain, xa_k_gain, xa_wo, ffn_norm, router_w=None):
    k, v = xa_kv(mem, mem_norm, xa_wkv.astype(BF16), xa_k_gain)
    return cross_attention(x2, xa_norm, xa_wq.astype(BF16), k, v, xa_q_gain, xa_wo.astype(BF16), ffn_norm, router_w,
                           B, S)


def dense_ffn_layer(x2, h, w13, w2, tm=1024):
    act = swiglu_up(h, w13, tm=min(2 * tm, x2.shape[0]), tn=256)
    return down_proj_residual(act, w2, x2, tm=min(2 * tm, x2.shape[0]), tn=1024, tk=512, tk_tail=256)


def moe_ffn_layer(x2, h, idx, wts, w13, w2, tm=512, tn=512):
    src_tok, row_w, pos, tile_expert, n_valid = moe_dispatch(idx[:, :TOP_K], wts[:, :TOP_K], tm)
    rows_of = lambda a, idx: a.at[idx].get(mode="promise_in_bounds")
    xs = rows_of(h, src_tok)
    act = moe_up(xs, w13, tile_expert, n_valid, tm=tm, tn=tn)
    row_w_rep = jnp.broadcast_to(row_w[:, None], (row_w.shape[0], 128))
    out = moe_down(act, w2, row_w_rep, tile_expert, n_valid, tm=tm, tn=tn)
    return x2 + (rows_of(out, pos[:, 0]) + rows_of(out, pos[:, 1]))


def kernel(x, mem, attn_norm, w_in, nsa_q_gain, nsa_k_gain, nsa_cmp_pos, nsa_cmp_w1, nsa_cmp_w2, nsa_out_gain, gdn_conv_w, gdn_A_log, gdn_dt_bias, gdn_out_gain, w_out, xa_norm, mem_norm, xa_wq, xa_wkv, xa_q_gain, xa_k_gain, xa_wo, ffn_norm, dense_w13, dense_w2, router_w, moe_w13, moe_w2):
    B, S, D = x.shape
    x2 = x.reshape(B * S, D)
    for l in range(attn_norm.shape[0]):
        x2 = mixer_layer(x2, B, S, attn_norm[l], w_in[l], nsa_q_gain[l], nsa_k_gain[l], nsa_cmp_pos[l], nsa_cmp_w1[l],
                         nsa_cmp_w2[l], nsa_out_gain[l], gdn_conv_w[l], gdn_A_log[l], gdn_dt_bias[l], gdn_out_gain[l],
                         w_out, l)
        xa_args = (x2, mem, B, S, xa_norm[l], mem_norm[l], xa_wq[l], xa_wkv[l], xa_q_gain[l], xa_k_gain[l], xa_wo[l],
                   ffn_norm[l])
        if l % 2 == 0:
            x2, h = xa_layer(*xa_args)
            x2 = dense_ffn_layer(x2, h, dense_w13[l // 2], dense_w2[l // 2])
        else:
            x2, h, idx, wts = xa_layer(*xa_args, router_w[l // 2])
            x2 = moe_ffn_layer(x2, h, idx, wts, moe_w13[l // 2], moe_w2[l // 2])
    return x2.reshape(B, S, D)
```

```python
import functools

import jax
import jax.numpy as jnp
import numpy as np
from jax import lax
from jax.experimental import pallas as pl
from jax.experimental.pallas import tpu as pltpu

F32 = jnp.float32
BF16 = jnp.bfloat16
I32 = jnp.int32

D_MODEL = 4096
HEAD_DIM = 128
EPS = 1e-6
SCALE = HEAD_DIM ** -0.5

NSA_HEADS = 16
NSA_KV_GROUPS = 2
NSA_GROUP_SIZE = NSA_HEADS // NSA_KV_GROUPS
CMP_BLOCK = 32
CMP_STRIDE = 16
SEL_BLOCK = 64
SEL_TOP_N = 16
SEL_LOCAL = 2
WINDOW = 512
Q_BLOCK = 128
FORCE = 1e9
NEG = -1e30
SEL_TILE = 512
LOG2E = 1.4426950408889634

GDN_HEADS = 16
GDN_CONV = 4
GDN_CHUNK = 64
GDN_PACK = 4
GDN_SUB = 16

D_NSA = NSA_HEADS * HEAD_DIM
D_KV_NSA = NSA_KV_GROUPS * HEAD_DIM
D_GDN = GDN_HEADS * HEAD_DIM
D_MIX = D_NSA + D_GDN

XA_HEADS = 4
D_XA = XA_HEADS * HEAD_DIM
N_EXPERTS = 8
TOP_K = 2

COL_QKV_B = 0
COL_Q_A = COL_QKV_B + 3 * D_GDN
COL_Z_B = COL_Q_A + D_NSA
COL_KV_A = COL_Z_B + D_GDN
COL_SMALL = COL_KV_A + 6 * D_KV_NSA
SMALL_W = 512
N_IN_R = COL_SMALL + SMALL_W
SM_BETA = 32
SM_A = 48

VMEM_LIMIT = 58 * 1024 * 1024

NT_DIMS = (((1,), (1,)), ((), ()))


def _rms(x, gain):
    return x * lax.rsqrt(jnp.mean(x * x, axis=-1, keepdims=True) + EPS) * gain


def _dot(a, b):
    return jnp.dot(a, b, preferred_element_type=F32)


def _dot_nt(a, b):
    return lax.dot_general(a, b, NT_DIMS, preferred_element_type=F32)


def _dot3(x, sel):
    hi = x.astype(BF16)
    rem = x - hi.astype(F32)
    mid = rem.astype(BF16)
    lo = (rem - mid.astype(F32)).astype(BF16)
    return _dot(hi, sel) + _dot(mid, sel) + _dot(lo, sel)


def _params(sem, vmem=VMEM_LIMIT):
    return pltpu.CompilerParams(dimension_semantics=sem, vmem_limit_bytes=vmem)


def _norm_kernel(x_ref, g_ref, o_ref):
    o_ref[...] = _rms(x_ref[...], g_ref[...]).astype(o_ref.dtype)


def rmsnorm_bf16(x, gain, tm=512):
    M, D = x.shape
    return pl.pallas_call(
        _norm_kernel,
        grid=(M // tm,),
        in_specs=[pl.BlockSpec((tm, D), lambda i: (i, 0)), pl.BlockSpec((1, D), lambda i: (0, 0))],
        out_specs=pl.BlockSpec((tm, D), lambda i: (i, 0)),
        out_shape=jax.ShapeDtypeStruct((M, D), BF16),
        compiler_params=_params(("parallel",)),
        name="rmsnorm",
    )(x, gain.reshape(1, D))


def _route(h, rw_hi, rw_lo):
    h_hi = h.astype(BF16)
    h_lo = (h - h_hi.astype(F32)).astype(BF16)
    logits = _dot(h_hi, rw_hi) + _dot(h_hi, rw_lo) + _dot(h_lo, rw_hi)
    lane = lax.broadcasted_iota(I32, logits.shape, 1).astype(F32)
    logits = jnp.where(lane < N_EXPERTS, logits, -jnp.inf)
    m1 = jnp.max(logits, axis=-1, keepdims=True)
    i1 = jnp.min(jnp.where(logits == m1, lane, 128.0), axis=-1, keepdims=True)
    rest = jnp.where(lane == i1, -jnp.inf, logits)
    m2 = jnp.max(rest, axis=-1, keepdims=True)
    i2 = jnp.min(jnp.where(rest == m2, lane, 128.0), axis=-1, keepdims=True)
    e2 = jnp.exp(m2 - m1)
    den = 1.0 + e2
    idx = jnp.where(lane == 0, i1, jnp.where(lane == 1, i2, 0.0)).astype(I32)
    wts = jnp.where(lane == 0, 1.0 / den, jnp.where(lane == 1, e2 / den, 0.0))
    return idx, wts


def _mm_kernel(x_ref, w_ref, o_ref):
    o_ref[...] = _dot(x_ref[...], w_ref[...].astype(BF16)).astype(o_ref.dtype)


def matmul_cols(x, w, *, tm, tn, out_dtype=F32, name="matmul"):
    M, K = x.shape
    N = w.shape[1]
    return pl.pallas_call(
        _mm_kernel,
        grid=(M // tm, N // tn),
        in_specs=[pl.BlockSpec((tm, K), lambda i, j: (i, 0)), pl.BlockSpec((K, tn), lambda i, j: (0, j))],
        out_specs=pl.BlockSpec((tm, tn), lambda i, j: (i, j)),
        out_shape=jax.ShapeDtypeStruct((M, N), out_dtype),
        compiler_params=_params(("parallel", "parallel")),
        name=name,
    )(x, w)


def _mm2_res_kernel(xa_ref, xb_ref, wa_ref, wb_ref, r_ref, o_ref):
    acc = _dot(xa_ref[...], wa_ref[...].astype(BF16)) + _dot(xb_ref[...], wb_ref[...].astype(BF16))
    o_ref[...] = r_ref[...] + acc


def matmul2_residual(xa, xb, w, layer, residual, *, tm, tn, name):
    M, Ka = xa.shape
    Kb = xb.shape[1]
    assert Ka == Kb and w.shape[1] == Ka + Kb
    N = w.shape[2]
    return pl.pallas_call(
        _mm2_res_kernel,
        grid=(M // tm, N // tn),
        in_specs=[pl.BlockSpec((tm, Ka), lambda i, j: (i, 0)), pl.BlockSpec((tm, Kb), lambda i, j: (i, 0)),
                  pl.BlockSpec((None, Ka, tn), lambda i, j: (layer, 0, j)),
                  pl.BlockSpec((None, Kb, tn), lambda i, j: (layer, 1, j)),
                  pl.BlockSpec((tm, tn), lambda i, j: (i, j))],
        out_specs=pl.BlockSpec((tm, tn), lambda i, j: (i, j)),
        out_shape=jax.ShapeDtypeStruct((M, N), F32),
        compiler_params=_params(("parallel", "parallel")),
        name=name,
    )(xa, xb, w, w, residual)


def _swiglu_up_kernel(x_ref, wg_ref, wu_ref, o_ref):
    x = x_ref[...]
    gate = _dot(x, wg_ref[...].astype(BF16))
    up = _dot(x, wu_ref[...].astype(BF16))
    o_ref[...] = (gate * jax.nn.sigmoid(gate) * up).astype(o_ref.dtype)


def swiglu_up(x, w13, *, tm, tn):
    M, K = x.shape
    F = w13.shape[1] // 2
    nj = F // tn
    return pl.pallas_call(
        _swiglu_up_kernel,
        grid=(M // tm, nj),
        in_specs=[pl.BlockSpec((tm, K), lambda i, j: (i, 0), pipeline_mode=pl.Buffered(1)),
                  pl.BlockSpec((K, tn), lambda i, j: (0, j)),
                  pl.BlockSpec((K, tn), lambda i, j: (0, j + nj))],
        out_specs=pl.BlockSpec((tm, tn), lambda i, j: (i, j)),
        out_shape=jax.ShapeDtypeStruct((M, F), BF16),
        compiler_params=_params(("parallel", "parallel")),
        name="swiglu_up",
    )(x, w13, w13)


def _down_res_kernel(a_ref, w_ref, at_ref, wt_ref, r_ref, o_ref, *, n_main):
    k = pl.program_id(2)

    @pl.when(k == 0)
    def _():
        o_ref[...] = r_ref[...]

    @pl.when(k < n_main)
    def _():
        o_ref[...] += _dot(a_ref[...], w_ref[...].astype(BF16))

    @pl.when(k >= n_main)
    def _():
        o_ref[...] += _dot(at_ref[...], wt_ref[...].astype(BF16))


def down_proj_residual(a, w2, residual, *, tm, tn, tk, tk_tail):
    M, F = a.shape
    N = w2.shape[1]
    n_main = F // tk
    tail = F - n_main * tk
    assert tail % tk_tail == 0 and (n_main * tk) % tk_tail == 0
    n_tail = tail // tk_tail
    first_tail = (n_main * tk) // tk_tail if n_tail else 0
    main_k = lambda k: jnp.minimum(k, n_main - 1)
    tail_k = lambda k: first_tail + jnp.maximum(k - n_main, 0)
    return pl.pallas_call(
        functools.partial(_down_res_kernel, n_main=n_main),
        grid=(M // tm, N // tn, n_main + n_tail),
        in_specs=[pl.BlockSpec((tm, tk), lambda i, j, k: (i, main_k(k))),
                  pl.BlockSpec((tk, tn), lambda i, j, k: (main_k(k), j)),
                  pl.BlockSpec((tm, tk_tail), lambda i, j, k: (i, tail_k(k))),
                  pl.BlockSpec((tk_tail, tn), lambda i, j, k: (tail_k(k), j)),
                  pl.BlockSpec((tm, tn), lambda i, j, k: (i, j))],
        out_specs=pl.BlockSpec((tm, tn), lambda i, j, k: (i, j)),
        out_shape=jax.ShapeDtypeStruct((M, N), F32),
        compiler_params=_params(("parallel", "parallel", "arbitrary")),
        name="down_proj",
    )(a, w2, a, w2, residual)


def _new_weights(te_ref, i):
    return jnp.logical_or(i == 0, te_ref[i] != te_ref[jnp.maximum(i - 1, 0)])


def _moe_up_kernel(te_ref, nv_ref, x_ref, wg_ref, wu_ref, o_ref, wgb_ref, wub_ref):
    i = pl.program_id(1)
    valid = i < nv_ref[0]

    @pl.when(_new_weights(te_ref, i))
    def _():
        wgb_ref[...] = wg_ref[...].astype(BF16)
        wub_ref[...] = wu_ref[...].astype(BF16)

    @pl.when(valid)
    def _():
        x = x_ref[...]
        gate = _dot(x, wgb_ref[...])
        up = _dot(x, wub_ref[...])
        o_ref[...] = (gate * jax.nn.sigmoid(gate) * up).astype(o_ref.dtype)

    @pl.when(jnp.logical_not(valid))
    def _():
        o_ref[...] = jnp.zeros_like(o_ref)


def moe_up(xs, w13, tile_expert, n_valid, *, tm, tn):
    P, K = xs.shape
    F = w13.shape[2] // 2
    nj = F // tn
    return pl.pallas_call(
        _moe_up_kernel,
        grid_spec=pltpu.PrefetchScalarGridSpec(
            num_scalar_prefetch=2,
            grid=(nj, P // tm),
            in_specs=[pl.BlockSpec((tm, K), lambda j, i, te, nv: (jnp.minimum(i, nv[0] - 1), 0)),
                      pl.BlockSpec((None, K, tn), lambda j, i, te, nv: (te[i], 0, j)),
                      pl.BlockSpec((None, K, tn), lambda j, i, te, nv: (te[i], 0, j + nj))],
            out_specs=pl.BlockSpec((tm, tn), lambda j, i, te, nv: (i, j)),
            scratch_shapes=[pltpu.VMEM((K, tn), BF16), pltpu.VMEM((K, tn), BF16)],
        ),
        out_shape=jax.ShapeDtypeStruct((P, F), BF16),
        compiler_params=_params(("arbitrary", "arbitrary")),
        name="moe_up",
    )(tile_expert, n_valid, xs, w13, w13)


def _moe_down_kernel(te_ref, nv_ref, a_ref, w_ref, rw_ref, o_ref, wb_ref):
    i = pl.program_id(1)
    valid = i < nv_ref[0]

    @pl.when(_new_weights(te_ref, i))
    def _():
        wb_ref[...] = w_ref[...].astype(BF16)

    @pl.when(valid)
    def _():
        o_ref[...] = _dot(a_ref[...], wb_ref[...]) * rw_ref[:, 0:1]

    @pl.when(jnp.logical_not(valid))
    def _():
        o_ref[...] = jnp.zeros_like(o_ref)


def moe_down(act, w2, row_w, tile_expert, n_valid, *, tm, tn):
    P, F = act.shape
    N = w2.shape[2]
    return pl.pallas_call(
        _moe_down_kernel,
        grid_spec=pltpu.PrefetchScalarGridSpec(
            num_scalar_prefetch=2,
            grid=(N // tn, P // tm),
            in_specs=[pl.BlockSpec((tm, F), lambda j, i, te, nv: (jnp.minimum(i, nv[0] - 1), 0)),
                      pl.BlockSpec((None, F, tn), lambda j, i, te, nv: (te[i], 0, j)),
                      pl.BlockSpec((tm, 128), lambda j, i, te, nv: (i, 0))],
            out_specs=pl.BlockSpec((tm, tn), lambda j, i, te, nv: (i, j)),
            scratch_shapes=[pltpu.VMEM((F, tn), BF16)],
        ),
        out_shape=jax.ShapeDtypeStruct((P, N), F32),
        compiler_params=_params(("arbitrary", "arbitrary")),
        name="moe_down",
    )(tile_expert, n_valid, act, w2, row_w)


def moe_dispatch(top_i, top_w, tm):
    M = top_i.shape[0]
    A = M * TOP_K
    P = A + N_EXPERTS * tm
    e_flat = top_i.reshape(A)
    onehot = (e_flat[:, None] == jnp.arange(N_EXPERTS, dtype=I32)[None, :]).astype(I32)
    csum = jnp.cumsum(onehot, axis=0)
    counts = csum[-1]
    padded = ((counts + tm - 1) // tm) * tm
    pend = jnp.cumsum(padded)
    pstart = pend - padded
    dest = jnp.sum(onehot * (pstart[None, :] + csum - onehot), axis=1)
    place = dict(mode="promise_in_bounds", unique_indices=True)
    src_tok = jnp.zeros((P,), I32).at[dest].set(jnp.arange(A, dtype=I32) // TOP_K, **place)
    row_w = jnp.zeros((P,), F32).at[dest].set(top_w.reshape(A), **place)
    pos = dest.reshape(M, TOP_K)
    tile_start = jnp.arange(P // tm, dtype=I32) * tm
    tile_expert = jnp.minimum(jnp.sum((tile_start[:, None] >= pend[None, :]).astype(I32), axis=1), N_EXPERTS - 1)
    n_valid = (pend[-1:] // tm).astype(I32)
    last_e = tile_expert[jnp.maximum(n_valid[0] - 1, 0)]
    tile_expert = jnp.where(tile_start // tm < n_valid[0], tile_expert, last_e)
    return src_tok, row_w, pos, tile_expert, n_valid


def _nsa_prep_kernel(kc_ref, vc_ref, ks_ref, vs_ref, kw_ref, vw_ref, kg_ref, raw_ref, kso_ref, vso_ref, kwo_ref, vwo_ref):
    raw_ref[0] = kc_ref[...]
    raw_ref[1] = vc_ref[...]
    kg = kg_ref[...]
    kso_ref[...] = _rms(ks_ref[...], kg[1:2]).astype(BF16)
    vso_ref[...] = vs_ref[...].astype(BF16)
    kwo_ref[...] = _rms(kw_ref[...], kg[2:3]).astype(BF16)
    vwo_ref[...] = vw_ref[...].astype(BF16)


def nsa_prep(z, k_gain, B, S, ts=512):
    G = NSA_KV_GROUPS
    nt = S // ts
    cb = COL_KV_A // HEAD_DIM

    def col(br, kvi):
        return pl.BlockSpec((ts, HEAD_DIM), lambda b, g, t: (b * nt + t, cb + br * 4 + kvi * 2 + g))

    kv_out = pl.BlockSpec((None, None, ts, HEAD_DIM), lambda b, g, t: (b, g, t, 0))
    kv_shape = jax.ShapeDtypeStruct((B, G, S, HEAD_DIM), BF16)
    return pl.pallas_call(
        _nsa_prep_kernel,
        grid=(B, G, nt),
        in_specs=[col(0, 0), col(0, 1), col(1, 0), col(1, 1), col(2, 0), col(2, 1),
                  pl.BlockSpec((3, HEAD_DIM), lambda b, g, t: (0, 0))],
        out_specs=[pl.BlockSpec((None, 2, None, ts, HEAD_DIM), lambda b, g, t: (b, 0, g, t, 0)),
                   kv_out, kv_out, kv_out, kv_out],
        out_shape=[jax.ShapeDtypeStruct((B, 2, G, S, HEAD_DIM), F32), kv_shape, kv_shape, kv_shape, kv_shape],
        compiler_params=_params(("parallel", "parallel", "parallel")),
        name="nsa_prep",
    )(z, z, z, z, z, z, k_gain)


def _nsa_compress_kernel(r_ref, pos_ref, w1_ref, w2_ref, kg_ref, o_ref):
    kv = pl.program_id(1)
    half = (CMP_BLOCK // 2) * HEAD_DIM
    r = r_ref[...]
    pos = pos_ref[...]
    n = r.shape[0]
    first = _dot((r + pos[:, :half]).astype(BF16), w1_ref[:half, :].astype(BF16))
    second = _dot((r + pos[:, half:]).astype(BF16), w1_ref[half:, :].astype(BF16))
    pre = first + pltpu.roll(second, n - 1, axis=0)
    hid = jax.nn.gelu(pre)
    comp = _dot(hid.astype(BF16), w2_ref[...].astype(BF16))
    o_ref[...] = jnp.where(kv == 0, _rms(comp, kg_ref[...]), comp).astype(o_ref.dtype)


def nsa_compress(raw, cmp_pos, cmp_w1, cmp_w2, k_gain0, B, S):
    G = NSA_KV_GROUPS
    n = S // CMP_STRIDE
    feat = CMP_STRIDE * HEAD_DIM
    r = raw.reshape(B, 2, G, n, feat)
    pos = cmp_pos.reshape(2, 1, CMP_BLOCK * HEAD_DIM)
    return pl.pallas_call(
        _nsa_compress_kernel,
        grid=(B, 2, G),
        in_specs=[pl.BlockSpec((None, None, None, n, feat), lambda b, kv, g: (b, kv, g, 0, 0)),
                  pl.BlockSpec((None, 1, CMP_BLOCK * HEAD_DIM), lambda b, kv, g: (kv, 0, 0)),
                  pl.BlockSpec((None, CMP_BLOCK * HEAD_DIM, HEAD_DIM), lambda b, kv, g: (kv, 0, 0)),
                  pl.BlockSpec((None, HEAD_DIM, HEAD_DIM), lambda b, kv, g: (kv, 0, 0)),
                  pl.BlockSpec((1, HEAD_DIM), lambda b, kv, g: (0, 0))],
        out_specs=pl.BlockSpec((None, None, None, n, HEAD_DIM), lambda b, kv, g: (b, kv, g, 0, 0)),
        out_shape=jax.ShapeDtypeStruct((B, 2, G, n, HEAD_DIM), BF16),
        compiler_params=_params(("parallel", "parallel", "parallel")),
        name="nsa_compress",
    )(r, pos, cmp_w1, cmp_w2, k_gain0.reshape(1, HEAD_DIM))


def _rep(x, n):
    return x if n == HEAD_DIM else jnp.concatenate([x] * (n // HEAD_DIM), axis=1)


def _flash_step(qb, kt, vt, bias, m_ref, l_ref, acc_ref):
    nk = kt.shape[0]
    rows = qb.shape[0]
    s = _dot_nt(qb, kt)
    s = (s.reshape(rows // Q_BLOCK, Q_BLOCK, nk) + bias[None]).reshape(rows, nk)
    m_prev = m_ref[...]
    m_new = jnp.maximum(m_prev, jnp.max(s, axis=-1, keepdims=True))
    alpha = jnp.exp2(m_prev - m_new)
    p = jnp.exp2(s - _rep(m_new, nk))
    l_ref[...] = alpha * l_ref[...] + jnp.sum(p, axis=-1, keepdims=True)
    acc_ref[...] = alpha * acc_ref[...] + _dot(p.astype(BF16), vt)
    m_ref[...] = m_new


def _nsa_kernel(q_ref, sm_ref, kc_ref, vc_ref, ks_ref, vs_ref, kw_ref, vw_ref, qg_ref, og_ref, c2s_ref, exp_ref,
                o_ref, selb_ref, m_ref, l_ref, acc_ref, *, k_top):
    Hg = NSA_GROUP_SIZE
    rows = Hg * Q_BLOCK
    c = pl.program_id(2)
    t0 = c * Q_BLOCK
    n_cmp = kc_ref.shape[0]
    n_sel = c2s_ref.shape[0]

    q = q_ref[...]
    qs = jnp.concatenate([q[:, h * HEAD_DIM:(h + 1) * HEAD_DIM] for h in range(Hg)], axis=0)
    qb = (_rms(qs, qg_ref[...]) * (SCALE * LOG2E)).astype(BF16)

    tq = t0 + lax.broadcasted_iota(I32, (Q_BLOCK, n_cmp), 0)
    c_end = lax.broadcasted_iota(I32, (Q_BLOCK, n_cmp), 1) * CMP_STRIDE + (CMP_BLOCK - 1)
    sc = _dot_nt(qb, kc_ref[...]).reshape(Hg, Q_BLOCK, n_cmp)
    sc = jnp.where((c_end <= tq)[None], sc, -jnp.inf)
    mc = jnp.max(sc, axis=-1, keepdims=True)
    mc = jnp.where(mc > -jnp.inf, mc, 0.0)
    pc = jnp.exp2(sc - mc)
    dc = jnp.sum(pc, axis=-1, keepdims=True)
    pc = pc / jnp.where(dc > 0, dc, 1.0)
    o_cmp = _dot(pc.reshape(rows, n_cmp).astype(BF16), vc_ref[...])

    psum = jnp.sum(pc, axis=0)
    c2s = c2s_ref[...]
    p_hi = psum.astype(BF16)
    rem = psum - p_hi.astype(F32)
    p_mid = rem.astype(BF16)
    p_lo = (rem - p_mid.astype(F32)).astype(BF16)
    imp = _dot_nt(c2s, p_hi) + _dot_nt(c2s, p_mid) + _dot_nt(c2s, p_lo)

    jj = lax.broadcasted_iota(I32, (n_sel, Q_BLOCK), 0)
    cur = lax.shift_right_logical(t0 + lax.broadcasted_iota(I32, (n_sel, Q_BLOCK), 1), 6)
    valid = jj <= cur
    forced = valid & ((jj == 0) | (jj > cur - SEL_LOCAL))
    score = jnp.where(forced, FORCE, jnp.where(valid, imp, -FORCE))
    rank = jnp.zeros((n_sel, Q_BLOCK), F32)
    for j2 in range(n_sel):
        other = score[j2:j2 + 1, :]
        tie = jnp.where(jj > j2, 1.0, 0.0)
        rank = rank + jnp.where(other > score, 1.0, jnp.where(other == score, tie, 0.0))
    sel_t = jnp.where(rank < k_top, 1.0, 0.0)
    if n_sel < Q_BLOCK:
        sel_t = jnp.concatenate([sel_t, jnp.zeros((Q_BLOCK - n_sel, Q_BLOCK), F32)], axis=0)
    sel = sel_t.T[:, :n_sel].astype(BF16)
    sel_keys = _dot(sel, exp_ref[...])
    for i in range(selb_ref.shape[0]):
        selb_ref[i] = (sel_keys[:, i * SEL_TILE:(i + 1) * SEL_TILE] - 1.0) * (-NEG)

    def reset():
        m_ref[...] = jnp.full(m_ref.shape, NEG, F32)
        l_ref[...] = jnp.zeros(l_ref.shape, F32)
        acc_ref[...] = jnp.zeros(acc_ref.shape, F32)

    reset()
    tq_s = t0 + lax.broadcasted_iota(I32, (Q_BLOCK, SEL_TILE), 0)
    kk_s = lax.broadcasted_iota(I32, (Q_BLOCK, SEL_TILE), 1)

    def sel_body(i, carry):
        k0 = pl.multiple_of(i * SEL_TILE, SEL_TILE)
        bias = selb_ref[i] + jnp.where(kk_s + k0 <= tq_s, 0.0, NEG)
        _flash_step(qb, ks_ref[pl.ds(k0, SEL_TILE), :], vs_ref[pl.ds(k0, SEL_TILE), :], bias, m_ref, l_ref, acc_ref)
        return carry

    lax.fori_loop(0, (t0 + Q_BLOCK + SEL_TILE - 1) // SEL_TILE, sel_body, 0)
    o_sel = acc_ref[...] / l_ref[...]

    wk = WINDOW + Q_BLOCK
    w0 = pl.multiple_of(jnp.maximum(t0 - WINDOW, 0), Q_BLOCK)
    tq_w = t0 + lax.broadcasted_iota(I32, (Q_BLOCK, wk), 0)
    wpos = w0 + lax.broadcasted_iota(I32, (Q_BLOCK, wk), 1)
    w_bias = jnp.where((wpos <= tq_w) & (wpos > tq_w - WINDOW), 0.0, NEG)
    sw = _dot_nt(qb, kw_ref[pl.ds(w0, wk), :]).reshape(Hg, Q_BLOCK, wk) + w_bias[None]
    pw = jnp.exp2(sw - jnp.max(sw, axis=-1, keepdims=True))
    lw = jnp.sum(pw, axis=-1, keepdims=True).reshape(rows, 1)
    o_win = _dot(pw.reshape(rows, wk).astype(BF16), vw_ref[pl.ds(w0, wk), :]) / lw

    gates = jax.nn.sigmoid(sm_ref[...])
    og = og_ref[...]
    for h in range(Hg):
        r = slice(h * Q_BLOCK, (h + 1) * Q_BLOCK)
        o = (gates[:, h:h + 1] * o_cmp[r] + gates[:, Hg + h:Hg + h + 1] * o_sel[r]
             + gates[:, 2 * Hg + h:2 * Hg + h + 1] * o_win[r])
        o_ref[:, h * HEAD_DIM:(h + 1) * HEAD_DIM] = _rms(o, og).astype(o_ref.dtype)


def nsa_attention(z, comp, ks, vs, kw, vw, q_gain, out_gain, B, S):
    G, Hg = NSA_KV_GROUPS, NSA_GROUP_SIZE
    nq = S // Q_BLOCK
    n_cmp = S // CMP_STRIDE
    n_sel = S // SEL_BLOCK
    k_top = min(SEL_TOP_N, n_sel)
    c_start = np.arange(n_cmp)[:, None] * CMP_STRIDE
    s_start = np.arange(n_sel)[None, :] * SEL_BLOCK
    overlap = np.minimum(c_start + CMP_BLOCK, s_start + SEL_BLOCK) - np.maximum(c_start, s_start)
    assert n_sel <= Q_BLOCK
    c2s = jnp.asarray((np.clip(overlap, 0, None) / CMP_STRIDE).T, dtype=BF16)
    expand = jnp.asarray(np.arange(S)[None, :] // SEL_BLOCK == np.arange(n_sel)[:, None], dtype=BF16)
    rows = Hg * Q_BLOCK
    wq = Hg * HEAD_DIM
    kv_spec = pl.BlockSpec((None, None, S, HEAD_DIM), lambda b, g, c: (b, g, 0, 0))
    const = lambda b, g, c: (0, 0)
    return pl.pallas_call(
        functools.partial(_nsa_kernel, k_top=k_top),
        grid=(B, G, nq),
        in_specs=[pl.BlockSpec((Q_BLOCK, wq), lambda b, g, c: (b * nq + c, COL_Q_A // wq + g)),
                  pl.BlockSpec((Q_BLOCK, HEAD_DIM), lambda b, g, c: (b * nq + c, COL_SMALL // HEAD_DIM + g)),
                  pl.BlockSpec((None, None, None, n_cmp, HEAD_DIM), lambda b, g, c: (b, 0, g, 0, 0)),
                  pl.BlockSpec((None, None, None, n_cmp, HEAD_DIM), lambda b, g, c: (b, 1, g, 0, 0)),
                  kv_spec, kv_spec, kv_spec, kv_spec,
                  pl.BlockSpec((1, HEAD_DIM), const), pl.BlockSpec((1, HEAD_DIM), const),
                  pl.BlockSpec((n_sel, n_cmp), const), pl.BlockSpec((n_sel, S), const)],
        out_specs=pl.BlockSpec((Q_BLOCK, wq), lambda b, g, c: (b * nq + c, g)),
        out_shape=jax.ShapeDtypeStruct((B * S, D_NSA), BF16),
        scratch_shapes=[pltpu.VMEM((S // SEL_TILE, Q_BLOCK, SEL_TILE), F32),
                        pltpu.VMEM((rows, HEAD_DIM), F32), pltpu.VMEM((rows, HEAD_DIM), F32),
                        pltpu.VMEM((rows, HEAD_DIM), F32)],
        compiler_params=_params(("parallel", "parallel", "arbitrary")),
        name="nsa_attention",
    )(z, z, comp, comp, ks, vs, kw, vw, q_gain.reshape(1, HEAD_DIM), out_gain.reshape(1, HEAD_DIM), c2s, expand)


def _gdn_chunk_kernel(xc_ref, xh_ref, sm_ref, cw_ref, alog_ref, dtb_ref, rep_ref, unfold_ref, u_ref, wq_ref, ak_ref,
                      egl_ref):
    for b in range(xc_ref.shape[0]):
        _gdn_chunk_body(xc_ref.at[b], xh_ref.at[b], sm_ref.at[b], cw_ref, alog_ref, dtb_ref, rep_ref, unfold_ref,
                        u_ref.at[b], wq_ref.at[b], ak_ref.at[b], egl_ref.at[b])


def _gdn_chunk_body(xc_ref, xh_ref, sm_ref, cw_ref, alog_ref, dtb_ref, rep_ref, unfold_ref, u_ref, wq_ref, ak_ref,
                    egl_ref):
    C = GDN_CHUNK
    PK = GDN_PACK
    R = PK * C
    n = pl.program_id(0)
    xp = jnp.concatenate([jnp.where(n > 0, xh_ref[...], 0.0), xc_ref[...]], axis=0)
    cw = cw_ref[...]
    y = None
    for j in range(GDN_CONV):
        shift = GDN_CONV - 1 - j
        xs = xp if shift == 0 else pltpu.roll(xp, shift, axis=0)
        term = xs[8:] * cw[j:j + 1, :]
        y = term if y is None else y + term
    y = y * jax.nn.sigmoid(y)

    sm = sm_ref[...]
    beta = jax.nn.sigmoid(sm)
    g = -jnp.exp(alog_ref[...]) * jax.nn.softplus(sm + dtb_ref[...])
    row = lax.broadcasted_iota(I32, g.shape, 0)
    gc = g
    d = 1
    while d < C:
        gc = gc + jnp.where(row >= d, pltpu.roll(gc, d, axis=0), 0.0)
        d *= 2
    g_last = gc[C - 1:C, :]

    ri = lax.broadcasted_iota(I32, (R, R), 0)
    ci = lax.broadcasted_iota(I32, (R, R), 1)
    same = lax.shift_right_logical(ri, 6) == lax.shift_right_logical(ci, 6)
    tri = same & (ri >= ci)
    strict = same & (ri > ci)
    SUB = GDN_SUB
    nb = R // SUB
    same16 = lax.shift_right_logical(ri, 4) == lax.shift_right_logical(ci, 4)
    same32 = lax.shift_right_logical(ri, 5) == lax.shift_right_logical(ci, 5)
    off32 = same32 & jnp.logical_not(same16)
    off64 = jnp.logical_not(same32)
    NG = GDN_HEADS // PK
    W4 = NG * SUB
    row_c = lax.broadcasted_iota(I32, (R, W4), 0) & (SUB - 1)
    col_s = lax.broadcasted_iota(I32, (nb, W4), 1) & (SUB - 1)

    groups = []
    for grp in range(NG):
        heads = [grp * PK + i for i in range(PK)]

        def stack(off):
            return jnp.concatenate([y[:, off + h * HEAD_DIM: off + (h + 1) * HEAD_DIM] for h in heads], axis=0)

        def col(x, base):
            return jnp.concatenate([x[:, base + h: base + h + 1] for h in heads], axis=0)

        q4 = stack(0)
        k4 = stack(D_GDN)
        v4 = stack(2 * D_GDN)
        q4 = q4 * lax.rsqrt(jnp.sum(q4 * q4, axis=-1, keepdims=True) + EPS) * SCALE
        k4 = k4 * lax.rsqrt(jnp.sum(k4 * k4, axis=-1, keepdims=True) + EPS)
        beta4 = col(beta, SM_BETA)
        gc4 = col(gc, SM_A)
        gl4 = jnp.concatenate([jnp.broadcast_to(g_last[:, SM_A + h: SM_A + h + 1], (C, 1)) for h in heads], axis=0)

        gb = jnp.broadcast_to(gc4, (R, R))
        decay = jnp.exp(jnp.where(tri, gb - gb.T, -jnp.inf))
        kb4 = k4 * beta4
        k4b = k4.astype(BF16)
        a = jnp.where(strict, _dot_nt(kb4.astype(BF16), k4b) * decay, 0.0)
        attn = _dot_nt(q4.astype(BF16), k4b) * decay
        at = a.T
        dct = jnp.concatenate([at[SUB * b:SUB * (b + 1), SUB * b:SUB * (b + 1)] for b in range(nb)], axis=0)
        groups.append((heads, q4, k4, v4, beta4, gc4, gl4, kb4, a, attn, dct))

    coef = _dot3(jnp.concatenate([g[-1] for g in groups], axis=1), rep_ref[...])
    t4 = jnp.where(row_c == (lax.broadcasted_iota(I32, (R, W4), 1) & (SUB - 1)), 1.0, 0.0)
    for i in range(1, SUB):
        s = jnp.sum((coef[:, i * HEAD_DIM: i * HEAD_DIM + W4] * t4).reshape(nb, SUB, W4), axis=1)
        new = jnp.where(col_s == i, 1.0, 0.0) - s
        t4 = jnp.where(row_c == i, jnp.broadcast_to(new[:, None, :], (nb, SUB, W4)).reshape(R, W4), t4)
    t_tiled = _dot3(t4, unfold_ref[...])

    for gi, (heads, q4, k4, v4, beta4, gc4, gl4, kb4, a, attn, _) in enumerate(groups):
        t16 = jnp.where(same16, t_tiled[:, gi * R:(gi + 1) * R], 0.0)
        t16b = t16.astype(BF16)
        a32 = jnp.where(off32, a, 0.0).astype(BF16)
        t32 = t16 - _dot(_dot(t16b, a32).astype(BF16), t16b)
        t32b = t32.astype(BF16)
        a64 = jnp.where(off64, a, 0.0).astype(BF16)
        t64 = t32 - _dot(_dot(t32b, a64).astype(BF16), t32b)
        eg = jnp.exp(gc4)
        rhs = jnp.concatenate([v4 * beta4, kb4 * eg], axis=1)
        rhs = _dot(t64.astype(BF16), rhs.astype(BF16))
        qd4 = q4 * eg
        kdt = (k4 * jnp.exp(gl4 - gc4)).T
        for i, h in enumerate(heads):
            r = slice(i * C, (i + 1) * C)
            u_ref[h] = rhs[r, :HEAD_DIM]
            wq_ref[h] = jnp.concatenate([rhs[r, HEAD_DIM:], qd4[r]], axis=0).astype(BF16)
            ak_ref[h] = jnp.concatenate([attn[r, r], kdt[:, r]], axis=0).astype(BF16)
            egl_ref[h] = jnp.broadcast_to(jnp.exp(g_last[:, SM_A + h: SM_A + h + 1]), (8, HEAD_DIM))


def gdn_chunks(z, conv_w, a_log, dt_bias, B, S):
    C, H = GDN_CHUNK, GDN_HEADS
    N = S // C
    W = 3 * D_GDN
    pad = lambda v: jnp.zeros((1, HEAD_DIM), F32).at[0, SM_A:SM_A + H].set(v)
    out5 = lambda r, cdim: pl.BlockSpec((B, None, H, r, cdim), lambda n: (0, n, 0, 0, 0))
    NG, SUB, R = H // GDN_PACK, GDN_SUB, GDN_PACK * C
    W4 = NG * SUB
    src = np.arange(W4)[:, None]
    dst = np.arange(SUB * HEAD_DIM)[None, :]
    lane = dst % HEAD_DIM
    rep = (lane < W4) & (src // SUB == lane // SUB) & (src % SUB == dst // HEAD_DIM)
    dst = np.arange(NG * R)[None, :]
    unfold = (src // SUB == dst // R) & (src % SUB == dst % SUB)
    z3 = z.reshape(B, S, z.shape[1])
    const = lambda n: (0, 0)
    return pl.pallas_call(
        _gdn_chunk_kernel,
        grid=(N,),
        in_specs=[pl.BlockSpec((B, C, W), lambda n: (0, n, 0)),
                  pl.BlockSpec((B, 8, W), lambda n: (0, jnp.maximum(n * (C // 8) - 1, 0), 0)),
                  pl.BlockSpec((B, C, HEAD_DIM), lambda n: (0, n, COL_SMALL // HEAD_DIM)),
                  pl.BlockSpec((GDN_CONV, W), const),
                  pl.BlockSpec((1, HEAD_DIM), const),
                  pl.BlockSpec((1, HEAD_DIM), const),
                  pl.BlockSpec((W4, SUB * HEAD_DIM), const),
                  pl.BlockSpec((W4, NG * R), const)],
        out_specs=[out5(C, HEAD_DIM), out5(2 * C, HEAD_DIM), out5(C + HEAD_DIM, C), out5(8, HEAD_DIM)],
        out_shape=[jax.ShapeDtypeStruct((B, N, H, C, HEAD_DIM), F32),
                   jax.ShapeDtypeStruct((B, N, H, 2 * C, HEAD_DIM), BF16),
                   jax.ShapeDtypeStruct((B, N, H, C + HEAD_DIM, C), BF16),
                   jax.ShapeDtypeStruct((B, N, H, 8, HEAD_DIM), F32)],
        compiler_params=_params(("parallel",)),
        name="gdn_chunks",
    )(z3, z3, z3, conv_w, pad(a_log), pad(dt_bias), jnp.asarray(rep, dtype=BF16), jnp.asarray(unfold, dtype=BF16))


def _gdn_scan_kernel(u_ref, wq_ref, ak_ref, egl_ref, z_ref, og_ref, o_ref, s_ref):
    C = GDN_CHUNK

    @pl.when(pl.program_id(0) == 0)
    def _():
        s_ref[...] = jnp.zeros(s_ref.shape, F32)

    og = og_ref[...]
    for b in range(u_ref.shape[0]):
        for h in range(GDN_HEADS):
            state = s_ref[b, h]
            ws = _dot(wq_ref[b, h], state.astype(BF16))
            v_new = u_ref[b, h] - ws[:C]
            av = _dot(ak_ref[b, h], v_new.astype(BF16))
            o = ws[C:] + av[:C]
            decayed = (state.reshape(HEAD_DIM // 8, 8, HEAD_DIM) * egl_ref[b, h][None]).reshape(HEAD_DIM, HEAD_DIM)
            s_ref[b, h] = decayed + av[C:]
            zh = z_ref[b, :, h * HEAD_DIM:(h + 1) * HEAD_DIM]
            o_ref[b, :, h * HEAD_DIM:(h + 1) * HEAD_DIM] = (_rms(o, og) * (zh * jax.nn.sigmoid(zh))).astype(o_ref.dtype)


def gdn_scan(u, wq, ak, egl, z, out_gain, B, S):
    C, H = GDN_CHUNK, GDN_HEADS
    N = S // C
    in5 = lambda r, cdim: pl.BlockSpec((B, None, H, r, cdim), lambda n: (0, n, 0, 0, 0))
    y = pl.pallas_call(
        _gdn_scan_kernel,
        grid=(N,),
        in_specs=[in5(C, HEAD_DIM), in5(2 * C, HEAD_DIM), in5(C + HEAD_DIM, C), in5(8, HEAD_DIM),
                  pl.BlockSpec((B, C, D_GDN), lambda n: (0, n, COL_Z_B // D_GDN)),
                  pl.BlockSpec((1, HEAD_DIM), lambda n: (0, 0))],
        out_specs=pl.BlockSpec((B, C, D_GDN), lambda n: (0, n, 0)),
        out_shape=jax.ShapeDtypeStruct((B, S, D_GDN), BF16),
        scratch_shapes=[pltpu.VMEM((B, H, HEAD_DIM, HEAD_DIM), F32)],
        compiler_params=_params(("arbitrary",)),
        name="gdn_scan",
    )(u, wq, ak, egl, z.reshape(B, S, z.shape[1]), out_gain.reshape(1, HEAD_DIM))
    return y.reshape(B * S, D_GDN)


def _xa_kv_kernel(mem_ref, g_ref, w_ref, kg_ref, k_ref, v_ref):
    kv = _dot(_rms(mem_ref[...], g_ref[...]).astype(BF16), w_ref[...])
    for h in range(XA_HEADS):
        k_ref[h] = _rms(kv[:, h * HEAD_DIM:(h + 1) * HEAD_DIM], kg_ref[...]).astype(BF16)
        v_ref[h] = kv[:, D_XA + h * HEAD_DIM: D_XA + (h + 1) * HEAD_DIM].astype(BF16)


def xa_kv(mem, mem_norm, wkv_bf16, k_gain):
    B, Mm, D = mem.shape
    spec = pl.BlockSpec((None, XA_HEADS, Mm, HEAD_DIM), lambda b: (b, 0, 0, 0))
    shape = jax.ShapeDtypeStruct((B, XA_HEADS, Mm, HEAD_DIM), BF16)
    return pl.pallas_call(
        _xa_kv_kernel,
        grid=(B,),
        in_specs=[pl.BlockSpec((None, Mm, D), lambda b: (b, 0, 0)), pl.BlockSpec((1, D), lambda b: (0, 0)),
                  pl.BlockSpec((D, 2 * D_XA), lambda b: (0, 0)), pl.BlockSpec((1, HEAD_DIM), lambda b: (0, 0))],
        out_specs=[spec, spec],
        out_shape=[shape, shape],
        compiler_params=_params(("parallel",)),
        name="xa_kv",
    )(mem, mem_norm.reshape(1, D), wkv_bf16, k_gain.reshape(1, HEAD_DIM))


def _xa_kernel(x_ref, g_ref, wq_ref, k_ref, v_ref, qg_ref, wo_ref, fg_ref, *rest, route):
    if route:
        rwh_ref, rwl_ref, o_ref, h_ref, idx_ref, wt_ref = rest
    else:
        o_ref, h_ref = rest
    x = x_ref[...]
    q = _dot(_rms(x, g_ref[...]).astype(BF16), wq_ref[...])
    outs = []
    for h in range(XA_HEADS):
        qh = _rms(q[:, h * HEAD_DIM:(h + 1) * HEAD_DIM], qg_ref[...]).astype(BF16)
        s = _dot_nt(qh, k_ref[h]) * SCALE
        p = jnp.exp(s - jnp.max(s, axis=-1, keepdims=True))
        p = p / jnp.sum(p, axis=-1, keepdims=True)
        outs.append(_dot(p.astype(BF16), v_ref[h]))
    o = jnp.concatenate(outs, axis=1).astype(BF16)
    x_new = x + _dot(o, wo_ref[...])
    o_ref[...] = x_new
    hn = _rms(x_new, fg_ref[...])
    h_ref[...] = hn.astype(h_ref.dtype)
    if route:
        idx_ref[...], wt_ref[...] = _route(hn, rwh_ref[...], rwl_ref[...])


def cross_attention(x, xa_norm, wq_bf16, k, v, q_gain, wo_bf16, ffn_norm, router_w, B, S, tm=256):
    M, D = x.shape
    Mm = k.shape[2]
    per_b = S // tm
    kv_spec = pl.BlockSpec((None, XA_HEADS, Mm, HEAD_DIM), lambda i: (i // per_b, 0, 0, 0))
    const = lambda i: (0, 0)
    row = lambda i: (i, 0)
    route = router_w is not None
    in_specs = [pl.BlockSpec((tm, D), row), pl.BlockSpec((1, D), const),
                pl.BlockSpec((D, D_XA), const), kv_spec, kv_spec,
                pl.BlockSpec((1, HEAD_DIM), const), pl.BlockSpec((D_XA, D), const), pl.BlockSpec((1, D), const)]
    args = [x, xa_norm.reshape(1, D), wq_bf16, k, v, q_gain.reshape(1, HEAD_DIM), wo_bf16, ffn_norm.reshape(1, D)]
    out_specs = [pl.BlockSpec((tm, D), row), pl.BlockSpec((tm, D), row)]
    out_shape = [jax.ShapeDtypeStruct((M, D), F32), jax.ShapeDtypeStruct((M, D), BF16)]
    if route:
        rw = jnp.pad(router_w, ((0, 0), (0, 128 - N_EXPERTS)))
        rw_hi = rw.astype(BF16)
        in_specs += [pl.BlockSpec((D, 128), const), pl.BlockSpec((D, 128), const)]
        args += [rw_hi, (rw - rw_hi.astype(F32)).astype(BF16)]
        out_specs += [pl.BlockSpec((tm, 128), row), pl.BlockSpec((tm, 128), row)]
        out_shape += [jax.ShapeDtypeStruct((M, 128), I32), jax.ShapeDtypeStruct((M, 128), F32)]
    return pl.pallas_call(
        functools.partial(_xa_kernel, route=route),
        grid=(M // tm,),
        in_specs=in_specs,
        out_specs=out_specs,
        out_shape=out_shape,
        compiler_params=_params(("parallel",)),
        name="cross_attention",
    )(*args)


def _reorder_w_in(w_in):
    o = np.cumsum((D_NSA, 6 * D_KV_NSA, 3 * NSA_HEADS, 3 * D_GDN, GDN_HEADS, GDN_HEADS, D_GDN))
    q_a, kv_a, gate_a, qkv_b = (0, o[0]), (o[0], o[1]), o[1], (o[2], o[3])
    beta_b, a_b, z_b = o[3], o[4], (o[5], o[6])
    Hg = NSA_GROUP_SIZE
    src = np.zeros((SMALL_W,), np.int64)
    used = np.zeros((SMALL_W,), bool)
    for g in range(NSA_KV_GROUPS):
        for br in range(3):
            for h in range(Hg):
                src[g * HEAD_DIM + br * Hg + h] = (g * Hg + h) * 3 + br
                used[g * HEAD_DIM + br * Hg + h] = True
    n_gate = 3 * NSA_HEADS
    for h in range(GDN_HEADS):
        src[SM_BETA + h] = n_gate + h
        src[SM_A + h] = n_gate + GDN_HEADS + h
        used[SM_BETA + h] = used[SM_A + h] = True
    narrow = jnp.concatenate([w_in[:, gate_a:gate_a + n_gate], w_in[:, beta_b:beta_b + 2 * GDN_HEADS]], axis=1)
    small = jnp.where(jnp.asarray(used)[None, :], jnp.take(narrow, jnp.asarray(src), axis=1), 0.0)
    parts = [w_in[:, qkv_b[0]:qkv_b[1]], w_in[:, q_a[0]:q_a[1]], w_in[:, z_b[0]:z_b[1]], w_in[:, kv_a[0]:kv_a[1]], small]
    return jnp.concatenate([p.astype(BF16) for p in parts], axis=1)


def mixer_layer(x2, B, S, attn_norm, w_in, nsa_q_gain, nsa_k_gain, nsa_cmp_pos, nsa_cmp_w1, nsa_cmp_w2, nsa_out_gain,
                gdn_conv_w, gdn_A_log, gdn_dt_bias, gdn_out_gain, w_out_stack, layer, tm=1024):
    h = rmsnorm_bf16(x2, attn_norm)
    z = matmul_cols(h, _reorder_w_in(w_in), tm=tm, tn=1024, name="in_proj")
    raw, ks, vs, kw, vw = nsa_prep(z, nsa_k_gain, B, S)
    comp = nsa_compress(raw, nsa_cmp_pos, nsa_cmp_w1, nsa_cmp_w2, nsa_k_gain[0], B, S)
    y_a = nsa_attention(z, comp, ks, vs, kw, vw, nsa_q_gain, nsa_out_gain, B, S)
    u, wq, ak, egl = gdn_chunks(z, gdn_conv_w, gdn_A_log, gdn_dt_bias, B, S)
    y_b = gdn_scan(u, wq, ak, egl, z, gdn_out_gain, B, S)
    return matmul2_residual(y_a, y_b, w_out_stack, layer, x2, tm=tm, tn=512, name="out_proj")


def xa_layer(x2, mem, B, S, xa_norm, mem_norm, xa_wq, xa_wkv, xa_q_gain, xa_k_gain, xa_wo, ffn_norm, router_w=None):
    k, v = xa_kv(mem, mem_norm, xa_wkv.astype(BF16), xa_k_gain)
    return cross_attention(x2, xa_norm, xa_wq.astype(BF16), k, v, xa_q_gain, xa_wo.astype(BF16), ffn_norm, router_w,
                           B, S)


def dense_ffn_layer(x2, h, w13, w2, tm=1024):
    act = swiglu_up(h, w13, tm=min(2 * tm, x2.shape[0]), tn=256)
    return down_proj_residual(act, w2, x2, tm=min(2 * tm, x2.shape[0]), tn=1024, tk=512, tk_tail=256)


def moe_ffn_layer(x2, h, idx, wts, w13, w2, tm=512, tn=512):
    src_tok, row_w, pos, tile_expert, n_valid = moe_dispatch(idx[:, :TOP_K], wts[:, :TOP_K], tm)
    rows_of = lambda a, idx: a.at[idx].get(mode="promise_in_bounds")
    xs = rows_of(h, src_tok)
    act = moe_up(xs, w13, tile_expert, n_valid, tm=tm, tn=tn)
    row_w_rep = jnp.broadcast_to(row_w[:, None], (row_w.shape[0], 128))
    out = moe_down(act, w2, row_w_rep, tile_expert, n_valid, tm=tm, tn=tn)
    return x2 + (rows_of(out, pos[:, 0]) + rows_of(out, pos[:, 1]))


def kernel(x, mem, attn_norm, w_in, nsa_q_gain, nsa_k_gain, nsa_cmp_pos, nsa_cmp_w1, nsa_cmp_w2, nsa_out_gain, gdn_conv_w, gdn_A_log, gdn_dt_bias, gdn_out_gain, w_out, xa_norm, mem_norm, xa_wq, xa_wkv, xa_q_gain, xa_k_gain, xa_wo, ffn_norm, dense_w13, dense_w2, router_w, moe_w13, moe_w2):
    B, S, D = x.shape
    x2 = x.reshape(B * S, D)
    for l in range(attn_norm.shape[0]):
        x2 = mixer_layer(x2, B, S, attn_norm[l], w_in[l], nsa_q_gain[l], nsa_k_gain[l], nsa_cmp_pos[l], nsa_cmp_w1[l],
                         nsa_cmp_w2[l], nsa_out_gain[l], gdn_conv_w[l], gdn_A_log[l], gdn_dt_bias[l], gdn_out_gain[l],
                         w_out, l)
        xa_args = (x2, mem, B, S, xa_norm[l], mem_norm[l], xa_wq[l], xa_wkv[l], xa_q_gain[l], xa_k_gain[l], xa_wo[l],
                   ffn_norm[l])
        if l % 2 == 0:
            x2, h = xa_layer(*xa_args)
            x2 = dense_ffn_layer(x2, h, dense_w13[l // 2], dense_w2[l // 2])
        else:
            x2, h, idx, wts = xa_layer(*xa_args, router_w[l // 2])
            x2 = moe_ffn_layer(x2, h, idx, wts, moe_w13[l // 2], moe_w2[l // 2])
    return x2.reshape(B, S, D)
```

```python
import functools

import jax
import jax.numpy as jnp
import numpy as np
from jax import lax
from jax.experimental import pallas as pl
from jax.experimental.pallas import tpu as pltpu

F32 = jnp.float32
BF16 = jnp.bfloat16
I32 = jnp.int32

D_MODEL = 4096
HEAD_DIM = 128
EPS = 1e-6
SCALE = HEAD_DIM ** -0.5

NSA_HEADS = 16
NSA_KV_GROUPS = 2
NSA_GROUP_SIZE = NSA_HEADS // NSA_KV_GROUPS
CMP_BLOCK = 32
CMP_STRIDE = 16
SEL_BLOCK = 64
SEL_TOP_N = 16
SEL_LOCAL = 2
WINDOW = 512
Q_BLOCK = 128
FORCE = 1e9
NEG = -1e30
SEL_TILE = 512
LOG2E = 1.4426950408889634

GDN_HEADS = 16
GDN_CONV = 4
GDN_CHUNK = 64
GDN_PACK = 4
GDN_SUB = 16

D_NSA = NSA_HEADS * HEAD_DIM
D_KV_NSA = NSA_KV_GROUPS * HEAD_DIM
D_GDN = GDN_HEADS * HEAD_DIM
D_MIX = D_NSA + D_GDN

XA_HEADS = 4
D_XA = XA_HEADS * HEAD_DIM
N_EXPERTS = 8
TOP_K = 2

COL_QKV_B = 0
COL_Q_A = COL_QKV_B + 3 * D_GDN
COL_Z_B = COL_Q_A + D_NSA
COL_KV_A = COL_Z_B + D_GDN
COL_SMALL = COL_KV_A + 6 * D_KV_NSA
SMALL_W = 512
N_IN_R = COL_SMALL + SMALL_W
SM_BETA = 32
SM_A = 48

VMEM_LIMIT = 58 * 1024 * 1024

NT_DIMS = (((1,), (1,)), ((), ()))


def _rms(x, gain):
    return x * lax.rsqrt(jnp.mean(x * x, axis=-1, keepdims=True) + EPS) * gain


def _dot(a, b):
    return jnp.dot(a, b, preferred_element_type=F32)


def _dot_nt(a, b):
    return lax.dot_general(a, b, NT_DIMS, preferred_element_type=F32)


def _dot3(x, sel):
    hi = x.astype(BF16)
    rem = x - hi.astype(F32)
    mid = rem.astype(BF16)
    lo = (rem - mid.astype(F32)).astype(BF16)
    return _dot(hi, sel) + _dot(mid, sel) + _dot(lo, sel)


def _params(sem, vmem=VMEM_LIMIT):
    return pltpu.CompilerParams(dimension_semantics=sem, vmem_limit_bytes=vmem)


def _norm_kernel(x_ref, g_ref, o_ref):
    o_ref[...] = _rms(x_ref[...], g_ref[...]).astype(o_ref.dtype)


def rmsnorm_bf16(x, gain, tm=512):
    M, D = x.shape
    return pl.pallas_call(
        _norm_kernel,
        grid=(M // tm,),
        in_specs=[pl.BlockSpec((tm, D), lambda i: (i, 0)), pl.BlockSpec((1, D), lambda i: (0, 0))],
        out_specs=pl.BlockSpec((tm, D), lambda i: (i, 0)),
        out_shape=jax.ShapeDtypeStruct((M, D), BF16),
        compiler_params=_params(("parallel",)),
        name="rmsnorm",
    )(x, gain.reshape(1, D))


def _route(h, rw_hi, rw_lo):
    h_hi = h.astype(BF16)
    h_lo = (h - h_hi.astype(F32)).astype(BF16)
    logits = _dot(h_hi, rw_hi) + _dot(h_hi, rw_lo) + _dot(h_lo, rw_hi)
    lane = lax.broadcasted_iota(I32, logits.shape, 1).astype(F32)
    logits = jnp.where(lane < N_EXPERTS, logits, -jnp.inf)
    m1 = jnp.max(logits, axis=-1, keepdims=True)
    i1 = jnp.min(jnp.where(logits == m1, lane, 128.0), axis=-1, keepdims=True)
    rest = jnp.where(lane == i1, -jnp.inf, logits)
    m2 = jnp.max(rest, axis=-1, keepdims=True)
    i2 = jnp.min(jnp.where(rest == m2, lane, 128.0), axis=-1, keepdims=True)
    e2 = jnp.exp(m2 - m1)
    den = 1.0 + e2
    idx = jnp.where(lane == 0, i1, jnp.where(lane == 1, i2, 0.0)).astype(I32)
    wts = jnp.where(lane == 0, 1.0 / den, jnp.where(lane == 1, e2 / den, 0.0))
    return idx, wts


def _mm_kernel(x_ref, w_ref, o_ref):
    o_ref[...] = _dot(x_ref[...], w_ref[...].astype(BF16)).astype(o_ref.dtype)


def matmul_cols(x, w, *, tm, tn, out_dtype=F32, name="matmul"):
    M, K = x.shape
    N = w.shape[1]
    return pl.pallas_call(
        _mm_kernel,
        grid=(M // tm, N // tn),
        in_specs=[pl.BlockSpec((tm, K), lambda i, j: (i, 0)), pl.BlockSpec((K, tn), lambda i, j: (0, j))],
        out_specs=pl.BlockSpec((tm, tn), lambda i, j: (i, j)),
        out_shape=jax.ShapeDtypeStruct((M, N), out_dtype),
        compiler_params=_params(("parallel", "parallel")),
        name=name,
    )(x, w)


def _mm2_res_kernel(xa_ref, xb_ref, wa_ref, wb_ref, r_ref, o_ref):
    acc = _dot(xa_ref[...], wa_ref[...].astype(BF16)) + _dot(xb_ref[...], wb_ref[...].astype(BF16))
    o_ref[...] = r_ref[...] + acc


def matmul2_residual(xa, xb, w, layer, residual, *, tm, tn, name):
    M, Ka = xa.shape
    Kb = xb.shape[1]
    assert Ka == Kb and w.shape[1] == Ka + Kb
    N = w.shape[2]
    return pl.pallas_call(
        _mm2_res_kernel,
        grid=(M // tm, N // tn),
        in_specs=[pl.BlockSpec((tm, Ka), lambda i, j: (i, 0)), pl.BlockSpec((tm, Kb), lambda i, j: (i, 0)),
                  pl.BlockSpec((None, Ka, tn), lambda i, j: (layer, 0, j)),
                  pl.BlockSpec((None, Kb, tn), lambda i, j: (layer, 1, j)),
                  pl.BlockSpec((tm, tn), lambda i, j: (i, j))],
        out_specs=pl.BlockSpec((tm, tn), lambda i, j: (i, j)),
        out_shape=jax.ShapeDtypeStruct((M, N), F32),
        compiler_params=_params(("parallel", "parallel")),
        name=name,
    )(xa, xb, w, w, residual)


def _swiglu_up_kernel(x_ref, wg_ref, wu_ref, o_ref):
    x = x_ref[...]
    gate = _dot(x, wg_ref[...].astype(BF16))
    up = _dot(x, wu_ref[...].astype(BF16))
    o_ref[...] = (gate * jax.nn.sigmoid(gate) * up).astype(o_ref.dtype)


def swiglu_up(x, w13, *, tm, tn):
    M, K = x.shape
    F = w13.shape[1] // 2
    nj = F // tn
    return pl.pallas_call(
        _swiglu_up_kernel,
        grid=(M // tm, nj),
        in_specs=[pl.BlockSpec((tm, K), lambda i, j: (i, 0), pipeline_mode=pl.Buffered(1)),
                  pl.BlockSpec((K, tn), lambda i, j: (0, j)),
                  pl.BlockSpec((K, tn), lambda i, j: (0, j + nj))],
        out_specs=pl.BlockSpec((tm, tn), lambda i, j: (i, j)),
        out_shape=jax.ShapeDtypeStruct((M, F), BF16),
        compiler_params=_params(("parallel", "parallel")),
        name="swiglu_up",
    )(x, w13, w13)


def _down_res_kernel(a_ref, w_ref, at_ref, wt_ref, r_ref, o_ref, *, n_main):
    k = pl.program_id(2)

    @pl.when(k == 0)
    def _():
        o_ref[...] = r_ref[...]

    @pl.when(k < n_main)
    def _():
        o_ref[...] += _dot(a_ref[...], w_ref[...].astype(BF16))

    @pl.when(k >= n_main)
    def _():
        o_ref[...] += _dot(at_ref[...], wt_ref[...].astype(BF16))


def down_proj_residual(a, w2, residual, *, tm, tn, tk, tk_tail):
    M, F = a.shape
    N = w2.shape[1]
    n_main = F // tk
    tail = F - n_main * tk
    assert tail % tk_tail == 0 and (n_main * tk) % tk_tail == 0
    n_tail = tail // tk_tail
    first_tail = (n_main * tk) // tk_tail if n_tail else 0
    main_k = lambda k: jnp.minimum(k, n_main - 1)
    tail_k = lambda k: first_tail + jnp.maximum(k - n_main, 0)
    return pl.pallas_call(
        functools.partial(_down_res_kernel, n_main=n_main),
        grid=(M // tm, N // tn, n_main + n_tail),
        in_specs=[pl.BlockSpec((tm, tk), lambda i, j, k: (i, main_k(k))),
                  pl.BlockSpec((tk, tn), lambda i, j, k: (main_k(k), j)),
                  pl.BlockSpec((tm, tk_tail), lambda i, j, k: (i, tail_k(k))),
                  pl.BlockSpec((tk_tail, tn), lambda i, j, k: (tail_k(k), j)),
                  pl.BlockSpec((tm, tn), lambda i, j, k: (i, j))],
        out_specs=pl.BlockSpec((tm, tn), lambda i, j, k: (i, j)),
        out_shape=jax.ShapeDtypeStruct((M, N), F32),
        compiler_params=_params(("parallel", "parallel", "arbitrary")),
        name="down_proj",
    )(a, w2, a, w2, residual)


def _new_weights(te_ref, i):
    return jnp.logical_or(i == 0, te_ref[i] != te_ref[jnp.maximum(i - 1, 0)])


def _moe_up_kernel(te_ref, nv_ref, x_ref, wg_ref, wu_ref, o_ref, wgb_ref, wub_ref):
    i = pl.program_id(1)
    valid = i < nv_ref[0]

    @pl.when(_new_weights(te_ref, i))
    def _():
        wgb_ref[...] = wg_ref[...].astype(BF16)
        wub_ref[...] = wu_ref[...].astype(BF16)

    @pl.when(valid)
    def _():
        x = x_ref[...]
        gate = _dot(x, wgb_ref[...])
        up = _dot(x, wub_ref[...])
        o_ref[...] = (gate * jax.nn.sigmoid(gate) * up).astype(o_ref.dtype)

    @pl.when(jnp.logical_not(valid))
    def _():
        o_ref[...] = jnp.zeros_like(o_ref)


def moe_up(xs, w13, tile_expert, n_valid, *, tm, tn):
    P, K = xs.shape
    F = w13.shape[2] // 2
    nj = F // tn
    return pl.pallas_call(
        _moe_up_kernel,
        grid_spec=pltpu.PrefetchScalarGridSpec(
            num_scalar_prefetch=2,
            grid=(nj, P // tm),
            in_specs=[pl.BlockSpec((tm, K), lambda j, i, te, nv: (jnp.minimum(i, nv[0] - 1), 0)),
                      pl.BlockSpec((None, K, tn), lambda j, i, te, nv: (te[i], 0, j)),
                      pl.BlockSpec((None, K, tn), lambda j, i, te, nv: (te[i], 0, j + nj))],
            out_specs=pl.BlockSpec((tm, tn), lambda j, i, te, nv: (i, j)),
            scratch_shapes=[pltpu.VMEM((K, tn), BF16), pltpu.VMEM((K, tn), BF16)],
        ),
        out_shape=jax.ShapeDtypeStruct((P, F), BF16),
        compiler_params=_params(("arbitrary", "arbitrary")),
        name="moe_up",
    )(tile_expert, n_valid, xs, w13, w13)


def _moe_down_kernel(te_ref, nv_ref, a_ref, w_ref, rw_ref, o_ref, wb_ref):
    i = pl.program_id(1)
    valid = i < nv_ref[0]

    @pl.when(_new_weights(te_ref, i))
    def _():
        wb_ref[...] = w_ref[...].astype(BF16)

    @pl.when(valid)
    def _():
        o_ref[...] = _dot(a_ref[...], wb_ref[...]) * rw_ref[:, 0:1]

    @pl.when(jnp.logical_not(valid))
    def _():
        o_ref[...] = jnp.zeros_like(o_ref)


def moe_down(act, w2, row_w, tile_expert, n_valid, *, tm, tn):
    P, F = act.shape
    N = w2.shape[2]
    return pl.pallas_call(
        _moe_down_kernel,
        grid_spec=pltpu.PrefetchScalarGridSpec(
            num_scalar_prefetch=2,
            grid=(N // tn, P // tm),
            in_specs=[pl.BlockSpec((tm, F), lambda j, i, te, nv: (jnp.minimum(i, nv[0] - 1), 0)),
                      pl.BlockSpec((None, F, tn), lambda j, i, te, nv: (te[i], 0, j)),
                      pl.BlockSpec((tm, 128), lambda j, i, te, nv: (i, 0))],
            out_specs=pl.BlockSpec((tm, tn), lambda j, i, te, nv: (i, j)),
            scratch_shapes=[pltpu.VMEM((F, tn), BF16)],
        ),
        out_shape=jax.ShapeDtypeStruct((P, N), F32),
        compiler_params=_params(("arbitrary", "arbitrary")),
        name="moe_down",
    )(tile_expert, n_valid, act, w2, row_w)


def moe_dispatch(top_i, top_w, tm):
    M = top_i.shape[0]
    A = M * TOP_K
    P = A + N_EXPERTS * tm
    e_flat = top_i.reshape(A)
    onehot = (e_flat[:, None] == jnp.arange(N_EXPERTS, dtype=I32)[None, :]).astype(I32)
    csum = jnp.cumsum(onehot, axis=0)
    counts = csum[-1]
    padded = ((counts + tm - 1) // tm) * tm
    pend = jnp.cumsum(padded)
    pstart = pend - padded
    dest = jnp.sum(onehot * (pstart[None, :] + csum - onehot), axis=1)
    place = dict(mode="promise_in_bounds", unique_indices=True)
    tok = (jnp.arange(A, dtype=I32) // TOP_K).astype(F32)
    placed = jnp.zeros((P, 2), F32).at[dest].set(jnp.stack([tok, top_w.reshape(A)], axis=1), **place)
    src_tok = placed[:, 0].astype(I32)
    row_w = placed[:, 1]
    pos = dest.reshape(M, TOP_K)
    tile_start = jnp.arange(P // tm, dtype=I32) * tm
    tile_expert = jnp.minimum(jnp.sum((tile_start[:, None] >= pend[None, :]).astype(I32), axis=1), N_EXPERTS - 1)
    n_valid = (pend[-1:] // tm).astype(I32)
    last_e = tile_expert[jnp.maximum(n_valid[0] - 1, 0)]
    tile_expert = jnp.where(tile_start // tm < n_valid[0], tile_expert, last_e)
    return src_tok, row_w, pos, tile_expert, n_valid


def _nsa_prep_kernel(kc_ref, vc_ref, ks_ref, vs_ref, kw_ref, vw_ref, kg_ref, raw_ref, kso_ref, vso_ref, kwo_ref, vwo_ref):
    raw_ref[0] = kc_ref[...]
    raw_ref[1] = vc_ref[...]
    kg = kg_ref[...]
    kso_ref[...] = _rms(ks_ref[...], kg[1:2]).astype(BF16)
    vso_ref[...] = vs_ref[...].astype(BF16)
    kwo_ref[...] = _rms(kw_ref[...], kg[2:3]).astype(BF16)
    vwo_ref[...] = vw_ref[...].astype(BF16)


def nsa_prep(z, k_gain, B, S, ts=512):
    G = NSA_KV_GROUPS
    nt = S // ts
    cb = COL_KV_A // HEAD_DIM

    def col(br, kvi):
        return pl.BlockSpec((ts, HEAD_DIM), lambda b, g, t: (b * nt + t, cb + br * 4 + kvi * 2 + g))

    kv_out = pl.BlockSpec((None, None, ts, HEAD_DIM), lambda b, g, t: (b, g, t, 0))
    kv_shape = jax.ShapeDtypeStruct((B, G, S, HEAD_DIM), BF16)
    return pl.pallas_call(
        _nsa_prep_kernel,
        grid=(B, G, nt),
        in_specs=[col(0, 0), col(0, 1), col(1, 0), col(1, 1), col(2, 0), col(2, 1),
                  pl.BlockSpec((3, HEAD_DIM), lambda b, g, t: (0, 0))],
        out_specs=[pl.BlockSpec((None, 2, None, ts, HEAD_DIM), lambda b, g, t: (b, 0, g, t, 0)),
                   kv_out, kv_out, kv_out, kv_out],
        out_shape=[jax.ShapeDtypeStruct((B, 2, G, S, HEAD_DIM), F32), kv_shape, kv_shape, kv_shape, kv_shape],
        compiler_params=_params(("parallel", "parallel", "parallel")),
        name="nsa_prep",
    )(z, z, z, z, z, z, k_gain)


def _nsa_compress_kernel(r_ref, pos_ref, w1_ref, w2_ref, kg_ref, o_ref):
    kv = pl.program_id(1)
    half = (CMP_BLOCK // 2) * HEAD_DIM
    r = r_ref[...]
    pos = pos_ref[...]
    n = r.shape[0]
    first = _dot((r + pos[:, :half]).astype(BF16), w1_ref[:half, :].astype(BF16))
    second = _dot((r + pos[:, half:]).astype(BF16), w1_ref[half:, :].astype(BF16))
    pre = first + pltpu.roll(second, n - 1, axis=0)
    hid = jax.nn.gelu(pre)
    comp = _dot(hid.astype(BF16), w2_ref[...].astype(BF16))
    o_ref[...] = jnp.where(kv == 0, _rms(comp, kg_ref[...]), comp).astype(o_ref.dtype)


def nsa_compress(raw, cmp_pos, cmp_w1, cmp_w2, k_gain0, B, S):
    G = NSA_KV_GROUPS
    n = S // CMP_STRIDE
    feat = CMP_STRIDE * HEAD_DIM
    r = raw.reshape(B, 2, G, n, feat)
    pos = cmp_pos.reshape(2, 1, CMP_BLOCK * HEAD_DIM)
    return pl.pallas_call(
        _nsa_compress_kernel,
        grid=(B, 2, G),
        in_specs=[pl.BlockSpec((None, None, None, n, feat), lambda b, kv, g: (b, kv, g, 0, 0)),
                  pl.BlockSpec((None, 1, CMP_BLOCK * HEAD_DIM), lambda b, kv, g: (kv, 0, 0)),
                  pl.BlockSpec((None, CMP_BLOCK * HEAD_DIM, HEAD_DIM), lambda b, kv, g: (kv, 0, 0)),
                  pl.BlockSpec((None, HEAD_DIM, HEAD_DIM), lambda b, kv, g: (kv, 0, 0)),
                  pl.BlockSpec((1, HEAD_DIM), lambda b, kv, g: (0, 0))],
        out_specs=pl.BlockSpec((None, None, None, n, HEAD_DIM), lambda b, kv, g: (b, kv, g, 0, 0)),
        out_shape=jax.ShapeDtypeStruct((B, 2, G, n, HEAD_DIM), BF16),
        compiler_params=_params(("parallel", "parallel", "parallel")),
        name="nsa_compress",
    )(r, pos, cmp_w1, cmp_w2, k_gain0.reshape(1, HEAD_DIM))


def _rep(x, n):
    return x if n == HEAD_DIM else jnp.concatenate([x] * (n // HEAD_DIM), axis=1)


def _flash_step(qb, kt, vt, bias, m_ref, l_ref, acc_ref):
    nk = kt.shape[0]
    rows = qb.shape[0]
    s = _dot_nt(qb, kt)
    s = (s.reshape(rows // Q_BLOCK, Q_BLOCK, nk) + bias[None]).reshape(rows, nk)
    m_prev = m_ref[...]
    m_new = jnp.maximum(m_prev, jnp.max(s, axis=-1, keepdims=True))
    alpha = jnp.exp2(m_prev - m_new)
    p = jnp.exp2(s - _rep(m_new, nk))
    l_ref[...] = alpha * l_ref[...] + jnp.sum(p, axis=-1, keepdims=True)
    acc_ref[...] = alpha * acc_ref[...] + _dot(p.astype(BF16), vt)
    m_ref[...] = m_new


def _nsa_kernel(q_ref, sm_ref, kc_ref, vc_ref, ks_ref, vs_ref, kw_ref, vw_ref, qg_ref, og_ref, c2s_ref, exp_ref,
                o_ref, selb_ref, m_ref, l_ref, acc_ref, *, k_top):
    Hg = NSA_GROUP_SIZE
    rows = Hg * Q_BLOCK
    c = pl.program_id(2)
    t0 = c * Q_BLOCK
    n_cmp = kc_ref.shape[0]
    n_sel = c2s_ref.shape[0]

    q = q_ref[...]
    qs = jnp.concatenate([q[:, h * HEAD_DIM:(h + 1) * HEAD_DIM] for h in range(Hg)], axis=0)
    qb = (_rms(qs, qg_ref[...]) * (SCALE * LOG2E)).astype(BF16)

    tq = t0 + lax.broadcasted_iota(I32, (Q_BLOCK, n_cmp), 0)
    c_end = lax.broadcasted_iota(I32, (Q_BLOCK, n_cmp), 1) * CMP_STRIDE + (CMP_BLOCK - 1)
    sc = _dot_nt(qb, kc_ref[...]).reshape(Hg, Q_BLOCK, n_cmp)
    sc = jnp.where((c_end <= tq)[None], sc, -jnp.inf)
    mc = jnp.max(sc, axis=-1, keepdims=True)
    mc = jnp.where(mc > -jnp.inf, mc, 0.0)
    pc = jnp.exp2(sc - mc)
    dc = jnp.sum(pc, axis=-1, keepdims=True)
    pc = pc / jnp.where(dc > 0, dc, 1.0)
    o_cmp = _dot(pc.reshape(rows, n_cmp).astype(BF16), vc_ref[...])

    psum = jnp.sum(pc, axis=0)
    c2s = c2s_ref[...]
    p_hi = psum.astype(BF16)
    rem = psum - p_hi.astype(F32)
    p_mid = rem.astype(BF16)
    p_lo = (rem - p_mid.astype(F32)).astype(BF16)
    imp = _dot_nt(c2s, p_hi) + _dot_nt(c2s, p_mid) + _dot_nt(c2s, p_lo)

    jj = lax.broadcasted_iota(I32, (n_sel, Q_BLOCK), 0)
    cur = lax.shift_right_logical(t0 + lax.broadcasted_iota(I32, (n_sel, Q_BLOCK), 1), 6)
    valid = jj <= cur
    forced = valid & ((jj == 0) | (jj > cur - SEL_LOCAL))
    score = jnp.where(forced, FORCE, jnp.where(valid, imp, -FORCE))
    rank = jnp.zeros((n_sel, Q_BLOCK), F32)
    for j2 in range(n_sel):
        other = score[j2:j2 + 1, :]
        tie = jnp.where(jj > j2, 1.0, 0.0)
        rank = rank + jnp.where(other > score, 1.0, jnp.where(other == score, tie, 0.0))
    sel_t = jnp.where(rank < k_top, 1.0, 0.0)
    if n_sel < Q_BLOCK:
        sel_t = jnp.concatenate([sel_t, jnp.zeros((Q_BLOCK - n_sel, Q_BLOCK), F32)], axis=0)
    sel = sel_t.T[:, :n_sel].astype(BF16)
    sel_keys = _dot(sel, exp_ref[...])
    for i in range(selb_ref.shape[0]):
        selb_ref[i] = (sel_keys[:, i * SEL_TILE:(i + 1) * SEL_TILE] - 1.0) * (-NEG)

    def reset():
        m_ref[...] = jnp.full(m_ref.shape, NEG, F32)
        l_ref[...] = jnp.zeros(l_ref.shape, F32)
        acc_ref[...] = jnp.zeros(acc_ref.shape, F32)

    reset()
    tq_s = t0 + lax.broadcasted_iota(I32, (Q_BLOCK, SEL_TILE), 0)
    kk_s = lax.broadcasted_iota(I32, (Q_BLOCK, SEL_TILE), 1)

    def sel_body(i, carry):
        k0 = pl.multiple_of(i * SEL_TILE, SEL_TILE)
        bias = selb_ref[i] + jnp.where(kk_s + k0 <= tq_s, 0.0, NEG)
        _flash_step(qb, ks_ref[pl.ds(k0, SEL_TILE), :], vs_ref[pl.ds(k0, SEL_TILE), :], bias, m_ref, l_ref, acc_ref)
        return carry

    lax.fori_loop(0, (t0 + Q_BLOCK + SEL_TILE - 1) // SEL_TILE, sel_body, 0)
    o_sel = acc_ref[...] / l_ref[...]

    wk = WINDOW + Q_BLOCK
    w0 = pl.multiple_of(jnp.maximum(t0 - WINDOW, 0), Q_BLOCK)
    tq_w = t0 + lax.broadcasted_iota(I32, (Q_BLOCK, wk), 0)
    wpos = w0 + lax.broadcasted_iota(I32, (Q_BLOCK, wk), 1)
    w_bias = jnp.where((wpos <= tq_w) & (wpos > tq_w - WINDOW), 0.0, NEG)
    sw = _dot_nt(qb, kw_ref[pl.ds(w0, wk), :]).reshape(Hg, Q_BLOCK, wk) + w_bias[None]
    pw = jnp.exp2(sw - jnp.max(sw, axis=-1, keepdims=True))
    lw = jnp.sum(pw, axis=-1, keepdims=True).reshape(rows, 1)
    o_win = _dot(pw.reshape(rows, wk).astype(BF16), vw_ref[pl.ds(w0, wk), :]) / lw

    gates = jax.nn.sigmoid(sm_ref[...])
    og = og_ref[...]
    for h in range(Hg):
        r = slice(h * Q_BLOCK, (h + 1) * Q_BLOCK)
        o = (gates[:, h:h + 1] * o_cmp[r] + gates[:, Hg + h:Hg + h + 1] * o_sel[r]
             + gates[:, 2 * Hg + h:2 * Hg + h + 1] * o_win[r])
        o_ref[:, h * HEAD_DIM:(h + 1) * HEAD_DIM] = _rms(o, og).astype(o_ref.dtype)


def nsa_attention(z, comp, ks, vs, kw, vw, q_gain, out_gain, B, S):
    G, Hg = NSA_KV_GROUPS, NSA_GROUP_SIZE
    nq = S // Q_BLOCK
    n_cmp = S // CMP_STRIDE
    n_sel = S // SEL_BLOCK
    k_top = min(SEL_TOP_N, n_sel)
    c_start = np.arange(n_cmp)[:, None] * CMP_STRIDE
    s_start = np.arange(n_sel)[None, :] * SEL_BLOCK
    overlap = np.minimum(c_start + CMP_BLOCK, s_start + SEL_BLOCK) - np.maximum(c_start, s_start)
    assert n_sel <= Q_BLOCK
    c2s = jnp.asarray((np.clip(overlap, 0, None) / CMP_STRIDE).T, dtype=BF16)
    expand = jnp.asarray(np.arange(S)[None, :] // SEL_BLOCK == np.arange(n_sel)[:, None], dtype=BF16)
    rows = Hg * Q_BLOCK
    wq = Hg * HEAD_DIM
    kv_spec = pl.BlockSpec((None, None, S, HEAD_DIM), lambda b, g, c: (b, g, 0, 0))
    const = lambda b, g, c: (0, 0)
    return pl.pallas_call(
        functools.partial(_nsa_kernel, k_top=k_top),
        grid=(B, G, nq),
        in_specs=[pl.BlockSpec((Q_BLOCK, wq), lambda b, g, c: (b * nq + c, COL_Q_A // wq + g)),
                  pl.BlockSpec((Q_BLOCK, HEAD_DIM), lambda b, g, c: (b * nq + c, COL_SMALL // HEAD_DIM + g)),
                  pl.BlockSpec((None, None, None, n_cmp, HEAD_DIM), lambda b, g, c: (b, 0, g, 0, 0)),
                  pl.BlockSpec((None, None, None, n_cmp, HEAD_DIM), lambda b, g, c: (b, 1, g, 0, 0)),
                  kv_spec, kv_spec, kv_spec, kv_spec,
                  pl.BlockSpec((1, HEAD_DIM), const), pl.BlockSpec((1, HEAD_DIM), const),
                  pl.BlockSpec((n_sel, n_cmp), const), pl.BlockSpec((n_sel, S), const)],
        out_specs=pl.BlockSpec((Q_BLOCK, wq), lambda b, g, c: (b * nq + c, g)),
        out_shape=jax.ShapeDtypeStruct((B * S, D_NSA), BF16),
        scratch_shapes=[pltpu.VMEM((S // SEL_TILE, Q_BLOCK, SEL_TILE), F32),
                        pltpu.VMEM((rows, HEAD_DIM), F32), pltpu.VMEM((rows, HEAD_DIM), F32),
                        pltpu.VMEM((rows, HEAD_DIM), F32)],
        compiler_params=_params(("parallel", "parallel", "arbitrary")),
        name="nsa_attention",
    )(z, z, comp, comp, ks, vs, kw, vw, q_gain.reshape(1, HEAD_DIM), out_gain.reshape(1, HEAD_DIM), c2s, expand)


def _gdn_chunk_kernel(xc_ref, xh_ref, sm_ref, cw_ref, alog_ref, dtb_ref, rep_ref, unfold_ref, u_ref, wq_ref, ak_ref,
                      egl_ref):
    for b in range(xc_ref.shape[0]):
        _gdn_chunk_body(xc_ref.at[b], xh_ref.at[b], sm_ref.at[b], cw_ref, alog_ref, dtb_ref, rep_ref, unfold_ref,
                        u_ref.at[b], wq_ref.at[b], ak_ref.at[b], egl_ref.at[b])


def _gdn_chunk_body(xc_ref, xh_ref, sm_ref, cw_ref, alog_ref, dtb_ref, rep_ref, unfold_ref, u_ref, wq_ref, ak_ref,
                    egl_ref):
    C = GDN_CHUNK
    PK = GDN_PACK
    R = PK * C
    n = pl.program_id(0)
    xp = jnp.concatenate([jnp.where(n > 0, xh_ref[...], 0.0), xc_ref[...]], axis=0)
    cw = cw_ref[...]
    y = None
    for j in range(GDN_CONV):
        shift = GDN_CONV - 1 - j
        xs = xp if shift == 0 else pltpu.roll(xp, shift, axis=0)
        term = xs[8:] * cw[j:j + 1, :]
        y = term if y is None else y + term
    y = y * jax.nn.sigmoid(y)

    sm = sm_ref[...]
    beta = jax.nn.sigmoid(sm)
    g = -jnp.exp(alog_ref[...]) * jax.nn.softplus(sm + dtb_ref[...])
    row = lax.broadcasted_iota(I32, g.shape, 0)
    gc = g
    d = 1
    while d < C:
        gc = gc + jnp.where(row >= d, pltpu.roll(gc, d, axis=0), 0.0)
        d *= 2
    g_last = gc[C - 1:C, :]

    ri = lax.broadcasted_iota(I32, (R, R), 0)
    ci = lax.broadcasted_iota(I32, (R, R), 1)
    same = lax.shift_right_logical(ri, 6) == lax.shift_right_logical(ci, 6)
    tri = same & (ri >= ci)
    strict = same & (ri > ci)
    SUB = GDN_SUB
    nb = R // SUB
    same16 = lax.shift_right_logical(ri, 4) == lax.shift_right_logical(ci, 4)
    same32 = lax.shift_right_logical(ri, 5) == lax.shift_right_logical(ci, 5)
    off32 = same32 & jnp.logical_not(same16)
    off64 = jnp.logical_not(same32)
    NG = GDN_HEADS // PK
    W4 = NG * SUB
    row_c = lax.broadcasted_iota(I32, (R, W4), 0) & (SUB - 1)
    col_s = lax.broadcasted_iota(I32, (nb, W4), 1) & (SUB - 1)

    groups = []
    for grp in range(NG):
        heads = [grp * PK + i for i in range(PK)]

        def stack(off):
            return jnp.concatenate([y[:, off + h * HEAD_DIM: off + (h + 1) * HEAD_DIM] for h in heads], axis=0)

        def col(x, base):
            return jnp.concatenate([x[:, base + h: base + h + 1] for h in heads], axis=0)

        q4 = stack(0)
        k4 = stack(D_GDN)
        v4 = stack(2 * D_GDN)
        q4 = q4 * lax.rsqrt(jnp.sum(q4 * q4, axis=-1, keepdims=True) + EPS) * SCALE
        k4 = k4 * lax.rsqrt(jnp.sum(k4 * k4, axis=-1, keepdims=True) + EPS)
        beta4 = col(beta, SM_BETA)
        gc4 = col(gc, SM_A)
        gl4 = jnp.concatenate([jnp.broadcast_to(g_last[:, SM_A + h: SM_A + h + 1], (C, 1)) for h in heads], axis=0)

        gb = jnp.broadcast_to(gc4, (R, R))
        decay = jnp.exp(jnp.where(tri, gb - gb.T, -jnp.inf))
        kb4 = k4 * beta4
        k4b = k4.astype(BF16)
        a = jnp.where(strict, _dot_nt(kb4.astype(BF16), k4b) * decay, 0.0)
        attn = _dot_nt(q4.astype(BF16), k4b) * decay
        at = a.T
        dct = jnp.concatenate([at[SUB * b:SUB * (b + 1), SUB * b:SUB * (b + 1)] for b in range(nb)], axis=0)
        groups.append((heads, q4, k4, v4, beta4, gc4, gl4, kb4, a, attn, dct))

    coef = _dot3(jnp.concatenate([g[-1] for g in groups], axis=1), rep_ref[...])
    t4 = jnp.where(row_c == (lax.broadcasted_iota(I32, (R, W4), 1) & (SUB - 1)), 1.0, 0.0)
    for i in range(1, SUB):
        s = jnp.sum((coef[:, i * HEAD_DIM: i * HEAD_DIM + W4] * t4).reshape(nb, SUB, W4), axis=1)
        new = jnp.where(col_s == i, 1.0, 0.0) - s
        t4 = jnp.where(row_c == i, jnp.broadcast_to(new[:, None, :], (nb, SUB, W4)).reshape(R, W4), t4)
    t_tiled = _dot3(t4, unfold_ref[...])

    for gi, (heads, q4, k4, v4, beta4, gc4, gl4, kb4, a, attn, _) in enumerate(groups):
        t16 = jnp.where(same16, t_tiled[:, gi * R:(gi + 1) * R], 0.0)
        t16b = t16.astype(BF16)
        a32 = jnp.where(off32, a, 0.0).astype(BF16)
        t32 = t16 - _dot(_dot(t16b, a32).astype(BF16), t16b)
        t32b = t32.astype(BF16)
        a64 = jnp.where(off64, a, 0.0).astype(BF16)
        t64 = t32 - _dot(_dot(t32b, a64).astype(BF16), t32b)
        eg = jnp.exp(gc4)
        rhs = jnp.concatenate([v4 * beta4, kb4 * eg], axis=1)
        rhs = _dot(t64.astype(BF16), rhs.astype(BF16))
        qd4 = q4 * eg
        kdt = (k4 * jnp.exp(gl4 - gc4)).T
        for i, h in enumerate(heads):
            r = slice(i * C, (i + 1) * C)
            u_ref[h] = rhs[r, :HEAD_DIM]
            wq_ref[h] = jnp.concatenate([rhs[r, HEAD_DIM:], qd4[r]], axis=0).astype(BF16)
            ak_ref[h] = jnp.concatenate([attn[r, r], kdt[:, r]], axis=0).astype(BF16)
            egl_ref[h] = jnp.broadcast_to(jnp.exp(g_last[:, SM_A + h: SM_A + h + 1]), (8, HEAD_DIM))


def gdn_chunks(z, conv_w, a_log, dt_bias, B, S):
    C, H = GDN_CHUNK, GDN_HEADS
    N = S // C
    W = 3 * D_GDN
    pad = lambda v: jnp.zeros((1, HEAD_DIM), F32).at[0, SM_A:SM_A + H].set(v)
    out5 = lambda r, cdim: pl.BlockSpec((B, None, H, r, cdim), lambda n: (0, n, 0, 0, 0))
    NG, SUB, R = H // GDN_PACK, GDN_SUB, GDN_PACK * C
    W4 = NG * SUB
    src = np.arange(W4)[:, None]
    dst = np.arange(SUB * HEAD_DIM)[None, :]
    lane = dst % HEAD_DIM
    rep = (lane < W4) & (src // SUB == lane // SUB) & (src % SUB == dst // HEAD_DIM)
    dst = np.arange(NG * R)[None, :]
    unfold = (src // SUB == dst // R) & (src % SUB == dst % SUB)
    z3 = z.reshape(B, S, z.shape[1])
    const = lambda n: (0, 0)
    return pl.pallas_call(
        _gdn_chunk_kernel,
        grid=(N,),
        in_specs=[pl.BlockSpec((B, C, W), lambda n: (0, n, 0)),
                  pl.BlockSpec((B, 8, W), lambda n: (0, jnp.maximum(n * (C // 8) - 1, 0), 0)),
                  pl.BlockSpec((B, C, HEAD_DIM), lambda n: (0, n, COL_SMALL // HEAD_DIM)),
                  pl.BlockSpec((GDN_CONV, W), const),
                  pl.BlockSpec((1, HEAD_DIM), const),
                  pl.BlockSpec((1, HEAD_DIM), const),
                  pl.BlockSpec((W4, SUB * HEAD_DIM), const),
                  pl.BlockSpec((W4, NG * R), const)],
        out_specs=[out5(C, HEAD_DIM), out5(2 * C, HEAD_DIM), out5(C + HEAD_DIM, C), out5(8, HEAD_DIM)],
        out_shape=[jax.ShapeDtypeStruct((B, N, H, C, HEAD_DIM), F32),
                   jax.ShapeDtypeStruct((B, N, H, 2 * C, HEAD_DIM), BF16),
                   jax.ShapeDtypeStruct((B, N, H, C + HEAD_DIM, C), BF16),
                   jax.ShapeDtypeStruct((B, N, H, 8, HEAD_DIM), F32)],
        compiler_params=_params(("parallel",)),
        name="gdn_chunks",
    )(z3, z3, z3, conv_w, pad(a_log), pad(dt_bias), jnp.asarray(rep, dtype=BF16), jnp.asarray(unfold, dtype=BF16))


def _gdn_scan_kernel(u_ref, wq_ref, ak_ref, egl_ref, z_ref, og_ref, o_ref, s_ref):
    C = GDN_CHUNK

    @pl.when(pl.program_id(0) == 0)
    def _():
        s_ref[...] = jnp.zeros(s_ref.shape, F32)

    og = og_ref[...]
    for b in range(u_ref.shape[0]):
        for h in range(GDN_HEADS):
            state = s_ref[b, h]
            ws = _dot(wq_ref[b, h], state.astype(BF16))
            v_new = u_ref[b, h] - ws[:C]
            av = _dot(ak_ref[b, h], v_new.astype(BF16))
            o = ws[C:] + av[:C]
            decayed = (state.reshape(HEAD_DIM // 8, 8, HEAD_DIM) * egl_ref[b, h][None]).reshape(HEAD_DIM, HEAD_DIM)
            s_ref[b, h] = decayed + av[C:]
            zh = z_ref[b, :, h * HEAD_DIM:(h + 1) * HEAD_DIM]
            o_ref[b, :, h * HEAD_DIM:(h + 1) * HEAD_DIM] = (_rms(o, og) * (zh * jax.nn.sigmoid(zh))).astype(o_ref.dtype)


def gdn_scan(u, wq, ak, egl, z, out_gain, B, S):
    C, H = GDN_CHUNK, GDN_HEADS
    N = S // C
    in5 = lambda r, cdim: pl.BlockSpec((B, None, H, r, cdim), lambda n: (0, n, 0, 0, 0))
    y = pl.pallas_call(
        _gdn_scan_kernel,
        grid=(N,),
        in_specs=[in5(C, HEAD_DIM), in5(2 * C, HEAD_DIM), in5(C + HEAD_DIM, C), in5(8, HEAD_DIM),
                  pl.BlockSpec((B, C, D_GDN), lambda n: (0, n, COL_Z_B // D_GDN)),
                  pl.BlockSpec((1, HEAD_DIM), lambda n: (0, 0))],
        out_specs=pl.BlockSpec((B, C, D_GDN), lambda n: (0, n, 0)),
        out_shape=jax.ShapeDtypeStruct((B, S, D_GDN), BF16),
        scratch_shapes=[pltpu.VMEM((B, H, HEAD_DIM, HEAD_DIM), F32)],
        compiler_params=_params(("arbitrary",)),
        name="gdn_scan",
    )(u, wq, ak, egl, z.reshape(B, S, z.shape[1]), out_gain.reshape(1, HEAD_DIM))
    return y.reshape(B * S, D_GDN)


def _xa_kv_kernel(mem_ref, g_ref, w_ref, kg_ref, k_ref, v_ref):
    kv = _dot(_rms(mem_ref[...], g_ref[...]).astype(BF16), w_ref[...])
    for h in range(XA_HEADS):
        k_ref[h] = _rms(kv[:, h * HEAD_DIM:(h + 1) * HEAD_DIM], kg_ref[...]).astype(BF16)
        v_ref[h] = kv[:, D_XA + h * HEAD_DIM: D_XA + (h + 1) * HEAD_DIM].astype(BF16)


def xa_kv(mem, mem_norm, wkv_bf16, k_gain):
    B, Mm, D = mem.shape
    spec = pl.BlockSpec((None, XA_HEADS, Mm, HEAD_DIM), lambda b: (b, 0, 0, 0))
    shape = jax.ShapeDtypeStruct((B, XA_HEADS, Mm, HEAD_DIM), BF16)
    return pl.pallas_call(
        _xa_kv_kernel,
        grid=(B,),
        in_specs=[pl.BlockSpec((None, Mm, D), lambda b: (b, 0, 0)), pl.BlockSpec((1, D), lambda b: (0, 0)),
                  pl.BlockSpec((D, 2 * D_XA), lambda b: (0, 0)), pl.BlockSpec((1, HEAD_DIM), lambda b: (0, 0))],
        out_specs=[spec, spec],
        out_shape=[shape, shape],
        compiler_params=_params(("parallel",)),
        name="xa_kv",
    )(mem, mem_norm.reshape(1, D), wkv_bf16, k_gain.reshape(1, HEAD_DIM))


def _xa_kernel(x_ref, g_ref, wq_ref, k_ref, v_ref, qg_ref, wo_ref, fg_ref, *rest, route):
    if route:
        rwh_ref, rwl_ref, o_ref, h_ref, idx_ref, wt_ref = rest
    else:
        o_ref, h_ref = rest
    x = x_ref[...]
    q = _dot(_rms(x, g_ref[...]).astype(BF16), wq_ref[...])
    outs = []
    for h in range(XA_HEADS):
        qh = _rms(q[:, h * HEAD_DIM:(h + 1) * HEAD_DIM], qg_ref[...]).astype(BF16)
        s = _dot_nt(qh, k_ref[h]) * SCALE
        p = jnp.exp(s - jnp.max(s, axis=-1, keepdims=True))
        p = p / jnp.sum(p, axis=-1, keepdims=True)
        outs.append(_dot(p.astype(BF16), v_ref[h]))
    o = jnp.concatenate(outs, axis=1).astype(BF16)
    x_new = x + _dot(o, wo_ref[...])
    o_ref[...] = x_new
    hn = _rms(x_new, fg_ref[...])
    h_ref[...] = hn.astype(h_ref.dtype)
    if route:
        idx_ref[...], wt_ref[...] = _route(hn, rwh_ref[...], rwl_ref[...])


def cross_attention(x, xa_norm, wq_bf16, k, v, q_gain, wo_bf16, ffn_norm, router_w, B, S, tm=256):
    M, D = x.shape
    Mm = k.shape[2]
    per_b = S // tm
    kv_spec = pl.BlockSpec((None, XA_HEADS, Mm, HEAD_DIM), lambda i: (i // per_b, 0, 0, 0))
    const = lambda i: (0, 0)
    row = lambda i: (i, 0)
    route = router_w is not None
    in_specs = [pl.BlockSpec((tm, D), row), pl.BlockSpec((1, D), const),
                pl.BlockSpec((D, D_XA), const), kv_spec, kv_spec,
                pl.BlockSpec((1, HEAD_DIM), const), pl.BlockSpec((D_XA, D), const), pl.BlockSpec((1, D), const)]
    args = [x, xa_norm.reshape(1, D), wq_bf16, k, v, q_gain.reshape(1, HEAD_DIM), wo_bf16, ffn_norm.reshape(1, D)]
    out_specs = [pl.BlockSpec((tm, D), row), pl.BlockSpec((tm, D), row)]
    out_shape = [jax.ShapeDtypeStruct((M, D), F32), jax.ShapeDtypeStruct((M, D), BF16)]
    if route:
        rw = jnp.pad(router_w, ((0, 0), (0, 128 - N_EXPERTS)))
        rw_hi = rw.astype(BF16)
        in_specs += [pl.BlockSpec((D, 128), const), pl.BlockSpec((D, 128), const)]
        args += [rw_hi, (rw - rw_hi.astype(F32)).astype(BF16)]
        out_specs += [pl.BlockSpec((tm, 128), row), pl.BlockSpec((tm, 128), row)]
        out_shape += [jax.ShapeDtypeStruct((M, 128), I32), jax.ShapeDtypeStruct((M, 128), F32)]
    return pl.pallas_call(
        functools.partial(_xa_kernel, route=route),
        grid=(M // tm,),
        in_specs=in_specs,
        out_specs=out_specs,
        out_shape=out_shape,
        compiler_params=_params(("parallel",)),
        name="cross_attention",
    )(*args)


def _reorder_w_in(w_in):
    o = np.cumsum((D_NSA, 6 * D_KV_NSA, 3 * NSA_HEADS, 3 * D_GDN, GDN_HEADS, GDN_HEADS, D_GDN))
    q_a, kv_a, gate_a, qkv_b = (0, o[0]), (o[0], o[1]), o[1], (o[2], o[3])
    beta_b, a_b, z_b = o[3], o[4], (o[5], o[6])
    Hg = NSA_GROUP_SIZE
    src = np.zeros((SMALL_W,), np.int64)
    used = np.zeros((SMALL_W,), bool)
    for g in range(NSA_KV_GROUPS):
        for br in range(3):
            for h in range(Hg):
                src[g * HEAD_DIM + br * Hg + h] = (g * Hg + h) * 3 + br
                used[g * HEAD_DIM + br * Hg + h] = True
    n_gate = 3 * NSA_HEADS
    for h in range(GDN_HEADS):
        src[SM_BETA + h] = n_gate + h
        src[SM_A + h] = n_gate + GDN_HEADS + h
        used[SM_BETA + h] = used[SM_A + h] = True
    narrow = jnp.concatenate([w_in[:, gate_a:gate_a + n_gate], w_in[:, beta_b:beta_b + 2 * GDN_HEADS]], axis=1)
    small = jnp.where(jnp.asarray(used)[None, :], jnp.take(narrow, jnp.asarray(src), axis=1), 0.0)
    parts = [w_in[:, qkv_b[0]:qkv_b[1]], w_in[:, q_a[0]:q_a[1]], w_in[:, z_b[0]:z_b[1]], w_in[:, kv_a[0]:kv_a[1]], small]
    return jnp.concatenate([p.astype(BF16) for p in parts], axis=1)


def mixer_layer(x2, B, S, attn_norm, w_in, nsa_q_gain, nsa_k_gain, nsa_cmp_pos, nsa_cmp_w1, nsa_cmp_w2, nsa_out_gain,
                gdn_conv_w, gdn_A_log, gdn_dt_bias, gdn_out_gain, w_out_stack, layer, tm=1024):
    h = rmsnorm_bf16(x2, attn_norm)
    z = matmul_cols(h, _reorder_w_in(w_in), tm=tm, tn=1024, name="in_proj")
    raw, ks, vs, kw, vw = nsa_prep(z, nsa_k_gain, B, S)
    comp = nsa_compress(raw, nsa_cmp_pos, nsa_cmp_w1, nsa_cmp_w2, nsa_k_gain[0], B, S)
    y_a = nsa_attention(z, comp, ks, vs, kw, vw, nsa_q_gain, nsa_out_gain, B, S)
    u, wq, ak, egl = gdn_chunks(z, gdn_conv_w, gdn_A_log, gdn_dt_bias, B, S)
    y_b = gdn_scan(u, wq, ak, egl, z, gdn_out_gain, B, S)
    return matmul2_residual(y_a, y_b, w_out_stack, layer, x2, tm=tm, tn=512, name="out_proj")


def xa_layer(x2, mem, B, S, xa_norm, mem_norm, xa_wq, xa_wkv, xa_q_gain, xa_k_gain, xa_wo, ffn_norm, router_w=None):
    k, v = xa_kv(mem, mem_norm, xa_wkv.astype(BF16), xa_k_gain)
    return cross_attention(x2, xa_norm, xa_wq.astype(BF16), k, v, xa_q_gain, xa_wo.astype(BF16), ffn_norm, router_w,
                           B, S)


def dense_ffn_layer(x2, h, w13, w2, tm=1024):
    act = swiglu_up(h, w13, tm=min(2 * tm, x2.shape[0]), tn=256)
    return down_proj_residual(act, w2, x2, tm=min(2 * tm, x2.shape[0]), tn=1024, tk=512, tk_tail=256)


def moe_ffn_layer(x2, h, idx, wts, w13, w2, tm=512, tn=512):
    src_tok, row_w, pos, tile_expert, n_valid = moe_dispatch(idx[:, :TOP_K], wts[:, :TOP_K], tm)
    rows_of = lambda a, idx: a.at[idx].get(mode="promise_in_bounds")
    xs = rows_of(h, src_tok)
    act = moe_up(xs, w13, tile_expert, n_valid, tm=tm, tn=tn)
    row_w_rep = jnp.broadcast_to(row_w[:, None], (row_w.shape[0], 128))
    out = moe_down(act, w2, row_w_rep, tile_expert, n_valid, tm=tm, tn=tn)
    return x2 + (rows_of(out, pos[:, 0]) + rows_of(out, pos[:, 1]))


def kernel(x, mem, attn_norm, w_in, nsa_q_gain, nsa_k_gain, nsa_cmp_pos, nsa_cmp_w1, nsa_cmp_w2, nsa_out_gain, gdn_conv_w, gdn_A_log, gdn_dt_bias, gdn_out_gain, w_out, xa_norm, mem_norm, xa_wq, xa_wkv, xa_q_gain, xa_k_gain, xa_wo, ffn_norm, dense_w13, dense_w2, router_w, moe_w13, moe_w2):
    B, S, D = x.shape
    x2 = x.reshape(B * S, D)
    for l in range(attn_norm.shape[0]):
        x2 = mixer_layer(x2, B, S, attn_norm[l], w_in[l], nsa_q_gain[l], nsa_k_gain[l], nsa_cmp_pos[l], nsa_cmp_w1[l],
                         nsa_cmp_w2[l], nsa_out_gain[l], gdn_conv_w[l], gdn_A_log[l], gdn_dt_bias[l], gdn_out_gain[l],
                         w_out, l)
        xa_args = (x2, mem, B, S, xa_norm[l], mem_norm[l], xa_wq[l], xa_wkv[l], xa_q_gain[l], xa_k_gain[l], xa_wo[l],
                   ffn_norm[l])
        if l % 2 == 0:
            x2, h = xa_layer(*xa_args)
            x2 = dense_ffn_layer(x2, h, dense_w13[l // 2], dense_w2[l // 2])
        else:
            x2, h, idx, wts = xa_layer(*xa_args, router_w[l // 2])
            x2 = moe_ffn_layer(x2, h, idx, wts, moe_w13[l // 2], moe_w2[l // 2])
    return x2.reshape(B, S, D)
```
